```python
import math
import jax, jax.numpy as jnp
from jax import lax
import numpy as np

D_MODEL = 1024
BATCH = 8
SEQ = 2048
DEPTH = 2
DEC_BATCH = 128
DEC_SEQ = 1
PAST_LEN = 16384
PAGE_SIZE = 128

N_MIXERS = 2
N_S5 = (DEPTH + 1) // 2
N_HG = DEPTH // 2
S5_GROUP = 16
S5_GROUPS = D_MODEL // S5_GROUP
S5_STATE = 64
S5_DT_MIN = 0.001
S5_DT_MAX = 0.1
HG_DK = 128
HG_HEADS = D_MODEL // HG_DK
HG_DV = D_MODEL // HG_HEADS
HG_WIDTH = HG_HEADS * HG_DK
HG_CHUNK = 64
N_MEM = 256
MEM_HEADS = 4
MEM_HD = D_MODEL // MEM_HEADS
FFN_DIM = 2816
EPS = 1e-6

kernel_name = "hybrid_s5_hgrn2_macaron_memxattn_step"


def rmsnorm(x, g):
    xf = x.astype(jnp.float32)
    y = xf * lax.rsqrt(jnp.mean(xf * xf, axis=-1, keepdims=True) + EPS)
    return (y * g.astype(jnp.float32)).astype(x.dtype)


def swiglu(x, w_in, w_out):
    gate, up = jnp.split(x @ w_in, 2, axis=-1)
    return (jax.nn.silu(gate) * up) @ w_out


def s5_mixer(u, h0_re, h0_im, a_re, a_im, log_dt, b_re, b_im, c_re, c_im, d, w_glu):
    bsz, t, _ = u.shape
    f32 = jnp.float32
    ug = u.astype(f32).reshape(bsz, t, S5_GROUPS, S5_GROUP)
    lam = lax.complex(a_re.astype(f32), a_im.astype(f32))
    dt = jnp.exp(log_dt.astype(f32))[:, None]
    lam_bar = jnp.exp(lam * dt)
    b = lax.complex(b_re.astype(f32), b_im.astype(f32))
    b_bar = ((lam_bar - 1.0) / lam)[..., None] * b
    c = lax.complex(c_re.astype(f32), c_im.astype(f32))
    bu = jnp.einsum('btgc,gpc->btgp', ug.astype(jnp.complex64), b_bar)
    a_seq = jnp.broadcast_to(lam_bar, (1, t, S5_GROUPS, S5_STATE))

    def combine(e1, e2):
        a1, x1 = e1
        a2, x2 = e2
        return a1 * a2, a2 * x1 + x2

    a_cum, x_cum = lax.associative_scan(combine, (a_seq, bu), axis=1)
    h0 = lax.complex(h0_re.astype(f32), h0_im.astype(f32))
    h = x_cum + a_cum * h0[:, None]
    y = jnp.einsum('btgp,gcp->btgc', h, c).real + d.astype(f32) * ug
    y = jax.nn.gelu(y.reshape(bsz, t, D_MODEL))
    ya, yb = jnp.split(y @ w_glu.astype(f32), 2, axis=-1)
    out = ya * jax.nn.sigmoid(yb)
    h_last = h[:, -1]
    return out.astype(u.dtype), h_last.real.astype(h0_re.dtype), h_last.imag.astype(h0_im.dtype)


def hgrn2_chunked(q, k, v, logf, s0):
    bsz, t = q.shape[:2]
    c = min(HG_CHUNK, t)
    n = -(-t // c)
    pad = n * c - t

    def prep(a):
        if pad:
            a = jnp.pad(a, ((0, 0), (0, pad), (0, 0), (0, 0)))
        return a.reshape(bsz, n, c, a.shape[2], a.shape[3]).transpose(1, 0, 2, 3, 4)

    qs, ks, vs, gs = prep(q), prep(k), prep(v), prep(logf)
    pos = jnp.arange(c)
    causal = (pos[:, None] >= pos[None, :])[None, :, :, None, None]

    def step(s, inp):
        qc, kc, vc, gc = inp
        bcum = jnp.cumsum(gc, axis=1)
        btot = bcum[:, -1]
        o_inter = jnp.einsum('bthk,bhkv->bthv', qc * jnp.exp(bcum), s)
        diff = bcum[:, :, None] - bcum[:, None, :]
        decay = jnp.exp(jnp.where(causal, diff, -jnp.inf))
        att = jnp.einsum('bthk,bshk,btshk->bhts', qc, kc, decay)
        o_intra = jnp.einsum('bhts,bshv->bthv', att, vc)
        k_dec = kc * jnp.exp(btot[:, None] - bcum)
        s_new = jnp.exp(btot)[..., None] * s + jnp.einsum('bshk,bshv->bhkv', k_dec, vc)
        return s_new, o_inter + o_intra

    s_fin, o = lax.scan(step, s0.astype(jnp.float32), (qs, ks, vs, gs))
    o = o.transpose(1, 0, 2, 3, 4).reshape(bsz, n * c, HG_HEADS, HG_DV)[:, :t]
    return o, s_fin


def hgrn2_mixer(x, s0, lb, w_in, norm_g, w_out):
    bsz, t, _ = x.shape
    f32 = jnp.float32
    proj = (x @ w_in).astype(f32)
    q, fz, i_in, g = jnp.split(proj, [HG_WIDTH, 2 * HG_WIDTH, 2 * HG_WIDTH + HG_HEADS * HG_DV], axis=-1)
    q = jax.nn.silu(q)
    lbf = lb.astype(f32)
    logf = jnp.logaddexp(jnp.log(lbf), jnp.log1p(-lbf) + jax.nn.log_sigmoid(fz))
    k = -jnp.expm1(logf)
    heads = lambda a, dd: a.reshape(bsz, t, HG_HEADS, dd)
    o, s_fin = hgrn2_chunked(heads(q, HG_DK), heads(k, HG_DK), heads(i_in, HG_DV), heads(logf, HG_DK), s0)
    o = rmsnorm(o, norm_g).reshape(bsz, t, HG_HEADS * HG_DV) * jax.nn.silu(g)
    return o.astype(x.dtype) @ w_out, s_fin.astype(s0.dtype)


def mem_kv(mem, norm_g, w_kv):
    bsz = mem.shape[0]
    kk, vv = jnp.split(rmsnorm(mem, norm_g) @ w_kv, 2, axis=-1)
    return kk.reshape(bsz, N_MEM, MEM_HEADS, MEM_HD), vv.reshape(bsz, N_MEM, MEM_HEADS, MEM_HD)


def cross_attend(x, mk, mv, w_q, w_o):
    bsz, t, _ = x.shape
    q = (x @ w_q).reshape(bsz, t, MEM_HEADS, MEM_HD).astype(jnp.float32)
    s = jnp.einsum('bthd,bmhd->bhtm', q, mk.astype(jnp.float32)) * (1.0 / math.sqrt(MEM_HD))
    p = jax.nn.softmax(s, axis=-1)
    o = jnp.einsum('bhtm,bmhd->bthd', p, mv.astype(jnp.float32)).reshape(bsz, t, D_MODEL)
    return o.astype(x.dtype) @ w_o


def trunk(x, s5_re, s5_im, hg_s, mem_k, mem_v, lower_bounds, p):
    new_re, new_im, new_hg = [], [], []
    for i in range(DEPTH):
        j = i // N_MIXERS
        x = x + 0.5 * swiglu(rmsnorm(x, p['ffn1_norm'][i]), p['ffn1_w_in'][i], p['ffn1_w_out'][i])
        h = rmsnorm(x, p['mix_norm'][i])
        if i % N_MIXERS == 0:
            y, hr, hi = s5_mixer(h, s5_re[j], s5_im[j], p['s5_a_re'][j], p['s5_a_im'][j], p['s5_log_dt'][j],
                                 p['s5_b_re'][j], p['s5_b_im'][j], p['s5_c_re'][j], p['s5_c_im'][j],
                                 p['s5_d'][j], p['s5_w_glu'][j])
            new_re.append(hr)
            new_im.append(hi)
        else:
            y, sn = hgrn2_mixer(h, hg_s[j], lower_bounds[i], p['hg_w_in'][j], p['hg_norm'][j], p['hg_w_out'][j])
            new_hg.append(sn)
        x = x + y
        x = x + cross_attend(rmsnorm(x, p['xattn_norm'][i]), mem_k[i], mem_v[i], p['xattn_w_q'][i], p['xattn_w_o'][i])
        x = x + 0.5 * swiglu(rmsnorm(x, p['ffn2_norm'][i]), p['ffn2_w_in'][i], p['ffn2_w_out'][i])
    x = rmsnorm(x, p['final_norm'])
    return x, jnp.stack(new_re), jnp.stack(new_im), jnp.stack(new_hg)


def setup_inputs(seed: int = 0) -> dict:
    key = jax.random.key(seed)
    ks = iter(jax.random.split(key, 48))
    f32 = jnp.float32

    def nrm(shape, scale):
        return jax.random.normal(next(ks), shape, f32) * scale

    def gain(shape):
        return 1.0 + 0.05 * jax.random.normal(next(ks), shape, f32)

    G, P, GC = S5_GROUPS, S5_STATE, S5_GROUP
    n_idx = jnp.arange(P, dtype=f32)
    return {
        "x_prompt": nrm((BATCH, SEQ, D_MODEL), 1.0),
        "x_sample": nrm((DEC_BATCH, DEC_SEQ, D_MODEL), 1.0),
        "mem_prompt": nrm((BATCH, N_MEM, D_MODEL), 1.0),
        "state_s5_re": nrm((N_S5, DEC_BATCH, G, P), 0.1),
        "state_s5_im": nrm((N_S5, DEC_BATCH, G, P), 0.1),
        "state_hgrn": nrm((N_HG, DEC_BATCH, HG_HEADS, HG_DK, HG_DV), 0.3),
        "cache_mem_k": nrm((DEPTH, DEC_BATCH, N_MEM, MEM_HEADS, MEM_HD), 1.0),
        "cache_mem_v": nrm((DEPTH, DEC_BATCH, N_MEM, MEM_HEADS, MEM_HD), 1.0),
        "ffn1_norm": gain((DEPTH, D_MODEL)),
        "ffn1_w_in": nrm((DEPTH, D_MODEL, 2 * FFN_DIM), D_MODEL ** -0.5),
        "ffn1_w_out": nrm((DEPTH, FFN_DIM, D_MODEL), FFN_DIM ** -0.5),
        "mix_norm": gain((DEPTH, D_MODEL)),
        "xattn_norm": gain((DEPTH, D_MODEL)),
        "mem_norm": gain((DEPTH, D_MODEL)),
        "xattn_w_q": nrm((DEPTH, D_MODEL, D_MODEL), D_MODEL ** -0.5),
        "xattn_w_kv": nrm((DEPTH, D_MODEL, 2 * D_MODEL), D_MODEL ** -0.5),
        "xattn_w_o": nrm((DEPTH, D_MODEL, D_MODEL), D_MODEL ** -0.5),
        "ffn2_norm": gain((DEPTH, D_MODEL)),
        "ffn2_w_in": nrm((DEPTH, D_MODEL, 2 * FFN_DIM), D_MODEL ** -0.5),
        "ffn2_w_out": nrm((DEPTH, FFN_DIM, D_MODEL), FFN_DIM ** -0.5),
        "s5_a_re": -0.5 + nrm((N_S5, G, P), 0.01),
        "s5_a_im": jnp.pi * n_idx + nrm((N_S5, G, P), 0.01),
        "s5_log_dt": jax.random.uniform(next(ks), (N_S5, G), f32, math.log(S5_DT_MIN), math.log(S5_DT_MAX)),
        "s5_b_re": nrm((N_S5, G, P, GC), (2 * GC) ** -0.5),
        "s5_b_im": nrm((N_S5, G, P, GC), (2 * GC) ** -0.5),
        "s5_c_re": nrm((N_S5, G, GC, P), 2 ** -0.5),
        "s5_c_im": nrm((N_S5, G, GC, P), 2 ** -0.5),
        "s5_d": nrm((N_S5, G, GC), 1.0),
        "s5_w_glu": nrm((N_S5, D_MODEL, 2 * D_MODEL), D_MODEL ** -0.5),
        "hg_w_in": nrm((N_HG, D_MODEL, 2 * HG_WIDTH + 2 * HG_HEADS * HG_DV), D_MODEL ** -0.5),
        "hg_lb_logits": nrm((DEPTH, HG_WIDTH), 0.5),
        "hg_norm": gain((N_HG, HG_DV)),
        "hg_w_out": nrm((N_HG, HG_HEADS * HG_DV, D_MODEL), D_MODEL ** -0.5),
        "final_norm": gain((D_MODEL,)),
    }


def reference(x_prompt, x_sample, mem_prompt, state_s5_re, state_s5_im, state_hgrn, cache_mem_k, cache_mem_v,
              ffn1_norm, ffn1_w_in, ffn1_w_out, mix_norm, xattn_norm, mem_norm, xattn_w_q, xattn_w_kv, xattn_w_o,
              ffn2_norm, ffn2_w_in, ffn2_w_out, s5_a_re, s5_a_im, s5_log_dt, s5_b_re, s5_b_im, s5_c_re, s5_c_im,
              s5_d, s5_w_glu, hg_w_in, hg_lb_logits, hg_norm, hg_w_out, final_norm):
    p = dict(ffn1_norm=ffn1_norm, ffn1_w_in=ffn1_w_in, ffn1_w_out=ffn1_w_out, mix_norm=mix_norm,
             xattn_norm=xattn_norm, xattn_w_q=xattn_w_q, xattn_w_o=xattn_w_o, ffn2_norm=ffn2_norm,
             ffn2_w_in=ffn2_w_in, ffn2_w_out=ffn2_w_out, s5_a_re=s5_a_re, s5_a_im=s5_a_im, s5_log_dt=s5_log_dt,
             s5_b_re=s5_b_re, s5_b_im=s5_b_im, s5_c_re=s5_c_re, s5_c_im=s5_c_im, s5_d=s5_d, s5_w_glu=s5_w_glu,
             hg_w_in=hg_w_in, hg_norm=hg_norm, hg_w_out=hg_w_out, final_norm=final_norm)
    sm = jax.nn.softmax(hg_lb_logits.astype(jnp.float32), axis=0)
    lower_bounds = jnp.cumsum(sm, axis=0) - sm[0]

    bp = x_prompt.shape[0]
    z_re = jnp.zeros((N_S5, bp, S5_GROUPS, S5_STATE), state_s5_re.dtype)
    z_hg = jnp.zeros((N_HG, bp, HG_HEADS, HG_DK, HG_DV), state_hgrn.dtype)
    kv = [mem_kv(mem_prompt, mem_norm[i], xattn_w_kv[i]) for i in range(DEPTH)]
    mem_k_p = jnp.stack([a for a, _ in kv])
    mem_v_p = jnp.stack([b for _, b in kv])
    y_prompt, s5_re_p, s5_im_p, hg_p = trunk(x_prompt, z_re, z_re, z_hg, mem_k_p, mem_v_p, lower_bounds, p)

    y_sample, s5_re_s, s5_im_s, hg_s = trunk(x_sample, state_s5_re, state_s5_im, state_hgrn,
                                             cache_mem_k, cache_mem_v, lower_bounds, p)
    return (y_prompt, y_sample, s5_re_p, s5_im_p, s5_re_s, s5_im_s, hg_p, hg_s, mem_k_p, mem_v_p)
```

```python
import functools
import math

import jax
import jax.numpy as jnp
from jax import lax
from jax.experimental import pallas as pl
from jax.experimental.pallas import tpu as pltpu

F32 = jnp.float32
BF16 = jnp.bfloat16

D_MODEL = 1024
DEPTH = 2
S5_GROUP = 16
S5_GROUPS = D_MODEL // S5_GROUP
S5_STATE = 64
S5_CHUNK = 16
HG_DK = 128
HG_HEADS = D_MODEL // HG_DK
HG_DV = D_MODEL // HG_HEADS
HG_WIDTH = HG_HEADS * HG_DK
HG_CHUNK = 64
N_MEM = 256
MEM_HEADS = 4
MEM_HD = D_MODEL // MEM_HEADS
FFN_DIM = 2816
EPS = 1e-6

V7X_VMEM_LIMIT_BYTES = 56 * 1024 * 1024

ROW_TILE = 512
FFN_CHUNK = 1408
HG_ROW_TILE = 256
XA_ROW_TILE = 512
S5_GROUPS_PER_STEP = 4
SAMPLE_TOKENS_PER_STEP = 8
XA_SAMPLE_TOKENS_PER_STEP = 4


def _params(*semantics):
    return pltpu.CompilerParams(dimension_semantics=semantics,
                                vmem_limit_bytes=V7X_VMEM_LIMIT_BYTES)


def _resident(shape, index_map):
    return pl.BlockSpec(shape, index_map, pipeline_mode=pl.Buffered(1))


def _rms(x, g):
    ms = jnp.mean(x * x, axis=-1, keepdims=True)
    return x * lax.rsqrt(ms + EPS) * g


def _sigmoid(x):
    return 1.0 / (1.0 + jnp.exp(-x))


def _silu(x):
    return x * _sigmoid(x)


def _gelu_tanh(x):
    c = math.sqrt(2.0 / math.pi)
    return 0.5 * x * (1.0 + jnp.tanh(c * (x + 0.044715 * (x * x * x))))


def _split3(x):
    hi = x.astype(BF16)
    r1 = x - hi.astype(F32)
    mid = r1.astype(BF16)
    lo = (r1 - mid.astype(F32)).astype(BF16)
    return hi, mid, lo


def _ffn_kernel(x_ref, g_ref, win_ref, wout_ref, *rest, post):
    x = x_ref[...]
    h = _rms(x, g_ref[...]).astype(BF16)
    acc = jnp.zeros_like(x)
    for c in range(FFN_DIM // FFN_CHUNK):
        lo = c * FFN_CHUNK
        gate = jnp.dot(h, win_ref[:, lo:lo + FFN_CHUNK], preferred_element_type=F32)
        up = jnp.dot(h, win_ref[:, FFN_DIM + lo:FFN_DIM + lo + FFN_CHUNK], preferred_element_type=F32)
        act = (_silu(gate) * up).astype(BF16)
        acc = acc + jnp.dot(act, wout_ref[lo:lo + FFN_CHUNK, :], preferred_element_type=F32)
    y = x + 0.5 * acc
    if post == "none":
        (y_ref,) = rest
        y_ref[...] = y
    elif post == "replace":
        g2_ref, y_ref = rest
        y_ref[...] = _rms(y, g2_ref[...])
    else:
        g2_ref, y_ref, u_ref = rest
        y_ref[...] = y
        u_ref[...] = _rms(y, g2_ref[...]).astype(u_ref.dtype)


def _ffn(x, g, w_in, w_out, layer, post="none", g2=None, u_dtype=BF16):
    rows = x.shape[0]
    tm = min(ROW_TILE, rows)
    row_spec = pl.BlockSpec((tm, D_MODEL), lambda i: (i, 0))
    vec_spec = pl.BlockSpec((1, D_MODEL), lambda i: (0, 0))
    in_specs = [row_spec,
                pl.BlockSpec((None, 1, D_MODEL), lambda i: (layer, 0, 0)),
                _resident((None, D_MODEL, 2 * FFN_DIM), lambda i: (layer, 0, 0)),
                _resident((None, FFN_DIM, D_MODEL), lambda i: (layer, 0, 0))]
    args = [x, g.reshape(DEPTH, 1, D_MODEL), w_in, w_out]
    out_shape = jax.ShapeDtypeStruct((rows, D_MODEL), F32)
    out_specs = row_spec
    if post != "none":
        in_specs.append(vec_spec)
        args.append(g2.reshape(1, D_MODEL))
    if post == "emit":
        out_shape = (out_shape, jax.ShapeDtypeStruct((rows, D_MODEL), u_dtype))
        out_specs = (row_spec, row_spec)
    return pl.pallas_call(
        functools.partial(_ffn_kernel, post=post),
        grid=(rows // tm,),
        in_specs=in_specs, out_specs=out_specs, out_shape=out_shape,
        compiler_params=_params("parallel"),
        name="ffn",
    )(*args)


def _proj_kernel(*refs, norm, glu, residual):
    refs = list(refs)
    x_ref = refs.pop(0)
    g_ref = refs.pop(0) if norm else None
    w_ref = refs.pop(0)
    res_ref = refs.pop(0) if residual else None
    (o_ref,) = refs
    x = x_ref[...]
    if norm:
        x = _rms(x, g_ref[...])
    y = jnp.dot(x.astype(BF16), w_ref[...], preferred_element_type=F32)
    if glu:
        half = y.shape[-1] // 2
        y = y[:, :half] * _sigmoid(y[:, half:])
    if residual:
        y = y + res_ref[...]
    o_ref[...] = y


def _proj(x, w, layer, g=None, g_layer=0, glu=False, res=None):
    rows, kdim = x.shape
    ndim = w.shape[-1]
    nout = ndim // 2 if glu else ndim
    tm = min(ROW_TILE, rows)
    in_specs = [pl.BlockSpec((tm, kdim), lambda i: (i, 0))]
    args = [x]
    if g is not None:
        in_specs.append(pl.BlockSpec((None, 1, kdim), lambda i: (g_layer, 0, 0)))
        args.append(g.reshape(g.shape[0], 1, kdim))
    in_specs.append(_resident((None, kdim, ndim), lambda i: (layer, 0, 0)))
    args.append(w)
    if res is not None:
        in_specs.append(pl.BlockSpec((tm, nout), lambda i: (i, 0)))
        args.append(res)
    return pl.pallas_call(
        functools.partial(_proj_kernel, norm=g is not None, glu=glu, residual=res is not None),
        grid=(rows // tm,),
        in_specs=in_specs,
        out_specs=pl.BlockSpec((tm, nout), lambda i: (i, 0)),
        out_shape=jax.ShapeDtypeStruct((rows, nout), F32),
        compiler_params=_params("parallel"),
        name="proj",
    )(*args)


def _memkv_kernel(x_ref, g_ref, w_ref, k_ref, v_ref):
    h = _rms(x_ref[...], g_ref[...]).astype(BF16)
    y = jnp.dot(h, w_ref[...], preferred_element_type=F32)
    k_ref[...] = y[:, :D_MODEL]
    v_ref[...] = y[:, D_MODEL:]


def _mem_kv(mem, g, w_kv):
    rows = mem.shape[0]
    tm = min(ROW_TILE, rows)
    out = jax.ShapeDtypeStruct((DEPTH, rows, D_MODEL), F32)
    out_spec = pl.BlockSpec((None, tm, D_MODEL), lambda l, i: (l, i, 0))
    return pl.pallas_call(
        _memkv_kernel,
        grid=(DEPTH, rows // tm),
        in_specs=[pl.BlockSpec((tm, D_MODEL), lambda l, i: (i, 0)),
                  pl.BlockSpec((None, 1, D_MODEL), lambda l, i: (l, 0, 0)),
                  pl.BlockSpec((None, D_MODEL, 2 * D_MODEL), lambda l, i: (l, 0, 0))],
        out_specs=(out_spec, out_spec), out_shape=(out, out),
        compiler_params=_params("parallel", "parallel"),
        name="mem_kv",
    )(mem, g.reshape(DEPTH, 1, D_MODEL), w_kv)


def _s5_matrices(a_re, a_im, log_dt, b_re, b_im, c_re, c_im, d, steps):
    G, P, GC = S5_GROUPS, S5_STATE, S5_GROUP
    L = steps
    dt = jnp.exp(log_dt)[:, None]
    xr, xi = a_re * dt, a_im * dt
    j = jnp.arange(L + 1, dtype=F32)[:, None, None]
    mag = jnp.exp(xr[None] * j)
    pw_re, pw_im = mag * jnp.cos(xi[None] * j), mag * jnp.sin(xi[None] * j)
    nr, ni = pw_re[1] - 1.0, pw_im[1]
    den = a_re * a_re + a_im * a_im
    fr, fi = (nr * a_re + ni * a_im) / den, (ni * a_re - nr * a_im) / den
    bb_re = fr[..., None] * b_re - fi[..., None] * b_im
    bb_im = fr[..., None] * b_im + fi[..., None] * b_re
    hp = lax.Precision.HIGHEST
    w_re = pw_re[:L, :, None, :] * c_re[None] - pw_im[:L, :, None, :] * c_im[None]
    w_im = pw_re[:L, :, None, :] * c_im[None] + pw_im[:L, :, None, :] * c_re[None]
    kern = (jnp.einsum("gpi,lgop->lgio", bb_re, w_re, precision=hp)
            - jnp.einsum("gpi,lgop->lgio", bb_im, w_im, precision=hp))
    li = jnp.arange(L)[:, None]
    lo = jnp.arange(L)[None, :]
    lag = lo - li
    tmat = jnp.where((lag >= 0)[:, :, None, None, None], kern[jnp.clip(lag, 0, L - 1)], 0.0)
    eye = jnp.eye(GC, dtype=F32)
    tmat = tmat + (li == lo)[:, :, None, None, None] * (d[None, None, :, :, None] * eye[None, None, None])
    tmat = tmat.transpose(2, 0, 3, 1, 4).reshape(G, L * GC, L * GC)
    rp_re, rp_im = pw_re[:L][::-1], pw_im[:L][::-1]
    m_re = rp_re[:, :, :, None] * bb_re[None] - rp_im[:, :, :, None] * bb_im[None]
    m_im = rp_re[:, :, :, None] * bb_im[None] + rp_im[:, :, :, None] * bb_re[None]
    m_re = m_re.transpose(1, 0, 3, 2).reshape(G, L * GC, P)
    m_im = m_im.transpose(1, 0, 3, 2).reshape(G, L * GC, P)
    q_re = pw_re[1:, :, None, :] * c_re[None] - pw_im[1:, :, None, :] * c_im[None]
    q_im = pw_re[1:, :, None, :] * c_im[None] + pw_im[1:, :, None, :] * c_re[None]
    p_re = q_re.transpose(1, 3, 0, 2).reshape(G, P, L * GC)
    p_im = (-q_im).transpose(1, 3, 0, 2).reshape(G, P, L * GC)
    return tmat, m_re, m_im, p_re, p_im, pw_re[L][:, None, :], pw_im[L][:, None, :]


def _s5_kernel(u_ref, t_ref, mre_ref, mim_ref, pre_ref, pim_ref, lre_ref, lim_ref, h0re_ref, h0im_ref,
               y_ref, hre_ref, him_ref, inj_re, inj_im, hs_re, hs_im, *, n_chunks, rb, gps, precision):
    def mm(a, b):
        return jnp.dot(a, b, preferred_element_type=F32, precision=precision)

    for g in range(gps):
        u = u_ref[g]
        inj_re[g] = mm(u, mre_ref[g])
        inj_im[g] = mm(u, mim_ref[g])

    lam_re = [jnp.broadcast_to(lre_ref[g], (rb, S5_STATE)) for g in range(gps)]
    lam_im = [jnp.broadcast_to(lim_ref[g], (rb, S5_STATE)) for g in range(gps)]

    def step(k, carry):
        rows = pl.ds(pl.multiple_of(k * rb, rb), rb)
        nxt = []
        for g in range(gps):
            hr, hi = carry[2 * g], carry[2 * g + 1]
            hs_re[g, rows, :] = hr
            hs_im[g, rows, :] = hi
            nxt.append(lam_re[g] * hr - lam_im[g] * hi + inj_re[g, rows, :])
            nxt.append(lam_re[g] * hi + lam_im[g] * hr + inj_im[g, rows, :])
        return tuple(nxt)

    init = []
    for g in range(gps):
        init += [h0re_ref[g], h0im_ref[g]]
    fin = lax.fori_loop(0, n_chunks, step, tuple(init))

    for g in range(gps):
        hre_ref[g] = fin[2 * g]
        him_ref[g] = fin[2 * g + 1]
        dt = u_ref.dtype
        y = (mm(u_ref[g], t_ref[g]) + mm(hs_re[g].astype(dt), pre_ref[g])
             + mm(hs_im[g].astype(dt), pim_ref[g]))
        y_ref[g] = _gelu_tanh(y).astype(y_ref.dtype)


def _s5_core(u, mats, h0_re, h0_im, n_chunks, rb, precision, y_dtype):
    G, R, W = u.shape
    P = S5_STATE
    gps = S5_GROUPS_PER_STEP
    tmat, m_re, m_im, p_re, p_im, l_re, l_im = mats

    def spec(a, b):
        return pl.BlockSpec((gps, a, b), lambda i: (i, 0, 0))

    st = jax.ShapeDtypeStruct((G, rb, P), F32)
    scr = lambda: pltpu.VMEM((gps, R, P), F32)
    return pl.pallas_call(
        functools.partial(_s5_kernel, n_chunks=n_chunks, rb=rb, gps=gps, precision=precision),
        grid=(G // gps,),
        in_specs=[spec(R, W), spec(W, W), spec(W, P), spec(W, P), spec(P, W), spec(P, W),
                  spec(1, P), spec(1, P), spec(rb, P), spec(rb, P)],
        out_specs=(spec(R, W), spec(rb, P), spec(rb, P)),
        out_shape=(jax.ShapeDtypeStruct((G, R, W), y_dtype), st, st),
        scratch_shapes=[scr(), scr(), scr(), scr()],
        compiler_params=_params("parallel"),
        name="s5_core",
    )(u, tmat, m_re, m_im, p_re, p_im, l_re, l_im, h0_re, h0_im)


def _s5_mixer(x, u, h0_re, h0_im, s5p, w_glu, j, batch, seq):
    G, GC, P = S5_GROUPS, S5_GROUP, S5_STATE
    steps = min(S5_CHUNK, seq)
    n_chunks = seq // steps
    prompt = seq > 1
    dt = BF16 if prompt else F32
    precision = None if prompt else lax.Precision.HIGHEST
    mats = _s5_matrices(*[p[j] for p in s5p], steps)
    mats = tuple(m.astype(dt) for m in mats[:5]) + mats[5:]
    ug = u.reshape(batch, n_chunks, steps, G, GC).transpose(3, 1, 0, 2, 4).reshape(G, n_chunks * batch, steps * GC)
    yg, hre, him = _s5_core(ug.astype(dt), mats, h0_re.transpose(1, 0, 2), h0_im.transpose(1, 0, 2),
                            n_chunks, batch, precision, dt)
    y = yg.reshape(G, n_chunks, batch, steps, GC).transpose(2, 1, 3, 0, 4).reshape(batch * seq, D_MODEL)
    x = _proj(y, w_glu, j, glu=True, res=x)
    return x, hre.transpose(1, 0, 2), him.transpose(1, 0, 2)


def _hg_lower_bound(logits, layer):
    m = jnp.max(logits, axis=0, keepdims=True)
    e = jnp.exp(logits - m)
    sm = e / jnp.sum(e, axis=0, keepdims=True)
    return jnp.sum(sm[:layer + 1], axis=0, keepdims=True) - sm[0:1]


def _hg_gates(z, lb):
    e = jnp.exp(-jnp.abs(z))
    r = 1.0 / (1.0 + e)
    log_sig = jnp.minimum(z, 0.0) - jnp.log1p(e)
    a = jnp.log(lb)
    b = jnp.log1p(-lb) + log_sig
    logf = jnp.maximum(a, b) + jnp.log1p(jnp.exp(-jnp.abs(a - b)))
    k = (1.0 - lb) * jnp.where(z >= 0.0, e * r, r)
    return logf, k


def _hgrn_prompt_kernel(x_ref, g_ref, win_ref, lbl_ref, ng_ref, wout_ref, y_ref, sfin_ref,
                        s_scr, q_scr, lf_scr, k_scr, v_scr, gt_scr, o_scr, *, layer, tt):
    t = pl.program_id(1)
    C = HG_CHUNK

    @pl.when(t == 0)
    def _():
        s_scr[...] = jnp.zeros_like(s_scr)

    x = x_ref[...]
    h = _rms(x, g_ref[...]).astype(BF16)
    proj = jnp.dot(h, win_ref[...], preferred_element_type=F32)
    lb = _hg_lower_bound(lbl_ref[...], layer)
    logf, kk = _hg_gates(proj[:, HG_WIDTH:2 * HG_WIDTH], lb)
    q_scr[...] = _silu(proj[:, :HG_WIDTH])
    lf_scr[...] = logf
    k_scr[...] = kk
    v_scr[...] = proj[:, 2 * HG_WIDTH:3 * HG_WIDTH].astype(BF16)
    gt_scr[...] = _silu(proj[:, 3 * HG_WIDTH:])

    row = lax.broadcasted_iota(jnp.int32, (C, C), 0)
    col = lax.broadcasted_iota(jnp.int32, (C, C), 1)
    causal = row >= col
    tri = causal.astype(BF16)
    ones = jnp.ones((C, HG_DV), BF16)
    ng = ng_ref[...]

    def chunk_step(c, carry):
        rows = pl.ds(pl.multiple_of(c * C, C), C)
        lf = lf_scr[rows, :]
        parts = _split3(lf)
        beta = sum(jnp.dot(tri, p, preferred_element_type=F32) for p in parts)
        btot_col = sum(lax.dot_general(p, ones, (((0,), (0,)), ((), ())), preferred_element_type=F32)
                       for p in parts)
        btot = beta[C - 1:C, :]
        mid = beta[C // 2 - 1:C // 2, :]
        q = q_scr[rows, :]
        k = k_scr[rows, :]
        q_dec = (q * jnp.exp(beta)).astype(BF16)
        q_mid = (q * jnp.exp(beta - mid)).astype(BF16)
        k_mid = (k * jnp.exp(mid - beta)).astype(BF16)
        k_dec = (k * jnp.exp(btot - beta)).astype(BF16)
        v = v_scr[rows, :]
        gt = gt_scr[rows, :]
        for hh in range(HG_HEADS):
            sk = slice(hh * HG_DK, (hh + 1) * HG_DK)
            sv = slice(hh * HG_DV, (hh + 1) * HG_DV)
            s = s_scr[hh]
            o = jnp.dot(q_dec[:, sk], s.astype(BF16), preferred_element_type=F32)
            att = lax.dot_general(q_mid[:, sk], k_mid[:, sk], (((1,), (1,)), ((), ())),
                                  preferred_element_type=F32)
            att = jnp.where(causal, att, 0.0).astype(BF16)
            o = o + jnp.dot(att, v[:, sv], preferred_element_type=F32)
            kv = lax.dot_general(k_dec[:, sk], v[:, sv], (((0,), (0,)), ((), ())),
                                 preferred_element_type=F32)
            s_scr[hh] = jnp.exp(btot_col[sk, :]) * s + kv
            o_scr[rows, sv] = (_rms(o, ng) * gt[:, sv]).astype(BF16)
        return carry

    lax.fori_loop(0, tt // C, chunk_step, 0)
    y_ref[...] = x + jnp.dot(o_scr[...], wout_ref[...], preferred_element_type=F32)

    @pl.when(t == pl.num_programs(1) - 1)
    def _():
        sfin_ref[...] = s_scr[...]


def _hgrn_prompt(x, g, w_in, lb_logits, norm_g, w_out, layer, j, batch, seq):
    tt = HG_ROW_TILE
    nt = seq // tt
    x3 = x.reshape(batch, seq, D_MODEL)
    row_spec = pl.BlockSpec((None, tt, D_MODEL), lambda b, t: (b, t, 0))
    y, s_fin = pl.pallas_call(
        functools.partial(_hgrn_prompt_kernel, layer=layer, tt=tt),
        grid=(batch, nt),
        in_specs=[row_spec,
                  pl.BlockSpec((None, 1, D_MODEL), lambda b, t: (layer, 0, 0)),
                  _resident((None, D_MODEL, 4 * HG_WIDTH), lambda b, t: (j, 0, 0)),
                  pl.BlockSpec((DEPTH, HG_WIDTH), lambda b, t: (0, 0)),
                  pl.BlockSpec((None, 1, HG_DV), lambda b, t: (j, 0, 0)),
                  _resident((None, HG_WIDTH, D_MODEL), lambda b, t: (j, 0, 0))],
        out_specs=(row_spec,
                   pl.BlockSpec((None, HG_HEADS, HG_DK, HG_DV), lambda b, t: (b, 0, 0, 0))),
        out_shape=(jax.ShapeDtypeStruct((batch, seq, D_MODEL), F32),
                   jax.ShapeDtypeStruct((batch, HG_HEADS, HG_DK, HG_DV), F32)),
        scratch_shapes=[pltpu.VMEM((HG_HEADS, HG_DK, HG_DV), F32),
                        pltpu.VMEM((tt, HG_WIDTH), F32), pltpu.VMEM((tt, HG_WIDTH), F32),
                        pltpu.VMEM((tt, HG_WIDTH), F32), pltpu.VMEM((tt, HG_WIDTH), BF16),
                        pltpu.VMEM((tt, HG_WIDTH), F32), pltpu.VMEM((tt, HG_WIDTH), BF16)],
        compiler_params=_params("parallel", "arbitrary"),
        name="hgrn_prompt",
    )(x3, g.reshape(DEPTH, 1, D_MODEL), w_in, lb_logits, norm_g.reshape(-1, 1, HG_DV), w_out)
    return y.reshape(batch * seq, D_MODEL), s_fin


def _hgrn_sample_kernel(proj_ref, lbl_ref, ng_ref, s_ref, snew_ref, o_ref, *, layer, tb):
    proj = proj_ref[...]
    lb = _hg_lower_bound(lbl_ref[...], layer)
    z = proj[:, HG_WIDTH:2 * HG_WIDTH]
    e = jnp.exp(-jnp.abs(z))
    r = 1.0 / (1.0 + e)
    sig = jnp.where(z >= 0.0, r, e * r)
    f = lb + (1.0 - lb) * sig
    k = (1.0 - lb) * jnp.where(z >= 0.0, e * r, r)
    q = _silu(proj[:, :HG_WIDTH])
    v = proj[:, 2 * HG_WIDTH:3 * HG_WIDTH]
    gt = _silu(proj[:, 3 * HG_WIDTH:])
    ng = ng_ref[...]
    for hh in range(HG_HEADS):
        sk = slice(hh * HG_DK, (hh + 1) * HG_DK)
        sv = slice(hh * HG_DV, (hh + 1) * HG_DV)
        f_t = f[:, sk].T
        k_t = k[:, sk].T
        q_t = q[:, sk].T
        for b in range(tb):
            s_new = f_t[:, b:b + 1] * s_ref[b, hh] + k_t[:, b:b + 1] * v[b:b + 1, sv]
            snew_ref[b, hh] = s_new
            o = jnp.sum(q_t[:, b:b + 1] * s_new, axis=0, keepdims=True)
            o_ref[b:b + 1, sv] = _rms(o, ng) * gt[b:b + 1, sv]


def _hgrn_sample(proj, lb_logits, norm_g, state, layer, j):
    nb = proj.shape[0]
    tb = SAMPLE_TOKENS_PER_STEP
    st_spec = pl.BlockSpec((tb, HG_HEADS, HG_DK, HG_DV), lambda i: (i, 0, 0, 0))
    return pl.pallas_call(
        functools.partial(_hgrn_sample_kernel, layer=layer, tb=tb),
        grid=(nb // tb,),
        in_specs=[pl.BlockSpec((tb, 4 * HG_WIDTH), lambda i: (i, 0)),
                  pl.BlockSpec((DEPTH, HG_WIDTH), lambda i: (0, 0)),
                  pl.BlockSpec((None, 1, HG_DV), lambda i: (j, 0, 0)),
                  st_spec],
        out_specs=(st_spec, pl.BlockSpec((tb, HG_WIDTH), lambda i: (i, 0))),
        out_shape=(jax.ShapeDtypeStruct(state.shape, F32), jax.ShapeDtypeStruct((nb, HG_WIDTH), F32)),
        compiler_params=_params("parallel"),
        name="hgrn_sample",
    )(proj, lb_logits, norm_g.reshape(-1, 1, HG_DV), state)


def _xattn_prompt_kernel(x_ref, g_ref, wq_ref, k_ref, v_ref, wo_ref, y_ref, kb_scr, vb_scr):
    @pl.when(pl.program_id(1) == 0)
    def _():
        kb_scr[...] = k_ref[...].astype(BF16)
        vb_scr[...] = v_ref[...].astype(BF16)

    x = x_ref[...]
    h = _rms(x, g_ref[...]).astype(BF16)
    q = jnp.dot(h, wq_ref[...], preferred_element_type=F32) * (1.0 / math.sqrt(MEM_HD))
    q = q.astype(BF16)
    outs = []
    for hh in range(MEM_HEADS):
        sl = slice(hh * MEM_HD, (hh + 1) * MEM_HD)
        s = lax.dot_general(q[:, sl], kb_scr[:, sl], (((1,), (1,)), ((), ())), preferred_element_type=F32)
        p = jnp.exp(s - jnp.max(s, axis=-1, keepdims=True))
        den = jnp.sum(p, axis=-1, keepdims=True)
        o = jnp.dot(p.astype(BF16), vb_scr[:, sl], preferred_element_type=F32)
        outs.append((o / den).astype(BF16))
    o = jnp.concatenate(outs, axis=-1)
    y_ref[...] = x + jnp.dot(o, wo_ref[...], preferred_element_type=F32)


def _xattn_prompt(x, g, w_q, mem_k, mem_v, w_o, layer, batch, seq):
    tt = XA_ROW_TILE
    row_spec = pl.BlockSpec((None, tt, D_MODEL), lambda b, t: (b, t, 0))
    kv_spec = pl.BlockSpec((None, N_MEM, D_MODEL), lambda b, t: (layer, b, 0))
    w_spec = _resident((None, D_MODEL, D_MODEL), lambda b, t: (layer, 0, 0))
    y = pl.pallas_call(
        _xattn_prompt_kernel,
        grid=(batch, seq // tt),
        in_specs=[row_spec, pl.BlockSpec((None, 1, D_MODEL), lambda b, t: (layer, 0, 0)),
                  w_spec, kv_spec, kv_spec, w_spec],
        out_specs=row_spec,
        out_shape=jax.ShapeDtypeStruct((batch, seq, D_MODEL), F32),
        scratch_shapes=[pltpu.VMEM((N_MEM, D_MODEL), BF16), pltpu.VMEM((N_MEM, D_MODEL), BF16)],
        compiler_params=_params("parallel", "arbitrary"),
        name="xattn_prompt",
    )(x.reshape(batch, seq, D_MODEL), g.reshape(DEPTH, 1, D_MODEL), w_q, mem_k, mem_v, w_o)
    return y.reshape(batch * seq, D_MODEL)


def _xattn_sample_kernel(q_ref, k_ref, v_ref, o_ref, *, tb):
    scale = 1.0 / math.sqrt(MEM_HD)
    for b in range(tb):
        prod = k_ref[b] * (q_ref[b] * scale)
        v = v_ref[b]
        outs = []
        for hh in range(MEM_HEADS):
            sl = slice(hh * MEM_HD, (hh + 1) * MEM_HD)
            s = jnp.sum(prod[:, sl], axis=-1, keepdims=True)
            p = jnp.exp(s - jnp.max(s, axis=0, keepdims=True))
            p = p / jnp.sum(p, axis=0, keepdims=True)
            outs.append(jnp.sum(p * v[:, sl], axis=0, keepdims=True))
        o_ref[b] = jnp.concatenate(outs, axis=-1)


def _xattn_sample(q, cache_k, cache_v, layer):
    nb = q.shape[0]
    tb = XA_SAMPLE_TOKENS_PER_STEP
    kv_spec = pl.BlockSpec((None, tb, N_MEM, D_MODEL), lambda i: (layer, i, 0, 0))
    q_spec = pl.BlockSpec((tb, 1, D_MODEL), lambda i: (i, 0, 0))
    return pl.pallas_call(
        functools.partial(_xattn_sample_kernel, tb=tb),
        grid=(nb // tb,),
        in_specs=[q_spec, kv_spec, kv_spec],
        out_specs=q_spec,
        out_shape=jax.ShapeDtypeStruct((nb, 1, D_MODEL), F32),
        compiler_params=_params("parallel"),
        name="xattn_sample",
    )(q.reshape(nb, 1, D_MODEL), cache_k, cache_v).reshape(nb, D_MODEL)


def _trunk(x, batch, seq, s5_re, s5_im, hg_state, mem_k, mem_v, w):
    prompt = seq > 1
    new_re, new_im, new_hg = [], [], []
    for i in range(DEPTH):
        j = i // 2
        if i % 2 == 0:
            x, u = _ffn(x, w["ffn1_norm"], w["ffn1_w_in"], w["ffn1_w_out"], i, post="emit",
                        g2=w["mix_norm"][i], u_dtype=BF16 if prompt else F32)
            x, hr, hi = _s5_mixer(x, u, s5_re[j], s5_im[j], w["s5"], w["s5_w_glu"], j, batch, seq)
            new_re.append(hr)
            new_im.append(hi)
        else:
            x = _ffn(x, w["ffn1_norm"], w["ffn1_w_in"], w["ffn1_w_out"], i)
            if prompt:
                x, sn = _hgrn_prompt(x, w["mix_norm"], w["hg_w_in"], w["hg_lb_logits"], w["hg_norm"],
                                     w["hg_w_out"], i, j, batch, seq)
            else:
                proj = _proj(x, w["hg_w_in"], j, g=w["mix_norm"], g_layer=i)
                sn, o = _hgrn_sample(proj, w["hg_lb_logits"], w["hg_norm"], hg_state[j], i, j)
                x = _proj(o, w["hg_w_out"], j, res=x)
            new_hg.append(sn)
        if prompt:
            x = _xattn_prompt(x, w["xattn_norm"], w["xattn_w_q"], mem_k, mem_v, w["xattn_w_o"], i, batch, seq)
        else:
            q = _proj(x, w["xattn_w_q"], i, g=w["xattn_norm"], g_layer=i)
            o = _xattn_sample(q, mem_k, mem_v, i)
            x = _proj(o, w["xattn_w_o"], i, res=x)
        last = i == DEPTH - 1
        x = _ffn(x, w["ffn2_norm"], w["ffn2_w_in"], w["ffn2_w_out"], i,
                 post="replace" if last else "none", g2=w["final_norm"] if last else None)
    return x, jnp.stack(new_re), jnp.stack(new_im), jnp.stack(new_hg)


def kernel(x_prompt, x_sample, mem_prompt, state_s5_re, state_s5_im, state_hgrn, cache_mem_k, cache_mem_v, ffn1_norm, ffn1_w_in, ffn1_w_out, mix_norm, xattn_norm, mem_norm, xattn_w_q, xattn_w_kv, xattn_w_o, ffn2_norm, ffn2_w_in, ffn2_w_out, s5_a_re, s5_a_im, s5_log_dt, s5_b_re, s5_b_im, s5_c_re, s5_c_im, s5_d, s5_w_glu, hg_w_in, hg_lb_logits, hg_norm, hg_w_out, final_norm):
    bp, seq, _ = x_prompt.shape
    bs = x_sample.shape[0]
    bf = lambda a: a.astype(BF16)
    w = dict(ffn1_norm=ffn1_norm, ffn1_w_in=bf(ffn1_w_in), ffn1_w_out=bf(ffn1_w_out), mix_norm=mix_norm,
             xattn_norm=xattn_norm, xattn_w_q=bf(xattn_w_q), xattn_w_o=bf(xattn_w_o), ffn2_norm=ffn2_norm,
             ffn2_w_in=bf(ffn2_w_in), ffn2_w_out=bf(ffn2_w_out),
             s5=(s5_a_re, s5_a_im, s5_log_dt, s5_b_re, s5_b_im, s5_c_re, s5_c_im, s5_d),
             s5_w_glu=bf(s5_w_glu), hg_w_in=bf(hg_w_in), hg_lb_logits=hg_lb_logits, hg_norm=hg_norm,
             hg_w_out=bf(hg_w_out), final_norm=final_norm)

    mem_k, mem_v = _mem_kv(mem_prompt.reshape(bp * N_MEM, D_MODEL), mem_norm, bf(xattn_w_kv))

    n_s5, n_hg = state_s5_re.shape[0], state_hgrn.shape[0]
    z_s5 = jnp.zeros((n_s5, bp, S5_GROUPS, S5_STATE), F32)
    y_p, re_p, im_p, hg_p = _trunk(x_prompt.reshape(bp * seq, D_MODEL), bp, seq, z_s5, z_s5, None,
                                   mem_k, mem_v, w)
    y_s, re_s, im_s, hg_s = _trunk(x_sample.reshape(bs, D_MODEL), bs, 1, state_s5_re, state_s5_im, state_hgrn,
                                   cache_mem_k.reshape(DEPTH, bs, N_MEM, D_MODEL),
                                   cache_mem_v.reshape(DEPTH, bs, N_MEM, D_MODEL), w)
    kv_shape = (DEPTH, bp, N_MEM, MEM_HEADS, MEM_HD)
    return (y_p.reshape(bp, seq, D_MODEL), y_s.reshape(bs, 1, D_MODEL), re_p, im_p, re_s, im_s, hg_p, hg_s,
            mem_k.reshape(kv_shape), mem_v.reshape(kv_shape))
```

```python
import functools
import math

import jax
import jax.numpy as jnp
from jax import lax
from jax.experimental import pallas as pl
from jax.experimental.pallas import tpu as pltpu

F32 = jnp.float32
BF16 = jnp.bfloat16

D_MODEL = 1024
DEPTH = 2
S5_GROUP = 16
S5_GROUPS = D_MODEL // S5_GROUP
S5_STATE = 64
S5_CHUNK = 16
S5_BLOCK_GROUPS = 8
S5_TIME_SLICES = 2
HG_DK = 128
HG_HEADS = D_MODEL // HG_DK
HG_DV = D_MODEL // HG_HEADS
HG_WIDTH = HG_HEADS * HG_DK
HG_CHUNK = 64
N_MEM = 256
MEM_HEADS = 4
MEM_HD = D_MODEL // MEM_HEADS
FFN_DIM = 2816
EPS = 1e-6

V7X_VMEM_LIMIT_BYTES = 56 * 1024 * 1024

ROW_TILE = 512
FFN_CHUNK = 1408
HG_ROW_TILE = 256
XA_ROW_TILE = 512
S5_GROUPS_PER_STEP = 4
SAMPLE_TOKENS_PER_STEP = 8
XA_SAMPLE_TOKENS_PER_STEP = 4


def _params(*semantics):
    return pltpu.CompilerParams(dimension_semantics=semantics,
                                vmem_limit_bytes=V7X_VMEM_LIMIT_BYTES)


def _resident(shape, index_map):
    return pl.BlockSpec(shape, index_map, pipeline_mode=pl.Buffered(1))


def _rms(x, g):
    ms = jnp.mean(x * x, axis=-1, keepdims=True)
    return x * lax.rsqrt(ms + EPS) * g


def _sigmoid(x):
    return 1.0 / (1.0 + jnp.exp(-x))


def _silu(x):
    return x * _sigmoid(x)


def _gelu_tanh(x):
    c = math.sqrt(2.0 / math.pi)
    return 0.5 * x * (1.0 + jnp.tanh(c * (x + 0.044715 * (x * x * x))))


def _split3(x):
    hi = x.astype(BF16)
    r1 = x - hi.astype(F32)
    mid = r1.astype(BF16)
    lo = (r1 - mid.astype(F32)).astype(BF16)
    return hi, mid, lo


def _ffn_kernel(x_ref, g_ref, win_ref, wout_ref, *rest, post):
    x = x_ref[...]
    h = _rms(x, g_ref[...]).astype(BF16)
    acc = jnp.zeros_like(x)
    for c in range(FFN_DIM // FFN_CHUNK):
        lo = c * FFN_CHUNK
        gate = jnp.dot(h, win_ref[:, lo:lo + FFN_CHUNK], preferred_element_type=F32)
        up = jnp.dot(h, win_ref[:, FFN_DIM + lo:FFN_DIM + lo + FFN_CHUNK], preferred_element_type=F32)
        act = (_silu(gate) * up).astype(BF16)
        acc = acc + jnp.dot(act, wout_ref[lo:lo + FFN_CHUNK, :], preferred_element_type=F32)
    y = x + 0.5 * acc
    if post == "none":
        (y_ref,) = rest
        y_ref[...] = y
    elif post == "replace":
        g2_ref, y_ref = rest
        y_ref[...] = _rms(y, g2_ref[...])
    else:
        g2_ref, y_ref, u_ref = rest
        y_ref[...] = y
        u_ref[...] = _rms(y, g2_ref[...]).astype(u_ref.dtype)


def _ffn(x, g, w_in, w_out, layer, post="none", g2=None, u_dtype=BF16):
    rows = x.shape[0]
    tm = min(ROW_TILE, rows)
    row_spec = pl.BlockSpec((tm, D_MODEL), lambda i: (i, 0))
    vec_spec = pl.BlockSpec((1, D_MODEL), lambda i: (0, 0))
    in_specs = [row_spec,
                pl.BlockSpec((None, 1, D_MODEL), lambda i: (layer, 0, 0)),
                _resident((None, D_MODEL, 2 * FFN_DIM), lambda i: (layer, 0, 0)),
                _resident((None, FFN_DIM, D_MODEL), lambda i: (layer, 0, 0))]
    args = [x, g.reshape(DEPTH, 1, D_MODEL), w_in, w_out]
    out_shape = jax.ShapeDtypeStruct((rows, D_MODEL), F32)
    out_specs = row_spec
    if post != "none":
        in_specs.append(vec_spec)
        args.append(g2.reshape(1, D_MODEL))
    if post == "emit":
        out_shape = (out_shape, jax.ShapeDtypeStruct((rows, D_MODEL), u_dtype))
        out_specs = (row_spec, row_spec)
    return pl.pallas_call(
        functools.partial(_ffn_kernel, post=post),
        grid=(rows // tm,),
        in_specs=in_specs, out_specs=out_specs, out_shape=out_shape,
        compiler_params=_params("parallel"),
        name="ffn",
    )(*args)


def _proj_kernel(*refs, norm, glu, residual):
    refs = list(refs)
    x_ref = refs.pop(0)
    g_ref = refs.pop(0) if norm else None
    w_ref = refs.pop(0)
    res_ref = refs.pop(0) if residual else None
    (o_ref,) = refs
    x = x_ref[...]
    if norm:
        x = _rms(x, g_ref[...])
    y = jnp.dot(x.astype(BF16), w_ref[...], preferred_element_type=F32)
    if glu:
        half = y.shape[-1] // 2
        y = y[:, :half] * _sigmoid(y[:, half:])
    if residual:
        y = y + res_ref[...]
    o_ref[...] = y


def _proj(x, w, layer, g=None, g_layer=0, glu=False, res=None):
    rows, kdim = x.shape
    ndim = w.shape[-1]
    nout = ndim // 2 if glu else ndim
    tm = min(ROW_TILE, rows)
    in_specs = [pl.BlockSpec((tm, kdim), lambda i: (i, 0))]
    args = [x]
    if g is not None:
        in_specs.append(pl.BlockSpec((None, 1, kdim), lambda i: (g_layer, 0, 0)))
        args.append(g.reshape(g.shape[0], 1, kdim))
    in_specs.append(_resident((None, kdim, ndim), lambda i: (layer, 0, 0)))
    args.append(w)
    if res is not None:
        in_specs.append(pl.BlockSpec((tm, nout), lambda i: (i, 0)))
        args.append(res)
    return pl.pallas_call(
        functools.partial(_proj_kernel, norm=g is not None, glu=glu, residual=res is not None),
        grid=(rows // tm,),
        in_specs=in_specs,
        out_specs=pl.BlockSpec((tm, nout), lambda i: (i, 0)),
        out_shape=jax.ShapeDtypeStruct((rows, nout), F32),
        compiler_params=_params("parallel"),
        name="proj",
    )(*args)


def _memkv_kernel(x_ref, g_ref, w_ref, k_ref, v_ref):
    h = _rms(x_ref[...], g_ref[...]).astype(BF16)
    y = jnp.dot(h, w_ref[...], preferred_element_type=F32)
    k_ref[...] = y[:, :D_MODEL]
    v_ref[...] = y[:, D_MODEL:]


def _mem_kv(mem, g, w_kv):
    rows = mem.shape[0]
    tm = min(ROW_TILE, rows)
    out = jax.ShapeDtypeStruct((DEPTH, rows, D_MODEL), F32)
    out_spec = pl.BlockSpec((None, tm, D_MODEL), lambda l, i: (l, i, 0))
    return pl.pallas_call(
        _memkv_kernel,
        grid=(DEPTH, rows // tm),
        in_specs=[pl.BlockSpec((tm, D_MODEL), lambda l, i: (i, 0)),
                  pl.BlockSpec((None, 1, D_MODEL), lambda l, i: (l, 0, 0)),
                  pl.BlockSpec((None, D_MODEL, 2 * D_MODEL), lambda l, i: (l, 0, 0))],
        out_specs=(out_spec, out_spec), out_shape=(out, out),
        compiler_params=_params("parallel", "parallel"),
        name="mem_kv",
    )(mem, g.reshape(DEPTH, 1, D_MODEL), w_kv)


def _s5_pieces(a_re, a_im, log_dt, b_re, b_im, c_re, c_im, d, steps):
    G, P, GC = S5_GROUPS, S5_STATE, S5_GROUP
    L = steps
    dt = jnp.exp(log_dt)[:, None]
    xr, xi = a_re * dt, a_im * dt
    j = jnp.arange(L + 1, dtype=F32)[:, None, None]
    mag = jnp.exp(xr[None] * j)
    pw_re, pw_im = mag * jnp.cos(xi[None] * j), mag * jnp.sin(xi[None] * j)
    nr, ni = pw_re[1] - 1.0, pw_im[1]
    den = a_re * a_re + a_im * a_im
    fr, fi = (nr * a_re + ni * a_im) / den, (ni * a_re - nr * a_im) / den
    bb_re = fr[..., None] * b_re - fi[..., None] * b_im
    bb_im = fr[..., None] * b_im + fi[..., None] * b_re
    hp = lax.Precision.HIGHEST
    w_re = pw_re[:L, :, None, :] * c_re[None] - pw_im[:L, :, None, :] * c_im[None]
    w_im = pw_re[:L, :, None, :] * c_im[None] + pw_im[:L, :, None, :] * c_re[None]
    kern = (jnp.einsum("gpi,lgop->lgio", bb_re, w_re, precision=hp)
            - jnp.einsum("gpi,lgop->lgio", bb_im, w_im, precision=hp))
    kern = kern.at[0].add(d[:, :, None] * jnp.eye(GC, dtype=F32)[None])
    rp_re, rp_im = pw_re[:L][::-1], pw_im[:L][::-1]
    m_re = rp_re[:, :, :, None] * bb_re[None] - rp_im[:, :, :, None] * bb_im[None]
    m_im = rp_re[:, :, :, None] * bb_im[None] + rp_im[:, :, :, None] * bb_re[None]
    m_re = m_re.transpose(1, 0, 3, 2)
    m_im = m_im.transpose(1, 0, 3, 2)
    q_re = pw_re[1:, :, None, :] * c_re[None] - pw_im[1:, :, None, :] * c_im[None]
    q_im = pw_re[1:, :, None, :] * c_im[None] + pw_im[1:, :, None, :] * c_re[None]
    p_re = q_re.transpose(1, 3, 0, 2)
    p_im = (-q_im).transpose(1, 3, 0, 2)
    return kern, m_re, m_im, p_re, p_im, pw_re[L], pw_im[L]


def _s5_group_mats(pieces):
    kern, m_re, m_im, p_re, p_im, l_re, l_im = pieces
    return (kern[0], m_re[:, 0], m_im[:, 0], p_re[:, :, 0], p_im[:, :, 0], l_re[:, None, :], l_im[:, None, :])


def _s5_block_mats(pieces):
    kern, m_re, m_im, p_re, p_im, l_re, l_im = lax.optimization_barrier(pieces)
    L = kern.shape[0]
    GB = S5_BLOCK_GROUPS
    NB = S5_GROUPS // GB
    GC, P = S5_GROUP, S5_STATE
    W = GB * GC
    eye = jnp.eye(GB, dtype=BF16)
    kb = kern.astype(BF16).reshape(L, NB, GB, GC, 1, GC) * eye[None, None, :, None, :, None]
    kb = kb.reshape(L, NB, W, W)
    rev = kb[::-1].transpose(1, 0, 2, 3).reshape(NB, L * W, W)
    zero = jnp.zeros((NB, W, W), BF16)
    w2 = jnp.concatenate([jnp.concatenate([rev, zero], axis=1), jnp.concatenate([zero, rev], axis=1)], axis=2)

    def m_block(m):
        m = m.astype(BF16).reshape(NB, GB, L, GC, P).transpose(0, 2, 1, 3, 4)
        m = m[:, :, :, :, None, :] * eye[None, None, :, None, :, None]
        return m.reshape(NB, L * W, GB * P)

    def p_block(p):
        p = p.astype(BF16).reshape(NB, GB, P, L, 1, GC) * eye[None, :, None, None, :, None]
        return p.reshape(NB, GB * P, L * W)

    mm = jnp.concatenate([m_block(m_re), m_block(m_im)], axis=2)
    pp = jnp.concatenate([p_block(p_re), p_block(p_im)], axis=1)
    lam = jnp.concatenate([l_re.reshape(NB, 1, GB * P), l_im.reshape(NB, 1, GB * P)], axis=2)
    return w2, mm, pp, lam


def _s5_kernel(u_ref, t_ref, mre_ref, mim_ref, pre_ref, pim_ref, lre_ref, lim_ref, h0re_ref, h0im_ref,
               y_ref, hre_ref, him_ref, inj_re, inj_im, hs_re, hs_im, *, n_chunks, rb, gps, precision):
    def mm(a, b):
        return jnp.dot(a, b, preferred_element_type=F32, precision=precision)

    for g in range(gps):
        u = u_ref[g]
        inj_re[g] = mm(u, mre_ref[g])
        inj_im[g] = mm(u, mim_ref[g])

    lam_re = [jnp.broadcast_to(lre_ref[g], (rb, S5_STATE)) for g in range(gps)]
    lam_im = [jnp.broadcast_to(lim_ref[g], (rb, S5_STATE)) for g in range(gps)]

    def step(k, carry):
        rows = pl.ds(pl.multiple_of(k * rb, rb), rb)
        nxt = []
        for g in range(gps):
            hr, hi = carry[2 * g], carry[2 * g + 1]
            hs_re[g, rows, :] = hr
            hs_im[g, rows, :] = hi
            nxt.append(lam_re[g] * hr - lam_im[g] * hi + inj_re[g, rows, :])
            nxt.append(lam_re[g] * hi + lam_im[g] * hr + inj_im[g, rows, :])
        return tuple(nxt)

    init = []
    for g in range(gps):
        init += [h0re_ref[g], h0im_ref[g]]
    fin = lax.fori_loop(0, n_chunks, step, tuple(init))

    for g in range(gps):
        hre_ref[g] = fin[2 * g]
        him_ref[g] = fin[2 * g + 1]
        dt = u_ref.dtype
        y = (mm(u_ref[g], t_ref[g]) + mm(hs_re[g].astype(dt), pre_ref[g])
             + mm(hs_im[g].astype(dt), pim_ref[g]))
        y_ref[g] = _gelu_tanh(y).astype(y_ref.dtype)


def _s5_core(u, mats, h0_re, h0_im, n_chunks, rb, precision, y_dtype):
    G, R, W = u.shape
    P = S5_STATE
    gps = S5_GROUPS_PER_STEP
    tmat, m_re, m_im, p_re, p_im, l_re, l_im = mats

    def spec(a, b):
        return pl.BlockSpec((gps, a, b), lambda i: (i, 0, 0))

    st = jax.ShapeDtypeStruct((G, rb, P), F32)
    scr = lambda: pltpu.VMEM((gps, R, P), F32)
    return pl.pallas_call(
        functools.partial(_s5_kernel, n_chunks=n_chunks, rb=rb, gps=gps, precision=precision),
        grid=(G // gps,),
        in_specs=[spec(R, W), spec(W, W), spec(W, P), spec(W, P), spec(P, W), spec(P, W),
                  spec(1, P), spec(1, P), spec(rb, P), spec(rb, P)],
        out_specs=(spec(R, W), spec(rb, P), spec(rb, P)),
        out_shape=(jax.ShapeDtypeStruct((G, R, W), y_dtype), st, st),
        scratch_shapes=[scr(), scr(), scr(), scr()],
        compiler_params=_params("parallel"),
        name="s5_core",
    )(u, tmat, m_re, m_im, p_re, p_im, l_re, l_im, h0_re, h0_im)


def _s5_seq_kernel(u_ref, w2_ref, m_ref, p_ref, lam_ref, h0_ref, y_ref, hfin_ref,
                   lhs_scr, inj_scr, hs_scr, h_scr, *, batch, nck):
    L, W = S5_CHUNK, S5_BLOCK_GROUPS * S5_GROUP
    NT = S5_BLOCK_GROUPS * S5_STATE // W

    @pl.when(pl.program_id(1) == 0)
    def _():
        h_scr[...] = h0_ref[...]

    for b in range(batch):
        for l in range(L):
            lhs_scr[b * nck:(b + 1) * nck, l * W:(l + 1) * W] = u_ref[b, pl.ds(l, nck, stride=L), :].astype(BF16)

    inj = jnp.dot(lhs_scr[...], m_ref[...], preferred_element_type=F32)
    for t in range(2 * NT):
        inj_scr[t] = inj[:, t * W:(t + 1) * W]

    lam = [jnp.broadcast_to(lam_ref[:, t * W:(t + 1) * W], (batch, W)) for t in range(2 * NT)]

    def step(k, h):
        rows = pl.ds(k, batch, stride=nck)
        nxt_re, nxt_im = [], []
        for t in range(NT):
            hr, hi = h[t], h[NT + t]
            hs_scr[t, rows, :] = hr
            hs_scr[NT + t, rows, :] = hi
            nxt_re.append(lam[t] * hr - lam[NT + t] * hi + inj_scr[t, rows, :])
            nxt_im.append(lam[t] * hi + lam[NT + t] * hr + inj_scr[NT + t, rows, :])
        return tuple(nxt_re + nxt_im)

    h = lax.fori_loop(0, nck, step, tuple(h_scr[:, t * W:(t + 1) * W] for t in range(2 * NT)))
    for t in range(2 * NT):
        h_scr[:, t * W:(t + 1) * W] = h[t]
    hfin_ref[...] = h_scr[...]

    hs = jnp.concatenate([hs_scr[t] for t in range(2 * NT)], axis=-1).astype(BF16)
    for pr in range(L // 2):
        kk = (2 * pr + 2) * W
        y = (jnp.dot(lhs_scr[:, :kk], w2_ref[(L - 1 - 2 * pr) * W:, :], preferred_element_type=F32)
             + jnp.dot(hs, p_ref[:, 2 * pr * W:(2 * pr + 2) * W], preferred_element_type=F32))
        y = _gelu_tanh(y)
        for s in range(2):
            for b in range(batch):
                y_ref[b, pl.ds(2 * pr + s, nck, stride=L), :] = y[b * nck:(b + 1) * nck, s * W:(s + 1) * W]


def _s5_seq(u, mats, h0, batch, seq):
    w2, mm, pp, lam = mats
    L, W = S5_CHUNK, S5_BLOCK_GROUPS * S5_GROUP
    NB = S5_GROUPS // S5_BLOCK_GROUPS
    SW = 2 * S5_BLOCK_GROUPS * S5_STATE
    ts = seq // S5_TIME_SLICES
    nck = ts // L
    rows = batch * nck
    u_spec = pl.BlockSpec((batch, ts, W), lambda i, t: (0, t, i))
    h_spec = pl.BlockSpec((None, batch, SW), lambda i, t: (i, 0, 0))

    def w_spec(a, b):
        return pl.BlockSpec((None, a, b), lambda i, t: (i, 0, 0))

    return pl.pallas_call(
        functools.partial(_s5_seq_kernel, batch=batch, nck=nck),
        grid=(NB, S5_TIME_SLICES),
        in_specs=[u_spec, w_spec((L + 1) * W, 2 * W), w_spec(L * W, SW), w_spec(SW, L * W), w_spec(1, SW), h_spec],
        out_specs=(u_spec, h_spec),
        out_shape=(jax.ShapeDtypeStruct((batch, seq, D_MODEL), F32), jax.ShapeDtypeStruct(h0.shape, F32)),
        scratch_shapes=[pltpu.VMEM((rows, L * W), BF16), pltpu.VMEM((SW // W, rows, W), F32),
                        pltpu.VMEM((SW // W, rows, W), F32), pltpu.VMEM((batch, SW), F32)],
        compiler_params=_params("parallel", "arbitrary"),
        name="s5_seq",
    )(u, w2, mm, pp, lam, h0)


def _s5_mixer(x, u, h0_re, h0_im, s5p, w_glu, j, batch, seq):
    G, GC, P = S5_GROUPS, S5_GROUP, S5_STATE
    if seq > 1:
        NB, HS = G // S5_BLOCK_GROUPS, S5_BLOCK_GROUPS * P
        mats = _s5_block_mats(_s5_pieces(*[p[j] for p in s5p], S5_CHUNK))
        to_blocks = lambda h: h.reshape(batch, NB, HS).transpose(1, 0, 2)
        h0 = jnp.concatenate([to_blocks(h0_re), to_blocks(h0_im)], axis=-1)
        y, hfin = _s5_seq(u.reshape(batch, seq, D_MODEL), mats, h0, batch, seq)
        y = y.reshape(batch * seq, D_MODEL)
        from_blocks = lambda h: h.transpose(1, 0, 2).reshape(batch, G, P)
        hre, him = from_blocks(hfin[..., :HS]), from_blocks(hfin[..., HS:])
    else:
        mats = _s5_group_mats(_s5_pieces(*[p[j] for p in s5p], 1))
        ug = u.reshape(batch, G, GC).transpose(1, 0, 2)
        yg, hre, him = _s5_core(ug, mats, h0_re.transpose(1, 0, 2), h0_im.transpose(1, 0, 2),
                                1, batch, lax.Precision.HIGHEST, F32)
        y = yg.transpose(1, 0, 2).reshape(batch, D_MODEL)
        hre, him = hre.transpose(1, 0, 2), him.transpose(1, 0, 2)
    x = _proj(y, w_glu, j, glu=True, res=x)
    return x, hre, him


def _hg_lower_bound(logits, layer):
    m = jnp.max(logits, axis=0, keepdims=True)
    e = jnp.exp(logits - m)
    sm = e / jnp.sum(e, axis=0, keepdims=True)
    return jnp.sum(sm[:layer + 1], axis=0, keepdims=True) - sm[0:1]


def _hg_gates(z, lb):
    e = jnp.exp(-jnp.abs(z))
    r = 1.0 / (1.0 + e)
    log_sig = jnp.minimum(z, 0.0) - jnp.log1p(e)
    a = jnp.log(lb)
    b = jnp.log1p(-lb) + log_sig
    logf = jnp.maximum(a, b) + jnp.log1p(jnp.exp(-jnp.abs(a - b)))
    k = (1.0 - lb) * jnp.where(z >= 0.0, e * r, r)
    return logf, k


def _hgrn_prompt_kernel(x_ref, g_ref, win_ref, lbl_ref, ng_ref, wout_ref, y_ref, sfin_ref,
                        s_scr, q_scr, lf_scr, k_scr, v_scr, gt_scr, o_scr, *, layer, tt):
    t = pl.program_id(1)
    C = HG_CHUNK

    @pl.when(t == 0)
    def _():
        s_scr[...] = jnp.zeros_like(s_scr)

    x = x_ref[...]
    h = _rms(x, g_ref[...]).astype(BF16)
    proj = jnp.dot(h, win_ref[...], preferred_element_type=F32)
    lb = _hg_lower_bound(lbl_ref[...], layer)
    logf, kk = _hg_gates(proj[:, HG_WIDTH:2 * HG_WIDTH], lb)
    q_scr[...] = _silu(proj[:, :HG_WIDTH])
    lf_scr[...] = logf
    k_scr[...] = kk
    v_scr[...] = proj[:, 2 * HG_WIDTH:3 * HG_WIDTH].astype(BF16)
    gt_scr[...] = _silu(proj[:, 3 * HG_WIDTH:])

    row = lax.broadcasted_iota(jnp.int32, (C, C), 0)
    col = lax.broadcasted_iota(jnp.int32, (C, C), 1)
    causal = row >= col
    tri = causal.astype(BF16)
    ones = jnp.ones((C, HG_DV), BF16)
    ng = ng_ref[...]

    def chunk_step(c, carry):
        rows = pl.ds(pl.multiple_of(c * C, C), C)
        lf = lf_scr[rows, :]
        parts = _split3(lf)
        beta = sum(jnp.dot(tri, p, preferred_element_type=F32) for p in parts)
        btot_col = sum(lax.dot_general(p, ones, (((0,), (0,)), ((), ())), preferred_element_type=F32)
                       for p in parts)
        btot = beta[C - 1:C, :]
        mid = beta[C // 2 - 1:C // 2, :]
        q = q_scr[rows, :]
        k = k_scr[rows, :]
        q_dec = (q * jnp.exp(beta)).astype(BF16)
        q_mid = (q * jnp.exp(beta - mid)).astype(BF16)
        k_mid = (k * jnp.exp(mid - beta)).astype(BF16)
        k_dec = (k * jnp.exp(btot - beta)).astype(BF16)
        v = v_scr[rows, :]
        gt = gt_scr[rows, :]
        for hh in range(HG_HEADS):
            sk = slice(hh * HG_DK, (hh + 1) * HG_DK)
            sv = slice(hh * HG_DV, (hh + 1) * HG_DV)
            s = s_scr[hh]
            o = jnp.dot(q_dec[:, sk], s.astype(BF16), preferred_element_type=F32)
            att = lax.dot_general(q_mid[:, sk], k_mid[:, sk], (((1,), (1,)), ((), ())),
                                  preferred_element_type=F32)
            att = jnp.where(causal, att, 0.0).astype(BF16)
            o = o + jnp.dot(att, v[:, sv], preferred_element_type=F32)
            kv = lax.dot_general(k_dec[:, sk], v[:, sv], (((0,), (0,)), ((), ())),
                                 preferred_element_type=F32)
            s_scr[hh] = jnp.exp(btot_col[sk, :]) * s + kv
            o_scr[rows, sv] = (_rms(o, ng) * gt[:, sv]).astype(BF16)
        return carry

    lax.fori_loop(0, tt // C, chunk_step, 0)
    y_ref[...] = x + jnp.dot(o_scr[...], wout_ref[...], preferred_element_type=F32)

    @pl.when(t == pl.num_programs(1) - 1)
    def _():
        sfin_ref[...] = s_scr[...]


def _hgrn_prompt(x, g, w_in, lb_logits, norm_g, w_out, layer, j, batch, seq):
    tt = HG_ROW_TILE
    nt = seq // tt
    x3 = x.reshape(batch, seq, D_MODEL)
    row_spec = pl.BlockSpec((None, tt, D_MODEL), lambda b, t: (b, t, 0))
    y, s_fin = pl.pallas_call(
        functools.partial(_hgrn_prompt_kernel, layer=layer, tt=tt),
        grid=(batch, nt),
        in_specs=[row_spec,
                  pl.BlockSpec((None, 1, D_MODEL), lambda b, t: (layer, 0, 0)),
                  _resident((None, D_MODEL, 4 * HG_WIDTH), lambda b, t: (j, 0, 0)),
                  pl.BlockSpec((DEPTH, HG_WIDTH), lambda b, t: (0, 0)),
                  pl.BlockSpec((None, 1, HG_DV), lambda b, t: (j, 0, 0)),
                  _resident((None, HG_WIDTH, D_MODEL), lambda b, t: (j, 0, 0))],
        out_specs=(row_spec,
                   pl.BlockSpec((None, HG_HEADS, HG_DK, HG_DV), lambda b, t: (b, 0, 0, 0))),
        out_shape=(jax.ShapeDtypeStruct((batch, seq, D_MODEL), F32),
                   jax.ShapeDtypeStruct((batch, HG_HEADS, HG_DK, HG_DV), F32)),
        scratch_shapes=[pltpu.VMEM((HG_HEADS, HG_DK, HG_DV), F32),
                        pltpu.VMEM((tt, HG_WIDTH), F32), pltpu.VMEM((tt, HG_WIDTH), F32),
                        pltpu.VMEM((tt, HG_WIDTH), F32), pltpu.VMEM((tt, HG_WIDTH), BF16),
                        pltpu.VMEM((tt, HG_WIDTH), F32), pltpu.VMEM((tt, HG_WIDTH), BF16)],
        compiler_params=_params("parallel", "arbitrary"),
        name="hgrn_prompt",
    )(x3, g.reshape(DEPTH, 1, D_MODEL), w_in, lb_logits, norm_g.reshape(-1, 1, HG_DV), w_out)
    return y.reshape(batch * seq, D_MODEL), s_fin


def _hgrn_sample_kernel(proj_ref, lbl_ref, ng_ref, s_ref, snew_ref, o_ref, *, layer, tb):
    proj = proj_ref[...]
    lb = _hg_lower_bound(lbl_ref[...], layer)
    z = proj[:, HG_WIDTH:2 * HG_WIDTH]
    e = jnp.exp(-jnp.abs(z))
    r = 1.0 / (1.0 + e)
    sig = jnp.where(z >= 0.0, r, e * r)
    f = lb + (1.0 - lb) * sig
    k = (1.0 - lb) * jnp.where(z >= 0.0, e * r, r)
    q = _silu(proj[:, :HG_WIDTH])
    v = proj[:, 2 * HG_WIDTH:3 * HG_WIDTH]
    gt = _silu(proj[:, 3 * HG_WIDTH:])
    ng = ng_ref[...]
    for hh in range(HG_HEADS):
        sk = slice(hh * HG_DK, (hh + 1) * HG_DK)
        sv = slice(hh * HG_DV, (hh + 1) * HG_DV)
        f_t = f[:, sk].T
        k_t = k[:, sk].T
        q_t = q[:, sk].T
        for b in range(tb):
            s_new = f_t[:, b:b + 1] * s_ref[b, hh] + k_t[:, b:b + 1] * v[b:b + 1, sv]
            snew_ref[b, hh] = s_new
            o = jnp.sum(q_t[:, b:b + 1] * s_new, axis=0, keepdims=True)
            o_ref[b:b + 1, sv] = _rms(o, ng) * gt[b:b + 1, sv]


def _hgrn_sample(proj, lb_logits, norm_g, state, layer, j):
    nb = proj.shape[0]
    tb = SAMPLE_TOKENS_PER_STEP
    st_spec = pl.BlockSpec((tb, HG_HEADS, HG_DK, HG_DV), lambda i: (i, 0, 0, 0))
    return pl.pallas_call(
        functools.partial(_hgrn_sample_kernel, layer=layer, tb=tb),
        grid=(nb // tb,),
        in_specs=[pl.BlockSpec((tb, 4 * HG_WIDTH), lambda i: (i, 0)),
                  pl.BlockSpec((DEPTH, HG_WIDTH), lambda i: (0, 0)),
                  pl.BlockSpec((None, 1, HG_DV), lambda i: (j, 0, 0)),
                  st_spec],
        out_specs=(st_spec, pl.BlockSpec((tb, HG_WIDTH), lambda i: (i, 0))),
        out_shape=(jax.ShapeDtypeStruct(state.shape, F32), jax.ShapeDtypeStruct((nb, HG_WIDTH), F32)),
        compiler_params=_params("parallel"),
        name="hgrn_sample",
    )(proj, lb_logits, norm_g.reshape(-1, 1, HG_DV), state)


def _xattn_prompt_kernel(x_ref, g_ref, wq_ref, k_ref, v_ref, wo_ref, y_ref, kb_scr, vb_scr):
    @pl.when(pl.program_id(1) == 0)
    def _():
        kb_scr[...] = k_ref[...].astype(BF16)
        vb_scr[...] = v_ref[...].astype(BF16)

    x = x_ref[...]
    h = _rms(x, g_ref[...]).astype(BF16)
    q = jnp.dot(h, wq_ref[...], preferred_element_type=F32) * (1.0 / math.sqrt(MEM_HD))
    q = q.astype(BF16)
    outs = []
    for hh in range(MEM_HEADS):
        sl = slice(hh * MEM_HD, (hh + 1) * MEM_HD)
        s = lax.dot_general(q[:, sl], kb_scr[:, sl], (((1,), (1,)), ((), ())), preferred_element_type=F32)
        p = jnp.exp(s - jnp.max(s, axis=-1, keepdims=True))
        den = jnp.sum(p, axis=-1, keepdims=True)
        o = jnp.dot(p.astype(BF16), vb_scr[:, sl], preferred_element_type=F32)
        outs.append((o / den).astype(BF16))
    o = jnp.concatenate(outs, axis=-1)
    y_ref[...] = x + jnp.dot(o, wo_ref[...], preferred_element_type=F32)


def _xattn_prompt(x, g, w_q, mem_k, mem_v, w_o, layer, batch, seq):
    tt = XA_ROW_TILE
    row_spec = pl.BlockSpec((None, tt, D_MODEL), lambda b, t: (b, t, 0))
    kv_spec = pl.BlockSpec((None, N_MEM, D_MODEL), lambda b, t: (layer, b, 0))
    w_spec = _resident((None, D_MODEL, D_MODEL), lambda b, t: (layer, 0, 0))
    y = pl.pallas_call(
        _xattn_prompt_kernel,
        grid=(batch, seq // tt),
        in_specs=[row_spec, pl.BlockSpec((None, 1, D_MODEL), lambda b, t: (layer, 0, 0)),
                  w_spec, kv_spec, kv_spec, w_spec],
        out_specs=row_spec,
        out_shape=jax.ShapeDtypeStruct((batch, seq, D_MODEL), F32),
        scratch_shapes=[pltpu.VMEM((N_MEM, D_MODEL), BF16), pltpu.VMEM((N_MEM, D_MODEL), BF16)],
        compiler_params=_params("parallel", "arbitrary"),
        name="xattn_prompt",
    )(x.reshape(batch, seq, D_MODEL), g.reshape(DEPTH, 1, D_MODEL), w_q, mem_k, mem_v, w_o)
    return y.reshape(batch * seq, D_MODEL)


def _xattn_sample_kernel(q_ref, k_ref, v_ref, o_ref, *, tb):
    scale = 1.0 / math.sqrt(MEM_HD)
    for b in range(tb):
        q = q_ref[b] * scale
        s = jnp.sum(k_ref[b] * q[None], axis=-1, keepdims=True)
        p = jnp.exp(s - jnp.max(s, axis=0, keepdims=True))
        p = p / jnp.sum(p, axis=0, keepdims=True)
        o_ref[b] = jnp.sum(p * v_ref[b], axis=0)


def _xattn_sample(q, cache_k, cache_v, layer):
    nb = q.shape[0]
    tb = XA_SAMPLE_TOKENS_PER_STEP
    kv_spec = pl.BlockSpec((None, tb, N_MEM, MEM_HEADS, MEM_HD), lambda i: (layer, i, 0, 0, 0))
    q_spec = pl.BlockSpec((tb, MEM_HEADS, MEM_HD), lambda i: (i, 0, 0))
    return pl.pallas_call(
        functools.partial(_xattn_sample_kernel, tb=tb),
        grid=(nb // tb,),
        in_specs=[q_spec, kv_spec, kv_spec],
        out_specs=q_spec,
        out_shape=jax.ShapeDtypeStruct((nb, MEM_HEADS, MEM_HD), F32),
        compiler_params=_params("parallel"),
        name="xattn_sample",
    )(q.reshape(nb, MEM_HEADS, MEM_HD), cache_k, cache_v).reshape(nb, D_MODEL)


def _trunk(x, batch, seq, s5_re, s5_im, hg_state, mem_k, mem_v, w):
    prompt = seq > 1
    new_re, new_im, new_hg = [], [], []
    for i in range(DEPTH):
        j = i // 2
        if i % 2 == 0:
            x, u = _ffn(x, w["ffn1_norm"], w["ffn1_w_in"], w["ffn1_w_out"], i, post="emit",
                        g2=w["mix_norm"][i], u_dtype=F32)
            x, hr, hi = _s5_mixer(x, u, s5_re[j], s5_im[j], w["s5"], w["s5_w_glu"], j, batch, seq)
            new_re.append(hr)
            new_im.append(hi)
        else:
            x = _ffn(x, w["ffn1_norm"], w["ffn1_w_in"], w["ffn1_w_out"], i)
            if prompt:
                x, sn = _hgrn_prompt(x, w["mix_norm"], w["hg_w_in"], w["hg_lb_logits"], w["hg_norm"],
                                     w["hg_w_out"], i, j, batch, seq)
            else:
                proj = _proj(x, w["hg_w_in"], j, g=w["mix_norm"], g_layer=i)
                sn, o = _hgrn_sample(proj, w["hg_lb_logits"], w["hg_norm"], hg_state[j], i, j)
                x = _proj(o, w["hg_w_out"], j, res=x)
            new_hg.append(sn)
        if prompt:
            x = _xattn_prompt(x, w["xattn_norm"], w["xattn_w_q"], mem_k, mem_v, w["xattn_w_o"], i, batch, seq)
        else:
            q = _proj(x, w["xattn_w_q"], i, g=w["xattn_norm"], g_layer=i)
            o = _xattn_sample(q, mem_k, mem_v, i)
            x = _proj(o, w["xattn_w_o"], i, res=x)
        last = i == DEPTH - 1
        x = _ffn(x, w["ffn2_norm"], w["ffn2_w_in"], w["ffn2_w_out"], i,
                 post="replace" if last else "none", g2=w["final_norm"] if last else None)
    return x, jnp.stack(new_re), jnp.stack(new_im), jnp.stack(new_hg)


def kernel(x_prompt, x_sample, mem_prompt, state_s5_re, state_s5_im, state_hgrn, cache_mem_k, cache_mem_v, ffn1_norm, ffn1_w_in, ffn1_w_out, mix_norm, xattn_norm, mem_norm, xattn_w_q, xattn_w_kv, xattn_w_o, ffn2_norm, ffn2_w_in, ffn2_w_out, s5_a_re, s5_a_im, s5_log_dt, s5_b_re, s5_b_im, s5_c_re, s5_c_im, s5_d, s5_w_glu, hg_w_in, hg_lb_logits, hg_norm, hg_w_out, final_norm):
    bp, seq, _ = x_prompt.shape
    bs = x_sample.shape[0]
    bf = lambda a: a.astype(BF16)
    w = dict(ffn1_norm=ffn1_norm, ffn1_w_in=bf(ffn1_w_in), ffn1_w_out=bf(ffn1_w_out), mix_norm=mix_norm,
             xattn_norm=xattn_norm, xattn_w_q=bf(xattn_w_q), xattn_w_o=bf(xattn_w_o), ffn2_norm=ffn2_norm,
             ffn2_w_in=bf(ffn2_w_in), ffn2_w_out=bf(ffn2_w_out),
             s5=(s5_a_re, s5_a_im, s5_log_dt, s5_b_re, s5_b_im, s5_c_re, s5_c_im, s5_d),
             s5_w_glu=bf(s5_w_glu), hg_w_in=bf(hg_w_in), hg_lb_logits=hg_lb_logits, hg_norm=hg_norm,
             hg_w_out=bf(hg_w_out), final_norm=final_norm)

    mem_k, mem_v = _mem_kv(mem_prompt.reshape(bp * N_MEM, D_MODEL), mem_norm, bf(xattn_w_kv))

    n_s5, n_hg = state_s5_re.shape[0], state_hgrn.shape[0]
    z_s5 = jnp.zeros((n_s5, bp, S5_GROUPS, S5_STATE), F32)
    y_p, re_p, im_p, hg_p = _trunk(x_prompt.reshape(bp * seq, D_MODEL), bp, seq, z_s5, z_s5, None,
                                   mem_k, mem_v, w)
    y_s, re_s, im_s, hg_s = _trunk(x_sample.reshape(bs, D_MODEL), bs, 1, state_s5_re, state_s5_im, state_hgrn,
                                   cache_mem_k, cache_mem_v, w)
    kv_shape = (DEPTH, bp, N_MEM, MEM_HEADS, MEM_HD)
    return (y_p.reshape(bp, seq, D_MODEL), y_s.reshape(bs, 1, D_MODEL), re_p, im_p, re_s, im_s, hg_p, hg_s,
            mem_k.reshape(kv_shape), mem_v.reshape(kv_shape))
```

```python
import functools
import math

import jax
import jax.numpy as jnp
from jax import lax
from jax.experimental import pallas as pl
from jax.experimental.pallas import tpu as pltpu

F32 = jnp.float32
BF16 = jnp.bfloat16

D_MODEL = 1024
DEPTH = 2
S5_GROUP = 16
S5_GROUPS = D_MODEL // S5_GROUP
S5_STATE = 64
S5_CHUNK = 16
S5_BLOCK_GROUPS = 8
S5_TIME_SLICES = 2
HG_DK = 128
HG_HEADS = D_MODEL // HG_DK
HG_DV = D_MODEL // HG_HEADS
HG_WIDTH = HG_HEADS * HG_DK
HG_CHUNK = 128
HG_FACTORED_MAX_DECAY = 60.0
N_MEM = 256
MEM_HEADS = 4
MEM_HD = D_MODEL // MEM_HEADS
FFN_DIM = 2816
EPS = 1e-6

V7X_VMEM_LIMIT_BYTES = 56 * 1024 * 1024

ROW_TILE = 512
FFN_CHUNK = 1408
HG_ROW_TILE = 256
XA_ROW_TILE = 512
S5_GROUPS_PER_STEP = 4
SAMPLE_TOKENS_PER_STEP = 8
XA_SAMPLE_TOKENS_PER_STEP = 4


def _params(*semantics):
    return pltpu.CompilerParams(dimension_semantics=semantics,
                                vmem_limit_bytes=V7X_VMEM_LIMIT_BYTES)


def _resident(shape, index_map):
    return pl.BlockSpec(shape, index_map, pipeline_mode=pl.Buffered(1))


def _rms(x, g):
    ms = jnp.mean(x * x, axis=-1, keepdims=True)
    return x * lax.rsqrt(ms + EPS) * g


def _sigmoid(x):
    return 1.0 / (1.0 + jnp.exp(-x))


def _silu(x):
    return x * _sigmoid(x)


def _gelu_tanh(x):
    c = math.sqrt(2.0 / math.pi)
    return 0.5 * x * (1.0 + jnp.tanh(c * (x + 0.044715 * (x * x * x))))


def _split3(x):
    hi = x.astype(BF16)
    r1 = x - hi.astype(F32)
    mid = r1.astype(BF16)
    lo = (r1 - mid.astype(F32)).astype(BF16)
    return hi, mid, lo


def _ffn_kernel(x_ref, g_ref, win_ref, wout_ref, *rest, post):
    x = x_ref[...]
    h = _rms(x, g_ref[...]).astype(BF16)
    acc = jnp.zeros_like(x)
    for c in range(FFN_DIM // FFN_CHUNK):
        lo = c * FFN_CHUNK
        gate = jnp.dot(h, win_ref[:, lo:lo + FFN_CHUNK], preferred_element_type=F32)
        up = jnp.dot(h, win_ref[:, FFN_DIM + lo:FFN_DIM + lo + FFN_CHUNK], preferred_element_type=F32)
        act = (_silu(gate) * up).astype(BF16)
        acc = acc + jnp.dot(act, wout_ref[lo:lo + FFN_CHUNK, :], preferred_element_type=F32)
    y = x + 0.5 * acc
    if post == "none":
        (y_ref,) = rest
        y_ref[...] = y
    elif post == "replace":
        g2_ref, y_ref = rest
        y_ref[...] = _rms(y, g2_ref[...])
    else:
        g2_ref, y_ref, u_ref = rest
        y_ref[...] = y
        u_ref[...] = _rms(y, g2_ref[...]).astype(u_ref.dtype)


def _ffn(x, g, w_in, w_out, layer, post="none", g2=None, u_dtype=BF16):
    rows = x.shape[0]
    tm = min(ROW_TILE, rows)
    row_spec = pl.BlockSpec((tm, D_MODEL), lambda i: (i, 0))
    vec_spec = pl.BlockSpec((1, D_MODEL), lambda i: (0, 0))
    in_specs = [row_spec,
                pl.BlockSpec((None, 1, D_MODEL), lambda i: (layer, 0, 0)),
                _resident((None, D_MODEL, 2 * FFN_DIM), lambda i: (layer, 0, 0)),
                _resident((None, FFN_DIM, D_MODEL), lambda i: (layer, 0, 0))]
    args = [x, g.reshape(DEPTH, 1, D_MODEL), w_in, w_out]
    out_shape = jax.ShapeDtypeStruct((rows, D_MODEL), F32)
    out_specs = row_spec
    if post != "none":
        in_specs.append(vec_spec)
        args.append(g2.reshape(1, D_MODEL))
    if post == "emit":
        out_shape = (out_shape, jax.ShapeDtypeStruct((rows, D_MODEL), u_dtype))
        out_specs = (row_spec, row_spec)
    return pl.pallas_call(
        functools.partial(_ffn_kernel, post=post),
        grid=(rows // tm,),
        in_specs=in_specs, out_specs=out_specs, out_shape=out_shape,
        compiler_params=_params("parallel"),
        name="ffn",
    )(*args)


def _proj_kernel(*refs, norm, glu, residual):
    refs = list(refs)
    x_ref = refs.pop(0)
    g_ref = refs.pop(0) if norm else None
    w_ref = refs.pop(0)
    res_ref = refs.pop(0) if residual else None
    (o_ref,) = refs
    x = x_ref[...]
    if norm:
        x = _rms(x, g_ref[...])
    y = jnp.dot(x.astype(BF16), w_ref[...], preferred_element_type=F32)
    if glu:
        half = y.shape[-1] // 2
        y = y[:, :half] * _sigmoid(y[:, half:])
    if residual:
        y = y + res_ref[...]
    o_ref[...] = y


def _proj(x, w, layer, g=None, g_layer=0, glu=False, res=None):
    rows, kdim = x.shape
    ndim = w.shape[-1]
    nout = ndim // 2 if glu else ndim
    tm = min(ROW_TILE, rows)
    in_specs = [pl.BlockSpec((tm, kdim), lambda i: (i, 0))]
    args = [x]
    if g is not None:
        in_specs.append(pl.BlockSpec((None, 1, kdim), lambda i: (g_layer, 0, 0)))
        args.append(g.reshape(g.shape[0], 1, kdim))
    in_specs.append(_resident((None, kdim, ndim), lambda i: (layer, 0, 0)))
    args.append(w)
    if res is not None:
        in_specs.append(pl.BlockSpec((tm, nout), lambda i: (i, 0)))
        args.append(res)
    return pl.pallas_call(
        functools.partial(_proj_kernel, norm=g is not None, glu=glu, residual=res is not None),
        grid=(rows // tm,),
        in_specs=in_specs,
        out_specs=pl.BlockSpec((tm, nout), lambda i: (i, 0)),
        out_shape=jax.ShapeDtypeStruct((rows, nout), F32),
        compiler_params=_params("parallel"),
        name="proj",
    )(*args)


def _memkv_kernel(x_ref, g_ref, w_ref, k_ref, v_ref):
    h = _rms(x_ref[...], g_ref[...]).astype(BF16)
    y = jnp.dot(h, w_ref[...], preferred_element_type=F32)
    k_ref[...] = y[:, :D_MODEL]
    v_ref[...] = y[:, D_MODEL:]


def _mem_kv(mem, g, w_kv):
    rows = mem.shape[0]
    tm = min(ROW_TILE, rows)
    out = jax.ShapeDtypeStruct((DEPTH, rows, D_MODEL), F32)
    out_spec = pl.BlockSpec((None, tm, D_MODEL), lambda l, i: (l, i, 0))
    return pl.pallas_call(
        _memkv_kernel,
        grid=(DEPTH, rows // tm),
        in_specs=[pl.BlockSpec((tm, D_MODEL), lambda l, i: (i, 0)),
                  pl.BlockSpec((None, 1, D_MODEL), lambda l, i: (l, 0, 0)),
                  pl.BlockSpec((None, D_MODEL, 2 * D_MODEL), lambda l, i: (l, 0, 0))],
        out_specs=(out_spec, out_spec), out_shape=(out, out),
        compiler_params=_params("parallel", "parallel"),
        name="mem_kv",
    )(mem, g.reshape(DEPTH, 1, D_MODEL), w_kv)


def _s5_pieces(a_re, a_im, log_dt, b_re, b_im, c_re, c_im, d, steps):
    G, P, GC = S5_GROUPS, S5_STATE, S5_GROUP
    L = steps
    dt = jnp.exp(log_dt)[:, None]
    xr, xi = a_re * dt, a_im * dt
    j = jnp.arange(L + 1, dtype=F32)[:, None, None]
    mag = jnp.exp(xr[None] * j)
    pw_re, pw_im = mag * jnp.cos(xi[None] * j), mag * jnp.sin(xi[None] * j)
    nr, ni = pw_re[1] - 1.0, pw_im[1]
    den = a_re * a_re + a_im * a_im
    fr, fi = (nr * a_re + ni * a_im) / den, (ni * a_re - nr * a_im) / den
    bb_re = fr[..., None] * b_re - fi[..., None] * b_im
    bb_im = fr[..., None] * b_im + fi[..., None] * b_re
    hp = lax.Precision.HIGHEST
    w_re = pw_re[:L, :, None, :] * c_re[None] - pw_im[:L, :, None, :] * c_im[None]
    w_im = pw_re[:L, :, None, :] * c_im[None] + pw_im[:L, :, None, :] * c_re[None]
    kern = (jnp.einsum("gpi,lgop->lgio", bb_re, w_re, precision=hp)
            - jnp.einsum("gpi,lgop->lgio", bb_im, w_im, precision=hp))
    kern = kern.at[0].add(d[:, :, None] * jnp.eye(GC, dtype=F32)[None])
    rp_re, rp_im = pw_re[:L][::-1], pw_im[:L][::-1]
    m_re = rp_re[:, :, :, None] * bb_re[None] - rp_im[:, :, :, None] * bb_im[None]
    m_im = rp_re[:, :, :, None] * bb_im[None] + rp_im[:, :, :, None] * bb_re[None]
    m_re = m_re.transpose(1, 0, 3, 2)
    m_im = m_im.transpose(1, 0, 3, 2)
    q_re = pw_re[1:, :, None, :] * c_re[None] - pw_im[1:, :, None, :] * c_im[None]
    q_im = pw_re[1:, :, None, :] * c_im[None] + pw_im[1:, :, None, :] * c_re[None]
    p_re = q_re.transpose(1, 3, 0, 2)
    p_im = (-q_im).transpose(1, 3, 0, 2)
    return kern, m_re, m_im, p_re, p_im, pw_re[L], pw_im[L]


def _s5_group_mats(pieces):
    kern, m_re, m_im, p_re, p_im, l_re, l_im = pieces
    return (kern[0], m_re[:, 0], m_im[:, 0], p_re[:, :, 0], p_im[:, :, 0], l_re[:, None, :], l_im[:, None, :])


def _s5_block_mats(pieces):
    kern, m_re, m_im, p_re, p_im, l_re, l_im = lax.optimization_barrier(pieces)
    L = kern.shape[0]
    GB = S5_BLOCK_GROUPS
    NB = S5_GROUPS // GB
    GC, P = S5_GROUP, S5_STATE
    taps =kern.astype(BF16).reshape(L, NB, GB * GC, GC).transpose(1, 0, 2, 3)

    def m_rows(m):
        return m.astype(BF16).reshape(NB, GB, L, GC, P).transpose(0, 2, 1, 3, 4).reshape(NB, L * GB * GC, P)

    mm = jnp.stack([m_rows(m_re), m_rows(m_im)], axis=1)
    pp = jnp.stack([p_re.astype(BF16).reshape(NB, GB * P, L * GC), p_im.astype(BF16).reshape(NB, GB * P, L * GC)],
                   axis=1)
    lam = jnp.concatenate([l_re.reshape(NB, 1, GB * P), l_im.reshape(NB, 1, GB * P)], axis=2)
    return taps, mm, pp, lam


def _s5_kernel(u_ref, t_ref, mre_ref, mim_ref, pre_ref, pim_ref, lre_ref, lim_ref, h0re_ref, h0im_ref,
               y_ref, hre_ref, him_ref, inj_re, inj_im, hs_re, hs_im, *, n_chunks, rb, gps, precision):
    def mm(a, b):
        return jnp.dot(a, b, preferred_element_type=F32, precision=precision)

    for g in range(gps):
        u = u_ref[g]
        inj_re[g] = mm(u, mre_ref[g])
        inj_im[g] = mm(u, mim_ref[g])

    lam_re = [jnp.broadcast_to(lre_ref[g], (rb, S5_STATE)) for g in range(gps)]
    lam_im = [jnp.broadcast_to(lim_ref[g], (rb, S5_STATE)) for g in range(gps)]

    def step(k, carry):
        rows = pl.ds(pl.multiple_of(k * rb, rb), rb)
        nxt = []
        for g in range(gps):
            hr, hi = carry[2 * g], carry[2 * g + 1]
            hs_re[g, rows, :] = hr
            hs_im[g, rows, :] = hi
            nxt.append(lam_re[g] * hr - lam_im[g] * hi + inj_re[g, rows, :])
            nxt.append(lam_re[g] * hi + lam_im[g] * hr + inj_im[g, rows, :])
        return tuple(nxt)

    init = []
    for g in range(gps):
        init += [h0re_ref[g], h0im_ref[g]]
    fin = lax.fori_loop(0, n_chunks, step, tuple(init))

    for g in range(gps):
        hre_ref[g] = fin[2 * g]
        him_ref[g] = fin[2 * g + 1]
        dt = u_ref.dtype
        y = (mm(u_ref[g], t_ref[g]) + mm(hs_re[g].astype(dt), pre_ref[g])
             + mm(hs_im[g].astype(dt), pim_ref[g]))
        y_ref[g] = _gelu_tanh(y).astype(y_ref.dtype)


def _s5_core(u, mats, h0_re, h0_im, n_chunks, rb, precision, y_dtype):
    G, R, W = u.shape
    P = S5_STATE
    gps = S5_GROUPS_PER_STEP
    tmat, m_re, m_im, p_re, p_im, l_re, l_im = mats

    def spec(a, b):
        return pl.BlockSpec((gps, a, b), lambda i: (i, 0, 0))

    st = jax.ShapeDtypeStruct((G, rb, P), F32)
    scr = lambda: pltpu.VMEM((gps, R, P), F32)
    return pl.pallas_call(
        functools.partial(_s5_kernel, n_chunks=n_chunks, rb=rb, gps=gps, precision=precision),
        grid=(G // gps,),
        in_specs=[spec(R, W), spec(W, W), spec(W, P), spec(W, P), spec(P, W), spec(P, W),
                  spec(1, P), spec(1, P), spec(rb, P), spec(rb, P)],
        out_specs=(spec(R, W), spec(rb, P), spec(rb, P)),
        out_shape=(jax.ShapeDtypeStruct((G, R, W), y_dtype), st, st),
        scratch_shapes=[scr(), scr(), scr(), scr()],
        compiler_params=_params("parallel"),
        name="s5_core",
    )(u, tmat, m_re, m_im, p_re, p_im, l_re, l_im, h0_re, h0_im)


def _iota2(shape):
    return lax.broadcasted_iota(jnp.int32, shape, 0), lax.broadcasted_iota(jnp.int32, shape, 1)


def _s5_expand(taps_ref, mc_ref, pc_ref, w2_ref, m_ref, p_ref):
    L, GC, P = S5_CHUNK, S5_GROUP, S5_STATE
    GB = S5_BLOCK_GROUPS
    W, HS = GB * GC, GB * P
    gc_bits, p_bits, w_bits = GC.bit_length() - 1, P.bit_length() - 1, W.bit_length() - 1

    r, c = _iota2((P, HS))
    rep_m = ((c & (P - 1)) == r).astype(BF16)
    r, c = _iota2((L * W, HS))
    mask_m = ((r >> gc_bits) & (GB - 1)) == (c >> p_bits)
    r, c = _iota2((L * GC, L * W))
    rep_p = (r == (((c >> w_bits) << gc_bits) | (c & (GC - 1)))).astype(BF16)
    r, c = _iota2((HS, L * W))
    mask_p = (r >> p_bits) == ((c >> gc_bits) & (GB - 1))
    for half in range(2):
        m_ref[:, half * HS:(half + 1) * HS] = jnp.where(
            mask_m, jnp.dot(mc_ref[half], rep_m, preferred_element_type=F32), 0.0).astype(BF16)
        p_ref[half * HS:(half + 1) * HS, :] = jnp.where(
            mask_p, jnp.dot(pc_ref[half], rep_p, preferred_element_type=F32), 0.0).astype(BF16)

    r, c = _iota2((GC, W))
    rep_k = ((c & (GC - 1)) == r).astype(BF16)
    r, c = _iota2((W, W))
    mask_k = (r >> gc_bits) == (c >> gc_bits)
    zero = jnp.zeros((W, W), BF16)
    w2_ref[L * W:, :W] = zero
    w2_ref[:W, W:] = zero
    for i in range(L):
        kb = jnp.where(mask_k, jnp.dot(taps_ref[L - 1 - i], rep_k, preferred_element_type=F32), 0.0).astype(BF16)
        w2_ref[i * W:(i + 1) * W, :W] = kb
        w2_ref[(i + 1) * W:(i + 2) * W, W:] = kb


def _s5_seq_kernel(u_ref, taps_ref, mc_ref, pc_ref, lam_ref, h0_ref, y_ref, hfin_ref,
                   w2_ref, m_ref, p_ref, lhs_scr, inj_scr, hs_scr, h_scr, *, batch, nck):
    L, W = S5_CHUNK, S5_BLOCK_GROUPS * S5_GROUP
    NT = S5_BLOCK_GROUPS * S5_STATE // W

    @pl.when(pl.program_id(1) == 0)
    def _():
        h_scr[...] = h0_ref[...]
        _s5_expand(taps_ref, mc_ref, pc_ref, w2_ref, m_ref, p_ref)

    for b in range(batch):
        for l in range(L):
            lhs_scr[b * nck:(b + 1) * nck, l * W:(l + 1) * W] = u_ref[b, pl.ds(l, nck, stride=L), :].astype(BF16)

    inj = jnp.dot(lhs_scr[...], m_ref[...], preferred_element_type=F32)
    for t in range(2 * NT):
        inj_scr[t] = inj[:, t * W:(t + 1) * W]

    lam = [jnp.broadcast_to(lam_ref[:, t * W:(t + 1) * W], (batch, W)) for t in range(2 * NT)]

    def step(k, h):
        rows = pl.ds(k, batch, stride=nck)
        nxt_re, nxt_im = [], []
        for t in range(NT):
            hr, hi = h[t], h[NT + t]
            hs_scr[t, rows, :] = hr
            hs_scr[NT + t, rows, :] = hi
            nxt_re.append(lam[t] * hr - lam[NT + t] * hi + inj_scr[t, rows, :])
            nxt_im.append(lam[t] * hi + lam[NT + t] * hr + inj_scr[NT + t, rows, :])
        return tuple(nxt_re + nxt_im)

    h = lax.fori_loop(0, nck, step, tuple(h_scr[:, t * W:(t + 1) * W] for t in range(2 * NT)))
    for t in range(2 * NT):
        h_scr[:, t * W:(t + 1) * W] = h[t]
    hfin_ref[...] = h_scr[...]

    hs = jnp.concatenate([hs_scr[t] for t in range(2 * NT)], axis=-1).astype(BF16)
    for pr in range(L // 2):
        kk = (2 * pr + 2) * W
        y = (jnp.dot(lhs_scr[:, :kk], w2_ref[(L - 1 - 2 * pr) * W:, :], preferred_element_type=F32)
             + jnp.dot(hs, p_ref[:, 2 * pr * W:(2 * pr + 2) * W], preferred_element_type=F32))
        y = _gelu_tanh(y)
        for s in range(2):
            for b in range(batch):
                y_ref[b, pl.ds(2 * pr + s, nck, stride=L), :] = y[b * nck:(b + 1) * nck, s * W:(s + 1) * W]


def _s5_seq(u, mats, h0, batch, seq):
    taps, mm, pp, lam = mats
    L, W = S5_CHUNK, S5_BLOCK_GROUPS * S5_GROUP
    NB = S5_GROUPS // S5_BLOCK_GROUPS
    HS = S5_BLOCK_GROUPS * S5_STATE
    SW = 2 * HS
    ts = seq // S5_TIME_SLICES
    nck = ts // L
    rows = batch * nck
    u_spec = pl.BlockSpec((batch, ts, W), lambda i, t: (0, t, i))
    h_spec = pl.BlockSpec((None, batch, SW), lambda i, t: (i, 0, 0))

    def w_spec(*dims):
        return pl.BlockSpec((None,) + dims, lambda i, t: (i,) + (0,) * len(dims))

    return pl.pallas_call(
        functools.partial(_s5_seq_kernel, batch=batch, nck=nck),
        grid=(NB, S5_TIME_SLICES),
        in_specs=[u_spec, w_spec(L, W, S5_GROUP), w_spec(2, L * W, S5_STATE), w_spec(2, HS, L * S5_GROUP),
                  w_spec(1, SW), h_spec],
        out_specs=(u_spec, h_spec),
        out_shape=(jax.ShapeDtypeStruct((batch, seq, D_MODEL), F32), jax.ShapeDtypeStruct(h0.shape, F32)),
        scratch_shapes=[pltpu.VMEM(((L + 1) * W, 2 * W), BF16), pltpu.VMEM((L * W, SW), BF16),
                        pltpu.VMEM((SW, L * W), BF16),
                        pltpu.VMEM((rows, L * W), BF16), pltpu.VMEM((SW // W, rows, W), F32),
                        pltpu.VMEM((SW // W, rows, W), F32), pltpu.VMEM((batch, SW), F32)],
        compiler_params=_params("parallel", "arbitrary"),
        name="s5_seq",
    )(u, taps, mm, pp, lam, h0)


def _s5_mixer(x, u, h0_re, h0_im, s5p, w_glu, j, batch, seq):
    G, GC, P = S5_GROUPS, S5_GROUP, S5_STATE
    if seq > 1:
        NB, HS = G // S5_BLOCK_GROUPS, S5_BLOCK_GROUPS * P
        mats = _s5_block_mats(_s5_pieces(*[p[j] for p in s5p], S5_CHUNK))
        to_blocks = lambda h: h.reshape(batch, NB, HS).transpose(1, 0, 2)
        h0 = jnp.concatenate([to_blocks(h0_re), to_blocks(h0_im)], axis=-1)
        y, hfin = _s5_seq(u.reshape(batch, seq, D_MODEL), mats, h0, batch, seq)
        y = y.reshape(batch * seq, D_MODEL)
        from_blocks = lambda h: h.transpose(1, 0, 2).reshape(batch, G, P)
        hre, him = from_blocks(hfin[..., :HS]), from_blocks(hfin[..., HS:])
    else:
        mats = _s5_group_mats(_s5_pieces(*[p[j] for p in s5p], 1))
        ug = u.reshape(batch, G, GC).transpose(1, 0, 2)
        yg, hre, him = _s5_core(ug, mats, h0_re.transpose(1, 0, 2), h0_im.transpose(1, 0, 2),
                                1, batch, lax.Precision.HIGHEST, F32)
        y = yg.transpose(1, 0, 2).reshape(batch, D_MODEL)
        hre, him = hre.transpose(1, 0, 2), him.transpose(1, 0, 2)
    x = _proj(y, w_glu, j, glu=True, res=x)
    return x, hre, him


def _hg_lower_bound(logits, layer):
    m = jnp.max(logits, axis=0, keepdims=True)
    e = jnp.exp(logits - m)
    sm = e / jnp.sum(e, axis=0, keepdims=True)
    return jnp.sum(sm[:layer + 1], axis=0, keepdims=True) - sm[0:1]


def _hg_gates(z, lb):
    e = jnp.exp(-jnp.abs(z))
    r = 1.0 / (1.0 + e)
    log_sig = jnp.minimum(z, 0.0) - jnp.log1p(e)
    a = jnp.log(lb)
    b = jnp.log1p(-lb) + log_sig
    logf = jnp.maximum(a, b) + jnp.log1p(jnp.exp(-jnp.abs(a - b)))
    k = (1.0 - lb) * jnp.where(z >= 0.0, e * r, r)
    return logf, k


def _hgrn_prompt_kernel(x_ref, g_ref, win_ref, lbl_ref, ng_ref, wout_ref, y_ref, sfin_ref,
                        s_scr, q_scr, k_scr, v_scr, gt_scr, o_scr, beta_scr, safe_scr, *, layer, tt):
    t = pl.program_id(1)
    C = HG_CHUNK

    @pl.when(t == 0)
    def _():
        s_scr[...] = jnp.zeros_like(s_scr)

    x = x_ref[...]
    h = _rms(x, g_ref[...]).astype(BF16)
    proj = jnp.dot(h, win_ref[...], preferred_element_type=F32)
    lb = _hg_lower_bound(lbl_ref[...], layer)
    logf, kk = _hg_gates(proj[:, HG_WIDTH:2 * HG_WIDTH], lb)
    q_scr[...] = _silu(proj[:, :HG_WIDTH])
    k_scr[...] = kk
    v_scr[...] = proj[:, 2 * HG_WIDTH:3 * HG_WIDTH]
    gt_scr[...] = _silu(proj[:, 3 * HG_WIDTH:])

    row, col = _iota2((C, C))
    causal = row >= col
    tri = causal.astype(BF16)
    ng = ng_ref[...]
    nt_dims = (((1,), (1,)), ((), ()))
    tn_dims = (((0,), (0,)), ((), ()))

    def finish_head(hh, rows, q_dec, k_dec, o_intra, btot_h):
        sv = slice(hh * HG_DV, (hh + 1) * HG_DV)
        st = s_scr[hh]
        o = o_intra + lax.dot_general(q_dec.astype(BF16), st.astype(BF16), nt_dims, preferred_element_type=F32)
        kv_t = lax.dot_general(v_scr[rows, sv].astype(BF16), k_dec.astype(BF16), tn_dims,
                               preferred_element_type=F32)
        return (_rms(o, ng) * gt_scr[rows, sv]).astype(BF16), st * jnp.exp(btot_h) + kv_t

    def store_heads(rows, results):
        o_scr[rows, :] = jnp.concatenate([o for o, _ in results], axis=-1)
        for hh, (_, s_new) in enumerate(results):
            s_scr[hh] = s_new

    for ci in range(tt // C):
        parts = _split3(logf[ci * C:(ci + 1) * C])
        beta = sum(jnp.dot(tri, p, preferred_element_type=F32) for p in parts)
        beta_scr[ci * C:(ci + 1) * C, :] = beta
        mid = beta[C // 2 - 1:C // 2, :]
        spread = jnp.maximum(jnp.max(-mid), jnp.max(mid - beta[C - 1:C, :]))
        safe_scr[ci] = (spread <= HG_FACTORED_MAX_DECAY).astype(jnp.int32)

    def chunk_step(c, carry):
        rows = pl.ds(pl.multiple_of(c * C, C), C)
        base = pl.multiple_of(c * C, C)
        btot = beta_scr[pl.ds(base + (C - 8), 8), :][7:8]
        mid = beta_scr[pl.ds(base + (C // 2 - 8), 8), :][7:8]
        safe = safe_scr[c] == 1

        @pl.when(safe)
        def _():
            e_mid = jnp.exp(mid)
            e_tot = jnp.exp(btot - mid)
            results = []
            for hh in range(HG_HEADS):
                sk = slice(hh * HG_DK, (hh + 1) * HG_DK)
                d = beta_scr[rows, sk] - mid[:, sk]
                q_mid = q_scr[rows, sk] * jnp.exp(d)
                k_mid = k_scr[rows, sk] * jnp.exp(-d)
                att = lax.dot_general(q_mid.astype(BF16), k_mid.astype(BF16), nt_dims, preferred_element_type=F32)
                att = jnp.where(causal, att, 0.0).astype(BF16)
                o_intra = jnp.dot(att, v_scr[rows, hh * HG_DV:(hh + 1) * HG_DV].astype(BF16),
                                  preferred_element_type=F32)
                results.append(finish_head(hh, rows, q_mid * e_mid[:, sk], k_mid * e_tot[:, sk], o_intra,
                                           btot[:, sk]))
            store_heads(rows, results)

        @pl.when(jnp.logical_not(safe))
        def _():
            t_idx = lax.broadcasted_iota(jnp.int32, (C, 1), 0)
            results = []
            for hh in range(HG_HEADS):
                sk = slice(hh * HG_DK, (hh + 1) * HG_DK)
                sv = slice(hh * HG_DV, (hh + 1) * HG_DV)
                b = beta_scr[rows, sk]
                q = q_scr[rows, sk]

                def key_step(s8, acc):
                    off = pl.multiple_of(s8 * 8, 8)
                    b_keys = beta_scr[pl.ds(base + off, 8), sk]
                    k_keys = k_scr[pl.ds(base + off, 8), sk]
                    v_keys = v_scr[pl.ds(base + off, 8), sv]
                    for i in range(8):
                        w = jnp.exp(jnp.minimum(b - b_keys[i:i + 1], 0.0))
                        a = jnp.sum(q * k_keys[i:i + 1] * w, axis=-1, keepdims=True)
                        a = jnp.where(t_idx >= off + i, a, 0.0)
                        acc = acc + a * v_keys[i:i + 1]
                    return acc

                o_intra = lax.fori_loop(0, C // 8, key_step, jnp.zeros((C, HG_DV), F32))
                results.append(finish_head(hh, rows, q * jnp.exp(b), k_scr[rows, sk] * jnp.exp(btot[:, sk] - b),
                                           o_intra, btot[:, sk]))
            store_heads(rows, results)

        return carry

    lax.fori_loop(0, tt // C, chunk_step, 0)
    y_ref[...] = x + jnp.dot(o_scr[...], wout_ref[...], preferred_element_type=F32)

    @pl.when(t == pl.num_programs(1) - 1)
    def _():
        for hh in range(HG_HEADS):
            sfin_ref[hh] = s_scr[hh].T


def _hgrn_prompt(x, g, w_in, lb_logits, norm_g, w_out, layer, j, batch, seq):
    tt = HG_ROW_TILE
    nt = seq // tt
    x3 = x.reshape(batch, seq, D_MODEL)
    row_spec = pl.BlockSpec((None, tt, D_MODEL), lambda b, t: (b, t, 0))
    y, s_fin = pl.pallas_call(
        functools.partial(_hgrn_prompt_kernel, layer=layer, tt=tt),
        grid=(batch, nt),
        in_specs=[row_spec,
                  pl.BlockSpec((None, 1, D_MODEL), lambda b, t: (layer, 0, 0)),
                  _resident((None, D_MODEL, 4 * HG_WIDTH), lambda b, t: (j, 0, 0)),
                  pl.BlockSpec((DEPTH, HG_WIDTH), lambda b, t: (0, 0)),
                  pl.BlockSpec((None, 1, HG_DV), lambda b, t: (j, 0, 0)),
                  _resident((None, HG_WIDTH, D_MODEL), lambda b, t: (j, 0, 0))],
        out_specs=(row_spec,
                   pl.BlockSpec((None, HG_HEADS, HG_DK, HG_DV), lambda b, t: (b, 0, 0, 0))),
        out_shape=(jax.ShapeDtypeStruct((batch, seq, D_MODEL), F32),
                   jax.ShapeDtypeStruct((batch, HG_HEADS, HG_DK, HG_DV), F32)),
        scratch_shapes=[pltpu.VMEM((HG_HEADS, HG_DV, HG_DK), F32),
                        pltpu.VMEM((tt, HG_WIDTH), F32), pltpu.VMEM((tt, HG_WIDTH), F32),
                        pltpu.VMEM((tt, HG_WIDTH), F32), pltpu.VMEM((tt, HG_WIDTH), F32),
                        pltpu.VMEM((tt, HG_WIDTH), BF16), pltpu.VMEM((tt, HG_WIDTH), F32),
                        pltpu.SMEM((tt // HG_CHUNK,), jnp.int32)],
        compiler_params=_params("parallel", "arbitrary"),
        name="hgrn_prompt",
    )(x3, g.reshape(DEPTH, 1, D_MODEL), w_in, lb_logits, norm_g.reshape(-1, 1, HG_DV), w_out)
    return y.reshape(batch * seq, D_MODEL), s_fin


def _hgrn_sample_kernel(proj_ref, lbl_ref, ng_ref, s_ref, snew_ref, o_ref, *, layer, tb):
    proj = proj_ref[...]
    lb = _hg_lower_bound(lbl_ref[...], layer)
    z = proj[:, HG_WIDTH:2 * HG_WIDTH]
    e = jnp.exp(-jnp.abs(z))
    r = 1.0 / (1.0 + e)
    sig = jnp.where(z >= 0.0, r, e * r)
    f = lb + (1.0 - lb) * sig
    k = (1.0 - lb) * jnp.where(z >= 0.0, e * r, r)
    q = _silu(proj[:, :HG_WIDTH])
    v = proj[:, 2 * HG_WIDTH:3 * HG_WIDTH]
    gt = _silu(proj[:, 3 * HG_WIDTH:])
    ng = ng_ref[...]
    for hh in range(HG_HEADS):
        sk = slice(hh * HG_DK, (hh + 1) * HG_DK)
        sv = slice(hh * HG_DV, (hh + 1) * HG_DV)
        f_t = f[:, sk].T
        k_t = k[:, sk].T
        q_t = q[:, sk].T
        for b in range(tb):
            s_new = f_t[:, b:b + 1] * s_ref[b, hh] + k_t[:, b:b + 1] * v[b:b + 1, sv]
            snew_ref[b, hh] = s_new
            o = jnp.sum(q_t[:, b:b + 1] * s_new, axis=0, keepdims=True)
            o_ref[b:b + 1, sv] = _rms(o, ng) * gt[b:b + 1, sv]


def _hgrn_sample(proj, lb_logits, norm_g, state, layer, j):
    nb = proj.shape[0]
    tb = SAMPLE_TOKENS_PER_STEP
    st_spec = pl.BlockSpec((tb, HG_HEADS, HG_DK, HG_DV), lambda i: (i, 0, 0, 0))
    return pl.pallas_call(
        functools.partial(_hgrn_sample_kernel, layer=layer, tb=tb),
        grid=(nb // tb,),
        in_specs=[pl.BlockSpec((tb, 4 * HG_WIDTH), lambda i: (i, 0)),
                  pl.BlockSpec((DEPTH, HG_WIDTH), lambda i: (0, 0)),
                  pl.BlockSpec((None, 1, HG_DV), lambda i: (j, 0, 0)),
                  st_spec],
        out_specs=(st_spec, pl.BlockSpec((tb, HG_WIDTH), lambda i: (i, 0))),
        out_shape=(jax.ShapeDtypeStruct(state.shape, F32), jax.ShapeDtypeStruct((nb, HG_WIDTH), F32)),
        compiler_params=_params("parallel"),
        name="hgrn_sample",
    )(proj, lb_logits, norm_g.reshape(-1, 1, HG_DV), state)


def _xattn_prompt_kernel(x_ref, g_ref, wq_ref, k_ref, v_ref, wo_ref, y_ref, kb_scr, vb_scr):
    @pl.when(pl.program_id(1) == 0)
    def _():
        kb_scr[...] = k_ref[...].astype(BF16)
        vb_scr[...] = v_ref[...].astype(BF16)

    x = x_ref[...]
    h = _rms(x, g_ref[...]).astype(BF16)
    q = jnp.dot(h, wq_ref[...], preferred_element_type=F32) * (1.0 / math.sqrt(MEM_HD))
    q = q.astype(BF16)
    outs = []
    for hh in range(MEM_HEADS):
        sl = slice(hh * MEM_HD, (hh + 1) * MEM_HD)
        s = lax.dot_general(q[:, sl], kb_scr[:, sl], (((1,), (1,)), ((), ())), preferred_element_type=F32)
        p = jnp.exp(s - jnp.max(s, axis=-1, keepdims=True))
        den = jnp.sum(p, axis=-1, keepdims=True)
        o = jnp.dot(p.astype(BF16), vb_scr[:, sl], preferred_element_type=F32)
        outs.append((o / den).astype(BF16))
    o = jnp.concatenate(outs, axis=-1)
    y_ref[...] = x + jnp.dot(o, wo_ref[...], preferred_element_type=F32)


def _xattn_prompt(x, g, w_q, mem_k, mem_v, w_o, layer, batch, seq):
    tt = XA_ROW_TILE
    row_spec = pl.BlockSpec((None, tt, D_MODEL), lambda b, t: (b, t, 0))
    kv_spec = pl.BlockSpec((None, N_MEM, D_MODEL), lambda b, t: (layer, b, 0))
    w_spec = _resident((None, D_MODEL, D_MODEL), lambda b, t: (layer, 0, 0))
    y = pl.pallas_call(
        _xattn_prompt_kernel,
        grid=(batch, seq // tt),
        in_specs=[row_spec, pl.BlockSpec((None, 1, D_MODEL), lambda b, t: (layer, 0, 0)),
                  w_spec, kv_spec, kv_spec, w_spec],
        out_specs=row_spec,
        out_shape=jax.ShapeDtypeStruct((batch, seq, D_MODEL), F32),
        scratch_shapes=[pltpu.VMEM((N_MEM, D_MODEL), BF16), pltpu.VMEM((N_MEM, D_MODEL), BF16)],
        compiler_params=_params("parallel", "arbitrary"),
        name="xattn_prompt",
    )(x.reshape(batch, seq, D_MODEL), g.reshape(DEPTH, 1, D_MODEL), w_q, mem_k, mem_v, w_o)
    return y.reshape(batch * seq, D_MODEL)


def _xattn_sample_kernel(q_ref, k_ref, v_ref, o_ref, *, tb):
    scale = 1.0 / math.sqrt(MEM_HD)
    for b in range(tb):
        q = q_ref[b] * scale
        s = jnp.sum(k_ref[b] * q[None], axis=-1, keepdims=True)
        p = jnp.exp(s - jnp.max(s, axis=0, keepdims=True))
        p = p / jnp.sum(p, axis=0, keepdims=True)
        o_ref[b] = jnp.sum(p * v_ref[b], axis=0)


def _xattn_sample(q, cache_k, cache_v, layer):
    nb = q.shape[0]
    tb = XA_SAMPLE_TOKENS_PER_STEP
    kv_spec = pl.BlockSpec((None, tb, N_MEM, MEM_HEADS, MEM_HD), lambda i: (layer, i, 0, 0, 0))
    q_spec = pl.BlockSpec((tb, MEM_HEADS, MEM_HD), lambda i: (i, 0, 0))
    return pl.pallas_call(
        functools.partial(_xattn_sample_kernel, tb=tb),
        grid=(nb // tb,),
        in_specs=[q_spec, kv_spec, kv_spec],
        out_specs=q_spec,
        out_shape=jax.ShapeDtypeStruct((nb, MEM_HEADS, MEM_HD), F32),
        compiler_params=_params("parallel"),
        name="xattn_sample",
    )(q.reshape(nb, MEM_HEADS, MEM_HD), cache_k, cache_v).reshape(nb, D_MODEL)


def _trunk(x, batch, seq, s5_re, s5_im, hg_state, mem_k, mem_v, w):
    prompt = seq > 1
    new_re, new_im, new_hg = [], [], []
    for i in range(DEPTH):
        j = i // 2
        if i % 2 == 0:
            x, u = _ffn(x, w["ffn1_norm"], w["ffn1_w_in"], w["ffn1_w_out"], i, post="emit",
                        g2=w["mix_norm"][i], u_dtype=F32)
            x, hr, hi = _s5_mixer(x, u, s5_re[j], s5_im[j], w["s5"], w["s5_w_glu"], j, batch, seq)
            new_re.append(hr)
            new_im.append(hi)
        else:
            x = _ffn(x, w["ffn1_norm"], w["ffn1_w_in"], w["ffn1_w_out"], i)
            if prompt:
                x, sn = _hgrn_prompt(x, w["mix_norm"], w["hg_w_in"], w["hg_lb_logits"], w["hg_norm"],
                                     w["hg_w_out"], i, j, batch, seq)
            else:
                proj = _proj(x, w["hg_w_in"], j, g=w["mix_norm"], g_layer=i)
                sn, o = _hgrn_sample(proj, w["hg_lb_logits"], w["hg_norm"], hg_state[j], i, j)
                x = _proj(o, w["hg_w_out"], j, res=x)
            new_hg.append(sn)
        if prompt:
            x = _xattn_prompt(x, w["xattn_norm"], w["xattn_w_q"], mem_k, mem_v, w["xattn_w_o"], i, batch, seq)
        else:
            q = _proj(x, w["xattn_w_q"], i, g=w["xattn_norm"], g_layer=i)
            o = _xattn_sample(q, mem_k, mem_v, i)
            x = _proj(o, w["xattn_w_o"], i, res=x)
        last = i == DEPTH - 1
        x = _ffn(x, w["ffn2_norm"], w["ffn2_w_in"], w["ffn2_w_out"], i,
                 post="replace" if last else "none", g2=w["final_norm"] if last else None)
    return x, jnp.stack(new_re), jnp.stack(new_im), jnp.stack(new_hg)


def kernel(x_prompt, x_sample, mem_prompt, state_s5_re, state_s5_im, state_hgrn, cache_mem_k, cache_mem_v, ffn1_norm, ffn1_w_in, ffn1_w_out, mix_norm, xattn_norm, mem_norm, xattn_w_q, xattn_w_kv, xattn_w_o, ffn2_norm, ffn2_w_in, ffn2_w_out, s5_a_re, s5_a_im, s5_log_dt, s5_b_re, s5_b_im, s5_c_re, s5_c_im, s5_d, s5_w_glu, hg_w_in, hg_lb_logits, hg_norm, hg_w_out, final_norm):
    bp, seq, _ = x_prompt.shape
    bs = x_sample.shape[0]
    bf = lambda a: a.astype(BF16)
    w = dict(ffn1_norm=ffn1_norm, ffn1_w_in=bf(ffn1_w_in), ffn1_w_out=bf(ffn1_w_out), mix_norm=mix_norm,
             xattn_norm=xattn_norm, xattn_w_q=bf(xattn_w_q), xattn_w_o=bf(xattn_w_o), ffn2_norm=ffn2_norm,
             ffn2_w_in=bf(ffn2_w_in), ffn2_w_out=bf(ffn2_w_out),
             s5=(s5_a_re, s5_a_im, s5_log_dt, s5_b_re, s5_b_im, s5_c_re, s5_c_im, s5_d),
             s5_w_glu=bf(s5_w_glu), hg_w_in=bf(hg_w_in), hg_lb_logits=hg_lb_logits, hg_norm=hg_norm,
             hg_w_out=bf(hg_w_out), final_norm=final_norm)

    mem_k, mem_v = _mem_kv(mem_prompt.reshape(bp * N_MEM, D_MODEL), mem_norm, bf(xattn_w_kv))

    n_s5, n_hg = state_s5_re.shape[0], state_hgrn.shape[0]
    z_s5 = jnp.zeros((n_s5, bp, S5_GROUPS, S5_STATE), F32)
    y_p, re_p, im_p, hg_p = _trunk(x_prompt.reshape(bp * seq, D_MODEL), bp, seq, z_s5, z_s5, None,
                                   mem_k, mem_v, w)
    y_s, re_s, im_s, hg_s = _trunk(x_sample.reshape(bs, D_MODEL), bs, 1, state_s5_re, state_s5_im, state_hgrn,
                                   cache_mem_k, cache_mem_v, w)
    kv_shape = (DEPTH, bp, N_MEM, MEM_HEADS, MEM_HD)
    return (y_p.reshape(bp, seq, D_MODEL), y_s.reshape(bs, 1, D_MODEL), re_p, im_p, re_s, im_s, hg_p, hg_s,
            mem_k.reshape(kv_shape), mem_v.reshape(kv_shape))
```

```python
import functools
import math

import jax
import jax.numpy as jnp
from jax import lax
from jax.experimental import pallas as pl
from jax.experimental.pallas import tpu as pltpu

F32 = jnp.float32
BF16 = jnp.bfloat16

D_MODEL = 1024
DEPTH = 2
S5_GROUP = 16
S5_GROUPS = D_MODEL // S5_GROUP
S5_STATE = 64
S5_CHUNK = 16
S5_BLOCK_GROUPS = 8
S5_TIME_SLICES = 2
HG_DK = 128
HG_HEADS = D_MODEL // HG_DK
HG_DV = D_MODEL // HG_HEADS
HG_WIDTH = HG_HEADS * HG_DK
HG_CHUNK = 128
HG_FACTORED_MAX_DECAY = 60.0
N_MEM = 256
MEM_HEADS = 4
MEM_HD = D_MODEL // MEM_HEADS
FFN_DIM = 2816
EPS = 1e-6

V7X_VMEM_LIMIT_BYTES = 56 * 1024 * 1024

ROW_TILE = 512
FFN_CHUNK = 1408
HG_ROW_TILE = 256
XA_ROW_TILE = 512
S5_GROUPS_PER_STEP = 4
SAMPLE_TOKENS_PER_STEP = 8
XA_SAMPLE_TOKENS_PER_STEP = 4


def _params(*semantics):
    return pltpu.CompilerParams(dimension_semantics=semantics,
                                vmem_limit_bytes=V7X_VMEM_LIMIT_BYTES)


def _resident(shape, index_map):
    return pl.BlockSpec(shape, index_map, pipeline_mode=pl.Buffered(1))


def _rms(x, g):
    ms = jnp.mean(x * x, axis=-1, keepdims=True)
    return x * lax.rsqrt(ms + EPS) * g


def _sigmoid(x):
    return 1.0 / (1.0 + jnp.exp(-x))


def _silu(x):
    return x * _sigmoid(x)


def _gelu_tanh(x):
    c = math.sqrt(2.0 / math.pi)
    return 0.5 * x * (1.0 + jnp.tanh(c * (x + 0.044715 * (x * x * x))))


def _split3(x):
    hi = x.astype(BF16)
    r1 = x - hi.astype(F32)
    mid = r1.astype(BF16)
    lo = (r1 - mid.astype(F32)).astype(BF16)
    return hi, mid, lo


def _ffn_kernel(x_ref, g_ref, win_ref, wout_ref, *rest, post):
    x = x_ref[...]
    h = _rms(x, g_ref[...]).astype(BF16)
    acc = jnp.zeros_like(x)
    for c in range(FFN_DIM // FFN_CHUNK):
        lo = c * FFN_CHUNK
        gate = jnp.dot(h, win_ref[:, lo:lo + FFN_CHUNK], preferred_element_type=F32)
        up = jnp.dot(h, win_ref[:, FFN_DIM + lo:FFN_DIM + lo + FFN_CHUNK], preferred_element_type=F32)
        act = (_silu(gate) * up).astype(BF16)
        acc = acc + jnp.dot(act, wout_ref[lo:lo + FFN_CHUNK, :], preferred_element_type=F32)
    y = x + 0.5 * acc
    if post == "none":
        (y_ref,) = rest
        y_ref[...] = y
    elif post == "replace":
        g2_ref, y_ref = rest
        y_ref[...] = _rms(y, g2_ref[...])
    else:
        g2_ref, y_ref, u_ref = rest
        y_ref[...] = y
        u_ref[...] = _rms(y, g2_ref[...]).astype(u_ref.dtype)


def _ffn(x, g, w_in, w_out, layer, post="none", g2=None, u_dtype=BF16):
    rows = x.shape[0]
    tm = min(ROW_TILE, rows)
    row_spec = pl.BlockSpec((tm, D_MODEL), lambda i: (i, 0))
    vec_spec = pl.BlockSpec((1, D_MODEL), lambda i: (0, 0))
    in_specs = [row_spec,
                pl.BlockSpec((None, 1, D_MODEL), lambda i: (layer, 0, 0)),
                _resident((None, D_MODEL, 2 * FFN_DIM), lambda i: (layer, 0, 0)),
                _resident((None, FFN_DIM, D_MODEL), lambda i: (layer, 0, 0))]
    args = [x, g.reshape(DEPTH, 1, D_MODEL), w_in, w_out]
    out_shape = jax.ShapeDtypeStruct((rows, D_MODEL), F32)
    out_specs = row_spec
    if post != "none":
        in_specs.append(vec_spec)
        args.append(g2.reshape(1, D_MODEL))
    if post == "emit":
        out_shape = (out_shape, jax.ShapeDtypeStruct((rows, D_MODEL), u_dtype))
        out_specs = (row_spec, row_spec)
    return pl.pallas_call(
        functools.partial(_ffn_kernel, post=post),
        grid=(rows // tm,),
        in_specs=in_specs, out_specs=out_specs, out_shape=out_shape,
        compiler_params=_params("parallel"),
        name="ffn",
    )(*args)


def _proj_kernel(*refs, norm, glu, residual):
    refs = list(refs)
    x_ref = refs.pop(0)
    g_ref = refs.pop(0) if norm else None
    w_ref = refs.pop(0)
    res_ref = refs.pop(0) if residual else None
    (o_ref,) = refs
    x = x_ref[...]
    if norm:
        x = _rms(x, g_ref[...])
    y = jnp.dot(x.astype(BF16), w_ref[...], preferred_element_type=F32)
    if glu:
        half = y.shape[-1] // 2
        y = y[:, :half] * _sigmoid(y[:, half:])
    if residual:
        y = y + res_ref[...]
    o_ref[...] = y


def _proj(x, w, layer, g=None, g_layer=0, glu=False, res=None):
    rows, kdim = x.shape
    ndim = w.shape[-1]
    nout = ndim // 2 if glu else ndim
    tm = min(ROW_TILE, rows)
    in_specs = [pl.BlockSpec((tm, kdim), lambda i: (i, 0))]
    args = [x]
    if g is not None:
        in_specs.append(pl.BlockSpec((None, 1, kdim), lambda i: (g_layer, 0, 0)))
        args.append(g.reshape(g.shape[0], 1, kdim))
    in_specs.append(_resident((None, kdim, ndim), lambda i: (layer, 0, 0)))
    args.append(w)
    if res is not None:
        in_specs.append(pl.BlockSpec((tm, nout), lambda i: (i, 0)))
        args.append(res)
    return pl.pallas_call(
        functools.partial(_proj_kernel, norm=g is not None, glu=glu, residual=res is not None),
        grid=(rows // tm,),
        in_specs=in_specs,
        out_specs=pl.BlockSpec((tm, nout), lambda i: (i, 0)),
        out_shape=jax.ShapeDtypeStruct((rows, nout), F32),
        compiler_params=_params("parallel"),
        name="proj",
    )(*args)


def _memkv_kernel(x_ref, g_ref, w_ref, k_ref, v_ref, kb_ref, vb_ref):
    h = _rms(x_ref[...], g_ref[...]).astype(BF16)
    y = jnp.dot(h, w_ref[...], preferred_element_type=F32)
    k, v = y[:, :D_MODEL], y[:, D_MODEL:]
    k_ref[...] = k.reshape(k_ref.shape)
    v_ref[...] = v.reshape(v_ref.shape)
    kb_ref[...] = k.astype(BF16)
    vb_ref[...] = v.astype(BF16)


def _mem_kv(mem, g, w_kv):
    batch = mem.shape[0]
    rows = batch * N_MEM
    nb = max(1, min(ROW_TILE, rows) // N_MEM)
    tm = nb * N_MEM
    out5 = jax.ShapeDtypeStruct((DEPTH, batch, N_MEM, MEM_HEADS, MEM_HD), F32)
    out2 = jax.ShapeDtypeStruct((DEPTH, rows, D_MODEL), BF16)
    spec5 = pl.BlockSpec((None, nb, N_MEM, MEM_HEADS, MEM_HD), lambda l, i: (l, i, 0, 0, 0))
    spec2 = pl.BlockSpec((None, tm, D_MODEL), lambda l, i: (l, i, 0))
    return pl.pallas_call(
        _memkv_kernel,
        grid=(DEPTH, rows // tm),
        in_specs=[pl.BlockSpec((tm, D_MODEL), lambda l, i: (i, 0)),
                  pl.BlockSpec((None, 1, D_MODEL), lambda l, i: (l, 0, 0)),
                  pl.BlockSpec((None, D_MODEL, 2 * D_MODEL), lambda l, i: (l, 0, 0))],
        out_specs=(spec5, spec5, spec2, spec2), out_shape=(out5, out5, out2, out2),
        compiler_params=_params("parallel", "parallel"),
        name="mem_kv",
    )(mem.reshape(rows, D_MODEL), g.reshape(DEPTH, 1, D_MODEL), w_kv)


def _s5_pieces(a_re, a_im, log_dt, b_re, b_im, c_re, c_im, d, steps):
    G, P, GC = S5_GROUPS, S5_STATE, S5_GROUP
    L = steps
    dt = jnp.exp(log_dt)[:, None]
    xr, xi = a_re * dt, a_im * dt
    j = jnp.arange(L + 1, dtype=F32)[:, None, None]
    mag = jnp.exp(xr[None] * j)
    pw_re, pw_im = mag * jnp.cos(xi[None] * j), mag * jnp.sin(xi[None] * j)
    nr, ni = pw_re[1] - 1.0, pw_im[1]
    den = a_re * a_re + a_im * a_im
    fr, fi = (nr * a_re + ni * a_im) / den, (ni * a_re - nr * a_im) / den
    bb_re = fr[..., None] * b_re - fi[..., None] * b_im
    bb_im = fr[..., None] * b_im + fi[..., None] * b_re
    hp = lax.Precision.HIGHEST
    w_re = pw_re[:L, :, None, :] * c_re[None] - pw_im[:L, :, None, :] * c_im[None]
    w_im = pw_re[:L, :, None, :] * c_im[None] + pw_im[:L, :, None, :] * c_re[None]
    kern = (jnp.einsum("gpi,lgop->lgio", bb_re, w_re, precision=hp)
            - jnp.einsum("gpi,lgop->lgio", bb_im, w_im, precision=hp))
    kern = kern.at[0].add(d[:, :, None] * jnp.eye(GC, dtype=F32)[None])
    rp_re, rp_im = pw_re[:L][::-1], pw_im[:L][::-1]
    m_re = rp_re[:, :, :, None] * bb_re[None] - rp_im[:, :, :, None] * bb_im[None]
    m_im = rp_re[:, :, :, None] * bb_im[None] + rp_im[:, :, :, None] * bb_re[None]
    m_re = m_re.transpose(1, 0, 3, 2)
    m_im = m_im.transpose(1, 0, 3, 2)
    q_re = pw_re[1:, :, None, :] * c_re[None] - pw_im[1:, :, None, :] * c_im[None]
    q_im = pw_re[1:, :, None, :] * c_im[None] + pw_im[1:, :, None, :] * c_re[None]
    p_re = q_re.transpose(1, 3, 0, 2)
    p_im = (-q_im).transpose(1, 3, 0, 2)
    return kern, m_re, m_im, p_re, p_im, pw_re[L], pw_im[L]


def _s5_group_mats(pieces):
    kern, m_re, m_im, p_re, p_im, l_re, l_im = pieces
    return (kern[0], m_re[:, 0], m_im[:, 0], p_re[:, :, 0], p_im[:, :, 0], l_re[:, None, :], l_im[:, None, :])


def _s5_block_mats(pieces):
    kern, m_re, m_im, p_re, p_im, l_re, l_im = lax.optimization_barrier(pieces)
    L = kern.shape[0]
    GB = S5_BLOCK_GROUPS
    NB = S5_GROUPS // GB
    GC, P = S5_GROUP, S5_STATE
    taps =kern.astype(BF16).reshape(L, NB, GB * GC, GC).transpose(1, 0, 2, 3)

    def m_rows(m):
        return m.astype(BF16).reshape(NB, GB, L, GC, P).transpose(0, 2, 1, 3, 4).reshape(NB, L * GB * GC, P)

    mm = jnp.stack([m_rows(m_re), m_rows(m_im)], axis=1)
    pp = jnp.stack([p_re.astype(BF16).reshape(NB, GB * P, L * GC), p_im.astype(BF16).reshape(NB, GB * P, L * GC)],
                   axis=1)
    lam = jnp.concatenate([l_re.reshape(NB, 1, GB * P), l_im.reshape(NB, 1, GB * P)], axis=2)
    return taps, mm, pp, lam


def _s5_kernel(u_ref, t_ref, mre_ref, mim_ref, pre_ref, pim_ref, lre_ref, lim_ref, h0re_ref, h0im_ref,
               y_ref, hre_ref, him_ref, inj_re, inj_im, hs_re, hs_im, *, n_chunks, rb, gps, precision):
    def mm(a, b):
        return jnp.dot(a, b, preferred_element_type=F32, precision=precision)

    for g in range(gps):
        u = u_ref[g]
        inj_re[g] = mm(u, mre_ref[g])
        inj_im[g] = mm(u, mim_ref[g])

    lam_re = [jnp.broadcast_to(lre_ref[g], (rb, S5_STATE)) for g in range(gps)]
    lam_im = [jnp.broadcast_to(lim_ref[g], (rb, S5_STATE)) for g in range(gps)]

    def step(k, carry):
        rows = pl.ds(pl.multiple_of(k * rb, rb), rb)
        nxt = []
        for g in range(gps):
            hr, hi = carry[2 * g], carry[2 * g + 1]
            hs_re[g, rows, :] = hr
            hs_im[g, rows, :] = hi
            nxt.append(lam_re[g] * hr - lam_im[g] * hi + inj_re[g, rows, :])
            nxt.append(lam_re[g] * hi + lam_im[g] * hr + inj_im[g, rows, :])
        return tuple(nxt)

    init = []
    for g in range(gps):
        init += [h0re_ref[g], h0im_ref[g]]
    fin = lax.fori_loop(0, n_chunks, step, tuple(init))

    for g in range(gps):
        hre_ref[g] = fin[2 * g]
        him_ref[g] = fin[2 * g + 1]
        dt = u_ref.dtype
        y = (mm(u_ref[g], t_ref[g]) + mm(hs_re[g].astype(dt), pre_ref[g])
             + mm(hs_im[g].astype(dt), pim_ref[g]))
        y_ref[g] = _gelu_tanh(y).astype(y_ref.dtype)


def _s5_core(u, mats, h0_re, h0_im, n_chunks, rb, precision, y_dtype):
    G, R, W = u.shape
    P = S5_STATE
    gps = S5_GROUPS_PER_STEP
    tmat, m_re, m_im, p_re, p_im, l_re, l_im = mats

    def spec(a, b):
        return pl.BlockSpec((gps, a, b), lambda i: (i, 0, 0))

    st = jax.ShapeDtypeStruct((G, rb, P), F32)
    scr = lambda: pltpu.VMEM((gps, R, P), F32)
    return pl.pallas_call(
        functools.partial(_s5_kernel, n_chunks=n_chunks, rb=rb, gps=gps, precision=precision),
        grid=(G // gps,),
        in_specs=[spec(R, W), spec(W, W), spec(W, P), spec(W, P), spec(P, W), spec(P, W),
                  spec(1, P), spec(1, P), spec(rb, P), spec(rb, P)],
        out_specs=(spec(R, W), spec(rb, P), spec(rb, P)),
        out_shape=(jax.ShapeDtypeStruct((G, R, W), y_dtype), st, st),
        scratch_shapes=[scr(), scr(), scr(), scr()],
        compiler_params=_params("parallel"),
        name="s5_core",
    )(u, tmat, m_re, m_im, p_re, p_im, l_re, l_im, h0_re, h0_im)


def _iota2(shape):
    return lax.broadcasted_iota(jnp.int32, shape, 0), lax.broadcasted_iota(jnp.int32, shape, 1)


def _s5_expand(taps_ref, mc_ref, pc_ref, w2_ref, m_ref, p_ref):
    L, GC, P = S5_CHUNK, S5_GROUP, S5_STATE
    GB = S5_BLOCK_GROUPS
    W, HS = GB * GC, GB * P
    gc_bits, p_bits, w_bits = GC.bit_length() - 1, P.bit_length() - 1, W.bit_length() - 1

    r, c = _iota2((P, HS))
    rep_m = ((c & (P - 1)) == r).astype(BF16)
    r, c = _iota2((L * W, HS))
    mask_m = ((r >> gc_bits) & (GB - 1)) == (c >> p_bits)
    r, c = _iota2((L * GC, L * W))
    rep_p = (r == (((c >> w_bits) << gc_bits) | (c & (GC - 1)))).astype(BF16)
    r, c = _iota2((HS, L * W))
    mask_p = (r >> p_bits) == ((c >> gc_bits) & (GB - 1))
    for half in range(2):
        m_ref[:, half * HS:(half + 1) * HS] = jnp.where(
            mask_m, jnp.dot(mc_ref[half], rep_m, preferred_element_type=F32), 0.0).astype(BF16)
        p_ref[half * HS:(half + 1) * HS, :] = jnp.where(
            mask_p, jnp.dot(pc_ref[half], rep_p, preferred_element_type=F32), 0.0).astype(BF16)

    r, c = _iota2((GC, W))
    rep_k = ((c & (GC - 1)) == r).astype(BF16)
    r, c = _iota2((W, W))
    mask_k = (r >> gc_bits) == (c >> gc_bits)
    zero = jnp.zeros((W, W), BF16)
    w2_ref[L * W:, :W] = zero
    w2_ref[:W, W:] = zero
    for i in range(L):
        kb = jnp.where(mask_k, jnp.dot(taps_ref[L - 1 - i], rep_k, preferred_element_type=F32), 0.0).astype(BF16)
        w2_ref[i * W:(i + 1) * W, :W] = kb
        w2_ref[(i + 1) * W:(i + 2) * W, W:] = kb


def _s5_seq_kernel(u_ref, taps_ref, mc_ref, pc_ref, lam_ref, h0_ref, y_ref, hfin_ref,
                   w2_ref, m_ref, p_ref, lhs_scr, inj_scr, hs_scr, h_scr, *, batch, nck):
    L, W = S5_CHUNK, S5_BLOCK_GROUPS * S5_GROUP
    NT = S5_BLOCK_GROUPS * S5_STATE // W

    @pl.when(pl.program_id(1) == 0)
    def _():
        h_scr[...] = h0_ref[...]
        _s5_expand(taps_ref, mc_ref, pc_ref, w2_ref, m_ref, p_ref)

    for b in range(batch):
        for l in range(L):
            lhs_scr[b * nck:(b + 1) * nck, l * W:(l + 1) * W] = u_ref[b, pl.ds(l, nck, stride=L), :].astype(BF16)

    def swap_major(x, a, b):
        return jnp.swapaxes(x.reshape(a, b, W), 0, 1).reshape(a * b, W)

    inj = jnp.dot(lhs_scr[...], m_ref[...], preferred_element_type=F32)
    for t in range(2 * NT):
        inj_scr[t] = swap_major(inj[:, t * W:(t + 1) * W], batch, nck)

    lam = [jnp.broadcast_to(lam_ref[:, t * W:(t + 1) * W], (batch, W)) for t in range(2 * NT)]

    def step(k, h):
        rows = pl.ds(pl.multiple_of(k * batch, batch), batch)
        nxt_re, nxt_im = [], []
        for t in range(NT):
            hr, hi = h[t], h[NT + t]
            hs_scr[t, rows, :] = hr
            hs_scr[NT + t, rows, :] = hi
            nxt_re.append(lam[t] * hr - lam[NT + t] * hi + inj_scr[t, rows, :])
            nxt_im.append(lam[t] * hi + lam[NT + t] * hr + inj_scr[NT + t, rows, :])
        return tuple(nxt_re + nxt_im)

    h = lax.fori_loop(0, nck, step, tuple(h_scr[:, t * W:(t + 1) * W] for t in range(2 * NT)))
    for t in range(2 * NT):
        h_scr[:, t * W:(t + 1) * W] = h[t]
    hfin_ref[...] = h_scr[...]

    hs = jnp.concatenate([swap_major(hs_scr[t], nck, batch) for t in range(2 * NT)], axis=-1).astype(BF16)
    for pr in range(L // 2):
        kk = (2 * pr + 2) * W
        y = (jnp.dot(lhs_scr[:, :kk], w2_ref[(L - 1 - 2 * pr) * W:, :], preferred_element_type=F32)
             + jnp.dot(hs, p_ref[:, 2 * pr * W:(2 * pr + 2) * W], preferred_element_type=F32))
        y = _gelu_tanh(y)
        for s in range(2):
            for b in range(batch):
                y_ref[b, pl.ds(2 * pr + s, nck, stride=L), :] = y[b * nck:(b + 1) * nck, s * W:(s + 1) * W]


def _s5_seq(u, mats, h0, batch, seq):
    taps, mm, pp, lam = mats
    L, W = S5_CHUNK, S5_BLOCK_GROUPS * S5_GROUP
    NB = S5_GROUPS // S5_BLOCK_GROUPS
    HS = S5_BLOCK_GROUPS * S5_STATE
    SW = 2 * HS
    ts = seq // S5_TIME_SLICES
    nck = ts // L
    rows = batch * nck
    u_spec = pl.BlockSpec((batch, ts, W), lambda i, t: (0, t, i))
    h_spec = pl.BlockSpec((None, batch, SW), lambda i, t: (i, 0, 0))

    def w_spec(*dims):
        return pl.BlockSpec((None,) + dims, lambda i, t: (i,) + (0,) * len(dims))

    return pl.pallas_call(
        functools.partial(_s5_seq_kernel, batch=batch, nck=nck),
        grid=(NB, S5_TIME_SLICES),
        in_specs=[u_spec, w_spec(L, W, S5_GROUP), w_spec(2, L * W, S5_STATE), w_spec(2, HS, L * S5_GROUP),
                  w_spec(1, SW), h_spec],
        out_specs=(u_spec, h_spec),
        out_shape=(jax.ShapeDtypeStruct((batch, seq, D_MODEL), F32), jax.ShapeDtypeStruct(h0.shape, F32)),
        scratch_shapes=[pltpu.VMEM(((L + 1) * W, 2 * W), BF16), pltpu.VMEM((L * W, SW), BF16),
                        pltpu.VMEM((SW, L * W), BF16),
                        pltpu.VMEM((rows, L * W), BF16), pltpu.VMEM((SW // W, rows, W), F32),
                        pltpu.VMEM((SW // W, rows, W), F32), pltpu.VMEM((batch, SW), F32)],
        compiler_params=_params("parallel", "arbitrary"),
        name="s5_seq",
    )(u, taps, mm, pp, lam, h0)


def _s5_mixer(x, u, h0_re, h0_im, s5p, w_glu, j, batch, seq):
    G, GC, P = S5_GROUPS, S5_GROUP, S5_STATE
    if seq > 1:
        NB, HS = G // S5_BLOCK_GROUPS, S5_BLOCK_GROUPS * P
        mats = _s5_block_mats(_s5_pieces(*[p[j] for p in s5p], S5_CHUNK))
        to_blocks = lambda h: h.reshape(batch, NB, HS).transpose(1, 0, 2)
        h0 = jnp.concatenate([to_blocks(h0_re), to_blocks(h0_im)], axis=-1)
        y, hfin = _s5_seq(u.reshape(batch, seq, D_MODEL), mats, h0, batch, seq)
        y = y.reshape(batch * seq, D_MODEL)
        from_blocks = lambda h: h.transpose(1, 0, 2).reshape(batch, G, P)
        hre, him = from_blocks(hfin[..., :HS]), from_blocks(hfin[..., HS:])
    else:
        mats = _s5_group_mats(_s5_pieces(*[p[j] for p in s5p], 1))
        ug = u.reshape(batch, G, GC).transpose(1, 0, 2)
        yg, hre, him = _s5_core(ug, mats, h0_re.transpose(1, 0, 2), h0_im.transpose(1, 0, 2),
                                1, batch, lax.Precision.HIGHEST, F32)
        y = yg.transpose(1, 0, 2).reshape(batch, D_MODEL)
        hre, him = hre.transpose(1, 0, 2), him.transpose(1, 0, 2)
    x = _proj(y, w_glu, j, glu=True, res=x)
    return x, hre, him


def _hg_lower_bound(logits, layer):
    m = jnp.max(logits, axis=0, keepdims=True)
    e = jnp.exp(logits - m)
    sm = e / jnp.sum(e, axis=0, keepdims=True)
    return jnp.sum(sm[:layer + 1], axis=0, keepdims=True) - sm[0:1]


def _hg_gates(z, lb):
    e = jnp.exp(-jnp.abs(z))
    r = 1.0 / (1.0 + e)
    log_sig = jnp.minimum(z, 0.0) - jnp.log1p(e)
    a = jnp.log(lb)
    b = jnp.log1p(-lb) + log_sig
    logf = jnp.maximum(a, b) + jnp.log1p(jnp.exp(-jnp.abs(a - b)))
    k = (1.0 - lb) * jnp.where(z >= 0.0, e * r, r)
    return logf, k


def _hgrn_prompt_kernel(x_ref, g_ref, win_ref, lbl_ref, ng_ref, wout_ref, y_ref, sfin_ref,
                        s_scr, q_scr, k_scr, v_scr, gt_scr, o_scr, beta_scr, safe_scr, *, layer, tt):
    t = pl.program_id(1)
    C = HG_CHUNK

    @pl.when(t == 0)
    def _():
        s_scr[...] = jnp.zeros_like(s_scr)

    x = x_ref[...]
    h = _rms(x, g_ref[...]).astype(BF16)
    proj = jnp.dot(h, win_ref[...], preferred_element_type=F32)
    lb = _hg_lower_bound(lbl_ref[...], layer)
    logf, kk = _hg_gates(proj[:, HG_WIDTH:2 * HG_WIDTH], lb)
    q_scr[...] = _silu(proj[:, :HG_WIDTH])
    k_scr[...] = kk
    v_scr[...] = proj[:, 2 * HG_WIDTH:3 * HG_WIDTH]
    gt_scr[...] = _silu(proj[:, 3 * HG_WIDTH:])

    row, col = _iota2((C, C))
    causal = row >= col
    tri = causal.astype(BF16)
    ng = ng_ref[...]
    nt_dims = (((1,), (1,)), ((), ()))
    tn_dims = (((0,), (0,)), ((), ()))

    def finish_head(hh, rows, q_dec, k_dec, o_intra, btot_h):
        sv = slice(hh * HG_DV, (hh + 1) * HG_DV)
        st = s_scr[hh]
        o = o_intra + lax.dot_general(q_dec.astype(BF16), st.astype(BF16), nt_dims, preferred_element_type=F32)
        kv_t = lax.dot_general(v_scr[rows, sv].astype(BF16), k_dec.astype(BF16), tn_dims,
                               preferred_element_type=F32)
        return (_rms(o, ng) * gt_scr[rows, sv]).astype(BF16), st * jnp.exp(btot_h) + kv_t

    def store_heads(rows, results):
        o_scr[rows, :] = jnp.concatenate([o for o, _ in results], axis=-1)
        for hh, (_, s_new) in enumerate(results):
            s_scr[hh] = s_new

    for ci in range(tt // C):
        parts = _split3(logf[ci * C:(ci + 1) * C])
        beta = sum(jnp.dot(tri, p, preferred_element_type=F32) for p in parts)
        beta_scr[ci * C:(ci + 1) * C, :] = beta
        mid = beta[C // 2 - 1:C // 2, :]
        spread = jnp.maximum(jnp.max(-mid), jnp.max(mid - beta[C - 1:C, :]))
        safe_scr[ci] = (spread <= HG_FACTORED_MAX_DECAY).astype(jnp.int32)

    def chunk_step(c, carry):
        rows = pl.ds(pl.multiple_of(c * C, C), C)
        base = pl.multiple_of(c * C, C)
        btot = beta_scr[pl.ds(base + (C - 8), 8), :][7:8]
        mid = beta_scr[pl.ds(base + (C // 2 - 8), 8), :][7:8]
        safe = safe_scr[c] == 1

        @pl.when(safe)
        def _():
            e_mid = jnp.exp(mid)
            e_tot = jnp.exp(btot - mid)
            results = []
            for hh in range(HG_HEADS):
                sk = slice(hh * HG_DK, (hh + 1) * HG_DK)
                d = beta_scr[rows, sk] - mid[:, sk]
                q_mid = q_scr[rows, sk] * jnp.exp(d)
                k_mid = k_scr[rows, sk] * jnp.exp(-d)
                att = lax.dot_general(q_mid.astype(BF16), k_mid.astype(BF16), nt_dims, preferred_element_type=F32)
                att = jnp.where(causal, att, 0.0).astype(BF16)
                o_intra = jnp.dot(att, v_scr[rows, hh * HG_DV:(hh + 1) * HG_DV].astype(BF16),
                                  preferred_element_type=F32)
                results.append(finish_head(hh, rows, q_mid * e_mid[:, sk], k_mid * e_tot[:, sk], o_intra,
                                           btot[:, sk]))
            store_heads(rows, results)

        @pl.when(jnp.logical_not(safe))
        def _():
            t_idx = lax.broadcasted_iota(jnp.int32, (C, 1), 0)
            results = []
            for hh in range(HG_HEADS):
                sk = slice(hh * HG_DK, (hh + 1) * HG_DK)
                sv = slice(hh * HG_DV, (hh + 1) * HG_DV)
                b = beta_scr[rows, sk]
                q = q_scr[rows, sk]

                def key_step(s8, acc):
                    off = pl.multiple_of(s8 * 8, 8)
                    b_keys = beta_scr[pl.ds(base + off, 8), sk]
                    k_keys = k_scr[pl.ds(base + off, 8), sk]
                    v_keys = v_scr[pl.ds(base + off, 8), sv]
                    for i in range(8):
                        w = jnp.exp(jnp.minimum(b - b_keys[i:i + 1], 0.0))
                        a = jnp.sum(q * k_keys[i:i + 1] * w, axis=-1, keepdims=True)
                        a = jnp.where(t_idx >= off + i, a, 0.0)
                        acc = acc + a * v_keys[i:i + 1]
                    return acc

                o_intra = lax.fori_loop(0, C // 8, key_step, jnp.zeros((C, HG_DV), F32))
                results.append(finish_head(hh, rows, q * jnp.exp(b), k_scr[rows, sk] * jnp.exp(btot[:, sk] - b),
                                           o_intra, btot[:, sk]))
            store_heads(rows, results)

        return carry

    lax.fori_loop(0, tt // C, chunk_step, 0)
    y_ref[...] = x + jnp.dot(o_scr[...], wout_ref[...], preferred_element_type=F32)

    @pl.when(t == pl.num_programs(1) - 1)
    def _():
        for hh in range(HG_HEADS):
            sfin_ref[hh] = s_scr[hh].T


def _hgrn_prompt(x, g, w_in, lb_logits, norm_g, w_out, layer, j, batch, seq):
    tt = HG_ROW_TILE
    nt = seq // tt
    x3 = x.reshape(batch, seq, D_MODEL)
    row_spec = pl.BlockSpec((None, tt, D_MODEL), lambda b, t: (b, t, 0))
    y, s_fin = pl.pallas_call(
        functools.partial(_hgrn_prompt_kernel, layer=layer, tt=tt),
        grid=(batch, nt),
        in_specs=[row_spec,
                  pl.BlockSpec((None, 1, D_MODEL), lambda b, t: (layer, 0, 0)),
                  _resident((None, D_MODEL, 4 * HG_WIDTH), lambda b, t: (j, 0, 0)),
                  pl.BlockSpec((DEPTH, HG_WIDTH), lambda b, t: (0, 0)),
                  pl.BlockSpec((None, 1, HG_DV), lambda b, t: (j, 0, 0)),
                  _resident((None, HG_WIDTH, D_MODEL), lambda b, t: (j, 0, 0))],
        out_specs=(row_spec,
                   pl.BlockSpec((None, HG_HEADS, HG_DK, HG_DV), lambda b, t: (b, 0, 0, 0))),
        out_shape=(jax.ShapeDtypeStruct((batch, seq, D_MODEL), F32),
                   jax.ShapeDtypeStruct((batch, HG_HEADS, HG_DK, HG_DV), F32)),
        scratch_shapes=[pltpu.VMEM((HG_HEADS, HG_DV, HG_DK), F32),
                        pltpu.VMEM((tt, HG_WIDTH), F32), pltpu.VMEM((tt, HG_WIDTH), F32),
                        pltpu.VMEM((tt, HG_WIDTH), F32), pltpu.VMEM((tt, HG_WIDTH), F32),
                        pltpu.VMEM((tt, HG_WIDTH), BF16), pltpu.VMEM((tt, HG_WIDTH), F32),
                        pltpu.SMEM((tt // HG_CHUNK,), jnp.int32)],
        compiler_params=_params("parallel", "arbitrary"),
        name="hgrn_prompt",
    )(x3, g.reshape(DEPTH, 1, D_MODEL), w_in, lb_logits, norm_g.reshape(-1, 1, HG_DV), w_out)
    return y.reshape(batch * seq, D_MODEL), s_fin


def _hgrn_sample_kernel(proj_ref, lbl_ref, ng_ref, s_ref, snew_ref, o_ref, *, layer, tb):
    proj = proj_ref[...]
    lb = _hg_lower_bound(lbl_ref[...], layer)
    z = proj[:, HG_WIDTH:2 * HG_WIDTH]
    e = jnp.exp(-jnp.abs(z))
    r = 1.0 / (1.0 + e)
    sig = jnp.where(z >= 0.0, r, e * r)
    f = lb + (1.0 - lb) * sig
    k = (1.0 - lb) * jnp.where(z >= 0.0, e * r, r)
    q = _silu(proj[:, :HG_WIDTH])
    v = proj[:, 2 * HG_WIDTH:3 * HG_WIDTH]
    gt = _silu(proj[:, 3 * HG_WIDTH:])
    ng = ng_ref[...]
    tok, lane = _iota2((tb, tb * HG_DV))
    spread = ((lane // HG_DV) == tok).astype(BF16)

    spread3 = jnp.concatenate([spread] * 3, axis=0)

    def columns(x, exact):
        if exact:
            return jnp.dot(jnp.concatenate(_split3(x.T), axis=1), spread3, preferred_element_type=F32)
        return jnp.dot(x.T.astype(BF16), spread, preferred_element_type=F32)

    for hh in range(HG_HEADS):
        sk = slice(hh * HG_DK, (hh + 1) * HG_DK)
        sv = slice(hh * HG_DV, (hh + 1) * HG_DV)
        f_c, k_c, q_c = columns(f[:, sk], True), columns(k[:, sk], False), columns(q[:, sk], False)
        for b in range(tb):
            blk = slice(b * HG_DV, (b + 1) * HG_DV)
            s_new = f_c[:, blk] * s_ref[b, hh] + k_c[:, blk] * v[b:b + 1, sv]
            snew_ref[b, hh] = s_new
            o = jnp.sum(q_c[:, blk] * s_new, axis=0, keepdims=True)
            o_ref[b:b + 1, sv] = _rms(o, ng) * gt[b:b + 1, sv]


def _hgrn_sample(proj, lb_logits, norm_g, state, layer, j):
    nb = proj.shape[0]
    tb = SAMPLE_TOKENS_PER_STEP
    st_spec = pl.BlockSpec((tb, HG_HEADS, HG_DK, HG_DV), lambda i: (i, 0, 0, 0))
    return pl.pallas_call(
        functools.partial(_hgrn_sample_kernel, layer=layer, tb=tb),
        grid=(nb // tb,),
        in_specs=[pl.BlockSpec((tb, 4 * HG_WIDTH), lambda i: (i, 0)),
                  pl.BlockSpec((DEPTH, HG_WIDTH), lambda i: (0, 0)),
                  pl.BlockSpec((None, 1, HG_DV), lambda i: (j, 0, 0)),
                  st_spec],
        out_specs=(st_spec, pl.BlockSpec((tb, HG_WIDTH), lambda i: (i, 0))),
        out_shape=(jax.ShapeDtypeStruct(state.shape, F32), jax.ShapeDtypeStruct((nb, HG_WIDTH), F32)),
        compiler_params=_params("parallel"),
        name="hgrn_sample",
    )(proj, lb_logits, norm_g.reshape(-1, 1, HG_DV), state)


def _xattn_prompt_kernel(x_ref, g_ref, wq_ref, k_ref, v_ref, wo_ref, y_ref):
    x = x_ref[...]
    h = _rms(x, g_ref[...]).astype(BF16)
    q = jnp.dot(h, wq_ref[...], preferred_element_type=F32) * (1.0 / math.sqrt(MEM_HD))
    q = q.astype(BF16)
    outs = []
    for hh in range(MEM_HEADS):
        sl = slice(hh * MEM_HD, (hh + 1) * MEM_HD)
        s = lax.dot_general(q[:, sl], k_ref[:, sl], (((1,), (1,)), ((), ())), preferred_element_type=F32)
        p = jnp.exp(s - jnp.max(s, axis=-1, keepdims=True))
        den = jnp.sum(p, axis=-1, keepdims=True)
        o = jnp.dot(p.astype(BF16), v_ref[:, sl], preferred_element_type=F32)
        outs.append((o / den).astype(BF16))
    o = jnp.concatenate(outs, axis=-1)
    y_ref[...] = x + jnp.dot(o, wo_ref[...], preferred_element_type=F32)


def _xattn_prompt(x, g, w_q, mem_k, mem_v, w_o, layer, batch, seq):
    tt = XA_ROW_TILE
    row_spec = pl.BlockSpec((None, tt, D_MODEL), lambda b, t: (b, t, 0))
    kv_spec = pl.BlockSpec((None, N_MEM, D_MODEL), lambda b, t: (layer, b, 0))
    w_spec = _resident((None, D_MODEL, D_MODEL), lambda b, t: (layer, 0, 0))
    y = pl.pallas_call(
        _xattn_prompt_kernel,
        grid=(batch, seq // tt),
        in_specs=[row_spec, pl.BlockSpec((None, 1, D_MODEL), lambda b, t: (layer, 0, 0)),
                  w_spec, kv_spec, kv_spec, w_spec],
        out_specs=row_spec,
        out_shape=jax.ShapeDtypeStruct((batch, seq, D_MODEL), F32),
        compiler_params=_params("parallel", "parallel"),
        name="xattn_prompt",
    )(x.reshape(batch, seq, D_MODEL), g.reshape(DEPTH, 1, D_MODEL), w_q, mem_k, mem_v, w_o)
    return y.reshape(batch * seq, D_MODEL)


def _xattn_sample_kernel(q_ref, k_ref, v_ref, o_ref, *, tb):
    scale = 1.0 / math.sqrt(MEM_HD)
    rows = N_MEM * MEM_HEADS
    head, lane = _iota2((MEM_HEADS, rows))
    own = (lane & (MEM_HEADS - 1)) == head
    for b in range(tb):
        q = (q_ref[b] * scale).astype(BF16)
        k = k_ref[b].reshape(rows, MEM_HD).astype(BF16)
        v = v_ref[b].reshape(rows, MEM_HD).astype(BF16)
        s = lax.dot_general(q, k, (((1,), (1,)), ((), ())), preferred_element_type=F32)
        s = jnp.where(own, s, -jnp.inf)
        p = jnp.exp(s - jnp.max(s, axis=-1, keepdims=True))
        den = jnp.sum(p, axis=-1, keepdims=True)
        o_ref[b] = jnp.dot(p.astype(BF16), v, preferred_element_type=F32) / den


def _xattn_sample(q, cache_k, cache_v, layer):
    nb = q.shape[0]
    tb = XA_SAMPLE_TOKENS_PER_STEP
    kv_spec = pl.BlockSpec((None, tb, N_MEM, MEM_HEADS, MEM_HD), lambda i: (layer, i, 0, 0, 0))
    q_spec = pl.BlockSpec((tb, MEM_HEADS, MEM_HD), lambda i: (i, 0, 0))
    return pl.pallas_call(
        functools.partial(_xattn_sample_kernel, tb=tb),
        grid=(nb // tb,),
        in_specs=[q_spec, kv_spec, kv_spec],
        out_specs=q_spec,
        out_shape=jax.ShapeDtypeStruct((nb, MEM_HEADS, MEM_HD), F32),
        compiler_params=_params("parallel"),
        name="xattn_sample",
    )(q.reshape(nb, MEM_HEADS, MEM_HD), cache_k, cache_v).reshape(nb, D_MODEL)


def _trunk(x, batch, seq, s5_re, s5_im, hg_state, mem_k, mem_v, w):
    prompt = seq > 1
    new_re, new_im, new_hg = [], [], []
    for i in range(DEPTH):
        j = i // 2
        if i % 2 == 0:
            x, u = _ffn(x, w["ffn1_norm"], w["ffn1_w_in"], w["ffn1_w_out"], i, post="emit",
                        g2=w["mix_norm"][i], u_dtype=F32)
            x, hr, hi = _s5_mixer(x, u, s5_re[j], s5_im[j], w["s5"], w["s5_w_glu"], j, batch, seq)
            new_re.append(hr)
            new_im.append(hi)
        else:
            x = _ffn(x, w["ffn1_norm"], w["ffn1_w_in"], w["ffn1_w_out"], i)
            if prompt:
                x, sn = _hgrn_prompt(x, w["mix_norm"], w["hg_w_in"], w["hg_lb_logits"], w["hg_norm"],
                                     w["hg_w_out"], i, j, batch, seq)
            else:
                proj = _proj(x, w["hg_w_in"], j, g=w["mix_norm"], g_layer=i)
                sn, o = _hgrn_sample(proj, w["hg_lb_logits"], w["hg_norm"], hg_state[j], i, j)
                x = _proj(o, w["hg_w_out"], j, res=x)
            new_hg.append(sn)
        if prompt:
            x = _xattn_prompt(x, w["xattn_norm"], w["xattn_w_q"], mem_k, mem_v, w["xattn_w_o"], i, batch, seq)
        else:
            q = _proj(x, w["xattn_w_q"], i, g=w["xattn_norm"], g_layer=i)
            o = _xattn_sample(q, mem_k, mem_v, i)
            x = _proj(o, w["xattn_w_o"], i, res=x)
        last = i == DEPTH - 1
        x = _ffn(x, w["ffn2_norm"], w["ffn2_w_in"], w["ffn2_w_out"], i,
                 post="replace" if last else "none", g2=w["final_norm"] if last else None)
    return x, jnp.stack(new_re), jnp.stack(new_im), jnp.stack(new_hg)


def kernel(x_prompt, x_sample, mem_prompt, state_s5_re, state_s5_im, state_hgrn, cache_mem_k, cache_mem_v, ffn1_norm, ffn1_w_in, ffn1_w_out, mix_norm, xattn_norm, mem_norm, xattn_w_q, xattn_w_kv, xattn_w_o, ffn2_norm, ffn2_w_in, ffn2_w_out, s5_a_re, s5_a_im, s5_log_dt, s5_b_re, s5_b_im, s5_c_re, s5_c_im, s5_d, s5_w_glu, hg_w_in, hg_lb_logits, hg_norm, hg_w_out, final_norm):
    bp, seq, _ = x_prompt.shape
    bs = x_sample.shape[0]
    bf = lambda a: a.astype(BF16)
    w = dict(ffn1_norm=ffn1_norm, ffn1_w_in=bf(ffn1_w_in), ffn1_w_out=bf(ffn1_w_out), mix_norm=mix_norm,
             xattn_norm=xattn_norm, xattn_w_q=bf(xattn_w_q), xattn_w_o=bf(xattn_w_o), ffn2_norm=ffn2_norm,
             ffn2_w_in=bf(ffn2_w_in), ffn2_w_out=bf(ffn2_w_out),
             s5=(s5_a_re, s5_a_im, s5_log_dt, s5_b_re, s5_b_im, s5_c_re, s5_c_im, s5_d),
             s5_w_glu=bf(s5_w_glu), hg_w_in=bf(hg_w_in), hg_lb_logits=hg_lb_logits, hg_norm=hg_norm,
             hg_w_out=bf(hg_w_out), final_norm=final_norm)

    mem_k, mem_v, mem_k_bf, mem_v_bf = _mem_kv(mem_prompt, mem_norm, bf(xattn_w_kv))

    n_s5 = state_s5_re.shape[0]
    z_s5 = jnp.zeros((n_s5, bp, S5_GROUPS, S5_STATE), F32)
    y_p, re_p, im_p, hg_p = _trunk(x_prompt.reshape(bp * seq, D_MODEL), bp, seq, z_s5, z_s5, None,
                                   mem_k_bf, mem_v_bf, w)
    y_s, re_s, im_s, hg_s = _trunk(x_sample.reshape(bs, D_MODEL), bs, 1, state_s5_re, state_s5_im, state_hgrn,
                                   cache_mem_k, cache_mem_v, w)
    return (y_p.reshape(bp, seq, D_MODEL), y_s.reshape(bs, 1, D_MODEL), re_p, im_p, re_s, im_s, hg_p, hg_s,
            mem_k, mem_v)
```

```python
import functools
import math

import jax
import jax.numpy as jnp
from jax import lax
from jax.experimental import pallas as pl
from jax.experimental.pallas import tpu as pltpu

F32 = jnp.float32
BF16 = jnp.bfloat16

D_MODEL = 1024
DEPTH = 2
S5_GROUP = 16
S5_GROUPS = D_MODEL // S5_GROUP
S5_STATE = 64
S5_CHUNK = 16
S5_BLOCK_GROUPS = 8
S5_TIME_SLICES = 2
HG_DK = 128
HG_HEADS = D_MODEL // HG_DK
HG_DV = D_MODEL // HG_HEADS
HG_WIDTH = HG_HEADS * HG_DK
HG_CHUNK = 128
HG_FACTORED_MAX_DECAY = 60.0
N_MEM = 256
MEM_HEADS = 4
MEM_HD = D_MODEL // MEM_HEADS
FFN_DIM = 2816
EPS = 1e-6

V7X_VMEM_LIMIT_BYTES = 56 * 1024 * 1024

ROW_TILE = 512
FFN_CHUNK = 1408
HG_ROW_TILE = 512
XA_ROW_TILE = 512
S5_GROUPS_PER_STEP = 16
SAMPLE_TOKENS_PER_STEP = 8
XA_SAMPLE_TOKENS_PER_STEP = 4


def _params(*semantics):
    return pltpu.CompilerParams(dimension_semantics=semantics,
                                vmem_limit_bytes=V7X_VMEM_LIMIT_BYTES)


def _resident(shape, index_map):
    return pl.BlockSpec(shape, index_map, pipeline_mode=pl.Buffered(1))


def _rms(x, g):
    ms = jnp.mean(x * x, axis=-1, keepdims=True)
    return x * lax.rsqrt(ms + EPS) * g


def _sigmoid(x):
    return 1.0 / (1.0 + jnp.exp(-x))


def _silu(x):
    return x * _sigmoid(x)


def _gelu_tanh(x):
    c = math.sqrt(2.0 / math.pi)
    return 0.5 * x * (1.0 + jnp.tanh(c * (x + 0.044715 * (x * x * x))))


def _split3(x):
    hi = x.astype(BF16)
    r1 = x - hi.astype(F32)
    mid = r1.astype(BF16)
    lo = (r1 - mid.astype(F32)).astype(BF16)
    return hi, mid, lo


def _ffn_kernel(x_ref, g_ref, win_ref, wout_ref, *rest, post):
    x = x_ref[...]
    h = _rms(x, g_ref[...]).astype(BF16)
    acc = jnp.zeros_like(x)
    for c in range(FFN_DIM // FFN_CHUNK):
        lo = c * FFN_CHUNK
        gate = jnp.dot(h, win_ref[:, lo:lo + FFN_CHUNK], preferred_element_type=F32)
        up = jnp.dot(h, win_ref[:, FFN_DIM + lo:FFN_DIM + lo + FFN_CHUNK], preferred_element_type=F32)
        act = (_silu(gate) * up).astype(BF16)
        acc = acc + jnp.dot(act, wout_ref[lo:lo + FFN_CHUNK, :], preferred_element_type=F32)
    y = x + 0.5 * acc
    if post == "none":
        (y_ref,) = rest
        y_ref[...] = y
    elif post == "replace":
        g2_ref, y_ref = rest
        y_ref[...] = _rms(y, g2_ref[...])
    else:
        g2_ref, y_ref, u_ref = rest
        y_ref[...] = y
        u_ref[...] = _rms(y, g2_ref[...]).astype(u_ref.dtype)


def _ffn(x, g, w_in, w_out, layer, post="none", g2=None, u_dtype=BF16):
    rows = x.shape[0]
    tm = min(ROW_TILE, rows)
    row_spec = pl.BlockSpec((tm, D_MODEL), lambda i: (i, 0))
    vec_spec = pl.BlockSpec((1, D_MODEL), lambda i: (0, 0))
    in_specs = [row_spec,
                pl.BlockSpec((None, 1, D_MODEL), lambda i: (layer, 0, 0)),
                _resident((None, D_MODEL, 2 * FFN_DIM), lambda i: (layer, 0, 0)),
                _resident((None, FFN_DIM, D_MODEL), lambda i: (layer, 0, 0))]
    args = [x, g.reshape(DEPTH, 1, D_MODEL), w_in, w_out]
    out_shape = jax.ShapeDtypeStruct((rows, D_MODEL), F32)
    out_specs = row_spec
    if post != "none":
        in_specs.append(vec_spec)
        args.append(g2.reshape(1, D_MODEL))
    if post == "emit":
        out_shape = (out_shape, jax.ShapeDtypeStruct((rows, D_MODEL), u_dtype))
        out_specs = (row_spec, row_spec)
    return pl.pallas_call(
        functools.partial(_ffn_kernel, post=post),
        grid=(rows // tm,),
        in_specs=in_specs, out_specs=out_specs, out_shape=out_shape,
        compiler_params=_params("parallel"),
        name="ffn",
    )(*args)


def _proj_kernel(*refs, norm, glu, residual):
    refs = list(refs)
    x_ref = refs.pop(0)
    g_ref = refs.pop(0) if norm else None
    w_ref = refs.pop(0)
    res_ref = refs.pop(0) if residual else None
    (o_ref,) = refs
    x = x_ref[...]
    if norm:
        x = _rms(x, g_ref[...])
    y = jnp.dot(x.astype(BF16), w_ref[...], preferred_element_type=F32)
    if glu:
        half = y.shape[-1] // 2
        y = y[:, :half] * _sigmoid(y[:, half:])
    if residual:
        y = y + res_ref[...]
    o_ref[...] = y


def _proj(x, w, layer, g=None, g_layer=0, glu=False, res=None):
    rows, kdim = x.shape
    ndim = w.shape[-1]
    nout = ndim // 2 if glu else ndim
    tm = min(ROW_TILE, rows)
    in_specs = [pl.BlockSpec((tm, kdim), lambda i: (i, 0))]
    args = [x]
    if g is not None:
        in_specs.append(pl.BlockSpec((None, 1, kdim), lambda i: (g_layer, 0, 0)))
        args.append(g.reshape(g.shape[0], 1, kdim))
    in_specs.append(_resident((None, kdim, ndim), lambda i: (layer, 0, 0)))
    args.append(w)
    if res is not None:
        in_specs.append(pl.BlockSpec((tm, nout), lambda i: (i, 0)))
        args.append(res)
    return pl.pallas_call(
        functools.partial(_proj_kernel, norm=g is not None, glu=glu, residual=res is not None),
        grid=(rows // tm,),
        in_specs=in_specs,
        out_specs=pl.BlockSpec((tm, nout), lambda i: (i, 0)),
        out_shape=jax.ShapeDtypeStruct((rows, nout), F32),
        compiler_params=_params("parallel"),
        name="proj",
    )(*args)


def _memkv_kernel(x_ref, g_ref, w_ref, k_ref, v_ref, kb_ref, vb_ref):
    h = _rms(x_ref[...], g_ref[...]).astype(BF16)
    y = jnp.dot(h, w_ref[...], preferred_element_type=F32)
    k, v = y[:, :D_MODEL], y[:, D_MODEL:]
    k_ref[...] = k.reshape(k_ref.shape)
    v_ref[...] = v.reshape(v_ref.shape)
    kb_ref[...] = k.astype(BF16)
    vb_ref[...] = v.astype(BF16)


def _mem_kv(mem, g, w_kv):
    batch = mem.shape[0]
    rows = batch * N_MEM
    nb = max(1, min(ROW_TILE, rows) // N_MEM)
    tm = nb * N_MEM
    out5 = jax.ShapeDtypeStruct((DEPTH, batch, N_MEM, MEM_HEADS, MEM_HD), F32)
    out2 = jax.ShapeDtypeStruct((DEPTH, rows, D_MODEL), BF16)
    spec5 = pl.BlockSpec((None, nb, N_MEM, MEM_HEADS, MEM_HD), lambda l, i: (l, i, 0, 0, 0))
    spec2 = pl.BlockSpec((None, tm, D_MODEL), lambda l, i: (l, i, 0))
    return pl.pallas_call(
        _memkv_kernel,
        grid=(DEPTH, rows // tm),
        in_specs=[pl.BlockSpec((tm, D_MODEL), lambda l, i: (i, 0)),
                  pl.BlockSpec((None, 1, D_MODEL), lambda l, i: (l, 0, 0)),
                  pl.BlockSpec((None, D_MODEL, 2 * D_MODEL), lambda l, i: (l, 0, 0))],
        out_specs=(spec5, spec5, spec2, spec2), out_shape=(out5, out5, out2, out2),
        compiler_params=_params("parallel", "parallel"),
        name="mem_kv",
    )(mem.reshape(rows, D_MODEL), g.reshape(DEPTH, 1, D_MODEL), w_kv)


def _s5_pieces(a_re, a_im, log_dt, b_re, b_im, c_re, c_im, d, steps):
    G, P, GC = S5_GROUPS, S5_STATE, S5_GROUP
    L = steps
    dt = jnp.exp(log_dt)[:, None]
    xr, xi = a_re * dt, a_im * dt
    j = jnp.arange(L + 1, dtype=F32)[:, None, None]
    mag = jnp.exp(xr[None] * j)
    pw_re, pw_im = mag * jnp.cos(xi[None] * j), mag * jnp.sin(xi[None] * j)
    nr, ni = pw_re[1] - 1.0, pw_im[1]
    den = a_re * a_re + a_im * a_im
    fr, fi = (nr * a_re + ni * a_im) / den, (ni * a_re - nr * a_im) / den
    bb_re = fr[..., None] * b_re - fi[..., None] * b_im
    bb_im = fr[..., None] * b_im + fi[..., None] * b_re
    hp = lax.Precision.HIGHEST
    w_re = pw_re[:L, :, None, :] * c_re[None] - pw_im[:L, :, None, :] * c_im[None]
    w_im = pw_re[:L, :, None, :] * c_im[None] + pw_im[:L, :, None, :] * c_re[None]
    kern = (jnp.einsum("gpi,lgop->lgio", bb_re, w_re, precision=hp)
            - jnp.einsum("gpi,lgop->lgio", bb_im, w_im, precision=hp))
    kern = kern.at[0].add(d[:, :, None] * jnp.eye(GC, dtype=F32)[None])
    rp_re, rp_im = pw_re[:L][::-1], pw_im[:L][::-1]
    m_re = rp_re[:, :, :, None] * bb_re[None] - rp_im[:, :, :, None] * bb_im[None]
    m_im = rp_re[:, :, :, None] * bb_im[None] + rp_im[:, :, :, None] * bb_re[None]
    m_re = m_re.transpose(1, 0, 3, 2)
    m_im = m_im.transpose(1, 0, 3, 2)
    q_re = pw_re[1:, :, None, :] * c_re[None] - pw_im[1:, :, None, :] * c_im[None]
    q_im = pw_re[1:, :, None, :] * c_im[None] + pw_im[1:, :, None, :] * c_re[None]
    p_re = q_re.transpose(1, 3, 0, 2)
    p_im = (-q_im).transpose(1, 3, 0, 2)
    return kern, m_re, m_im, p_re, p_im, pw_re[L], pw_im[L]


def _s5_group_mats(pieces):
    kern, m_re, m_im, p_re, p_im, l_re, l_im = pieces
    return (kern[0], m_re[:, 0], m_im[:, 0], p_re[:, :, 0], p_im[:, :, 0], l_re[:, None, :], l_im[:, None, :])


def _s5_block_mats(pieces):
    kern, m_re, m_im, p_re, p_im, l_re, l_im = lax.optimization_barrier(pieces)
    L = kern.shape[0]
    GB = S5_BLOCK_GROUPS
    NB = S5_GROUPS // GB
    GC, P = S5_GROUP, S5_STATE
    taps =kern.astype(BF16).reshape(L, NB, GB * GC, GC).transpose(1, 0, 2, 3)

    def m_rows(m):
        return m.astype(BF16).reshape(NB, GB, L, GC, P).transpose(0, 2, 1, 3, 4).reshape(NB, L * GB * GC, P)

    mm = jnp.stack([m_rows(m_re), m_rows(m_im)], axis=1)
    pp = jnp.stack([p_re.astype(BF16).reshape(NB, GB * P, L * GC), p_im.astype(BF16).reshape(NB, GB * P, L * GC)],
                   axis=1)
    lam = jnp.concatenate([l_re.reshape(NB, 1, GB * P), l_im.reshape(NB, 1, GB * P)], axis=2)
    return taps, mm, pp, lam


def _s5_kernel(u_ref, t_ref, mre_ref, mim_ref, pre_ref, pim_ref, lre_ref, lim_ref, h0re_ref, h0im_ref,
               y_ref, hre_ref, him_ref, inj_re, inj_im, hs_re, hs_im, *, n_chunks, rb, gps, precision):
    def mm(a, b):
        return jnp.dot(a, b, preferred_element_type=F32, precision=precision)

    for g in range(gps):
        u = u_ref[g]
        inj_re[g] = mm(u, mre_ref[g])
        inj_im[g] = mm(u, mim_ref[g])

    lam_re = [jnp.broadcast_to(lre_ref[g], (rb, S5_STATE)) for g in range(gps)]
    lam_im = [jnp.broadcast_to(lim_ref[g], (rb, S5_STATE)) for g in range(gps)]

    def step(k, carry):
        rows = pl.ds(pl.multiple_of(k * rb, rb), rb)
        nxt = []
        for g in range(gps):
            hr, hi = carry[2 * g], carry[2 * g + 1]
            hs_re[g, rows, :] = hr
            hs_im[g, rows, :] = hi
            nxt.append(lam_re[g] * hr - lam_im[g] * hi + inj_re[g, rows, :])
            nxt.append(lam_re[g] * hi + lam_im[g] * hr + inj_im[g, rows, :])
        return tuple(nxt)

    init = []
    for g in range(gps):
        init += [h0re_ref[g], h0im_ref[g]]
    fin = lax.fori_loop(0, n_chunks, step, tuple(init))

    for g in range(gps):
        hre_ref[g] = fin[2 * g]
        him_ref[g] = fin[2 * g + 1]
        dt = u_ref.dtype
        y = (mm(u_ref[g], t_ref[g]) + mm(hs_re[g].astype(dt), pre_ref[g])
             + mm(hs_im[g].astype(dt), pim_ref[g]))
        y_ref[g] = _gelu_tanh(y).astype(y_ref.dtype)


def _s5_core(u, mats, h0_re, h0_im, n_chunks, rb, precision, y_dtype):
    G, R, W = u.shape
    P = S5_STATE
    gps = S5_GROUPS_PER_STEP
    tmat, m_re, m_im, p_re, p_im, l_re, l_im = mats

    def spec(a, b):
        return pl.BlockSpec((gps, a, b), lambda i: (i, 0, 0))

    st = jax.ShapeDtypeStruct((G, rb, P), F32)
    scr = lambda: pltpu.VMEM((gps, R, P), F32)
    return pl.pallas_call(
        functools.partial(_s5_kernel, n_chunks=n_chunks, rb=rb, gps=gps, precision=precision),
        grid=(G // gps,),
        in_specs=[spec(R, W), spec(W, W), spec(W, P), spec(W, P), spec(P, W), spec(P, W),
                  spec(1, P), spec(1, P), spec(rb, P), spec(rb, P)],
        out_specs=(spec(R, W), spec(rb, P), spec(rb, P)),
        out_shape=(jax.ShapeDtypeStruct((G, R, W), y_dtype), st, st),
        scratch_shapes=[scr(), scr(), scr(), scr()],
        compiler_params=_params("parallel"),
        name="s5_core",
    )(u, tmat, m_re, m_im, p_re, p_im, l_re, l_im, h0_re, h0_im)


def _iota2(shape):
    return lax.broadcasted_iota(jnp.int32, shape, 0), lax.broadcasted_iota(jnp.int32, shape, 1)


def _s5_expand(taps_ref, mc_ref, pc_ref, w2_ref, m_ref, p_ref):
    L, GC, P = S5_CHUNK, S5_GROUP, S5_STATE
    GB = S5_BLOCK_GROUPS
    W, HS = GB * GC, GB * P
    gc_bits, p_bits, w_bits = GC.bit_length() - 1, P.bit_length() - 1, W.bit_length() - 1

    r, c = _iota2((P, HS))
    rep_m = ((c & (P - 1)) == r).astype(BF16)
    r, c = _iota2((L * W, HS))
    mask_m = ((r >> gc_bits) & (GB - 1)) == (c >> p_bits)
    r, c = _iota2((L * GC, L * W))
    rep_p = (r == (((c >> w_bits) << gc_bits) | (c & (GC - 1)))).astype(BF16)
    r, c = _iota2((HS, L * W))
    mask_p = (r >> p_bits) == ((c >> gc_bits) & (GB - 1))
    for half in range(2):
        m_ref[:, half * HS:(half + 1) * HS] = jnp.where(
            mask_m, jnp.dot(mc_ref[half], rep_m, preferred_element_type=F32), 0.0).astype(BF16)
        p_ref[half * HS:(half + 1) * HS, :] = jnp.where(
            mask_p, jnp.dot(pc_ref[half], rep_p, preferred_element_type=F32), 0.0).astype(BF16)

    r, c = _iota2((GC, W))
    rep_k = ((c & (GC - 1)) == r).astype(BF16)
    r, c = _iota2((W, W))
    mask_k = (r >> gc_bits) == (c >> gc_bits)
    zero = jnp.zeros((W, W), BF16)
    w2_ref[L * W:, :W] = zero
    w2_ref[:W, W:] = zero
    for i in range(L):
        kb = jnp.where(mask_k, jnp.dot(taps_ref[L - 1 - i], rep_k, preferred_element_type=F32), 0.0).astype(BF16)
        w2_ref[i * W:(i + 1) * W, :W] = kb
        w2_ref[(i + 1) * W:(i + 2) * W, W:] = kb


def _s5_seq_kernel(u_ref, taps_ref, mc_ref, pc_ref, lam_ref, h0_ref, y_ref, hfin_ref,
                   w2_ref, m_ref, p_ref, lhs_scr, inj_scr, hs_scr, h_scr, *, batch, nck):
    L, W = S5_CHUNK, S5_BLOCK_GROUPS * S5_GROUP
    NT = S5_BLOCK_GROUPS * S5_STATE // W

    @pl.when(pl.program_id(1) == 0)
    def _():
        h_scr[...] = h0_ref[...]
        _s5_expand(taps_ref, mc_ref, pc_ref, w2_ref, m_ref, p_ref)

    for b in range(batch):
        for l in range(L):
            lhs_scr[b * nck:(b + 1) * nck, l * W:(l + 1) * W] = u_ref[b, pl.ds(l, nck, stride=L), :].astype(BF16)

    def swap_major(x, a, b):
        return jnp.swapaxes(x.reshape(a, b, W), 0, 1).reshape(a * b, W)

    inj = jnp.dot(lhs_scr[...], m_ref[...], preferred_element_type=F32)
    for t in range(2 * NT):
        inj_scr[t] = swap_major(inj[:, t * W:(t + 1) * W], batch, nck)

    lam = [jnp.broadcast_to(lam_ref[:, t * W:(t + 1) * W], (batch, W)) for t in range(2 * NT)]

    def step(k, h):
        rows = pl.ds(pl.multiple_of(k * batch, batch), batch)
        nxt_re, nxt_im = [], []
        for t in range(NT):
            hr, hi = h[t], h[NT + t]
            hs_scr[t, rows, :] = hr
            hs_scr[NT + t, rows, :] = hi
            nxt_re.append(lam[t] * hr - lam[NT + t] * hi + inj_scr[t, rows, :])
            nxt_im.append(lam[t] * hi + lam[NT + t] * hr + inj_scr[NT + t, rows, :])
        return tuple(nxt_re + nxt_im)

    h = lax.fori_loop(0, nck, step, tuple(h_scr[:, t * W:(t + 1) * W] for t in range(2 * NT)))
    for t in range(2 * NT):
        h_scr[:, t * W:(t + 1) * W] = h[t]
    hfin_ref[...] = h_scr[...]

    hs = jnp.concatenate([swap_major(hs_scr[t], nck, batch) for t in range(2 * NT)], axis=-1).astype(BF16)
    for pr in range(L // 2):
        kk = (2 * pr + 2) * W
        y = (jnp.dot(lhs_scr[:, :kk], w2_ref[(L - 1 - 2 * pr) * W:, :], preferred_element_type=F32)
             + jnp.dot(hs, p_ref[:, 2 * pr * W:(2 * pr + 2) * W], preferred_element_type=F32))
        y = _gelu_tanh(y)
        for s in range(2):
            for b in range(batch):
                y_ref[b, pl.ds(2 * pr + s, nck, stride=L), :] = y[b * nck:(b + 1) * nck, s * W:(s + 1) * W]


def _s5_seq(u, mats, h0, batch, seq):
    taps, mm, pp, lam = mats
    L, W = S5_CHUNK, S5_BLOCK_GROUPS * S5_GROUP
    NB = S5_GROUPS // S5_BLOCK_GROUPS
    HS = S5_BLOCK_GROUPS * S5_STATE
    SW = 2 * HS
    ts = seq // S5_TIME_SLICES
    nck = ts // L
    rows = batch * nck
    u_spec = pl.BlockSpec((batch, ts, W), lambda i, t: (0, t, i))
    h_spec = pl.BlockSpec((None, batch, SW), lambda i, t: (i, 0, 0))

    def w_spec(*dims):
        return pl.BlockSpec((None,) + dims, lambda i, t: (i,) + (0,) * len(dims))

    return pl.pallas_call(
        functools.partial(_s5_seq_kernel, batch=batch, nck=nck),
        grid=(NB, S5_TIME_SLICES),
        in_specs=[u_spec, w_spec(L, W, S5_GROUP), w_spec(2, L * W, S5_STATE), w_spec(2, HS, L * S5_GROUP),
                  w_spec(1, SW), h_spec],
        out_specs=(u_spec, h_spec),
        out_shape=(jax.ShapeDtypeStruct((batch, seq, D_MODEL), F32), jax.ShapeDtypeStruct(h0.shape, F32)),
        scratch_shapes=[pltpu.VMEM(((L + 1) * W, 2 * W), BF16), pltpu.VMEM((L * W, SW), BF16),
                        pltpu.VMEM((SW, L * W), BF16),
                        pltpu.VMEM((rows, L * W), BF16), pltpu.VMEM((SW // W, rows, W), F32),
                        pltpu.VMEM((SW // W, rows, W), F32), pltpu.VMEM((batch, SW), F32)],
        compiler_params=_params("parallel", "arbitrary"),
        name="s5_seq",
    )(u, taps, mm, pp, lam, h0)


def _s5_mixer(x, u, h0_re, h0_im, s5p, w_glu, j, batch, seq):
    G, GC, P = S5_GROUPS, S5_GROUP, S5_STATE
    if seq > 1:
        NB, HS = G // S5_BLOCK_GROUPS, S5_BLOCK_GROUPS * P
        mats = _s5_block_mats(_s5_pieces(*[p[j] for p in s5p], S5_CHUNK))
        to_blocks = lambda h: h.reshape(batch, NB, HS).transpose(1, 0, 2)
        h0 = jnp.concatenate([to_blocks(h0_re), to_blocks(h0_im)], axis=-1)
        y, hfin = _s5_seq(u.reshape(batch, seq, D_MODEL), mats, h0, batch, seq)
        y = y.reshape(batch * seq, D_MODEL)
        from_blocks = lambda h: h.transpose(1, 0, 2).reshape(batch, G, P)
        hre, him = from_blocks(hfin[..., :HS]), from_blocks(hfin[..., HS:])
    else:
        mats = _s5_group_mats(_s5_pieces(*[p[j] for p in s5p], 1))
        ug = u.reshape(batch, G, GC).transpose(1, 0, 2)
        yg, hre, him = _s5_core(ug, mats, h0_re.transpose(1, 0, 2), h0_im.transpose(1, 0, 2),
                                1, batch, lax.Precision.HIGHEST, F32)
        y = yg.transpose(1, 0, 2).reshape(batch, D_MODEL)
        hre, him = hre.transpose(1, 0, 2), him.transpose(1, 0, 2)
    x = _proj(y, w_glu, j, glu=True, res=x)
    return x, hre, him


def _hg_lower_bound(logits, layer):
    m = jnp.max(logits, axis=0, keepdims=True)
    e = jnp.exp(logits - m)
    sm = e / jnp.sum(e, axis=0, keepdims=True)
    return jnp.sum(sm[:layer + 1], axis=0, keepdims=True) - sm[0:1]


def _hg_gates(z, lb):
    e = jnp.exp(-jnp.abs(z))
    r = 1.0 / (1.0 + e)
    er = e * r
    pos = z >= 0.0
    f = lb + (1.0 - lb) * jnp.where(pos, r, er)
    logf = jnp.where(f > 0.0, jnp.log(f), z)
    k = (1.0 - lb) * jnp.where(pos, er, r)
    return logf, k


def _hgrn_prompt_kernel(x_ref, g_ref, win_ref, lbl_ref, ng_ref, wout_ref, y_ref, sfin_ref,
                        s_scr, q_scr, k_scr, v_scr, gt_scr, o_scr, beta_scr, safe_scr, *, layer, tt):
    t = pl.program_id(1)
    C = HG_CHUNK

    @pl.when(t == 0)
    def _():
        s_scr[...] = jnp.zeros_like(s_scr)

    x = x_ref[...]
    h = _rms(x, g_ref[...]).astype(BF16)
    def proj(i):
        return jnp.dot(h, win_ref[:, i * HG_WIDTH:(i + 1) * HG_WIDTH], preferred_element_type=F32)

    lb = _hg_lower_bound(lbl_ref[...], layer)
    logf, kk = _hg_gates(proj(1), lb)
    k_scr[...] = kk
    q_scr[...] = _silu(proj(0))
    gt_scr[...] = _silu(proj(3))
    v_scr[...] = proj(2)

    row, col = _iota2((C, C))
    causal = row >= col
    tri = causal.astype(BF16)
    ng = ng_ref[...]
    nt_dims = (((1,), (1,)), ((), ()))
    tn_dims = (((0,), (0,)), ((), ()))

    def finish_head(hh, rows, q_dec, k_dec, o_intra, btot_h):
        sv = slice(hh * HG_DV, (hh + 1) * HG_DV)
        st = s_scr[hh]
        o = o_intra + lax.dot_general(q_dec.astype(BF16), st.astype(BF16), nt_dims, preferred_element_type=F32)
        kv_t = lax.dot_general(v_scr[rows, sv].astype(BF16), k_dec.astype(BF16), tn_dims,
                               preferred_element_type=F32)
        return (_rms(o, ng) * gt_scr[rows, sv]).astype(BF16), st * jnp.exp(btot_h) + kv_t

    def store_heads(rows, results):
        o_scr[rows, :] = jnp.concatenate([o for o, _ in results], axis=-1)
        for hh, (_, s_new) in enumerate(results):
            s_scr[hh] = s_new

    for ci in range(tt // C):
        parts = _split3(logf[ci * C:(ci + 1) * C])
        beta = sum(jnp.dot(tri, p, preferred_element_type=F32) for p in parts)
        beta_scr[ci * C:(ci + 1) * C, :] = beta
        mid = beta[C // 2 - 1:C // 2, :]
        spread = jnp.maximum(jnp.max(-mid), jnp.max(mid - beta[C - 1:C, :]))
        safe_scr[ci] = (spread <= HG_FACTORED_MAX_DECAY).astype(jnp.int32)

    def chunk_step(c, carry):
        rows = pl.ds(pl.multiple_of(c * C, C), C)
        base = pl.multiple_of(c * C, C)
        btot = beta_scr[pl.ds(base + (C - 8), 8), :][7:8]
        mid = beta_scr[pl.ds(base + (C // 2 - 8), 8), :][7:8]
        safe = safe_scr[c] == 1

        @pl.when(safe)
        def _():
            e_mid = jnp.exp(mid)
            e_tot = jnp.exp(btot - mid)
            results = []
            for hh in range(HG_HEADS):
                sk = slice(hh * HG_DK, (hh + 1) * HG_DK)
                d = beta_scr[rows, sk] - mid[:, sk]
                q_mid = q_scr[rows, sk] * jnp.exp(d)
                k_mid = k_scr[rows, sk] * jnp.exp(-d)
                att = lax.dot_general(q_mid.astype(BF16), k_mid.astype(BF16), nt_dims, preferred_element_type=F32)
                att = jnp.where(causal, att, 0.0).astype(BF16)
                o_intra = jnp.dot(att, v_scr[rows, hh * HG_DV:(hh + 1) * HG_DV].astype(BF16),
                                  preferred_element_type=F32)
                results.append(finish_head(hh, rows, q_mid * e_mid[:, sk], k_mid * e_tot[:, sk], o_intra,
                                           btot[:, sk]))
            store_heads(rows, results)

        @pl.when(jnp.logical_not(safe))
        def _():
            t_idx = lax.broadcasted_iota(jnp.int32, (C, 1), 0)
            results = []
            for hh in range(HG_HEADS):
                sk = slice(hh * HG_DK, (hh + 1) * HG_DK)
                sv = slice(hh * HG_DV, (hh + 1) * HG_DV)
                b = beta_scr[rows, sk]
                q = q_scr[rows, sk]

                def key_step(s8, acc):
                    off = pl.multiple_of(s8 * 8, 8)
                    b_keys = beta_scr[pl.ds(base + off, 8), sk]
                    k_keys = k_scr[pl.ds(base + off, 8), sk]
                    v_keys = v_scr[pl.ds(base + off, 8), sv]
                    for i in range(8):
                        w = jnp.exp(jnp.minimum(b - b_keys[i:i + 1], 0.0))
                        a = jnp.sum(q * k_keys[i:i + 1] * w, axis=-1, keepdims=True)
                        a = jnp.where(t_idx >= off + i, a, 0.0)
                        acc = acc + a * v_keys[i:i + 1]
                    return acc

                o_intra = lax.fori_loop(0, C // 8, key_step, jnp.zeros((C, HG_DV), F32))
                results.append(finish_head(hh, rows, q * jnp.exp(b), k_scr[rows, sk] * jnp.exp(btot[:, sk] - b),
                                           o_intra, btot[:, sk]))
            store_heads(rows, results)

        return carry

    lax.fori_loop(0, tt // C, chunk_step, 0)
    y_ref[...] = x + jnp.dot(o_scr[...], wout_ref[...], preferred_element_type=F32)

    @pl.when(t == pl.num_programs(1) - 1)
    def _():
        for hh in range(HG_HEADS):
            sfin_ref[hh] = s_scr[hh].T


def _hgrn_prompt(x, g, w_in, lb_logits, norm_g, w_out, layer, j, batch, seq):
    tt = HG_ROW_TILE
    nt = seq // tt
    x3 = x.reshape(batch, seq, D_MODEL)
    row_spec = pl.BlockSpec((None, tt, D_MODEL), lambda b, t: (b, t, 0))
    y, s_fin = pl.pallas_call(
        functools.partial(_hgrn_prompt_kernel, layer=layer, tt=tt),
        grid=(batch, nt),
        in_specs=[row_spec,
                  pl.BlockSpec((None, 1, D_MODEL), lambda b, t: (layer, 0, 0)),
                  _resident((None, D_MODEL, 4 * HG_WIDTH), lambda b, t: (j, 0, 0)),
                  pl.BlockSpec((DEPTH, HG_WIDTH), lambda b, t: (0, 0)),
                  pl.BlockSpec((None, 1, HG_DV), lambda b, t: (j, 0, 0)),
                  _resident((None, HG_WIDTH, D_MODEL), lambda b, t: (j, 0, 0))],
        out_specs=(row_spec,
                   pl.BlockSpec((None, HG_HEADS, HG_DK, HG_DV), lambda b, t: (b, 0, 0, 0))),
        out_shape=(jax.ShapeDtypeStruct((batch, seq, D_MODEL), F32),
                   jax.ShapeDtypeStruct((batch, HG_HEADS, HG_DK, HG_DV), F32)),
        scratch_shapes=[pltpu.VMEM((HG_HEADS, HG_DV, HG_DK), F32),
                        pltpu.VMEM((tt, HG_WIDTH), F32), pltpu.VMEM((tt, HG_WIDTH), F32),
                        pltpu.VMEM((tt, HG_WIDTH), F32), pltpu.VMEM((tt, HG_WIDTH), F32),
                        pltpu.VMEM((tt, HG_WIDTH), BF16), pltpu.VMEM((tt, HG_WIDTH), F32),
                        pltpu.SMEM((tt // HG_CHUNK,), jnp.int32)],
        compiler_params=_params("parallel", "arbitrary"),
        name="hgrn_prompt",
    )(x3, g.reshape(DEPTH, 1, D_MODEL), w_in, lb_logits, norm_g.reshape(-1, 1, HG_DV), w_out)
    return y.reshape(batch * seq, D_MODEL), s_fin


def _hgrn_sample_kernel(proj_ref, lbl_ref, ng_ref, s_ref, snew_ref, o_ref, *, layer, tb):
    proj = proj_ref[...]
    lb = _hg_lower_bound(lbl_ref[...], layer)
    z = proj[:, HG_WIDTH:2 * HG_WIDTH]
    e = jnp.exp(-jnp.abs(z))
    r = 1.0 / (1.0 + e)
    sig = jnp.where(z >= 0.0, r, e * r)
    f = lb + (1.0 - lb) * sig
    k = (1.0 - lb) * jnp.where(z >= 0.0, e * r, r)
    q = _silu(proj[:, :HG_WIDTH])
    v = proj[:, 2 * HG_WIDTH:3 * HG_WIDTH]
    gt = _silu(proj[:, 3 * HG_WIDTH:])
    ng = ng_ref[...]
    tok, lane = _iota2((tb, tb * HG_DV))
    spread = ((lane // HG_DV) == tok).astype(BF16)

    spread3 = jnp.concatenate([spread] * 3, axis=0)

    def columns(x, exact):
        if exact:
            return jnp.dot(jnp.concatenate(_split3(x.T), axis=1), spread3, preferred_element_type=F32)
        return jnp.dot(x.T.astype(BF16), spread, preferred_element_type=F32)

    for hh in range(HG_HEADS):
        sk = slice(hh * HG_DK, (hh + 1) * HG_DK)
        sv = slice(hh * HG_DV, (hh + 1) * HG_DV)
        f_c, k_c, q_c = columns(f[:, sk], True), columns(k[:, sk], False), columns(q[:, sk], False)
        for b in range(tb):
            blk = slice(b * HG_DV, (b + 1) * HG_DV)
            s_new = f_c[:, blk] * s_ref[b, hh] + k_c[:, blk] * v[b:b + 1, sv]
            snew_ref[b, hh] = s_new
            o = jnp.sum(q_c[:, blk] * s_new, axis=0, keepdims=True)
            o_ref[b:b + 1, sv] = _rms(o, ng) * gt[b:b + 1, sv]


def _hgrn_sample(proj, lb_logits, norm_g, state, layer, j):
    nb = proj.shape[0]
    tb = SAMPLE_TOKENS_PER_STEP
    st_spec = pl.BlockSpec((tb, HG_HEADS, HG_DK, HG_DV), lambda i: (i, 0, 0, 0))
    return pl.pallas_call(
        functools.partial(_hgrn_sample_kernel, layer=layer, tb=tb),
        grid=(nb // tb,),
        in_specs=[pl.BlockSpec((tb, 4 * HG_WIDTH), lambda i: (i, 0)),
                  pl.BlockSpec((DEPTH, HG_WIDTH), lambda i: (0, 0)),
                  pl.BlockSpec((None, 1, HG_DV), lambda i: (j, 0, 0)),
                  st_spec],
        out_specs=(st_spec, pl.BlockSpec((tb, HG_WIDTH), lambda i: (i, 0))),
        out_shape=(jax.ShapeDtypeStruct(state.shape, F32), jax.ShapeDtypeStruct((nb, HG_WIDTH), F32)),
        compiler_params=_params("parallel"),
        name="hgrn_sample",
    )(proj, lb_logits, norm_g.reshape(-1, 1, HG_DV), state)


def _xattn_prompt_kernel(x_ref, g_ref, wq_ref, k_ref, v_ref, wo_ref, y_ref):
    x = x_ref[...]
    h = _rms(x, g_ref[...]).astype(BF16)
    q = jnp.dot(h, wq_ref[...], preferred_element_type=F32) * (1.0 / math.sqrt(MEM_HD))
    q = q.astype(BF16)
    outs = []
    for hh in range(MEM_HEADS):
        sl = slice(hh * MEM_HD, (hh + 1) * MEM_HD)
        s = lax.dot_general(q[:, sl], k_ref[:, sl], (((1,), (1,)), ((), ())), preferred_element_type=F32)
        p = jnp.exp(s - jnp.max(s, axis=-1, keepdims=True))
        den = jnp.sum(p, axis=-1, keepdims=True)
        o = jnp.dot(p.astype(BF16), v_ref[:, sl], preferred_element_type=F32)
        outs.append((o / den).astype(BF16))
    o = jnp.concatenate(outs, axis=-1)
    y_ref[...] = x + jnp.dot(o, wo_ref[...], preferred_element_type=F32)


def _xattn_prompt(x, g, w_q, mem_k, mem_v, w_o, layer, batch, seq):
    tt = XA_ROW_TILE
    row_spec = pl.BlockSpec((None, tt, D_MODEL), lambda b, t: (b, t, 0))
    kv_spec = pl.BlockSpec((None, N_MEM, D_MODEL), lambda b, t: (layer, b, 0))
    w_spec = _resident((None, D_MODEL, D_MODEL), lambda b, t: (layer, 0, 0))
    y = pl.pallas_call(
        _xattn_prompt_kernel,
        grid=(batch, seq // tt),
        in_specs=[row_spec, pl.BlockSpec((None, 1, D_MODEL), lambda b, t: (layer, 0, 0)),
                  w_spec, kv_spec, kv_spec, w_spec],
        out_specs=row_spec,
        out_shape=jax.ShapeDtypeStruct((batch, seq, D_MODEL), F32),
        compiler_params=_params("parallel", "parallel"),
        name="xattn_prompt",
    )(x.reshape(batch, seq, D_MODEL), g.reshape(DEPTH, 1, D_MODEL), w_q, mem_k, mem_v, w_o)
    return y.reshape(batch * seq, D_MODEL)


def _xattn_sample_kernel(q_ref, k_ref, v_ref, o_ref, *, tb):
    scale = 1.0 / math.sqrt(MEM_HD)
    rows = N_MEM * MEM_HEADS
    head, lane = _iota2((MEM_HEADS, rows))
    own = (lane & (MEM_HEADS - 1)) == head
    for b in range(tb):
        q = (q_ref[b] * scale).astype(BF16)
        k = k_ref[b].reshape(rows, MEM_HD).astype(BF16)
        v = v_ref[b].reshape(rows, MEM_HD).astype(BF16)
        s = lax.dot_general(q, k, (((1,), (1,)), ((), ())), preferred_element_type=F32)
        s = jnp.where(own, s, -jnp.inf)
        p = jnp.exp(s - jnp.max(s, axis=-1, keepdims=True))
        den = jnp.sum(p, axis=-1, keepdims=True)
        o_ref[b] = jnp.dot(p.astype(BF16), v, preferred_element_type=F32) / den


def _xattn_sample(q, cache_k, cache_v, layer):
    nb = q.shape[0]
    tb = XA_SAMPLE_TOKENS_PER_STEP
    kv_spec = pl.BlockSpec((None, tb, N_MEM, MEM_HEADS, MEM_HD), lambda i: (layer, i, 0, 0, 0))
    q_spec = pl.BlockSpec((tb, MEM_HEADS, MEM_HD), lambda i: (i, 0, 0))
    return pl.pallas_call(
        functools.partial(_xattn_sample_kernel, tb=tb),
        grid=(nb // tb,),
        in_specs=[q_spec, kv_spec, kv_spec],
        out_specs=q_spec,
        out_shape=jax.ShapeDtypeStruct((nb, MEM_HEADS, MEM_HD), F32),
        compiler_params=_params("parallel"),
        name="xattn_sample",
    )(q.reshape(nb, MEM_HEADS, MEM_HD), cache_k, cache_v).reshape(nb, D_MODEL)


def _trunk(x, batch, seq, s5_re, s5_im, hg_state, mem_k, mem_v, w):
    prompt = seq > 1
    new_re, new_im, new_hg = [], [], []
    for i in range(DEPTH):
        j = i // 2
        if i % 2 == 0:
            x, u = _ffn(x, w["ffn1_norm"], w["ffn1_w_in"], w["ffn1_w_out"], i, post="emit",
                        g2=w["mix_norm"][i], u_dtype=F32)
            x, hr, hi = _s5_mixer(x, u, s5_re[j], s5_im[j], w["s5"], w["s5_w_glu"], j, batch, seq)
            new_re.append(hr)
            new_im.append(hi)
        else:
            x = _ffn(x, w["ffn1_norm"], w["ffn1_w_in"], w["ffn1_w_out"], i)
            if prompt:
                x, sn = _hgrn_prompt(x, w["mix_norm"], w["hg_w_in"], w["hg_lb_logits"], w["hg_norm"],
                                     w["hg_w_out"], i, j, batch, seq)
            else:
                proj = _proj(x, w["hg_w_in"], j, g=w["mix_norm"], g_layer=i)
                sn, o = _hgrn_sample(proj, w["hg_lb_logits"], w["hg_norm"], hg_state[j], i, j)
                x = _proj(o, w["hg_w_out"], j, res=x)
            new_hg.append(sn)
        if prompt:
            x = _xattn_prompt(x, w["xattn_norm"], w["xattn_w_q"], mem_k, mem_v, w["xattn_w_o"], i, batch, seq)
        else:
            q = _proj(x, w["xattn_w_q"], i, g=w["xattn_norm"], g_layer=i)
            o = _xattn_sample(q, mem_k, mem_v, i)
            x = _proj(o, w["xattn_w_o"], i, res=x)
        last = i == DEPTH - 1
        x = _ffn(x, w["ffn2_norm"], w["ffn2_w_in"], w["ffn2_w_out"], i,
                 post="replace" if last else "none", g2=w["final_norm"] if last else None)
    return x, jnp.stack(new_re), jnp.stack(new_im), jnp.stack(new_hg)


def kernel(x_prompt, x_sample, mem_prompt, state_s5_re, state_s5_im, state_hgrn, cache_mem_k, cache_mem_v, ffn1_norm, ffn1_w_in, ffn1_w_out, mix_norm, xattn_norm, mem_norm, xattn_w_q, xattn_w_kv, xattn_w_o, ffn2_norm, ffn2_w_in, ffn2_w_out, s5_a_re, s5_a_im, s5_log_dt, s5_b_re, s5_b_im, s5_c_re, s5_c_im, s5_d, s5_w_glu, hg_w_in, hg_lb_logits, hg_norm, hg_w_out, final_norm):
    bp, seq, _ = x_prompt.shape
    bs = x_sample.shape[0]
    bf = lambda a: a.astype(BF16)
    w = dict(ffn1_norm=ffn1_norm, ffn1_w_in=bf(ffn1_w_in), ffn1_w_out=bf(ffn1_w_out), mix_norm=mix_norm,
             xattn_norm=xattn_norm, xattn_w_q=bf(xattn_w_q), xattn_w_o=bf(xattn_w_o), ffn2_norm=ffn2_norm,
             ffn2_w_in=bf(ffn2_w_in), ffn2_w_out=bf(ffn2_w_out),
             s5=(s5_a_re, s5_a_im, s5_log_dt, s5_b_re, s5_b_im, s5_c_re, s5_c_im, s5_d),
             s5_w_glu=bf(s5_w_glu), hg_w_in=bf(hg_w_in), hg_lb_logits=hg_lb_logits, hg_norm=hg_norm,
             hg_w_out=bf(hg_w_out), final_norm=final_norm)

    mem_k, mem_v, mem_k_bf, mem_v_bf = _mem_kv(mem_prompt, mem_norm, bf(xattn_w_kv))

    n_s5 = state_s5_re.shape[0]
    z_s5 = jnp.zeros((n_s5, bp, S5_GROUPS, S5_STATE), F32)
    y_p, re_p, im_p, hg_p = _trunk(x_prompt.reshape(bp * seq, D_MODEL), bp, seq, z_s5, z_s5, None,
                                   mem_k_bf, mem_v_bf, w)
    y_s, re_s, im_s, hg_s = _trunk(x_sample.reshape(bs, D_MODEL), bs, 1, state_s5_re, state_s5_im, state_hgrn,
                                   cache_mem_k, cache_mem_v, w)
    return (y_p.reshape(bp, seq, D_MODEL), y_s.reshape(bs, 1, D_MODEL), re_p, im_p, re_s, im_s, hg_p, hg_s,
            mem_k, mem_v)
```

```python
import functools
import math

import jax
import jax.numpy as jnp
from jax import lax
from jax.experimental import pallas as pl
from jax.experimental.pallas import tpu as pltpu

F32 = jnp.float32
BF16 = jnp.bfloat16

D_MODEL = 1024
DEPTH = 2
S5_GROUP = 16
S5_GROUPS = D_MODEL // S5_GROUP
S5_STATE = 64
S5_CHUNK = 16
S5_BLOCK_GROUPS = 8
S5_TIME_SLICES = 2
HG_DK = 128
HG_HEADS = D_MODEL // HG_DK
HG_DV = D_MODEL // HG_HEADS
HG_WIDTH = HG_HEADS * HG_DK
HG_CHUNK = 128
HG_FACTORED_MAX_DECAY = 60.0
N_MEM = 256
MEM_HEADS = 4
MEM_HD = D_MODEL // MEM_HEADS
FFN_DIM = 2816
EPS = 1e-6

V7X_VMEM_LIMIT_BYTES = 56 * 1024 * 1024

ROW_TILE = 512
FFN_CHUNK = 1408
HG_ROW_TILE = 512
XA_ROW_TILE = 512
S5_GROUPS_PER_STEP = 16
SAMPLE_TOKENS_PER_STEP = 8
XA_SAMPLE_TOKENS_PER_STEP = 4


def _params(*semantics):
    return pltpu.CompilerParams(dimension_semantics=semantics,
                                vmem_limit_bytes=V7X_VMEM_LIMIT_BYTES)


def _resident(shape, index_map):
    return pl.BlockSpec(shape, index_map, pipeline_mode=pl.Buffered(1))


def _rms(x, g):
    ms = jnp.mean(x * x, axis=-1, keepdims=True)
    return x * lax.rsqrt(ms + EPS) * g


def _sigmoid(x):
    return 1.0 / (1.0 + jnp.exp(-x))


def _silu(x):
    return x * _sigmoid(x)


def _gelu_tanh(x):
    c = math.sqrt(2.0 / math.pi)
    return 0.5 * x * (1.0 + jnp.tanh(c * (x + 0.044715 * (x * x * x))))


def _split3(x):
    hi = x.astype(BF16)
    r1 = x - hi.astype(F32)
    mid = r1.astype(BF16)
    lo = (r1 - mid.astype(F32)).astype(BF16)
    return hi, mid, lo


def _ffn_kernel(x_ref, g_ref, win_ref, wout_ref, *rest, post, rider_tokens):
    rest = list(rest)
    g2_ref = rest.pop(0) if post != "none" else None
    if rider_tokens:
        q_ref, k_ref, v_ref = rest[:3]
        rest = rest[3:]
        _xattn_sample_kernel(q_ref, k_ref, v_ref, rest.pop(), tb=rider_tokens)
    x = x_ref[...]
    h = _rms(x, g_ref[...]).astype(BF16)
    acc = jnp.zeros_like(x)
    for c in range(FFN_DIM // FFN_CHUNK):
        lo = c * FFN_CHUNK
        gate = jnp.dot(h, win_ref[:, lo:lo + FFN_CHUNK], preferred_element_type=F32)
        up = jnp.dot(h, win_ref[:, FFN_DIM + lo:FFN_DIM + lo + FFN_CHUNK], preferred_element_type=F32)
        act = (_silu(gate) * up).astype(BF16)
        acc = acc + jnp.dot(act, wout_ref[lo:lo + FFN_CHUNK, :], preferred_element_type=F32)
    y = x + 0.5 * acc
    y_ref = rest[0]
    if post == "replace":
        y_ref[...] = _rms(y, g2_ref[...])
    else:
        y_ref[...] = y
        if post == "emit":
            rest[1][...] = _rms(y, g2_ref[...]).astype(rest[1].dtype)


def _ffn(x, g, w_in, w_out, layer, post="none", g2=None, u_dtype=BF16, rider=None):
    rows = x.shape[0]
    tm = min(ROW_TILE, rows)
    steps = rows // tm
    row_spec = pl.BlockSpec((tm, D_MODEL), lambda i: (i, 0))
    vec_spec = pl.BlockSpec((1, D_MODEL), lambda i: (0, 0))
    in_specs = [row_spec,
                pl.BlockSpec((None, 1, D_MODEL), lambda i: (layer, 0, 0)),
                _resident((None, D_MODEL, 2 * FFN_DIM), lambda i: (layer, 0, 0)),
                _resident((None, FFN_DIM, D_MODEL), lambda i: (layer, 0, 0))]
    args = [x, g.reshape(DEPTH, 1, D_MODEL), w_in, w_out]
    out_shape = [jax.ShapeDtypeStruct((rows, D_MODEL), F32)]
    out_specs = [row_spec]
    if post != "none":
        in_specs.append(vec_spec)
        args.append(g2.reshape(1, D_MODEL))
    if post == "emit":
        out_shape.append(jax.ShapeDtypeStruct((rows, D_MODEL), u_dtype))
        out_specs.append(row_spec)
    tb = 0
    if rider is not None:
        q, cache_k, cache_v, r_layer = rider
        nb = q.shape[0]
        assert nb % steps == 0
        tb = nb // steps
        kv_spec = pl.BlockSpec((None, tb, N_MEM, MEM_HEADS, MEM_HD), lambda i: (r_layer, i, 0, 0, 0))
        q_spec = pl.BlockSpec((tb, MEM_HEADS, MEM_HD), lambda i: (i, 0, 0))
        in_specs += [q_spec, kv_spec, kv_spec]
        args += [q.reshape(nb, MEM_HEADS, MEM_HD), cache_k, cache_v]
        out_shape.append(jax.ShapeDtypeStruct((nb, MEM_HEADS, MEM_HD), F32))
        out_specs.append(q_spec)
    outs = list(pl.pallas_call(
        functools.partial(_ffn_kernel, post=post, rider_tokens=tb),
        grid=(steps,),
        in_specs=in_specs, out_specs=tuple(out_specs), out_shape=tuple(out_shape),
        compiler_params=_params("parallel"),
        name="ffn",
    )(*args))
    if rider is not None:
        outs[-1] = outs[-1].reshape(-1, D_MODEL)
    return outs[0] if len(outs) == 1 else tuple(outs)


def _proj_kernel(*refs, norm, glu, residual):
    refs = list(refs)
    x_ref = refs.pop(0)
    g_ref = refs.pop(0) if norm else None
    w_ref = refs.pop(0)
    res_ref = refs.pop(0) if residual else None
    (o_ref,) = refs
    x = x_ref[...]
    if norm:
        x = _rms(x, g_ref[...])
    y = jnp.dot(x.astype(BF16), w_ref[...], preferred_element_type=F32)
    if glu:
        half = y.shape[-1] // 2
        y = y[:, :half] * _sigmoid(y[:, half:])
    if residual:
        y = y + res_ref[...]
    o_ref[...] = y


def _proj(x, w, layer, g=None, g_layer=0, glu=False, res=None):
    rows, kdim = x.shape
    ndim = w.shape[-1]
    nout = ndim // 2 if glu else ndim
    tm = min(ROW_TILE, rows)
    in_specs = [pl.BlockSpec((tm, kdim), lambda i: (i, 0))]
    args = [x]
    if g is not None:
        in_specs.append(pl.BlockSpec((None, 1, kdim), lambda i: (g_layer, 0, 0)))
        args.append(g.reshape(g.shape[0], 1, kdim))
    in_specs.append(_resident((None, kdim, ndim), lambda i: (layer, 0, 0)))
    args.append(w)
    if res is not None:
        in_specs.append(pl.BlockSpec((tm, nout), lambda i: (i, 0)))
        args.append(res)
    return pl.pallas_call(
        functools.partial(_proj_kernel, norm=g is not None, glu=glu, residual=res is not None),
        grid=(rows // tm,),
        in_specs=in_specs,
        out_specs=pl.BlockSpec((tm, nout), lambda i: (i, 0)),
        out_shape=jax.ShapeDtypeStruct((rows, nout), F32),
        compiler_params=_params("parallel"),
        name="proj",
    )(*args)


def _memkv_kernel(x_ref, g_ref, w_ref, k_ref, v_ref, kb_ref, vb_ref):
    h = _rms(x_ref[...], g_ref[...]).astype(BF16)
    y = jnp.dot(h, w_ref[...], preferred_element_type=F32)
    k, v = y[:, :D_MODEL], y[:, D_MODEL:]
    k_ref[...] = k.reshape(k_ref.shape)
    v_ref[...] = v.reshape(v_ref.shape)
    kb_ref[...] = k.astype(BF16)
    vb_ref[...] = v.astype(BF16)


def _mem_kv(mem, g, w_kv):
    batch = mem.shape[0]
    rows = batch * N_MEM
    nb = max(1, min(ROW_TILE, rows) // N_MEM)
    tm = nb * N_MEM
    out5 = jax.ShapeDtypeStruct((DEPTH, batch, N_MEM, MEM_HEADS, MEM_HD), F32)
    out2 = jax.ShapeDtypeStruct((DEPTH, rows, D_MODEL), BF16)
    spec5 = pl.BlockSpec((None, nb, N_MEM, MEM_HEADS, MEM_HD), lambda l, i: (l, i, 0, 0, 0))
    spec2 = pl.BlockSpec((None, tm, D_MODEL), lambda l, i: (l, i, 0))
    return pl.pallas_call(
        _memkv_kernel,
        grid=(DEPTH, rows // tm),
        in_specs=[pl.BlockSpec((tm, D_MODEL), lambda l, i: (i, 0)),
                  pl.BlockSpec((None, 1, D_MODEL), lambda l, i: (l, 0, 0)),
                  pl.BlockSpec((None, D_MODEL, 2 * D_MODEL), lambda l, i: (l, 0, 0))],
        out_specs=(spec5, spec5, spec2, spec2), out_shape=(out5, out5, out2, out2),
        compiler_params=_params("parallel", "parallel"),
        name="mem_kv",
    )(mem.reshape(rows, D_MODEL), g.reshape(DEPTH, 1, D_MODEL), w_kv)


def _s5_pieces(a_re, a_im, log_dt, b_re, b_im, c_re, c_im, d, steps):
    G, P, GC = S5_GROUPS, S5_STATE, S5_GROUP
    L = steps
    dt = jnp.exp(log_dt)[:, None]
    xr, xi = a_re * dt, a_im * dt
    j = jnp.arange(L + 1, dtype=F32)[:, None, None]
    mag = jnp.exp(xr[None] * j)
    pw_re, pw_im = mag * jnp.cos(xi[None] * j), mag * jnp.sin(xi[None] * j)
    nr, ni = pw_re[1] - 1.0, pw_im[1]
    den = a_re * a_re + a_im * a_im
    fr, fi = (nr * a_re + ni * a_im) / den, (ni * a_re - nr * a_im) / den
    bb_re = fr[..., None] * b_re - fi[..., None] * b_im
    bb_im = fr[..., None] * b_im + fi[..., None] * b_re
    hp = lax.Precision.HIGHEST
    w_re = pw_re[:L, :, None, :] * c_re[None] - pw_im[:L, :, None, :] * c_im[None]
    w_im = pw_re[:L, :, None, :] * c_im[None] + pw_im[:L, :, None, :] * c_re[None]
    kern = (jnp.einsum("gpi,lgop->lgio", bb_re, w_re, precision=hp)
            - jnp.einsum("gpi,lgop->lgio", bb_im, w_im, precision=hp))
    kern = kern.at[0].add(d[:, :, None] * jnp.eye(GC, dtype=F32)[None])
    rp_re, rp_im = pw_re[:L][::-1], pw_im[:L][::-1]
    m_re = rp_re[:, :, :, None] * bb_re[None] - rp_im[:, :, :, None] * bb_im[None]
    m_im = rp_re[:, :, :, None] * bb_im[None] + rp_im[:, :, :, None] * bb_re[None]
    m_re = m_re.transpose(1, 0, 3, 2)
    m_im = m_im.transpose(1, 0, 3, 2)
    q_re = pw_re[1:, :, None, :] * c_re[None] - pw_im[1:, :, None, :] * c_im[None]
    q_im = pw_re[1:, :, None, :] * c_im[None] + pw_im[1:, :, None, :] * c_re[None]
    p_re = q_re.transpose(1, 3, 0, 2)
    p_im = (-q_im).transpose(1, 3, 0, 2)
    return kern, m_re, m_im, p_re, p_im, pw_re[L], pw_im[L]


def _s5_group_mats(pieces):
    kern, m_re, m_im, p_re, p_im, l_re, l_im = pieces
    return (kern[0], m_re[:, 0], m_im[:, 0], p_re[:, :, 0], p_im[:, :, 0], l_re[:, None, :], l_im[:, None, :])


def _s5_block_mats(pieces):
    kern, m_re, m_im, p_re, p_im, l_re, l_im = lax.optimization_barrier(pieces)
    L = kern.shape[0]
    GB = S5_BLOCK_GROUPS
    NB = S5_GROUPS // GB
    GC, P = S5_GROUP, S5_STATE
    taps =kern.astype(BF16).reshape(L, NB, GB * GC, GC).transpose(1, 0, 2, 3)

    def m_rows(m):
        return m.astype(BF16).reshape(NB, GB, L, GC, P).transpose(0, 2, 1, 3, 4).reshape(NB, L * GB * GC, P)

    mm = jnp.stack([m_rows(m_re), m_rows(m_im)], axis=1)
    pp = jnp.stack([p_re.astype(BF16).reshape(NB, GB * P, L * GC), p_im.astype(BF16).reshape(NB, GB * P, L * GC)],
                   axis=1)
    lam = jnp.concatenate([l_re.reshape(NB, 1, GB * P), l_im.reshape(NB, 1, GB * P)], axis=2)
    return taps, mm, pp, lam


def _s5_kernel(u_ref, t_ref, mre_ref, mim_ref, pre_ref, pim_ref, lre_ref, lim_ref, h0re_ref, h0im_ref,
               y_ref, hre_ref, him_ref, inj_re, inj_im, hs_re, hs_im, *, n_chunks, rb, gps, precision):
    def mm(a, b):
        return jnp.dot(a, b, preferred_element_type=F32, precision=precision)

    for g in range(gps):
        u = u_ref[g]
        inj_re[g] = mm(u, mre_ref[g])
        inj_im[g] = mm(u, mim_ref[g])

    lam_re = [jnp.broadcast_to(lre_ref[g], (rb, S5_STATE)) for g in range(gps)]
    lam_im = [jnp.broadcast_to(lim_ref[g], (rb, S5_STATE)) for g in range(gps)]

    def step(k, carry):
        rows = pl.ds(pl.multiple_of(k * rb, rb), rb)
        nxt = []
        for g in range(gps):
            hr, hi = carry[2 * g], carry[2 * g + 1]
            hs_re[g, rows, :] = hr
            hs_im[g, rows, :] = hi
            nxt.append(lam_re[g] * hr - lam_im[g] * hi + inj_re[g, rows, :])
            nxt.append(lam_re[g] * hi + lam_im[g] * hr + inj_im[g, rows, :])
        return tuple(nxt)

    init = []
    for g in range(gps):
        init += [h0re_ref[g], h0im_ref[g]]
    fin = lax.fori_loop(0, n_chunks, step, tuple(init))

    for g in range(gps):
        hre_ref[g] = fin[2 * g]
        him_ref[g] = fin[2 * g + 1]
        dt = u_ref.dtype
        y = (mm(u_ref[g], t_ref[g]) + mm(hs_re[g].astype(dt), pre_ref[g])
             + mm(hs_im[g].astype(dt), pim_ref[g]))
        y_ref[g] = _gelu_tanh(y).astype(y_ref.dtype)


def _s5_core(u, mats, h0_re, h0_im, n_chunks, rb, precision, y_dtype):
    G, R, W = u.shape
    P = S5_STATE
    gps = S5_GROUPS_PER_STEP
    tmat, m_re, m_im, p_re, p_im, l_re, l_im = mats

    def spec(a, b):
        return pl.BlockSpec((gps, a, b), lambda i: (i, 0, 0))

    st = jax.ShapeDtypeStruct((G, rb, P), F32)
    scr = lambda: pltpu.VMEM((gps, R, P), F32)
    return pl.pallas_call(
        functools.partial(_s5_kernel, n_chunks=n_chunks, rb=rb, gps=gps, precision=precision),
        grid=(G // gps,),
        in_specs=[spec(R, W), spec(W, W), spec(W, P), spec(W, P), spec(P, W), spec(P, W),
                  spec(1, P), spec(1, P), spec(rb, P), spec(rb, P)],
        out_specs=(spec(R, W), spec(rb, P), spec(rb, P)),
        out_shape=(jax.ShapeDtypeStruct((G, R, W), y_dtype), st, st),
        scratch_shapes=[scr(), scr(), scr(), scr()],
        compiler_params=_params("parallel"),
        name="s5_core",
    )(u, tmat, m_re, m_im, p_re, p_im, l_re, l_im, h0_re, h0_im)


def _iota2(shape):
    return lax.broadcasted_iota(jnp.int32, shape, 0), lax.broadcasted_iota(jnp.int32, shape, 1)


def _s5_expand(taps_ref, mc_ref, pc_ref, w2_ref, m_ref, p_ref):
    L, GC, P = S5_CHUNK, S5_GROUP, S5_STATE
    GB = S5_BLOCK_GROUPS
    W, HS = GB * GC, GB * P
    gc_bits, p_bits, w_bits = GC.bit_length() - 1, P.bit_length() - 1, W.bit_length() - 1

    r, c = _iota2((P, HS))
    rep_m = ((c & (P - 1)) == r).astype(BF16)
    r, c = _iota2((L * W, HS))
    mask_m = ((r >> gc_bits) & (GB - 1)) == (c >> p_bits)
    r, c = _iota2((L * GC, L * W))
    rep_p = (r == (((c >> w_bits) << gc_bits) | (c & (GC - 1)))).astype(BF16)
    r, c = _iota2((HS, L * W))
    mask_p = (r >> p_bits) == ((c >> gc_bits) & (GB - 1))
    for half in range(2):
        m_ref[:, half * HS:(half + 1) * HS] = jnp.where(
            mask_m, jnp.dot(mc_ref[half], rep_m, preferred_element_type=F32), 0.0).astype(BF16)
        p_ref[half * HS:(half + 1) * HS, :] = jnp.where(
            mask_p, jnp.dot(pc_ref[half], rep_p, preferred_element_type=F32), 0.0).astype(BF16)

    r, c = _iota2((GC, W))
    rep_k = ((c & (GC - 1)) == r).astype(BF16)
    r, c = _iota2((W, W))
    mask_k = (r >> gc_bits) == (c >> gc_bits)
    zero = jnp.zeros((W, W), BF16)
    w2_ref[L * W:, :W] = zero
    w2_ref[:W, W:] = zero
    for i in range(L):
        kb = jnp.where(mask_k, jnp.dot(taps_ref[L - 1 - i], rep_k, preferred_element_type=F32), 0.0).astype(BF16)
        w2_ref[i * W:(i + 1) * W, :W] = kb
        w2_ref[(i + 1) * W:(i + 2) * W, W:] = kb


def _s5_seq_kernel(u_ref, taps_ref, mc_ref, pc_ref, lam_ref, h0_ref, y_ref, hfin_ref,
                   w2_ref, m_ref, p_ref, lhs_scr, inj_scr, hs_scr, h_scr, *, batch, nck):
    L, W = S5_CHUNK, S5_BLOCK_GROUPS * S5_GROUP
    NT = S5_BLOCK_GROUPS * S5_STATE // W

    @pl.when(pl.program_id(1) == 0)
    def _():
        h_scr[...] = h0_ref[...]
        _s5_expand(taps_ref, mc_ref, pc_ref, w2_ref, m_ref, p_ref)

    for b in range(batch):
        for l in range(L):
            lhs_scr[b * nck:(b + 1) * nck, l * W:(l + 1) * W] = u_ref[b, pl.ds(l, nck, stride=L), :].astype(BF16)

    def swap_major(x, a, b):
        return jnp.swapaxes(x.reshape(a, b, W), 0, 1).reshape(a * b, W)

    inj = jnp.dot(lhs_scr[...], m_ref[...], preferred_element_type=F32)
    for t in range(2 * NT):
        inj_scr[t] = swap_major(inj[:, t * W:(t + 1) * W], batch, nck)

    lam = [jnp.broadcast_to(lam_ref[:, t * W:(t + 1) * W], (batch, W)) for t in range(2 * NT)]

    def step(k, h):
        rows = pl.ds(pl.multiple_of(k * batch, batch), batch)
        nxt_re, nxt_im = [], []
        for t in range(NT):
            hr, hi = h[t], h[NT + t]
            hs_scr[t, rows, :] = hr
            hs_scr[NT + t, rows, :] = hi
            nxt_re.append(lam[t] * hr - lam[NT + t] * hi + inj_scr[t, rows, :])
            nxt_im.append(lam[t] * hi + lam[NT + t] * hr + inj_scr[NT + t, rows, :])
        return tuple(nxt_re + nxt_im)

    h = lax.fori_loop(0, nck, step, tuple(h_scr[:, t * W:(t + 1) * W] for t in range(2 * NT)))
    for t in range(2 * NT):
        h_scr[:, t * W:(t + 1) * W] = h[t]
    hfin_ref[...] = h_scr[...]

    hs = jnp.concatenate([swap_major(hs_scr[t], nck, batch) for t in range(2 * NT)], axis=-1).astype(BF16)
    for pr in range(L // 2):
        kk = (2 * pr + 2) * W
        y = (jnp.dot(lhs_scr[:, :kk], w2_ref[(L - 1 - 2 * pr) * W:, :], preferred_element_type=F32)
             + jnp.dot(hs, p_ref[:, 2 * pr * W:(2 * pr + 2) * W], preferred_element_type=F32))
        y = _gelu_tanh(y)
        for s in range(2):
            for b in range(batch):
                y_ref[b, pl.ds(2 * pr + s, nck, stride=L), :] = y[b * nck:(b + 1) * nck, s * W:(s + 1) * W]


def _s5_seq(u, mats, h0, batch, seq):
    taps, mm, pp, lam = mats
    L, W = S5_CHUNK, S5_BLOCK_GROUPS * S5_GROUP
    NB = S5_GROUPS // S5_BLOCK_GROUPS
    HS = S5_BLOCK_GROUPS * S5_STATE
    SW = 2 * HS
    ts = seq // S5_TIME_SLICES
    nck = ts // L
    rows = batch * nck
    u_spec = pl.BlockSpec((batch, ts, W), lambda i, t: (0, t, i))
    h_spec = pl.BlockSpec((None, batch, SW), lambda i, t: (i, 0, 0))

    def w_spec(*dims):
        return pl.BlockSpec((None,) + dims, lambda i, t: (i,) + (0,) * len(dims))

    return pl.pallas_call(
        functools.partial(_s5_seq_kernel, batch=batch, nck=nck),
        grid=(NB, S5_TIME_SLICES),
        in_specs=[u_spec, w_spec(L, W, S5_GROUP), w_spec(2, L * W, S5_STATE), w_spec(2, HS, L * S5_GROUP),
                  w_spec(1, SW), h_spec],
        out_specs=(u_spec, h_spec),
        out_shape=(jax.ShapeDtypeStruct((batch, seq, D_MODEL), F32), jax.ShapeDtypeStruct(h0.shape, F32)),
        scratch_shapes=[pltpu.VMEM(((L + 1) * W, 2 * W), BF16), pltpu.VMEM((L * W, SW), BF16),
                        pltpu.VMEM((SW, L * W), BF16),
                        pltpu.VMEM((rows, L * W), BF16), pltpu.VMEM((SW // W, rows, W), F32),
                        pltpu.VMEM((SW // W, rows, W), F32), pltpu.VMEM((batch, SW), F32)],
        compiler_params=_params("parallel", "arbitrary"),
        name="s5_seq",
    )(u, taps, mm, pp, lam, h0)


def _s5_mixer(x, u, h0_re, h0_im, s5p, w_glu, j, batch, seq):
    G, GC, P = S5_GROUPS, S5_GROUP, S5_STATE
    if seq > 1:
        NB, HS = G // S5_BLOCK_GROUPS, S5_BLOCK_GROUPS * P
        mats = _s5_block_mats(_s5_pieces(*[p[j] for p in s5p], S5_CHUNK))
        to_blocks = lambda h: h.reshape(batch, NB, HS).transpose(1, 0, 2)
        h0 = jnp.concatenate([to_blocks(h0_re), to_blocks(h0_im)], axis=-1)
        y, hfin = _s5_seq(u.reshape(batch, seq, D_MODEL), mats, h0, batch, seq)
        y = y.reshape(batch * seq, D_MODEL)
        from_blocks = lambda h: h.transpose(1, 0, 2).reshape(batch, G, P)
        hre, him = from_blocks(hfin[..., :HS]), from_blocks(hfin[..., HS:])
    else:
        mats = _s5_group_mats(_s5_pieces(*[p[j] for p in s5p], 1))
        ug = u.reshape(batch, G, GC).transpose(1, 0, 2)
        yg, hre, him = _s5_core(ug, mats, h0_re.transpose(1, 0, 2), h0_im.transpose(1, 0, 2),
                                1, batch, lax.Precision.HIGHEST, F32)
        y = yg.transpose(1, 0, 2).reshape(batch, D_MODEL)
        hre, him = hre.transpose(1, 0, 2), him.transpose(1, 0, 2)
    x = _proj(y, w_glu, j, glu=True, res=x)
    return x, hre, him


def _hg_lower_bound(logits, layer):
    m = jnp.max(logits, axis=0, keepdims=True)
    e = jnp.exp(logits - m)
    sm = e / jnp.sum(e, axis=0, keepdims=True)
    return jnp.sum(sm[:layer + 1], axis=0, keepdims=True) - sm[0:1]


def _hg_gates(z, lb):
    e = jnp.exp(-jnp.abs(z))
    r = 1.0 / (1.0 + e)
    er = e * r
    pos = z >= 0.0
    f = lb + (1.0 - lb) * jnp.where(pos, r, er)
    logf = jnp.where(f > 0.0, jnp.log(f), z)
    k = (1.0 - lb) * jnp.where(pos, er, r)
    return logf, k


def _hgrn_prompt_kernel(x_ref, g_ref, win_ref, lbl_ref, ng_ref, wout_ref, y_ref, sfin_ref,
                        s_scr, q_scr, k_scr, v_scr, gt_scr, o_scr, beta_scr, safe_scr, *, layer, tt):
    t = pl.program_id(1)
    C = HG_CHUNK

    @pl.when(t == 0)
    def _():
        s_scr[...] = jnp.zeros_like(s_scr)

    x = x_ref[...]
    h = _rms(x, g_ref[...]).astype(BF16)
    def proj(i):
        return jnp.dot(h, win_ref[:, i * HG_WIDTH:(i + 1) * HG_WIDTH], preferred_element_type=F32)

    lb = _hg_lower_bound(lbl_ref[...], layer)
    logf, kk = _hg_gates(proj(1), lb)
    k_scr[...] = kk
    q_scr[...] = _silu(proj(0))
    gt_scr[...] = _silu(proj(3))
    v_scr[...] = proj(2)

    row, col = _iota2((C, C))
    causal = row >= col
    tri = causal.astype(BF16)
    ng = ng_ref[...]
    nt_dims = (((1,), (1,)), ((), ()))
    tn_dims = (((0,), (0,)), ((), ()))

    def finish_head(hh, rows, q_dec, k_dec, o_intra, btot_h):
        sv = slice(hh * HG_DV, (hh + 1) * HG_DV)
        st = s_scr[hh]
        o = o_intra + lax.dot_general(q_dec.astype(BF16), st.astype(BF16), nt_dims, preferred_element_type=F32)
        kv_t = lax.dot_general(v_scr[rows, sv].astype(BF16), k_dec.astype(BF16), tn_dims,
                               preferred_element_type=F32)
        return (_rms(o, ng) * gt_scr[rows, sv]).astype(BF16), st * jnp.exp(btot_h) + kv_t

    def store_heads(rows, results):
        o_scr[rows, :] = jnp.concatenate([o for o, _ in results], axis=-1)
        for hh, (_, s_new) in enumerate(results):
            s_scr[hh] = s_new

    for ci in range(tt // C):
        parts = _split3(logf[ci * C:(ci + 1) * C])
        beta = sum(jnp.dot(tri, p, preferred_element_type=F32) for p in parts)
        beta_scr[ci * C:(ci + 1) * C, :] = beta
        mid = beta[C // 2 - 1:C // 2, :]
        spread = jnp.maximum(jnp.max(-mid), jnp.max(mid - beta[C - 1:C, :]))
        safe_scr[ci] = (spread <= HG_FACTORED_MAX_DECAY).astype(jnp.int32)

    def chunk_step(c, carry):
        rows = pl.ds(pl.multiple_of(c * C, C), C)
        base = pl.multiple_of(c * C, C)
        btot = beta_scr[pl.ds(base + (C - 8), 8), :][7:8]
        mid = beta_scr[pl.ds(base + (C // 2 - 8), 8), :][7:8]
        safe = safe_scr[c] == 1

        @pl.when(safe)
        def _():
            e_mid = jnp.exp(mid)
            e_tot = jnp.exp(btot - mid)
            results = []
            for hh in range(HG_HEADS):
                sk = slice(hh * HG_DK, (hh + 1) * HG_DK)
                d = beta_scr[rows, sk] - mid[:, sk]
                q_mid = q_scr[rows, sk] * jnp.exp(d)
                k_mid = k_scr[rows, sk] * jnp.exp(-d)
                att = lax.dot_general(q_mid.astype(BF16), k_mid.astype(BF16), nt_dims, preferred_element_type=F32)
                att = jnp.where(causal, att, 0.0).astype(BF16)
                o_intra = jnp.dot(att, v_scr[rows, hh * HG_DV:(hh + 1) * HG_DV].astype(BF16),
                                  preferred_element_type=F32)
                results.append(finish_head(hh, rows, q_mid * e_mid[:, sk], k_mid * e_tot[:, sk], o_intra,
                                           btot[:, sk]))
            store_heads(rows, results)

        @pl.when(jnp.logical_not(safe))
        def _():
            t_idx = lax.broadcasted_iota(jnp.int32, (C, 1), 0)
            results = []
            for hh in range(HG_HEADS):
                sk = slice(hh * HG_DK, (hh + 1) * HG_DK)
                sv = slice(hh * HG_DV, (hh + 1) * HG_DV)
                b = beta_scr[rows, sk]
                q = q_scr[rows, sk]

                def key_step(s8, acc):
                    off = pl.multiple_of(s8 * 8, 8)
                    b_keys = beta_scr[pl.ds(base + off, 8), sk]
                    k_keys = k_scr[pl.ds(base + off, 8), sk]
                    v_keys = v_scr[pl.ds(base + off, 8), sv]
                    for i in range(8):
                        w = jnp.exp(jnp.minimum(b - b_keys[i:i + 1], 0.0))
                        a = jnp.sum(q * k_keys[i:i + 1] * w, axis=-1, keepdims=True)
                        a = jnp.where(t_idx >= off + i, a, 0.0)
                        acc = acc + a * v_keys[i:i + 1]
                    return acc

                o_intra = lax.fori_loop(0, C // 8, key_step, jnp.zeros((C, HG_DV), F32))
                results.append(finish_head(hh, rows, q * jnp.exp(b), k_scr[rows, sk] * jnp.exp(btot[:, sk] - b),
                                           o_intra, btot[:, sk]))
            store_heads(rows, results)

        return carry

    lax.fori_loop(0, tt // C, chunk_step, 0)
    y_ref[...] = x + jnp.dot(o_scr[...], wout_ref[...], preferred_element_type=F32)

    @pl.when(t == pl.num_programs(1) - 1)
    def _():
        for hh in range(HG_HEADS):
            sfin_ref[hh] = s_scr[hh].T


def _hgrn_prompt(x, g, w_in, lb_logits, norm_g, w_out, layer, j, batch, seq):
    tt = HG_ROW_TILE
    nt = seq // tt
    x3 = x.reshape(batch, seq, D_MODEL)
    row_spec = pl.BlockSpec((None, tt, D_MODEL), lambda b, t: (b, t, 0))
    y, s_fin = pl.pallas_call(
        functools.partial(_hgrn_prompt_kernel, layer=layer, tt=tt),
        grid=(batch, nt),
        in_specs=[row_spec,
                  pl.BlockSpec((None, 1, D_MODEL), lambda b, t: (layer, 0, 0)),
                  _resident((None, D_MODEL, 4 * HG_WIDTH), lambda b, t: (j, 0, 0)),
                  pl.BlockSpec((DEPTH, HG_WIDTH), lambda b, t: (0, 0)),
                  pl.BlockSpec((None, 1, HG_DV), lambda b, t: (j, 0, 0)),
                  _resident((None, HG_WIDTH, D_MODEL), lambda b, t: (j, 0, 0))],
        out_specs=(row_spec,
                   pl.BlockSpec((None, HG_HEADS, HG_DK, HG_DV), lambda b, t: (b, 0, 0, 0))),
        out_shape=(jax.ShapeDtypeStruct((batch, seq, D_MODEL), F32),
                   jax.ShapeDtypeStruct((batch, HG_HEADS, HG_DK, HG_DV), F32)),
        scratch_shapes=[pltpu.VMEM((HG_HEADS, HG_DV, HG_DK), F32),
                        pltpu.VMEM((tt, HG_WIDTH), F32), pltpu.VMEM((tt, HG_WIDTH), F32),
                        pltpu.VMEM((tt, HG_WIDTH), F32), pltpu.VMEM((tt, HG_WIDTH), F32),
                        pltpu.VMEM((tt, HG_WIDTH), BF16), pltpu.VMEM((tt, HG_WIDTH), F32),
                        pltpu.SMEM((tt // HG_CHUNK,), jnp.int32)],
        compiler_params=_params("parallel", "arbitrary"),
        name="hgrn_prompt",
    )(x3, g.reshape(DEPTH, 1, D_MODEL), w_in, lb_logits, norm_g.reshape(-1, 1, HG_DV), w_out)
    return y.reshape(batch * seq, D_MODEL), s_fin


def _hgrn_sample_kernel(proj_ref, lbl_ref, ng_ref, s_ref, snew_ref, o_ref, *, layer, tb):
    proj = proj_ref[...]
    lb = _hg_lower_bound(lbl_ref[...], layer)
    z = proj[:, HG_WIDTH:2 * HG_WIDTH]
    e = jnp.exp(-jnp.abs(z))
    r = 1.0 / (1.0 + e)
    sig = jnp.where(z >= 0.0, r, e * r)
    f = lb + (1.0 - lb) * sig
    k = (1.0 - lb) * jnp.where(z >= 0.0, e * r, r)
    q = _silu(proj[:, :HG_WIDTH])
    v = proj[:, 2 * HG_WIDTH:3 * HG_WIDTH]
    gt = _silu(proj[:, 3 * HG_WIDTH:])
    ng = ng_ref[...]
    tok, lane = _iota2((tb, tb * HG_DV))
    spread = ((lane // HG_DV) == tok).astype(BF16)

    spread3 = jnp.concatenate([spread] * 3, axis=0)

    def columns(x, exact):
        if exact:
            return jnp.dot(jnp.concatenate(_split3(x.T), axis=1), spread3, preferred_element_type=F32)
        return jnp.dot(x.T.astype(BF16), spread, preferred_element_type=F32)

    for hh in range(HG_HEADS):
        sk = slice(hh * HG_DK, (hh + 1) * HG_DK)
        sv = slice(hh * HG_DV, (hh + 1) * HG_DV)
        f_c, k_c, q_c = columns(f[:, sk], True), columns(k[:, sk], False), columns(q[:, sk], False)
        for b in range(tb):
            blk = slice(b * HG_DV, (b + 1) * HG_DV)
            s_new = f_c[:, blk] * s_ref[b, hh] + k_c[:, blk] * v[b:b + 1, sv]
            snew_ref[b, hh] = s_new
            o = jnp.sum(q_c[:, blk] * s_new, axis=0, keepdims=True)
            o_ref[b:b + 1, sv] = _rms(o, ng) * gt[b:b + 1, sv]


def _hgrn_sample(proj, lb_logits, norm_g, state, layer, j):
    nb = proj.shape[0]
    tb = SAMPLE_TOKENS_PER_STEP
    st_spec = pl.BlockSpec((tb, HG_HEADS, HG_DK, HG_DV), lambda i: (i, 0, 0, 0))
    return pl.pallas_call(
        functools.partial(_hgrn_sample_kernel, layer=layer, tb=tb),
        grid=(nb // tb,),
        in_specs=[pl.BlockSpec((tb, 4 * HG_WIDTH), lambda i: (i, 0)),
                  pl.BlockSpec((DEPTH, HG_WIDTH), lambda i: (0, 0)),
                  pl.BlockSpec((None, 1, HG_DV), lambda i: (j, 0, 0)),
                  st_spec],
        out_specs=(st_spec, pl.BlockSpec((tb, HG_WIDTH), lambda i: (i, 0))),
        out_shape=(jax.ShapeDtypeStruct(state.shape, F32), jax.ShapeDtypeStruct((nb, HG_WIDTH), F32)),
        compiler_params=_params("parallel"),
        name="hgrn_sample",
    )(proj, lb_logits, norm_g.reshape(-1, 1, HG_DV), state)


def _xattn_prompt_kernel(x_ref, g_ref, wq_ref, k_ref, v_ref, wo_ref, y_ref):
    x = x_ref[...]
    h = _rms(x, g_ref[...]).astype(BF16)
    q = jnp.dot(h, wq_ref[...], preferred_element_type=F32) * (1.0 / math.sqrt(MEM_HD))
    q = q.astype(BF16)
    outs = []
    for hh in range(MEM_HEADS):
        sl = slice(hh * MEM_HD, (hh + 1) * MEM_HD)
        s = lax.dot_general(q[:, sl], k_ref[:, sl], (((1,), (1,)), ((), ())), preferred_element_type=F32)
        p = jnp.exp(s - jnp.max(s, axis=-1, keepdims=True))
        den = jnp.sum(p, axis=-1, keepdims=True)
        o = jnp.dot(p.astype(BF16), v_ref[:, sl], preferred_element_type=F32)
        outs.append((o / den).astype(BF16))
    o = jnp.concatenate(outs, axis=-1)
    y_ref[...] = x + jnp.dot(o, wo_ref[...], preferred_element_type=F32)


def _xattn_prompt(x, g, w_q, mem_k, mem_v, w_o, layer, batch, seq):
    tt = XA_ROW_TILE
    row_spec = pl.BlockSpec((None, tt, D_MODEL), lambda b, t: (b, t, 0))
    kv_spec = pl.BlockSpec((None, N_MEM, D_MODEL), lambda b, t: (layer, b, 0))
    w_spec = _resident((None, D_MODEL, D_MODEL), lambda b, t: (layer, 0, 0))
    y = pl.pallas_call(
        _xattn_prompt_kernel,
        grid=(batch, seq // tt),
        in_specs=[row_spec, pl.BlockSpec((None, 1, D_MODEL), lambda b, t: (layer, 0, 0)),
                  w_spec, kv_spec, kv_spec, w_spec],
        out_specs=row_spec,
        out_shape=jax.ShapeDtypeStruct((batch, seq, D_MODEL), F32),
        compiler_params=_params("parallel", "parallel"),
        name="xattn_prompt",
    )(x.reshape(batch, seq, D_MODEL), g.reshape(DEPTH, 1, D_MODEL), w_q, mem_k, mem_v, w_o)
    return y.reshape(batch * seq, D_MODEL)


def _xattn_sample_kernel(q_ref, k_ref, v_ref, o_ref, *, tb):
    scale = 1.0 / math.sqrt(MEM_HD)
    rows = N_MEM * MEM_HEADS
    head, lane = _iota2((MEM_HEADS, rows))
    own = (lane & (MEM_HEADS - 1)) == head
    for b in range(tb):
        q = (q_ref[b] * scale).astype(BF16)
        k = k_ref[b].reshape(rows, MEM_HD).astype(BF16)
        v = v_ref[b].reshape(rows, MEM_HD).astype(BF16)
        s = lax.dot_general(q, k, (((1,), (1,)), ((), ())), preferred_element_type=F32)
        s = jnp.where(own, s, -jnp.inf)
        p = jnp.exp(s - jnp.max(s, axis=-1, keepdims=True))
        den = jnp.sum(p, axis=-1, keepdims=True)
        o_ref[b] = jnp.dot(p.astype(BF16), v, preferred_element_type=F32) / den


def _xattn_sample(q, cache_k, cache_v, layer):
    nb = q.shape[0]
    tb = XA_SAMPLE_TOKENS_PER_STEP
    kv_spec = pl.BlockSpec((None, tb, N_MEM, MEM_HEADS, MEM_HD), lambda i: (layer, i, 0, 0, 0))
    q_spec = pl.BlockSpec((tb, MEM_HEADS, MEM_HD), lambda i: (i, 0, 0))
    return pl.pallas_call(
        functools.partial(_xattn_sample_kernel, tb=tb),
        grid=(nb // tb,),
        in_specs=[q_spec, kv_spec, kv_spec],
        out_specs=q_spec,
        out_shape=jax.ShapeDtypeStruct((nb, MEM_HEADS, MEM_HD), F32),
        compiler_params=_params("parallel"),
        name="xattn_sample",
    )(q.reshape(nb, MEM_HEADS, MEM_HD), cache_k, cache_v).reshape(nb, D_MODEL)


def _ffn1(x, w, i, ffn, emit):
    if emit:
        return ffn(x, w["ffn1_norm"], w["ffn1_w_in"], w["ffn1_w_out"], i, post="emit", g2=w["mix_norm"][i],
                   u_dtype=F32)
    return ffn(x, w["ffn1_norm"], w["ffn1_w_in"], w["ffn1_w_out"], i)


def _ffn2(x, w, i, ffn):
    last = i == DEPTH - 1
    return ffn(x, w["ffn2_norm"], w["ffn2_w_in"], w["ffn2_w_out"], i,
               post="replace" if last else "none", g2=w["final_norm"] if last else None)


def _prompt_trunk(x, batch, seq, mem_k, mem_v, w, ffn):
    zero = jnp.zeros((batch, S5_GROUPS, S5_STATE), F32)
    new_re, new_im, new_hg = [], [], []
    for i in range(DEPTH):
        j = i // 2
        if i % 2 == 0:
            x, u = _ffn1(x, w, i, ffn, True)
            x, hr, hi = _s5_mixer(x, u, zero, zero, w["s5"], w["s5_w_glu"], j, batch, seq)
            new_re.append(hr)
            new_im.append(hi)
        else:
            x = _ffn1(x, w, i, ffn, False)
            x, sn = _hgrn_prompt(x, w["mix_norm"], w["hg_w_in"], w["hg_lb_logits"], w["hg_norm"],
                                 w["hg_w_out"], i, j, batch, seq)
            new_hg.append(sn)
        x = _xattn_prompt(x, w["xattn_norm"], w["xattn_w_q"], mem_k, mem_v, w["xattn_w_o"], i, batch, seq)
        x = _ffn2(x, w, i, ffn)
    return x, jnp.stack(new_re), jnp.stack(new_im), jnp.stack(new_hg)


def _sample_trunk(x, batch, s5_re, s5_im, hg_state, w):
    new_re, new_im, new_hg = [], [], []
    for i in range(DEPTH):
        j = i // 2
        if i % 2 == 0:
            x, u = _ffn1(x, w, i, _ffn, True)
            x, hr, hi = _s5_mixer(x, u, s5_re[j], s5_im[j], w["s5"], w["s5_w_glu"], j, batch, 1)
            new_re.append(hr)
            new_im.append(hi)
        else:
            x = _ffn1(x, w, i, _ffn, False)
            proj = _proj(x, w["hg_w_in"], j, g=w["mix_norm"], g_layer=i)
            sn, o = _hgrn_sample(proj, w["hg_lb_logits"], w["hg_norm"], hg_state[j], i, j)
            x = _proj(o, w["hg_w_out"], j, res=x)
            new_hg.append(sn)
        q = _proj(x, w["xattn_w_q"], i, g=w["xattn_norm"], g_layer=i)
        o = yield q, i
        x = _proj(o, w["xattn_w_o"], i, res=x)
        x = _ffn2(x, w, i, _ffn)
    return x, jnp.stack(new_re), jnp.stack(new_im), jnp.stack(new_hg)


def kernel(x_prompt, x_sample, mem_prompt, state_s5_re, state_s5_im, state_hgrn, cache_mem_k, cache_mem_v, ffn1_norm, ffn1_w_in, ffn1_w_out, mix_norm, xattn_norm, mem_norm, xattn_w_q, xattn_w_kv, xattn_w_o, ffn2_norm, ffn2_w_in, ffn2_w_out, s5_a_re, s5_a_im, s5_log_dt, s5_b_re, s5_b_im, s5_c_re, s5_c_im, s5_d, s5_w_glu, hg_w_in, hg_lb_logits, hg_norm, hg_w_out, final_norm):
    bp, seq, _ = x_prompt.shape
    bs = x_sample.shape[0]
    bf = lambda a: a.astype(BF16)
    w = dict(ffn1_norm=ffn1_norm, ffn1_w_in=bf(ffn1_w_in), ffn1_w_out=bf(ffn1_w_out), mix_norm=mix_norm,
             xattn_norm=xattn_norm, xattn_w_q=bf(xattn_w_q), xattn_w_o=bf(xattn_w_o), ffn2_norm=ffn2_norm,
             ffn2_w_in=bf(ffn2_w_in), ffn2_w_out=bf(ffn2_w_out),
             s5=(s5_a_re, s5_a_im, s5_log_dt, s5_b_re, s5_b_im, s5_c_re, s5_c_im, s5_d),
             s5_w_glu=bf(s5_w_glu), hg_w_in=bf(hg_w_in), hg_lb_logits=hg_lb_logits, hg_norm=hg_norm,
             hg_w_out=bf(hg_w_out), final_norm=final_norm)

    mem_k, mem_v, mem_k_bf, mem_v_bf = _mem_kv(mem_prompt, mem_norm, bf(xattn_w_kv))

    sample = _sample_trunk(x_sample.reshape(bs, D_MODEL), bs, state_s5_re, state_s5_im, state_hgrn, w)
    pending = [next(sample)]
    sample_out = []

    def resume(o):
        try:
            pending[0] = sample.send(o)
        except StopIteration as done:
            pending[0] = None
            sample_out.append(done.value)

    def ffn_with_rider(x, *args, **kwargs):
        steps = x.shape[0] // min(ROW_TILE, x.shape[0])
        if pending[0] is None or bs % steps:
            return _ffn(x, *args, **kwargs)
        q, layer = pending[0]
        *outs, o = _ffn(x, *args, rider=(q, cache_mem_k, cache_mem_v, layer), **kwargs)
        resume(o)
        return outs[0] if len(outs) == 1 else tuple(outs)

    y_p, re_p, im_p, hg_p = _prompt_trunk(x_prompt.reshape(bp * seq, D_MODEL), bp, seq, mem_k_bf, mem_v_bf, w,
                                          ffn_with_rider)
    while pending[0] is not None:
        q, layer = pending[0]
        resume(_xattn_sample(q, cache_mem_k, cache_mem_v, layer))
    y_s, re_s, im_s, hg_s = sample_out[0]
    return (y_p.reshape(bp, seq, D_MODEL), y_s.reshape(bs, 1, D_MODEL), re_p, im_p, re_s, im_s, hg_p, hg_s,
            mem_k, mem_v)
```

```python
import functools
import math

import jax
import jax.numpy as jnp
from jax import lax
from jax.experimental import pallas as pl
from jax.experimental.pallas import tpu as pltpu

F32 = jnp.float32
BF16 = jnp.bfloat16

D_MODEL = 1024
DEPTH = 2
S5_GROUP = 16
S5_GROUPS = D_MODEL // S5_GROUP
S5_STATE = 64
S5_CHUNK = 16
S5_BLOCK_GROUPS = 8
S5_TIME_SLICES = 2
HG_DK = 128
HG_HEADS = D_MODEL // HG_DK
HG_DV = D_MODEL // HG_HEADS
HG_WIDTH = HG_HEADS * HG_DK
HG_CHUNK = 128
HG_FACTORED_MAX_DECAY = 60.0
N_MEM = 256
MEM_HEADS = 4
MEM_HD = D_MODEL // MEM_HEADS
FFN_DIM = 2816
EPS = 1e-6

V7X_VMEM_LIMIT_BYTES = 56 * 1024 * 1024

ROW_TILE = 512
FFN_CHUNK = 1408
FFN_ROW_TILE_WITH_RIDER = 256
HG_ROW_TILE = 512
XA_ROW_TILE = 512
S5_GROUPS_PER_STEP = 16
SAMPLE_TOKENS_PER_STEP = 8
XA_SAMPLE_TOKENS_PER_STEP = 4


def _params(*semantics):
    return pltpu.CompilerParams(dimension_semantics=semantics,
                                vmem_limit_bytes=V7X_VMEM_LIMIT_BYTES)


def _resident(shape, index_map):
    return pl.BlockSpec(shape, index_map, pipeline_mode=pl.Buffered(1))


def _rms(x, g):
    ms = jnp.mean(x * x, axis=-1, keepdims=True)
    return x * lax.rsqrt(ms + EPS) * g


def _sigmoid(x):
    return 1.0 / (1.0 + jnp.exp(-x))


def _silu(x):
    return x * _sigmoid(x)


def _gelu_tanh(x):
    c = math.sqrt(2.0 / math.pi)
    return 0.5 * x * (1.0 + jnp.tanh(c * (x + 0.044715 * (x * x * x))))


def _dot(a, w):
    return jnp.dot(a, w.astype(BF16), preferred_element_type=F32)


def _split3(x):
    hi = x.astype(BF16)
    r1 = x - hi.astype(F32)
    mid = r1.astype(BF16)
    lo = (r1 - mid.astype(F32)).astype(BF16)
    return hi, mid, lo


def _ffn_kernel(x_ref, g_ref, win_ref, wout_ref, *rest, post, rider_tokens):
    rest = list(rest)
    g2_ref = rest.pop(0) if post != "none" else None
    if rider_tokens:
        q_ref, k_ref, v_ref = rest[:3]
        rest = rest[3:]
        _xattn_sample_kernel(q_ref, k_ref, v_ref, rest.pop(), tb=rider_tokens)
    x = x_ref[...]
    h = _rms(x, g_ref[...]).astype(BF16)
    acc = jnp.zeros_like(x)
    for c in range(FFN_DIM // FFN_CHUNK):
        lo = c * FFN_CHUNK
        gate = _dot(h, win_ref[:, lo:lo + FFN_CHUNK])
        up = _dot(h, win_ref[:, FFN_DIM + lo:FFN_DIM + lo + FFN_CHUNK])
        act = (_silu(gate) * up).astype(BF16)
        acc = acc + _dot(act, wout_ref[lo:lo + FFN_CHUNK, :])
    y = x + 0.5 * acc
    y_ref = rest[0]
    if post == "replace":
        y_ref[...] = _rms(y, g2_ref[...])
    else:
        y_ref[...] = y
        if post == "emit":
            rest[1][...] = _rms(y, g2_ref[...]).astype(rest[1].dtype)


def _ffn(x, g, w_in, w_out, layer, post="none", g2=None, u_dtype=BF16, rider=None):
    rows = x.shape[0]
    tm = min(ROW_TILE if rider is None else FFN_ROW_TILE_WITH_RIDER, rows)
    steps = rows // tm
    row_spec = pl.BlockSpec((tm, D_MODEL), lambda i: (i, 0))
    vec_spec = pl.BlockSpec((1, D_MODEL), lambda i: (0, 0))
    in_specs = [row_spec,
                pl.BlockSpec((None, 1, D_MODEL), lambda i: (layer, 0, 0)),
                _resident((None, D_MODEL, 2 * FFN_DIM), lambda i: (layer, 0, 0)),
                _resident((None, FFN_DIM, D_MODEL), lambda i: (layer, 0, 0))]
    args = [x, g.reshape(DEPTH, 1, D_MODEL), w_in, w_out]
    out_shape = [jax.ShapeDtypeStruct((rows, D_MODEL), F32)]
    out_specs = [row_spec]
    if post != "none":
        in_specs.append(vec_spec)
        args.append(g2.reshape(1, D_MODEL))
    if post == "emit":
        out_shape.append(jax.ShapeDtypeStruct((rows, D_MODEL), u_dtype))
        out_specs.append(row_spec)
    tb = 0
    if rider is not None:
        q, cache_k, cache_v, r_layer = rider
        nb = q.shape[0]
        assert nb % steps == 0
        tb = nb // steps
        kv_spec = pl.BlockSpec((None, tb, N_MEM, MEM_HEADS, MEM_HD), lambda i: (r_layer, i, 0, 0, 0))
        q_spec = pl.BlockSpec((tb, MEM_HEADS, MEM_HD), lambda i: (i, 0, 0))
        in_specs += [q_spec, kv_spec, kv_spec]
        args += [q.reshape(nb, MEM_HEADS, MEM_HD), cache_k, cache_v]
        out_shape.append(jax.ShapeDtypeStruct((nb, MEM_HEADS, MEM_HD), F32))
        out_specs.append(q_spec)
    outs = list(pl.pallas_call(
        functools.partial(_ffn_kernel, post=post, rider_tokens=tb),
        grid=(steps,),
        in_specs=in_specs, out_specs=tuple(out_specs), out_shape=tuple(out_shape),
        compiler_params=_params("parallel"),
        name="ffn",
    )(*args))
    if rider is not None:
        outs[-1] = outs[-1].reshape(-1, D_MODEL)
    return outs[0] if len(outs) == 1 else tuple(outs)


def _proj_kernel(*refs, norm, glu, residual):
    refs = list(refs)
    x_ref = refs.pop(0)
    g_ref = refs.pop(0) if norm else None
    w_ref = refs.pop(0)
    res_ref = refs.pop(0) if residual else None
    (o_ref,) = refs
    x = x_ref[...]
    if norm:
        x = _rms(x, g_ref[...])
    y = _dot(x.astype(BF16), w_ref[...])
    if glu:
        half = y.shape[-1] // 2
        y = y[:, :half] * _sigmoid(y[:, half:])
    if residual:
        y = y + res_ref[...]
    o_ref[...] = y


def _proj(x, w, layer, g=None, g_layer=0, glu=False, res=None):
    rows, kdim = x.shape
    ndim = w.shape[-1]
    nout = ndim // 2 if glu else ndim
    tm = min(ROW_TILE, rows)
    in_specs = [pl.BlockSpec((tm, kdim), lambda i: (i, 0))]
    args = [x]
    if g is not None:
        in_specs.append(pl.BlockSpec((None, 1, kdim), lambda i: (g_layer, 0, 0)))
        args.append(g.reshape(g.shape[0], 1, kdim))
    in_specs.append(_resident((None, kdim, ndim), lambda i: (layer, 0, 0)))
    args.append(w)
    if res is not None:
        in_specs.append(pl.BlockSpec((tm, nout), lambda i: (i, 0)))
        args.append(res)
    return pl.pallas_call(
        functools.partial(_proj_kernel, norm=g is not None, glu=glu, residual=res is not None),
        grid=(rows // tm,),
        in_specs=in_specs,
        out_specs=pl.BlockSpec((tm, nout), lambda i: (i, 0)),
        out_shape=jax.ShapeDtypeStruct((rows, nout), F32),
        compiler_params=_params("parallel"),
        name="proj",
    )(*args)


def _memkv_kernel(x_ref, g_ref, w_ref, k_ref, v_ref, kb_ref, vb_ref):
    h = _rms(x_ref[...], g_ref[...]).astype(BF16)
    y = _dot(h, w_ref[...])
    k, v = y[:, :D_MODEL], y[:, D_MODEL:]
    k_ref[...] = k.reshape(k_ref.shape)
    v_ref[...] = v.reshape(v_ref.shape)
    kb_ref[...] = k.astype(BF16)
    vb_ref[...] = v.astype(BF16)


def _mem_kv(mem, g, w_kv):
    batch = mem.shape[0]
    rows = batch * N_MEM
    nb = max(1, min(ROW_TILE, rows) // N_MEM)
    tm = nb * N_MEM
    out5 = jax.ShapeDtypeStruct((DEPTH, batch, N_MEM, MEM_HEADS, MEM_HD), F32)
    out2 = jax.ShapeDtypeStruct((DEPTH, rows, D_MODEL), BF16)
    spec5 = pl.BlockSpec((None, nb, N_MEM, MEM_HEADS, MEM_HD), lambda l, i: (l, i, 0, 0, 0))
    spec2 = pl.BlockSpec((None, tm, D_MODEL), lambda l, i: (l, i, 0))
    return pl.pallas_call(
        _memkv_kernel,
        grid=(DEPTH, rows // tm),
        in_specs=[pl.BlockSpec((tm, D_MODEL), lambda l, i: (i, 0)),
                  pl.BlockSpec((None, 1, D_MODEL), lambda l, i: (l, 0, 0)),
                  pl.BlockSpec((None, D_MODEL, 2 * D_MODEL), lambda l, i: (l, 0, 0))],
        out_specs=(spec5, spec5, spec2, spec2), out_shape=(out5, out5, out2, out2),
        compiler_params=_params("parallel", "parallel"),
        name="mem_kv",
    )(mem.reshape(rows, D_MODEL), g.reshape(DEPTH, 1, D_MODEL), w_kv)


def _s5_pieces(a_re, a_im, log_dt, b_re, b_im, c_re, c_im, d, steps):
    G, P, GC = S5_GROUPS, S5_STATE, S5_GROUP
    L = steps
    dt = jnp.exp(log_dt)[:, None]
    xr, xi = a_re * dt, a_im * dt
    j = jnp.arange(L + 1, dtype=F32)[:, None, None]
    mag = jnp.exp(xr[None] * j)
    pw_re, pw_im = mag * jnp.cos(xi[None] * j), mag * jnp.sin(xi[None] * j)
    nr, ni = pw_re[1] - 1.0, pw_im[1]
    den = a_re * a_re + a_im * a_im
    fr, fi = (nr * a_re + ni * a_im) / den, (ni * a_re - nr * a_im) / den
    bb_re = fr[..., None] * b_re - fi[..., None] * b_im
    bb_im = fr[..., None] * b_im + fi[..., None] * b_re
    hp = lax.Precision.HIGHEST
    w_re = pw_re[:L, :, None, :] * c_re[None] - pw_im[:L, :, None, :] * c_im[None]
    w_im = pw_re[:L, :, None, :] * c_im[None] + pw_im[:L, :, None, :] * c_re[None]
    kern = (jnp.einsum("gpi,lgop->lgio", bb_re, w_re, precision=hp)
            - jnp.einsum("gpi,lgop->lgio", bb_im, w_im, precision=hp))
    kern = kern.at[0].add(d[:, :, None] * jnp.eye(GC, dtype=F32)[None])
    rp_re, rp_im = pw_re[:L][::-1], pw_im[:L][::-1]
    m_re = rp_re[:, :, :, None] * bb_re[None] - rp_im[:, :, :, None] * bb_im[None]
    m_im = rp_re[:, :, :, None] * bb_im[None] + rp_im[:, :, :, None] * bb_re[None]
    m_re = m_re.transpose(1, 0, 3, 2)
    m_im = m_im.transpose(1, 0, 3, 2)
    q_re = pw_re[1:, :, None, :] * c_re[None] - pw_im[1:, :, None, :] * c_im[None]
    q_im = pw_re[1:, :, None, :] * c_im[None] + pw_im[1:, :, None, :] * c_re[None]
    p_re = q_re.transpose(1, 3, 0, 2)
    p_im = (-q_im).transpose(1, 3, 0, 2)
    return kern, m_re, m_im, p_re, p_im, pw_re[L], pw_im[L]


def _s5_group_mats(pieces):
    kern, m_re, m_im, p_re, p_im, l_re, l_im = pieces
    return (kern[0], m_re[:, 0], m_im[:, 0], p_re[:, :, 0], p_im[:, :, 0], l_re[:, None, :], l_im[:, None, :])


def _s5_block_mats(pieces):
    kern, m_re, m_im, p_re, p_im, l_re, l_im = lax.optimization_barrier(pieces)
    L = kern.shape[0]
    GB = S5_BLOCK_GROUPS
    NB = S5_GROUPS // GB
    GC, P = S5_GROUP, S5_STATE
    taps =kern.astype(BF16).reshape(L, NB, GB * GC, GC).transpose(1, 0, 2, 3)

    def m_rows(m):
        return m.astype(BF16).reshape(NB, GB, L, GC, P).transpose(0, 2, 1, 3, 4).reshape(NB, L * GB * GC, P)

    mm = jnp.stack([m_rows(m_re), m_rows(m_im)], axis=1)
    pp = jnp.stack([p_re.astype(BF16).reshape(NB, GB * P, L * GC), p_im.astype(BF16).reshape(NB, GB * P, L * GC)],
                   axis=1)
    lam = jnp.concatenate([l_re.reshape(NB, 1, GB * P), l_im.reshape(NB, 1, GB * P)], axis=2)
    return taps, mm, pp, lam


def _s5_kernel(u_ref, t_ref, mre_ref, mim_ref, pre_ref, pim_ref, lre_ref, lim_ref, h0re_ref, h0im_ref,
               y_ref, hre_ref, him_ref, inj_re, inj_im, hs_re, hs_im, *, n_chunks, rb, gps, precision):
    def mm(a, b):
        return jnp.dot(a, b, preferred_element_type=F32, precision=precision)

    for g in range(gps):
        u = u_ref[g]
        inj_re[g] = mm(u, mre_ref[g])
        inj_im[g] = mm(u, mim_ref[g])

    lam_re = [jnp.broadcast_to(lre_ref[g], (rb, S5_STATE)) for g in range(gps)]
    lam_im = [jnp.broadcast_to(lim_ref[g], (rb, S5_STATE)) for g in range(gps)]

    def step(k, carry):
        rows = pl.ds(pl.multiple_of(k * rb, rb), rb)
        nxt = []
        for g in range(gps):
            hr, hi = carry[2 * g], carry[2 * g + 1]
            hs_re[g, rows, :] = hr
            hs_im[g, rows, :] = hi
            nxt.append(lam_re[g] * hr - lam_im[g] * hi + inj_re[g, rows, :])
            nxt.append(lam_re[g] * hi + lam_im[g] * hr + inj_im[g, rows, :])
        return tuple(nxt)

    init = []
    for g in range(gps):
        init += [h0re_ref[g], h0im_ref[g]]
    fin = lax.fori_loop(0, n_chunks, step, tuple(init))

    for g in range(gps):
        hre_ref[g] = fin[2 * g]
        him_ref[g] = fin[2 * g + 1]
        dt = u_ref.dtype
        y = (mm(u_ref[g], t_ref[g]) + mm(hs_re[g].astype(dt), pre_ref[g])
             + mm(hs_im[g].astype(dt), pim_ref[g]))
        y_ref[g] = _gelu_tanh(y).astype(y_ref.dtype)


def _s5_core(u, mats, h0_re, h0_im, n_chunks, rb, precision, y_dtype):
    G, R, W = u.shape
    P = S5_STATE
    gps = S5_GROUPS_PER_STEP
    tmat, m_re, m_im, p_re, p_im, l_re, l_im = mats

    def spec(a, b):
        return pl.BlockSpec((gps, a, b), lambda i: (i, 0, 0))

    st = jax.ShapeDtypeStruct((G, rb, P), F32)
    scr = lambda: pltpu.VMEM((gps, R, P), F32)
    return pl.pallas_call(
        functools.partial(_s5_kernel, n_chunks=n_chunks, rb=rb, gps=gps, precision=precision),
        grid=(G // gps,),
        in_specs=[spec(R, W), spec(W, W), spec(W, P), spec(W, P), spec(P, W), spec(P, W),
                  spec(1, P), spec(1, P), spec(rb, P), spec(rb, P)],
        out_specs=(spec(R, W), spec(rb, P), spec(rb, P)),
        out_shape=(jax.ShapeDtypeStruct((G, R, W), y_dtype), st, st),
        scratch_shapes=[scr(), scr(), scr(), scr()],
        compiler_params=_params("parallel"),
        name="s5_core",
    )(u, tmat, m_re, m_im, p_re, p_im, l_re, l_im, h0_re, h0_im)


def _iota2(shape):
    return lax.broadcasted_iota(jnp.int32, shape, 0), lax.broadcasted_iota(jnp.int32, shape, 1)


def _s5_expand(taps_ref, mc_ref, pc_ref, w2_ref, m_ref, p_ref):
    L, GC, P = S5_CHUNK, S5_GROUP, S5_STATE
    GB = S5_BLOCK_GROUPS
    W, HS = GB * GC, GB * P
    gc_bits, p_bits, w_bits = GC.bit_length() - 1, P.bit_length() - 1, W.bit_length() - 1

    r, c = _iota2((P, HS))
    rep_m = ((c & (P - 1)) == r).astype(BF16)
    r, c = _iota2((L * W, HS))
    mask_m = ((r >> gc_bits) & (GB - 1)) == (c >> p_bits)
    r, c = _iota2((L * GC, L * W))
    rep_p = (r == (((c >> w_bits) << gc_bits) | (c & (GC - 1)))).astype(BF16)
    r, c = _iota2((HS, L * W))
    mask_p = (r >> p_bits) == ((c >> gc_bits) & (GB - 1))
    for half in range(2):
        m_ref[:, half * HS:(half + 1) * HS] = jnp.where(
            mask_m, jnp.dot(mc_ref[half], rep_m, preferred_element_type=F32), 0.0).astype(BF16)
        p_ref[half * HS:(half + 1) * HS, :] = jnp.where(
            mask_p, jnp.dot(pc_ref[half], rep_p, preferred_element_type=F32), 0.0).astype(BF16)

    r, c = _iota2((GC, W))
    rep_k = ((c & (GC - 1)) == r).astype(BF16)
    r, c = _iota2((W, W))
    mask_k = (r >> gc_bits) == (c >> gc_bits)
    zero = jnp.zeros((W, W), BF16)
    w2_ref[L * W:, :W] = zero
    w2_ref[:W, W:] = zero
    for i in range(L):
        kb = jnp.where(mask_k, jnp.dot(taps_ref[L - 1 - i], rep_k, preferred_element_type=F32), 0.0).astype(BF16)
        w2_ref[i * W:(i + 1) * W, :W] = kb
        w2_ref[(i + 1) * W:(i + 2) * W, W:] = kb


def _s5_seq_kernel(u_ref, taps_ref, mc_ref, pc_ref, lam_ref, h0_ref, y_ref, hfin_ref,
                   w2_ref, m_ref, p_ref, lhs_scr, inj_scr, hs_scr, h_scr, *, batch, nck):
    L, W = S5_CHUNK, S5_BLOCK_GROUPS * S5_GROUP
    NT = S5_BLOCK_GROUPS * S5_STATE // W

    @pl.when(pl.program_id(1) == 0)
    def _():
        h_scr[...] = h0_ref[...]
        _s5_expand(taps_ref, mc_ref, pc_ref, w2_ref, m_ref, p_ref)

    for b in range(batch):
        for l in range(L):
            lhs_scr[b * nck:(b + 1) * nck, l * W:(l + 1) * W] = u_ref[b, pl.ds(l, nck, stride=L), :].astype(BF16)

    def swap_major(x, a, b):
        return jnp.swapaxes(x.reshape(a, b, W), 0, 1).reshape(a * b, W)

    inj = jnp.dot(lhs_scr[...], m_ref[...], preferred_element_type=F32)
    for t in range(2 * NT):
        inj_scr[t] = swap_major(inj[:, t * W:(t + 1) * W], batch, nck)

    lam = [jnp.broadcast_to(lam_ref[:, t * W:(t + 1) * W], (batch, W)) for t in range(2 * NT)]

    def step(k, h):
        rows = pl.ds(pl.multiple_of(k * batch, batch), batch)
        nxt_re, nxt_im = [], []
        for t in range(NT):
            hr, hi = h[t], h[NT + t]
            hs_scr[t, rows, :] = hr
            hs_scr[NT + t, rows, :] = hi
            nxt_re.append(lam[t] * hr - lam[NT + t] * hi + inj_scr[t, rows, :])
            nxt_im.append(lam[t] * hi + lam[NT + t] * hr + inj_scr[NT + t, rows, :])
        return tuple(nxt_re + nxt_im)

    h = lax.fori_loop(0, nck, step, tuple(h_scr[:, t * W:(t + 1) * W] for t in range(2 * NT)))
    for t in range(2 * NT):
        h_scr[:, t * W:(t + 1) * W] = h[t]
    hfin_ref[...] = h_scr[...]

    hs = jnp.concatenate([swap_major(hs_scr[t], nck, batch) for t in range(2 * NT)], axis=-1).astype(BF16)
    for pr in range(L // 2):
        kk = (2 * pr + 2) * W
        y = (jnp.dot(lhs_scr[:, :kk], w2_ref[(L - 1 - 2 * pr) * W:, :], preferred_element_type=F32)
             + jnp.dot(hs, p_ref[:, 2 * pr * W:(2 * pr + 2) * W], preferred_element_type=F32))
        y = _gelu_tanh(y)
        for s in range(2):
            for b in range(batch):
                y_ref[b, pl.ds(2 * pr + s, nck, stride=L), :] = y[b * nck:(b + 1) * nck, s * W:(s + 1) * W]


def _s5_seq(u, mats, h0, batch, seq):
    taps, mm, pp, lam = mats
    L, W = S5_CHUNK, S5_BLOCK_GROUPS * S5_GROUP
    NB = S5_GROUPS // S5_BLOCK_GROUPS
    HS = S5_BLOCK_GROUPS * S5_STATE
    SW = 2 * HS
    ts = seq // S5_TIME_SLICES
    nck = ts // L
    rows = batch * nck
    u_spec = pl.BlockSpec((batch, ts, W), lambda i, t: (0, t, i))
    h_spec = pl.BlockSpec((None, batch, SW), lambda i, t: (i, 0, 0))

    def w_spec(*dims):
        return pl.BlockSpec((None,) + dims, lambda i, t: (i,) + (0,) * len(dims))

    return pl.pallas_call(
        functools.partial(_s5_seq_kernel, batch=batch, nck=nck),
        grid=(NB, S5_TIME_SLICES),
        in_specs=[u_spec, w_spec(L, W, S5_GROUP), w_spec(2, L * W, S5_STATE), w_spec(2, HS, L * S5_GROUP),
                  w_spec(1, SW), h_spec],
        out_specs=(u_spec, h_spec),
        out_shape=(jax.ShapeDtypeStruct((batch, seq, D_MODEL), F32), jax.ShapeDtypeStruct(h0.shape, F32)),
        scratch_shapes=[pltpu.VMEM(((L + 1) * W, 2 * W), BF16), pltpu.VMEM((L * W, SW), BF16),
                        pltpu.VMEM((SW, L * W), BF16),
                        pltpu.VMEM((rows, L * W), BF16), pltpu.VMEM((SW // W, rows, W), F32),
                        pltpu.VMEM((SW // W, rows, W), F32), pltpu.VMEM((batch, SW), F32)],
        compiler_params=_params("parallel", "arbitrary"),
        name="s5_seq",
    )(u, taps, mm, pp, lam, h0)


def _s5_mixer(x, u, h0_re, h0_im, s5p, w_glu, j, batch, seq):
    G, GC, P = S5_GROUPS, S5_GROUP, S5_STATE
    if seq > 1:
        NB, HS = G // S5_BLOCK_GROUPS, S5_BLOCK_GROUPS * P
        mats = _s5_block_mats(_s5_pieces(*[p[j] for p in s5p], S5_CHUNK))
        to_blocks = lambda h: h.reshape(batch, NB, HS).transpose(1, 0, 2)
        h0 = jnp.concatenate([to_blocks(h0_re), to_blocks(h0_im)], axis=-1)
        y, hfin = _s5_seq(u.reshape(batch, seq, D_MODEL), mats, h0, batch, seq)
        y = y.reshape(batch * seq, D_MODEL)
        from_blocks = lambda h: h.transpose(1, 0, 2).reshape(batch, G, P)
        hre, him = from_blocks(hfin[..., :HS]), from_blocks(hfin[..., HS:])
    else:
        mats = _s5_group_mats(_s5_pieces(*[p[j] for p in s5p], 1))
        ug = u.reshape(batch, G, GC).transpose(1, 0, 2)
        yg, hre, him = _s5_core(ug, mats, h0_re.transpose(1, 0, 2), h0_im.transpose(1, 0, 2),
                                1, batch, lax.Precision.HIGHEST, F32)
        y = yg.transpose(1, 0, 2).reshape(batch, D_MODEL)
        hre, him = hre.transpose(1, 0, 2), him.transpose(1, 0, 2)
    x = _proj(y, w_glu, j, glu=True, res=x)
    return x, hre, him


def _hg_lower_bound(logits, layer):
    m = jnp.max(logits, axis=0, keepdims=True)
    e = jnp.exp(logits - m)
    sm = e / jnp.sum(e, axis=0, keepdims=True)
    return jnp.sum(sm[:layer + 1], axis=0, keepdims=True) - sm[0:1]


def _hg_gates(z, lb):
    e = jnp.exp(-jnp.abs(z))
    r = 1.0 / (1.0 + e)
    er = e * r
    pos = z >= 0.0
    f = lb + (1.0 - lb) * jnp.where(pos, r, er)
    logf = jnp.where(f > 0.0, jnp.log(f), z)
    k = (1.0 - lb) * jnp.where(pos, er, r)
    return logf, k


def _hgrn_prompt_kernel(x_ref, g_ref, win_ref, lbl_ref, ng_ref, wout_ref, y_ref, sfin_ref,
                        s_scr, q_scr, k_scr, v_scr, gt_scr, o_scr, beta_scr, safe_scr, *, layer, tt):
    t = pl.program_id(1)
    C = HG_CHUNK

    @pl.when(t == 0)
    def _():
        s_scr[...] = jnp.zeros_like(s_scr)

    x = x_ref[...]
    h = _rms(x, g_ref[...]).astype(BF16)
    def proj(i):
        return _dot(h, win_ref[:, i * HG_WIDTH:(i + 1) * HG_WIDTH])

    lb = _hg_lower_bound(lbl_ref[...], layer)
    logf, kk = _hg_gates(proj(1), lb)
    k_scr[...] = kk
    q_scr[...] = _silu(proj(0))
    gt_scr[...] = _silu(proj(3))
    v_scr[...] = proj(2)

    row, col = _iota2((C, C))
    causal = row >= col
    tri = causal.astype(BF16)
    ng = ng_ref[...]
    nt_dims = (((1,), (1,)), ((), ()))
    tn_dims = (((0,), (0,)), ((), ()))

    def finish_head(hh, rows, q_dec, k_dec, o_intra, btot_h):
        sv = slice(hh * HG_DV, (hh + 1) * HG_DV)
        st = s_scr[hh]
        o = o_intra + lax.dot_general(q_dec.astype(BF16), st.astype(BF16), nt_dims, preferred_element_type=F32)
        kv_t = lax.dot_general(v_scr[rows, sv].astype(BF16), k_dec.astype(BF16), tn_dims,
                               preferred_element_type=F32)
        return (_rms(o, ng) * gt_scr[rows, sv]).astype(BF16), st * jnp.exp(btot_h) + kv_t

    def store_heads(rows, results):
        o_scr[rows, :] = jnp.concatenate([o for o, _ in results], axis=-1)
        for hh, (_, s_new) in enumerate(results):
            s_scr[hh] = s_new

    for ci in range(tt // C):
        parts = _split3(logf[ci * C:(ci + 1) * C])
        beta = sum(jnp.dot(tri, p, preferred_element_type=F32) for p in parts)
        beta_scr[ci * C:(ci + 1) * C, :] = beta
        mid = beta[C // 2 - 1:C // 2, :]
        spread = jnp.maximum(jnp.max(-mid), jnp.max(mid - beta[C - 1:C, :]))
        safe_scr[ci] = (spread <= HG_FACTORED_MAX_DECAY).astype(jnp.int32)

    def chunk_step(c, carry):
        rows = pl.ds(pl.multiple_of(c * C, C), C)
        base = pl.multiple_of(c * C, C)
        btot = beta_scr[pl.ds(base + (C - 8), 8), :][7:8]
        mid = beta_scr[pl.ds(base + (C // 2 - 8), 8), :][7:8]
        safe = safe_scr[c] == 1

        @pl.when(safe)
        def _():
            e_mid = jnp.exp(mid)
            e_tot = jnp.exp(btot - mid)
            results = []
            for hh in range(HG_HEADS):
                sk = slice(hh * HG_DK, (hh + 1) * HG_DK)
                d = beta_scr[rows, sk] - mid[:, sk]
                q_mid = q_scr[rows, sk] * jnp.exp(d)
                k_mid = k_scr[rows, sk] * jnp.exp(-d)
                att = lax.dot_general(q_mid.astype(BF16), k_mid.astype(BF16), nt_dims, preferred_element_type=F32)
                att = jnp.where(causal, att, 0.0).astype(BF16)
                o_intra = jnp.dot(att, v_scr[rows, hh * HG_DV:(hh + 1) * HG_DV].astype(BF16),
                                  preferred_element_type=F32)
                results.append(finish_head(hh, rows, q_mid * e_mid[:, sk], k_mid * e_tot[:, sk], o_intra,
                                           btot[:, sk]))
            store_heads(rows, results)

        @pl.when(jnp.logical_not(safe))
        def _():
            t_idx = lax.broadcasted_iota(jnp.int32, (C, 1), 0)
            results = []
            for hh in range(HG_HEADS):
                sk = slice(hh * HG_DK, (hh + 1) * HG_DK)
                sv = slice(hh * HG_DV, (hh + 1) * HG_DV)
                b = beta_scr[rows, sk]
                q = q_scr[rows, sk]

                def key_step(s8, acc):
                    off = pl.multiple_of(s8 * 8, 8)
                    b_keys = beta_scr[pl.ds(base + off, 8), sk]
                    k_keys = k_scr[pl.ds(base + off, 8), sk]
                    v_keys = v_scr[pl.ds(base + off, 8), sv]
                    for i in range(8):
                        w = jnp.exp(jnp.minimum(b - b_keys[i:i + 1], 0.0))
                        a = jnp.sum(q * k_keys[i:i + 1] * w, axis=-1, keepdims=True)
                        a = jnp.where(t_idx >= off + i, a, 0.0)
                        acc = acc + a * v_keys[i:i + 1]
                    return acc

                o_intra = lax.fori_loop(0, C // 8, key_step, jnp.zeros((C, HG_DV), F32))
                results.append(finish_head(hh, rows, q * jnp.exp(b), k_scr[rows, sk] * jnp.exp(btot[:, sk] - b),
                                           o_intra, btot[:, sk]))
            store_heads(rows, results)

        return carry

    lax.fori_loop(0, tt // C, chunk_step, 0)
    y_ref[...] = x + _dot(o_scr[...], wout_ref[...])

    @pl.when(t == pl.num_programs(1) - 1)
    def _():
        for hh in range(HG_HEADS):
            sfin_ref[hh] = s_scr[hh].T


def _hgrn_prompt(x, g, w_in, lb_logits, norm_g, w_out, layer, j, batch, seq):
    tt = HG_ROW_TILE
    nt = seq // tt
    x3 = x.reshape(batch, seq, D_MODEL)
    row_spec = pl.BlockSpec((None, tt, D_MODEL), lambda b, t: (b, t, 0))
    y, s_fin = pl.pallas_call(
        functools.partial(_hgrn_prompt_kernel, layer=layer, tt=tt),
        grid=(batch, nt),
        in_specs=[row_spec,
                  pl.BlockSpec((None, 1, D_MODEL), lambda b, t: (layer, 0, 0)),
                  _resident((None, D_MODEL, 4 * HG_WIDTH), lambda b, t: (j, 0, 0)),
                  pl.BlockSpec((DEPTH, HG_WIDTH), lambda b, t: (0, 0)),
                  pl.BlockSpec((None, 1, HG_DV), lambda b, t: (j, 0, 0)),
                  _resident((None, HG_WIDTH, D_MODEL), lambda b, t: (j, 0, 0))],
        out_specs=(row_spec,
                   pl.BlockSpec((None, HG_HEADS, HG_DK, HG_DV), lambda b, t: (b, 0, 0, 0))),
        out_shape=(jax.ShapeDtypeStruct((batch, seq, D_MODEL), F32),
                   jax.ShapeDtypeStruct((batch, HG_HEADS, HG_DK, HG_DV), F32)),
        scratch_shapes=[pltpu.VMEM((HG_HEADS, HG_DV, HG_DK), F32),
                        pltpu.VMEM((tt, HG_WIDTH), F32), pltpu.VMEM((tt, HG_WIDTH), F32),
                        pltpu.VMEM((tt, HG_WIDTH), F32), pltpu.VMEM((tt, HG_WIDTH), F32),
                        pltpu.VMEM((tt, HG_WIDTH), BF16), pltpu.VMEM((tt, HG_WIDTH), F32),
                        pltpu.SMEM((tt // HG_CHUNK,), jnp.int32)],
        compiler_params=_params("parallel", "arbitrary"),
        name="hgrn_prompt",
    )(x3, g.reshape(DEPTH, 1, D_MODEL), w_in, lb_logits, norm_g.reshape(-1, 1, HG_DV), w_out)
    return y.reshape(batch * seq, D_MODEL), s_fin


def _hgrn_sample_kernel(proj_ref, lbl_ref, ng_ref, s_ref, snew_ref, o_ref, *, layer, tb):
    proj = proj_ref[...]
    lb = _hg_lower_bound(lbl_ref[...], layer)
    z = proj[:, HG_WIDTH:2 * HG_WIDTH]
    e = jnp.exp(-jnp.abs(z))
    r = 1.0 / (1.0 + e)
    sig = jnp.where(z >= 0.0, r, e * r)
    f = lb + (1.0 - lb) * sig
    k = (1.0 - lb) * jnp.where(z >= 0.0, e * r, r)
    q = _silu(proj[:, :HG_WIDTH])
    v = proj[:, 2 * HG_WIDTH:3 * HG_WIDTH]
    gt = _silu(proj[:, 3 * HG_WIDTH:])
    ng = ng_ref[...]
    tok, lane = _iota2((tb, tb * HG_DV))
    spread = ((lane // HG_DV) == tok).astype(BF16)

    spread3 = jnp.concatenate([spread] * 3, axis=0)

    def columns(x, exact):
        if exact:
            return jnp.dot(jnp.concatenate(_split3(x.T), axis=1), spread3, preferred_element_type=F32)
        return jnp.dot(x.T.astype(BF16), spread, preferred_element_type=F32)

    for hh in range(HG_HEADS):
        sk = slice(hh * HG_DK, (hh + 1) * HG_DK)
        sv = slice(hh * HG_DV, (hh + 1) * HG_DV)
        f_c, k_c, q_c = columns(f[:, sk], True), columns(k[:, sk], False), columns(q[:, sk], False)
        for b in range(tb):
            blk = slice(b * HG_DV, (b + 1) * HG_DV)
            s_new = f_c[:, blk] * s_ref[b, hh] + k_c[:, blk] * v[b:b + 1, sv]
            snew_ref[b, hh] = s_new
            o = jnp.sum(q_c[:, blk] * s_new, axis=0, keepdims=True)
            o_ref[b:b + 1, sv] = _rms(o, ng) * gt[b:b + 1, sv]


def _hgrn_sample(proj, lb_logits, norm_g, state, layer, j):
    nb = proj.shape[0]
    tb = SAMPLE_TOKENS_PER_STEP
    st_spec = pl.BlockSpec((tb, HG_HEADS, HG_DK, HG_DV), lambda i: (i, 0, 0, 0))
    return pl.pallas_call(
        functools.partial(_hgrn_sample_kernel, layer=layer, tb=tb),
        grid=(nb // tb,),
        in_specs=[pl.BlockSpec((tb, 4 * HG_WIDTH), lambda i: (i, 0)),
                  pl.BlockSpec((DEPTH, HG_WIDTH), lambda i: (0, 0)),
                  pl.BlockSpec((None, 1, HG_DV), lambda i: (j, 0, 0)),
                  st_spec],
        out_specs=(st_spec, pl.BlockSpec((tb, HG_WIDTH), lambda i: (i, 0))),
        out_shape=(jax.ShapeDtypeStruct(state.shape, F32), jax.ShapeDtypeStruct((nb, HG_WIDTH), F32)),
        compiler_params=_params("parallel"),
        name="hgrn_sample",
    )(proj, lb_logits, norm_g.reshape(-1, 1, HG_DV), state)


def _xattn_prompt_kernel(x_ref, g_ref, wq_ref, k_ref, v_ref, wo_ref, y_ref):
    x = x_ref[...]
    h = _rms(x, g_ref[...]).astype(BF16)
    q = _dot(h, wq_ref[...]) * (1.0 / math.sqrt(MEM_HD))
    q = q.astype(BF16)
    outs = []
    for hh in range(MEM_HEADS):
        sl = slice(hh * MEM_HD, (hh + 1) * MEM_HD)
        s = lax.dot_general(q[:, sl], k_ref[:, sl], (((1,), (1,)), ((), ())), preferred_element_type=F32)
        p = jnp.exp(s - jnp.max(s, axis=-1, keepdims=True))
        den = jnp.sum(p, axis=-1, keepdims=True)
        o = jnp.dot(p.astype(BF16), v_ref[:, sl], preferred_element_type=F32)
        outs.append((o / den).astype(BF16))
    o = jnp.concatenate(outs, axis=-1)
    y_ref[...] = x + _dot(o, wo_ref[...])


def _xattn_prompt(x, g, w_q, mem_k, mem_v, w_o, layer, batch, seq):
    tt = XA_ROW_TILE
    row_spec = pl.BlockSpec((None, tt, D_MODEL), lambda b, t: (b, t, 0))
    kv_spec = pl.BlockSpec((None, N_MEM, D_MODEL), lambda b, t: (layer, b, 0))
    w_spec = _resident((None, D_MODEL, D_MODEL), lambda b, t: (layer, 0, 0))
    y = pl.pallas_call(
        _xattn_prompt_kernel,
        grid=(batch, seq // tt),
        in_specs=[row_spec, pl.BlockSpec((None, 1, D_MODEL), lambda b, t: (layer, 0, 0)),
                  w_spec, kv_spec, kv_spec, w_spec],
        out_specs=row_spec,
        out_shape=jax.ShapeDtypeStruct((batch, seq, D_MODEL), F32),
        compiler_params=_params("parallel", "parallel"),
        name="xattn_prompt",
    )(x.reshape(batch, seq, D_MODEL), g.reshape(DEPTH, 1, D_MODEL), w_q, mem_k, mem_v, w_o)
    return y.reshape(batch * seq, D_MODEL)


def _xattn_sample_kernel(q_ref, k_ref, v_ref, o_ref, *, tb):
    scale = 1.0 / math.sqrt(MEM_HD)
    rows = N_MEM * MEM_HEADS
    head, lane = _iota2((MEM_HEADS, rows))
    own = (lane & (MEM_HEADS - 1)) == head
    for b in range(tb):
        q = (q_ref[b] * scale).astype(BF16)
        k = k_ref[b].reshape(rows, MEM_HD).astype(BF16)
        v = v_ref[b].reshape(rows, MEM_HD).astype(BF16)
        s = lax.dot_general(q, k, (((1,), (1,)), ((), ())), preferred_element_type=F32)
        s = jnp.where(own, s, -jnp.inf)
        p = jnp.exp(s - jnp.max(s, axis=-1, keepdims=True))
        den = jnp.sum(p, axis=-1, keepdims=True)
        o_ref[b] = jnp.dot(p.astype(BF16), v, preferred_element_type=F32) / den


def _xattn_sample(q, cache_k, cache_v, layer):
    nb = q.shape[0]
    tb = XA_SAMPLE_TOKENS_PER_STEP
    kv_spec = pl.BlockSpec((None, tb, N_MEM, MEM_HEADS, MEM_HD), lambda i: (layer, i, 0, 0, 0))
    q_spec = pl.BlockSpec((tb, MEM_HEADS, MEM_HD), lambda i: (i, 0, 0))
    return pl.pallas_call(
        functools.partial(_xattn_sample_kernel, tb=tb),
        grid=(nb // tb,),
        in_specs=[q_spec, kv_spec, kv_spec],
        out_specs=q_spec,
        out_shape=jax.ShapeDtypeStruct((nb, MEM_HEADS, MEM_HD), F32),
        compiler_params=_params("parallel"),
        name="xattn_sample",
    )(q.reshape(nb, MEM_HEADS, MEM_HD), cache_k, cache_v).reshape(nb, D_MODEL)


def _ffn1(x, w, i, ffn, emit):
    if emit:
        return ffn(x, w["ffn1_norm"], w["ffn1_w_in"], w["ffn1_w_out"], i, post="emit", g2=w["mix_norm"][i],
                   u_dtype=F32)
    return ffn(x, w["ffn1_norm"], w["ffn1_w_in"], w["ffn1_w_out"], i)


def _ffn2(x, w, i, ffn):
    last = i == DEPTH - 1
    return ffn(x, w["ffn2_norm"], w["ffn2_w_in"], w["ffn2_w_out"], i,
               post="replace" if last else "none", g2=w["final_norm"] if last else None)


def _prompt_trunk(x, batch, seq, mem_k, mem_v, w, ffn):
    zero = jnp.zeros((batch, S5_GROUPS, S5_STATE), F32)
    new_re, new_im, new_hg = [], [], []
    for i in range(DEPTH):
        j = i // 2
        if i % 2 == 0:
            x, u = _ffn1(x, w, i, ffn, True)
            x, hr, hi = _s5_mixer(x, u, zero, zero, w["s5"], w["s5_w_glu"], j, batch, seq)
            new_re.append(hr)
            new_im.append(hi)
        else:
            x = _ffn1(x, w, i, ffn, False)
            x, sn = _hgrn_prompt(x, w["mix_norm"], w["hg_w_in"], w["hg_lb_logits"], w["hg_norm"],
                                 w["hg_w_out"], i, j, batch, seq)
            new_hg.append(sn)
        x = _xattn_prompt(x, w["xattn_norm"], w["xattn_w_q"], mem_k, mem_v, w["xattn_w_o"], i, batch, seq)
        x = _ffn2(x, w, i, ffn)
    return x, jnp.stack(new_re), jnp.stack(new_im), jnp.stack(new_hg)


def _sample_trunk(x, batch, s5_re, s5_im, hg_state, w):
    new_re, new_im, new_hg = [], [], []
    for i in range(DEPTH):
        j = i // 2
        if i % 2 == 0:
            x, u = _ffn1(x, w, i, _ffn, True)
            x, hr, hi = _s5_mixer(x, u, s5_re[j], s5_im[j], w["s5"], w["s5_w_glu"], j, batch, 1)
            new_re.append(hr)
            new_im.append(hi)
        else:
            x = _ffn1(x, w, i, _ffn, False)
            proj = _proj(x, w["hg_w_in"], j, g=w["mix_norm"], g_layer=i)
            sn, o = _hgrn_sample(proj, w["hg_lb_logits"], w["hg_norm"], hg_state[j], i, j)
            x = _proj(o, w["hg_w_out"], j, res=x)
            new_hg.append(sn)
        q = _proj(x, w["xattn_w_q"], i, g=w["xattn_norm"], g_layer=i)
        o = yield q, i
        x = _proj(o, w["xattn_w_o"], i, res=x)
        x = _ffn2(x, w, i, _ffn)
    return x, jnp.stack(new_re), jnp.stack(new_im), jnp.stack(new_hg)


def kernel(x_prompt, x_sample, mem_prompt, state_s5_re, state_s5_im, state_hgrn, cache_mem_k, cache_mem_v, ffn1_norm, ffn1_w_in, ffn1_w_out, mix_norm, xattn_norm, mem_norm, xattn_w_q, xattn_w_kv, xattn_w_o, ffn2_norm, ffn2_w_in, ffn2_w_out, s5_a_re, s5_a_im, s5_log_dt, s5_b_re, s5_b_im, s5_c_re, s5_c_im, s5_d, s5_w_glu, hg_w_in, hg_lb_logits, hg_norm, hg_w_out, final_norm):
    bp, seq, _ = x_prompt.shape
    bs = x_sample.shape[0]
    w = dict(ffn1_norm=ffn1_norm, ffn1_w_in=ffn1_w_in, ffn1_w_out=ffn1_w_out, mix_norm=mix_norm,
             xattn_norm=xattn_norm, xattn_w_q=xattn_w_q, xattn_w_o=xattn_w_o, ffn2_norm=ffn2_norm,
             ffn2_w_in=ffn2_w_in, ffn2_w_out=ffn2_w_out,
             s5=(s5_a_re, s5_a_im, s5_log_dt, s5_b_re, s5_b_im, s5_c_re, s5_c_im, s5_d),
             s5_w_glu=s5_w_glu, hg_w_in=hg_w_in, hg_lb_logits=hg_lb_logits, hg_norm=hg_norm,
             hg_w_out=hg_w_out, final_norm=final_norm)

    mem_k, mem_v, mem_k_bf, mem_v_bf = _mem_kv(mem_prompt, mem_norm, xattn_w_kv)

    sample = _sample_trunk(x_sample.reshape(bs, D_MODEL), bs, state_s5_re, state_s5_im, state_hgrn, w)
    pending = [next(sample)]
    sample_out = []

    def resume(o):
        try:
            pending[0] = sample.send(o)
        except StopIteration as done:
            pending[0] = None
            sample_out.append(done.value)

    def ffn_with_rider(x, *args, **kwargs):
        steps = x.shape[0] // min(FFN_ROW_TILE_WITH_RIDER, x.shape[0])
        if pending[0] is None or bs % steps:
            return _ffn(x, *args, **kwargs)
        q, layer = pending[0]
        *outs, o = _ffn(x, *args, rider=(q, cache_mem_k, cache_mem_v, layer), **kwargs)
        resume(o)
        return outs[0] if len(outs) == 1 else tuple(outs)

    y_p, re_p, im_p, hg_p = _prompt_trunk(x_prompt.reshape(bp * seq, D_MODEL), bp, seq, mem_k_bf, mem_v_bf, w,
                                          ffn_with_rider)
    while pending[0] is not None:
        q, layer = pending[0]
        resume(_xattn_sample(q, cache_mem_k, cache_mem_v, layer))
    y_s, re_s, im_s, hg_s = sample_out[0]
    return (y_p.reshape(bp, seq, D_MODEL), y_s.reshape(bs, 1, D_MODEL), re_p, im_p, re_s, im_s, hg_p, hg_s,
            mem_k, mem_v)
```

```python
import functools
import math

import jax
import jax.numpy as jnp
from jax import lax
from jax.experimental import pallas as pl
from jax.experimental.pallas import tpu as pltpu

F32 = jnp.float32
BF16 = jnp.bfloat16

D_MODEL = 1024
DEPTH = 2
S5_GROUP = 16
S5_GROUPS = D_MODEL // S5_GROUP
S5_STATE = 64
S5_CHUNK = 16
S5_BLOCK_GROUPS = 8
S5_TIME_SLICES = 2
HG_DK = 128
HG_HEADS = D_MODEL // HG_DK
HG_DV = D_MODEL // HG_HEADS
HG_WIDTH = HG_HEADS * HG_DK
HG_CHUNK = 128
HG_FACTORED_MAX_DECAY = 60.0
N_MEM = 256
MEM_HEADS = 4
MEM_HD = D_MODEL // MEM_HEADS
FFN_DIM = 2816
EPS = 1e-6

V7X_VMEM_LIMIT_BYTES = 56 * 1024 * 1024

ROW_TILE = 512
FFN_CHUNK = 1408
FFN_ROW_TILE_WITH_RIDER = 256
HG_ROW_TILE = 512
XA_ROW_TILE = 512
S5_GROUPS_PER_STEP = 16
SAMPLE_TOKENS_PER_STEP = 8
XA_SAMPLE_TOKENS_PER_STEP = 4


def _params(*semantics):
    return pltpu.CompilerParams(dimension_semantics=semantics,
                                vmem_limit_bytes=V7X_VMEM_LIMIT_BYTES)


def _resident(shape, index_map):
    return pl.BlockSpec(shape, index_map, pipeline_mode=pl.Buffered(1))


def _rms(x, g):
    ms = jnp.mean(x * x, axis=-1, keepdims=True)
    return x * lax.rsqrt(ms + EPS) * g


def _sigmoid(x):
    return 1.0 / (1.0 + jnp.exp(-x))


def _silu(x):
    return x * _sigmoid(x)


def _gelu_tanh(x):
    c = math.sqrt(2.0 / math.pi)
    return 0.5 * x * (1.0 + jnp.tanh(c * (x + 0.044715 * (x * x * x))))


def _dot(a, w):
    return jnp.dot(a, w.astype(BF16), preferred_element_type=F32)


def _split3(x):
    hi = x.astype(BF16)
    r1 = x - hi.astype(F32)
    mid = r1.astype(BF16)
    lo = (r1 - mid.astype(F32)).astype(BF16)
    return hi, mid, lo


def _ffn_kernel(x_ref, g_ref, win_ref, wout_ref, *rest, post, rider_tokens):
    rest = list(rest)
    g2_ref = rest.pop(0) if post != "none" else None
    if rider_tokens:
        q_ref, k_ref, v_ref = rest[:3]
        rest = rest[3:]
        _xattn_sample_kernel(q_ref, k_ref, v_ref, rest.pop(), tb=rider_tokens)
    x = x_ref[...]
    h = _rms(x, g_ref[...]).astype(BF16)
    acc = jnp.zeros_like(x)
    for c in range(FFN_DIM // FFN_CHUNK):
        lo = c * FFN_CHUNK
        gate = _dot(h, win_ref[:, lo:lo + FFN_CHUNK])
        up = _dot(h, win_ref[:, FFN_DIM + lo:FFN_DIM + lo + FFN_CHUNK])
        act = (_silu(gate) * up).astype(BF16)
        acc = acc + _dot(act, wout_ref[lo:lo + FFN_CHUNK, :])
    y = x + 0.5 * acc
    y_ref = rest[0]
    if post == "replace":
        y_ref[...] = _rms(y, g2_ref[...])
    else:
        y_ref[...] = y
        if post == "emit":
            rest[1][...] = _rms(y, g2_ref[...]).astype(rest[1].dtype)


def _ffn(x, g, w_in, w_out, layer, post="none", g2=None, u_dtype=BF16, rider=None):
    rows = x.shape[0]
    tm = min(ROW_TILE if rider is None else FFN_ROW_TILE_WITH_RIDER, rows)
    steps = rows // tm
    row_spec = pl.BlockSpec((tm, D_MODEL), lambda i: (i, 0))
    vec_spec = pl.BlockSpec((1, D_MODEL), lambda i: (0, 0))
    in_specs = [row_spec,
                pl.BlockSpec((None, 1, D_MODEL), lambda i: (layer, 0, 0)),
                _resident((None, D_MODEL, 2 * FFN_DIM), lambda i: (layer, 0, 0)),
                _resident((None, FFN_DIM, D_MODEL), lambda i: (layer, 0, 0))]
    args = [x, g.reshape(DEPTH, 1, D_MODEL), w_in, w_out]
    out_shape = [jax.ShapeDtypeStruct((rows, D_MODEL), F32)]
    out_specs = [row_spec]
    if post != "none":
        in_specs.append(vec_spec)
        args.append(g2.reshape(1, D_MODEL))
    if post == "emit":
        out_shape.append(jax.ShapeDtypeStruct((rows, D_MODEL), u_dtype))
        out_specs.append(row_spec)
    tb = 0
    if rider is not None:
        q, cache_k, cache_v, r_layer = rider
        nb = q.shape[0]
        assert nb % steps == 0
        tb = nb // steps
        kv_spec = pl.BlockSpec((None, tb, N_MEM, MEM_HEADS, MEM_HD), lambda i: (r_layer, i, 0, 0, 0))
        q_spec = pl.BlockSpec((tb, MEM_HEADS, MEM_HD), lambda i: (i, 0, 0))
        in_specs += [q_spec, kv_spec, kv_spec]
        args += [q.reshape(nb, MEM_HEADS, MEM_HD), cache_k, cache_v]
        out_shape.append(jax.ShapeDtypeStruct((nb, MEM_HEADS, MEM_HD), F32))
        out_specs.append(q_spec)
    outs = list(pl.pallas_call(
        functools.partial(_ffn_kernel, post=post, rider_tokens=tb),
        grid=(steps,),
        in_specs=in_specs, out_specs=tuple(out_specs), out_shape=tuple(out_shape),
        compiler_params=_params("parallel"),
        name="ffn",
    )(*args))
    if rider is not None:
        outs[-1] = outs[-1].reshape(-1, D_MODEL)
    return outs[0] if len(outs) == 1 else tuple(outs)


def _proj_kernel(*refs, norm, glu, residual):
    refs = list(refs)
    x_ref = refs.pop(0)
    g_ref = refs.pop(0) if norm else None
    w_ref = refs.pop(0)
    res_ref = refs.pop(0) if residual else None
    (o_ref,) = refs
    x = x_ref[...]
    if norm:
        x = _rms(x, g_ref[...])
    y = _dot(x.astype(BF16), w_ref[...])
    if glu:
        half = y.shape[-1] // 2
        y = y[:, :half] * _sigmoid(y[:, half:])
    if residual:
        y = y + res_ref[...]
    o_ref[...] = y


def _proj(x, w, layer, g=None, g_layer=0, glu=False, res=None):
    rows, kdim = x.shape
    ndim = w.shape[-1]
    nout = ndim // 2 if glu else ndim
    tm = min(ROW_TILE, rows)
    in_specs = [pl.BlockSpec((tm, kdim), lambda i: (i, 0))]
    args = [x]
    if g is not None:
        in_specs.append(pl.BlockSpec((None, 1, kdim), lambda i: (g_layer, 0, 0)))
        args.append(g.reshape(g.shape[0], 1, kdim))
    in_specs.append(_resident((None, kdim, ndim), lambda i: (layer, 0, 0)))
    args.append(w)
    if res is not None:
        in_specs.append(pl.BlockSpec((tm, nout), lambda i: (i, 0)))
        args.append(res)
    return pl.pallas_call(
        functools.partial(_proj_kernel, norm=g is not None, glu=glu, residual=res is not None),
        grid=(rows // tm,),
        in_specs=in_specs,
        out_specs=pl.BlockSpec((tm, nout), lambda i: (i, 0)),
        out_shape=jax.ShapeDtypeStruct((rows, nout), F32),
        compiler_params=_params("parallel"),
        name="proj",
    )(*args)


def _memkv_kernel(x_ref, g_ref, w_ref, k_ref, v_ref, kb_ref, vb_ref):
    h = _rms(x_ref[...], g_ref[...]).astype(BF16)
    y = _dot(h, w_ref[...])
    k, v = y[:, :D_MODEL], y[:, D_MODEL:]
    k_ref[...] = k.reshape(k_ref.shape)
    v_ref[...] = v.reshape(v_ref.shape)
    kb_ref[...] = k.astype(BF16)
    vb_ref[...] = v.astype(BF16)


def _mem_kv(mem, g, w_kv):
    batch = mem.shape[0]
    rows = batch * N_MEM
    nb = max(1, min(ROW_TILE, rows) // N_MEM)
    tm = nb * N_MEM
    out5 = jax.ShapeDtypeStruct((DEPTH, batch, N_MEM, MEM_HEADS, MEM_HD), F32)
    out2 = jax.ShapeDtypeStruct((DEPTH, rows, D_MODEL), BF16)
    spec5 = pl.BlockSpec((None, nb, N_MEM, MEM_HEADS, MEM_HD), lambda l, i: (l, i, 0, 0, 0))
    spec2 = pl.BlockSpec((None, tm, D_MODEL), lambda l, i: (l, i, 0))
    return pl.pallas_call(
        _memkv_kernel,
        grid=(DEPTH, rows // tm),
        in_specs=[pl.BlockSpec((tm, D_MODEL), lambda l, i: (i, 0)),
                  pl.BlockSpec((None, 1, D_MODEL), lambda l, i: (l, 0, 0)),
                  pl.BlockSpec((None, D_MODEL, 2 * D_MODEL), lambda l, i: (l, 0, 0))],
        out_specs=(spec5, spec5, spec2, spec2), out_shape=(out5, out5, out2, out2),
        compiler_params=_params("parallel", "parallel"),
        name="mem_kv",
    )(mem.reshape(rows, D_MODEL), g.reshape(DEPTH, 1, D_MODEL), w_kv)


def _s5_pieces(a_re, a_im, log_dt, b_re, b_im, c_re, c_im, d, steps):
    G, P, GC = S5_GROUPS, S5_STATE, S5_GROUP
    L = steps
    dt = jnp.exp(log_dt)[:, None]
    xr, xi = a_re * dt, a_im * dt
    j = jnp.arange(L + 1, dtype=F32)[:, None, None]
    mag = jnp.exp(xr[None] * j)
    pw_re, pw_im = mag * jnp.cos(xi[None] * j), mag * jnp.sin(xi[None] * j)
    nr, ni = pw_re[1] - 1.0, pw_im[1]
    den = a_re * a_re + a_im * a_im
    fr, fi = ((nr * a_re + ni * a_im) / den)[:, None, :], ((ni * a_re - nr * a_im) / den)[:, None, :]
    bt_re, bt_im = b_re.transpose(0, 2, 1), b_im.transpose(0, 2, 1)
    bb_re = fr * bt_re - fi * bt_im
    bb_im = fr * bt_im + fi * bt_re

    def times_c(p_re, p_im):
        p_re, p_im = p_re[:, :, None, :], p_im[:, :, None, :]
        return p_re * c_re[None] - p_im * c_im[None], p_re * c_im[None] + p_im * c_re[None]

    w_re, w_im = times_c(pw_re[:L], pw_im[:L])
    kern = jnp.sum(bb_re[None, :, :, None, :] * w_re[:, :, None, :, :]
                   - bb_im[None, :, :, None, :] * w_im[:, :, None, :, :], axis=-1)
    kern = kern.at[0].add(d[:, :, None] * jnp.eye(GC, dtype=F32)[None])
    rp_re, rp_im = pw_re[:L][::-1][:, :, None, :], pw_im[:L][::-1][:, :, None, :]
    m_re = rp_re * bb_re[None] - rp_im * bb_im[None]
    m_im = rp_re * bb_im[None] + rp_im * bb_re[None]
    q_re, q_im = times_c(pw_re[1:], pw_im[1:])
    return kern, m_re, m_im, q_re, -q_im, pw_re[L], pw_im[L]


def _s5_group_mats(pieces):
    kern, m_re, m_im, p_re, p_im, l_re, l_im = pieces
    return (kern[0], m_re[0], m_im[0], p_re[0].transpose(0, 2, 1), p_im[0].transpose(0, 2, 1),
            l_re[:, None, :], l_im[:, None, :])


def _s5_block_mats(pieces):
    kern, m_re, m_im, p_re, p_im, l_re, l_im = pieces
    L = kern.shape[0]
    GB = S5_BLOCK_GROUPS
    NB = S5_GROUPS // GB
    GC, P = S5_GROUP, S5_STATE
    taps = kern.astype(BF16).reshape(L, NB, GB * GC, GC).transpose(1, 0, 2, 3)

    def rows(m):
        return m.astype(BF16).reshape(L, NB, GB * GC, P).transpose(1, 0, 2, 3).reshape(NB, L * GB * GC, P)

    mm = jnp.stack([rows(m_re), rows(m_im)], axis=1)
    pp = jnp.stack([rows(p_re), rows(p_im)], axis=1)
    lam = jnp.concatenate([l_re.reshape(NB, 1, GB * P), l_im.reshape(NB, 1, GB * P)], axis=2)
    return taps, mm, pp, lam


def _s5_kernel(u_ref, t_ref, mre_ref, mim_ref, pre_ref, pim_ref, lre_ref, lim_ref, h0re_ref, h0im_ref,
               y_ref, hre_ref, him_ref, inj_re, inj_im, hs_re, hs_im, *, n_chunks, rb, gps, precision):
    def mm(a, b):
        return jnp.dot(a, b, preferred_element_type=F32, precision=precision)

    for g in range(gps):
        u = u_ref[g]
        inj_re[g] = mm(u, mre_ref[g])
        inj_im[g] = mm(u, mim_ref[g])

    lam_re = [jnp.broadcast_to(lre_ref[g], (rb, S5_STATE)) for g in range(gps)]
    lam_im = [jnp.broadcast_to(lim_ref[g], (rb, S5_STATE)) for g in range(gps)]

    def step(k, carry):
        rows = pl.ds(pl.multiple_of(k * rb, rb), rb)
        nxt = []
        for g in range(gps):
            hr, hi = carry[2 * g], carry[2 * g + 1]
            hs_re[g, rows, :] = hr
            hs_im[g, rows, :] = hi
            nxt.append(lam_re[g] * hr - lam_im[g] * hi + inj_re[g, rows, :])
            nxt.append(lam_re[g] * hi + lam_im[g] * hr + inj_im[g, rows, :])
        return tuple(nxt)

    init = []
    for g in range(gps):
        init += [h0re_ref[g], h0im_ref[g]]
    fin = lax.fori_loop(0, n_chunks, step, tuple(init))

    for g in range(gps):
        hre_ref[g] = fin[2 * g]
        him_ref[g] = fin[2 * g + 1]
        dt = u_ref.dtype
        y = (mm(u_ref[g], t_ref[g]) + mm(hs_re[g].astype(dt), pre_ref[g])
             + mm(hs_im[g].astype(dt), pim_ref[g]))
        y_ref[g] = _gelu_tanh(y).astype(y_ref.dtype)


def _s5_core(u, mats, h0_re, h0_im, n_chunks, rb, precision, y_dtype):
    G, R, W = u.shape
    P = S5_STATE
    gps = S5_GROUPS_PER_STEP
    tmat, m_re, m_im, p_re, p_im, l_re, l_im = mats

    def spec(a, b):
        return pl.BlockSpec((gps, a, b), lambda i: (i, 0, 0))

    st = jax.ShapeDtypeStruct((G, rb, P), F32)
    scr = lambda: pltpu.VMEM((gps, R, P), F32)
    return pl.pallas_call(
        functools.partial(_s5_kernel, n_chunks=n_chunks, rb=rb, gps=gps, precision=precision),
        grid=(G // gps,),
        in_specs=[spec(R, W), spec(W, W), spec(W, P), spec(W, P), spec(P, W), spec(P, W),
                  spec(1, P), spec(1, P), spec(rb, P), spec(rb, P)],
        out_specs=(spec(R, W), spec(rb, P), spec(rb, P)),
        out_shape=(jax.ShapeDtypeStruct((G, R, W), y_dtype), st, st),
        scratch_shapes=[scr(), scr(), scr(), scr()],
        compiler_params=_params("parallel"),
        name="s5_core",
    )(u, tmat, m_re, m_im, p_re, p_im, l_re, l_im, h0_re, h0_im)


def _iota2(shape):
    return lax.broadcasted_iota(jnp.int32, shape, 0), lax.broadcasted_iota(jnp.int32, shape, 1)


def _s5_expand(taps_ref, mc_ref, pc_ref, w2_ref, m_ref, p_ref):
    L, GC, P = S5_CHUNK, S5_GROUP, S5_STATE
    GB = S5_BLOCK_GROUPS
    W, HS = GB * GC, GB * P
    gc_bits, p_bits = GC.bit_length() - 1, P.bit_length() - 1

    r, c = _iota2((P, HS))
    rep_m = ((c & (P - 1)) == r).astype(BF16)
    r, c = _iota2((L * W, HS))
    mask_m = ((r >> gc_bits) & (GB - 1)) == (c >> p_bits)
    for half in range(2):
        for src, dst in ((mc_ref, m_ref), (pc_ref, p_ref)):
            dst[:, half * HS:(half + 1) * HS] = jnp.where(
                mask_m, jnp.dot(src[half], rep_m, preferred_element_type=F32), 0.0).astype(BF16)

    r, c = _iota2((GC, W))
    rep_k = ((c & (GC - 1)) == r).astype(BF16)
    r, c = _iota2((W, W))
    mask_k = (r >> gc_bits) == (c >> gc_bits)
    zero = jnp.zeros((W, W), BF16)
    w2_ref[L * W:, :W] = zero
    w2_ref[:W, W:] = zero
    for i in range(L):
        kb = jnp.where(mask_k, jnp.dot(taps_ref[L - 1 - i], rep_k, preferred_element_type=F32), 0.0).astype(BF16)
        w2_ref[i * W:(i + 1) * W, :W] = kb
        w2_ref[(i + 1) * W:(i + 2) * W, W:] = kb


def _s5_seq_kernel(u_ref, taps_ref, mc_ref, pc_ref, lam_ref, h0_ref, y_ref, hfin_ref,
                   w2_ref, m_ref, p_ref, lhs_scr, inj_scr, hs_scr, h_scr, *, batch, nck):
    L, W = S5_CHUNK, S5_BLOCK_GROUPS * S5_GROUP
    NT = S5_BLOCK_GROUPS * S5_STATE // W

    @pl.when(pl.program_id(1) == 0)
    def _():
        h_scr[...] = h0_ref[...]
        _s5_expand(taps_ref, mc_ref, pc_ref, w2_ref, m_ref, p_ref)

    for b in range(batch):
        for l in range(L):
            lhs_scr[b * nck:(b + 1) * nck, l * W:(l + 1) * W] = u_ref[b, pl.ds(l, nck, stride=L), :].astype(BF16)

    def swap_major(x, a, b):
        return jnp.swapaxes(x.reshape(a, b, W), 0, 1).reshape(a * b, W)

    inj = jnp.dot(lhs_scr[...], m_ref[...], preferred_element_type=F32)
    for t in range(2 * NT):
        inj_scr[t] = swap_major(inj[:, t * W:(t + 1) * W], batch, nck)

    lam = [jnp.broadcast_to(lam_ref[:, t * W:(t + 1) * W], (batch, W)) for t in range(2 * NT)]

    def step(k, h):
        rows = pl.ds(pl.multiple_of(k * batch, batch), batch)
        nxt_re, nxt_im = [], []
        for t in range(NT):
            hr, hi = h[t], h[NT + t]
            hs_scr[t, rows, :] = hr
            hs_scr[NT + t, rows, :] = hi
            nxt_re.append(lam[t] * hr - lam[NT + t] * hi + inj_scr[t, rows, :])
            nxt_im.append(lam[t] * hi + lam[NT + t] * hr + inj_scr[NT + t, rows, :])
        return tuple(nxt_re + nxt_im)

    h = lax.fori_loop(0, nck, step, tuple(h_scr[:, t * W:(t + 1) * W] for t in range(2 * NT)))
    for t in range(2 * NT):
        h_scr[:, t * W:(t + 1) * W] = h[t]
    hfin_ref[...] = h_scr[...]

    hs = jnp.concatenate([swap_major(hs_scr[t], nck, batch) for t in range(2 * NT)], axis=-1).astype(BF16)
    for pr in range(L // 2):
        kk = (2 * pr + 2) * W
        y = (jnp.dot(lhs_scr[:, :kk], w2_ref[(L - 1 - 2 * pr) * W:, :], preferred_element_type=F32)
             + lax.dot_general(hs, p_ref[2 * pr * W:(2 * pr + 2) * W, :], (((1,), (1,)), ((), ())),
                               preferred_element_type=F32))
        y = _gelu_tanh(y)
        for s in range(2):
            for b in range(batch):
                y_ref[b, pl.ds(2 * pr + s, nck, stride=L), :] = y[b * nck:(b + 1) * nck, s * W:(s + 1) * W]


def _s5_seq(u, mats, h0, batch, seq):
    taps, mm, pp, lam = mats
    L, W = S5_CHUNK, S5_BLOCK_GROUPS * S5_GROUP
    NB = S5_GROUPS // S5_BLOCK_GROUPS
    HS = S5_BLOCK_GROUPS * S5_STATE
    SW = 2 * HS
    ts = seq // S5_TIME_SLICES
    nck = ts // L
    rows = batch * nck
    u_spec = pl.BlockSpec((batch, ts, W), lambda i, t: (0, t, i))
    h_spec = pl.BlockSpec((None, batch, SW), lambda i, t: (i, 0, 0))

    def w_spec(*dims):
        return pl.BlockSpec((None,) + dims, lambda i, t: (i,) + (0,) * len(dims))

    return pl.pallas_call(
        functools.partial(_s5_seq_kernel, batch=batch, nck=nck),
        grid=(NB, S5_TIME_SLICES),
        in_specs=[u_spec, w_spec(L, W, S5_GROUP), w_spec(2, L * W, S5_STATE), w_spec(2, L * W, S5_STATE),
                  w_spec(1, SW), h_spec],
        out_specs=(u_spec, h_spec),
        out_shape=(jax.ShapeDtypeStruct((batch, seq, D_MODEL), F32), jax.ShapeDtypeStruct(h0.shape, F32)),
        scratch_shapes=[pltpu.VMEM(((L + 1) * W, 2 * W), BF16), pltpu.VMEM((L * W, SW), BF16),
                        pltpu.VMEM((L * W, SW), BF16),
                        pltpu.VMEM((rows, L * W), BF16), pltpu.VMEM((SW // W, rows, W), F32),
                        pltpu.VMEM((SW // W, rows, W), F32), pltpu.VMEM((batch, SW), F32)],
        compiler_params=_params("parallel", "arbitrary"),
        name="s5_seq",
    )(u, taps, mm, pp, lam, h0)


def _s5_mixer(x, u, h0_re, h0_im, s5p, w_glu, j, batch, seq):
    G, GC, P = S5_GROUPS, S5_GROUP, S5_STATE
    if seq > 1:
        NB, HS = G // S5_BLOCK_GROUPS, S5_BLOCK_GROUPS * P
        mats = _s5_block_mats(_s5_pieces(*[p[j] for p in s5p], S5_CHUNK))
        to_blocks = lambda h: h.reshape(batch, NB, HS).transpose(1, 0, 2)
        h0 = jnp.concatenate([to_blocks(h0_re), to_blocks(h0_im)], axis=-1)
        y, hfin = _s5_seq(u.reshape(batch, seq, D_MODEL), mats, h0, batch, seq)
        y = y.reshape(batch * seq, D_MODEL)
        from_blocks = lambda h: h.transpose(1, 0, 2).reshape(batch, G, P)
        hre, him = from_blocks(hfin[..., :HS]), from_blocks(hfin[..., HS:])
    else:
        mats = _s5_group_mats(_s5_pieces(*[p[j] for p in s5p], 1))
        mats = tuple(m.astype(BF16) for m in mats[:5]) + mats[5:]
        ug = u.reshape(batch, G, GC).transpose(1, 0, 2).astype(BF16)
        yg, hre, him = _s5_core(ug, mats, h0_re.transpose(1, 0, 2), h0_im.transpose(1, 0, 2),
                                1, batch, None, F32)
        y = yg.transpose(1, 0, 2).reshape(batch, D_MODEL)
        hre, him = hre.transpose(1, 0, 2), him.transpose(1, 0, 2)
    x = _proj(y, w_glu, j, glu=True, res=x)
    return x, hre, him


def _hg_lower_bound(logits, layer):
    m = jnp.max(logits, axis=0, keepdims=True)
    e = jnp.exp(logits - m)
    sm = e / jnp.sum(e, axis=0, keepdims=True)
    return jnp.sum(sm[:layer + 1], axis=0, keepdims=True) - sm[0:1]


def _hg_gates(z, lb):
    e = jnp.exp(-jnp.abs(z))
    r = 1.0 / (1.0 + e)
    er = e * r
    pos = z >= 0.0
    f = lb + (1.0 - lb) * jnp.where(pos, r, er)
    logf = jnp.where(f > 0.0, jnp.log(f), z)
    k = (1.0 - lb) * jnp.where(pos, er, r)
    return logf, k


def _hgrn_prompt_kernel(x_ref, g_ref, win_ref, lbl_ref, ng_ref, wout_ref, y_ref, sfin_ref,
                        s_scr, q_scr, k_scr, v_scr, gt_scr, o_scr, beta_scr, safe_scr, *, layer, tt):
    t = pl.program_id(1)
    C = HG_CHUNK

    @pl.when(t == 0)
    def _():
        s_scr[...] = jnp.zeros_like(s_scr)

    x = x_ref[...]
    h = _rms(x, g_ref[...]).astype(BF16)
    def proj(i):
        return _dot(h, win_ref[:, i * HG_WIDTH:(i + 1) * HG_WIDTH])

    lb = _hg_lower_bound(lbl_ref[...], layer)
    logf, kk = _hg_gates(proj(1), lb)
    k_scr[...] = kk
    q_scr[...] = _silu(proj(0))
    gt_scr[...] = _silu(proj(3))
    v_scr[...] = proj(2)

    row, col = _iota2((C, C))
    causal = row >= col
    tri = causal.astype(BF16)
    ng = ng_ref[...]
    nt_dims = (((1,), (1,)), ((), ()))
    tn_dims = (((0,), (0,)), ((), ()))

    def finish_head(hh, rows, q_dec, k_dec, o_intra, btot_h):
        sv = slice(hh * HG_DV, (hh + 1) * HG_DV)
        st = s_scr[hh]
        o = o_intra + lax.dot_general(q_dec.astype(BF16), st.astype(BF16), nt_dims, preferred_element_type=F32)
        kv_t = lax.dot_general(v_scr[rows, sv].astype(BF16), k_dec.astype(BF16), tn_dims,
                               preferred_element_type=F32)
        return (_rms(o, ng) * gt_scr[rows, sv]).astype(BF16), st * jnp.exp(btot_h) + kv_t

    def store_heads(rows, results):
        o_scr[rows, :] = jnp.concatenate([o for o, _ in results], axis=-1)
        for hh, (_, s_new) in enumerate(results):
            s_scr[hh] = s_new

    for ci in range(tt // C):
        parts = _split3(logf[ci * C:(ci + 1) * C])
        beta = sum(jnp.dot(tri, p, preferred_element_type=F32) for p in parts)
        beta_scr[ci * C:(ci + 1) * C, :] = beta
        mid = beta[C // 2 - 1:C // 2, :]
        spread = jnp.maximum(jnp.max(-mid), jnp.max(mid - beta[C - 1:C, :]))
        safe_scr[ci] = (spread <= HG_FACTORED_MAX_DECAY).astype(jnp.int32)

    def chunk_step(c, carry):
        rows = pl.ds(pl.multiple_of(c * C, C), C)
        base = pl.multiple_of(c * C, C)
        btot = beta_scr[pl.ds(base + (C - 8), 8), :][7:8]
        mid = beta_scr[pl.ds(base + (C // 2 - 8), 8), :][7:8]
        safe = safe_scr[c] == 1

        @pl.when(safe)
        def _():
            e_mid = jnp.exp(mid)
            e_tot = jnp.exp(btot - mid)
            results = []
            for hh in range(HG_HEADS):
                sk = slice(hh * HG_DK, (hh + 1) * HG_DK)
                d = beta_scr[rows, sk] - mid[:, sk]
                q_mid = q_scr[rows, sk] * jnp.exp(d)
                k_mid = k_scr[rows, sk] * jnp.exp(-d)
                att = lax.dot_general(q_mid.astype(BF16), k_mid.astype(BF16), nt_dims, preferred_element_type=F32)
                att = jnp.where(causal, att, 0.0).astype(BF16)
                o_intra = jnp.dot(att, v_scr[rows, hh * HG_DV:(hh + 1) * HG_DV].astype(BF16),
                                  preferred_element_type=F32)
                results.append(finish_head(hh, rows, q_mid * e_mid[:, sk], k_mid * e_tot[:, sk], o_intra,
                                           btot[:, sk]))
            store_heads(rows, results)

        @pl.when(jnp.logical_not(safe))
        def _():
            t_idx = lax.broadcasted_iota(jnp.int32, (C, 1), 0)
            results = []
            for hh in range(HG_HEADS):
                sk = slice(hh * HG_DK, (hh + 1) * HG_DK)
                sv = slice(hh * HG_DV, (hh + 1) * HG_DV)
                b = beta_scr[rows, sk]
                q = q_scr[rows, sk]

                def key_step(s8, acc):
                    off = pl.multiple_of(s8 * 8, 8)
                    b_keys = beta_scr[pl.ds(base + off, 8), sk]
                    k_keys = k_scr[pl.ds(base + off, 8), sk]
                    v_keys = v_scr[pl.ds(base + off, 8), sv]
                    for i in range(8):
                        w = jnp.exp(jnp.minimum(b - b_keys[i:i + 1], 0.0))
                        a = jnp.sum(q * k_keys[i:i + 1] * w, axis=-1, keepdims=True)
                        a = jnp.where(t_idx >= off + i, a, 0.0)
                        acc = acc + a * v_keys[i:i + 1]
                    return acc

                o_intra = lax.fori_loop(0, C // 8, key_step, jnp.zeros((C, HG_DV), F32))
                results.append(finish_head(hh, rows, q * jnp.exp(b), k_scr[rows, sk] * jnp.exp(btot[:, sk] - b),
                                           o_intra, btot[:, sk]))
            store_heads(rows, results)

        return carry

    lax.fori_loop(0, tt // C, chunk_step, 0)
    y_ref[...] = x + _dot(o_scr[...], wout_ref[...])

    @pl.when(t == pl.num_programs(1) - 1)
    def _():
        for hh in range(HG_HEADS):
            sfin_ref[hh] = s_scr[hh].T


def _hgrn_prompt(x, g, w_in, lb_logits, norm_g, w_out, layer, j, batch, seq):
    tt = HG_ROW_TILE
    nt = seq // tt
    x3 = x.reshape(batch, seq, D_MODEL)
    row_spec = pl.BlockSpec((None, tt, D_MODEL), lambda b, t: (b, t, 0))
    y, s_fin = pl.pallas_call(
        functools.partial(_hgrn_prompt_kernel, layer=layer, tt=tt),
        grid=(batch, nt),
        in_specs=[row_spec,
                  pl.BlockSpec((None, 1, D_MODEL), lambda b, t: (layer, 0, 0)),
                  _resident((None, D_MODEL, 4 * HG_WIDTH), lambda b, t: (j, 0, 0)),
                  pl.BlockSpec((DEPTH, HG_WIDTH), lambda b, t: (0, 0)),
                  pl.BlockSpec((None, 1, HG_DV), lambda b, t: (j, 0, 0)),
                  _resident((None, HG_WIDTH, D_MODEL), lambda b, t: (j, 0, 0))],
        out_specs=(row_spec,
                   pl.BlockSpec((None, HG_HEADS, HG_DK, HG_DV), lambda b, t: (b, 0, 0, 0))),
        out_shape=(jax.ShapeDtypeStruct((batch, seq, D_MODEL), F32),
                   jax.ShapeDtypeStruct((batch, HG_HEADS, HG_DK, HG_DV), F32)),
        scratch_shapes=[pltpu.VMEM((HG_HEADS, HG_DV, HG_DK), F32),
                        pltpu.VMEM((tt, HG_WIDTH), F32), pltpu.VMEM((tt, HG_WIDTH), F32),
                        pltpu.VMEM((tt, HG_WIDTH), F32), pltpu.VMEM((tt, HG_WIDTH), F32),
                        pltpu.VMEM((tt, HG_WIDTH), BF16), pltpu.VMEM((tt, HG_WIDTH), F32),
                        pltpu.SMEM((tt // HG_CHUNK,), jnp.int32)],
        compiler_params=_params("parallel", "arbitrary"),
        name="hgrn_prompt",
    )(x3, g.reshape(DEPTH, 1, D_MODEL), w_in, lb_logits, norm_g.reshape(-1, 1, HG_DV), w_out)
    return y.reshape(batch * seq, D_MODEL), s_fin


def _hgrn_sample_kernel(proj_ref, lbl_ref, ng_ref, s_ref, snew_ref, o_ref, *, layer, tb):
    proj = proj_ref[...]
    lb = _hg_lower_bound(lbl_ref[...], layer)
    z = proj[:, HG_WIDTH:2 * HG_WIDTH]
    e = jnp.exp(-jnp.abs(z))
    r = 1.0 / (1.0 + e)
    sig = jnp.where(z >= 0.0, r, e * r)
    f = lb + (1.0 - lb) * sig
    k = (1.0 - lb) * jnp.where(z >= 0.0, e * r, r)
    q = _silu(proj[:, :HG_WIDTH])
    v = proj[:, 2 * HG_WIDTH:3 * HG_WIDTH]
    gt = _silu(proj[:, 3 * HG_WIDTH:])
    ng = ng_ref[...]
    tok, lane = _iota2((tb, tb * HG_DV))
    spread = ((lane // HG_DV) == tok).astype(BF16)

    spread3 = jnp.concatenate([spread] * 3, axis=0)

    def columns(x, exact):
        if exact:
            return jnp.dot(jnp.concatenate(_split3(x.T), axis=1), spread3, preferred_element_type=F32)
        return jnp.dot(x.T.astype(BF16), spread, preferred_element_type=F32)

    for hh in range(HG_HEADS):
        sk = slice(hh * HG_DK, (hh + 1) * HG_DK)
        sv = slice(hh * HG_DV, (hh + 1) * HG_DV)
        f_c, k_c, q_c = columns(f[:, sk], True), columns(k[:, sk], False), columns(q[:, sk], False)
        for b in range(tb):
            blk = slice(b * HG_DV, (b + 1) * HG_DV)
            s_new = f_c[:, blk] * s_ref[b, hh] + k_c[:, blk] * v[b:b + 1, sv]
            snew_ref[b, hh] = s_new
            o = jnp.sum(q_c[:, blk] * s_new, axis=0, keepdims=True)
            o_ref[b:b + 1, sv] = _rms(o, ng) * gt[b:b + 1, sv]


def _hgrn_sample(proj, lb_logits, norm_g, state, layer, j):
    nb = proj.shape[0]
    tb = SAMPLE_TOKENS_PER_STEP
    st_spec = pl.BlockSpec((tb, HG_HEADS, HG_DK, HG_DV), lambda i: (i, 0, 0, 0))
    return pl.pallas_call(
        functools.partial(_hgrn_sample_kernel, layer=layer, tb=tb),
        grid=(nb // tb,),
        in_specs=[pl.BlockSpec((tb, 4 * HG_WIDTH), lambda i: (i, 0)),
                  pl.BlockSpec((DEPTH, HG_WIDTH), lambda i: (0, 0)),
                  pl.BlockSpec((None, 1, HG_DV), lambda i: (j, 0, 0)),
                  st_spec],
        out_specs=(st_spec, pl.BlockSpec((tb, HG_WIDTH), lambda i: (i, 0))),
        out_shape=(jax.ShapeDtypeStruct(state.shape, F32), jax.ShapeDtypeStruct((nb, HG_WIDTH), F32)),
        compiler_params=_params("parallel"),
        name="hgrn_sample",
    )(proj, lb_logits, norm_g.reshape(-1, 1, HG_DV), state)


def _xattn_prompt_kernel(x_ref, g_ref, wq_ref, k_ref, v_ref, wo_ref, y_ref):
    x = x_ref[...]
    h = _rms(x, g_ref[...]).astype(BF16)
    q = _dot(h, wq_ref[...]) * (1.0 / math.sqrt(MEM_HD))
    q = q.astype(BF16)
    outs = []
    for hh in range(MEM_HEADS):
        sl = slice(hh * MEM_HD, (hh + 1) * MEM_HD)
        s = lax.dot_general(q[:, sl], k_ref[:, sl], (((1,), (1,)), ((), ())), preferred_element_type=F32)
        p = jnp.exp(s - jnp.max(s, axis=-1, keepdims=True))
        den = jnp.sum(p, axis=-1, keepdims=True)
        o = jnp.dot(p.astype(BF16), v_ref[:, sl], preferred_element_type=F32)
        outs.append((o / den).astype(BF16))
    o = jnp.concatenate(outs, axis=-1)
    y_ref[...] = x + _dot(o, wo_ref[...])


def _xattn_prompt(x, g, w_q, mem_k, mem_v, w_o, layer, batch, seq):
    tt = XA_ROW_TILE
    row_spec = pl.BlockSpec((None, tt, D_MODEL), lambda b, t: (b, t, 0))
    kv_spec = pl.BlockSpec((None, N_MEM, D_MODEL), lambda b, t: (layer, b, 0))
    w_spec = _resident((None, D_MODEL, D_MODEL), lambda b, t: (layer, 0, 0))
    y = pl.pallas_call(
        _xattn_prompt_kernel,
        grid=(batch, seq // tt),
        in_specs=[row_spec, pl.BlockSpec((None, 1, D_MODEL), lambda b, t: (layer, 0, 0)),
                  w_spec, kv_spec, kv_spec, w_spec],
        out_specs=row_spec,
        out_shape=jax.ShapeDtypeStruct((batch, seq, D_MODEL), F32),
        compiler_params=_params("parallel", "parallel"),
        name="xattn_prompt",
    )(x.reshape(batch, seq, D_MODEL), g.reshape(DEPTH, 1, D_MODEL), w_q, mem_k, mem_v, w_o)
    return y.reshape(batch * seq, D_MODEL)


def _xattn_sample_kernel(q_ref, k_ref, v_ref, o_ref, *, tb):
    scale = 1.0 / math.sqrt(MEM_HD)
    rows = N_MEM * MEM_HEADS
    head, lane = _iota2((MEM_HEADS, rows))
    own = (lane & (MEM_HEADS - 1)) == head
    for b in range(tb):
        q = (q_ref[b] * scale).astype(BF16)
        k = k_ref[b].reshape(rows, MEM_HD).astype(BF16)
        v = v_ref[b].reshape(rows, MEM_HD).astype(BF16)
        s = lax.dot_general(q, k, (((1,), (1,)), ((), ())), preferred_element_type=F32)
        s = jnp.where(own, s, -jnp.inf)
        p = jnp.exp(s - jnp.max(s, axis=-1, keepdims=True))
        den = jnp.sum(p, axis=-1, keepdims=True)
        o_ref[b] = jnp.dot(p.astype(BF16), v, preferred_element_type=F32) / den


def _xattn_sample(q, cache_k, cache_v, layer):
    nb = q.shape[0]
    tb = XA_SAMPLE_TOKENS_PER_STEP
    kv_spec = pl.BlockSpec((None, tb, N_MEM, MEM_HEADS, MEM_HD), lambda i: (layer, i, 0, 0, 0))
    q_spec = pl.BlockSpec((tb, MEM_HEADS, MEM_HD), lambda i: (i, 0, 0))
    return pl.pallas_call(
        functools.partial(_xattn_sample_kernel, tb=tb),
        grid=(nb // tb,),
        in_specs=[q_spec, kv_spec, kv_spec],
        out_specs=q_spec,
        out_shape=jax.ShapeDtypeStruct((nb, MEM_HEADS, MEM_HD), F32),
        compiler_params=_params("parallel"),
        name="xattn_sample",
    )(q.reshape(nb, MEM_HEADS, MEM_HD), cache_k, cache_v).reshape(nb, D_MODEL)


def _ffn1(x, w, i, ffn, emit):
    if emit:
        return ffn(x, w["ffn1_norm"], w["ffn1_w_in"], w["ffn1_w_out"], i, post="emit", g2=w["mix_norm"][i],
                   u_dtype=F32)
    return ffn(x, w["ffn1_norm"], w["ffn1_w_in"], w["ffn1_w_out"], i)


def _ffn2(x, w, i, ffn):
    last = i == DEPTH - 1
    return ffn(x, w["ffn2_norm"], w["ffn2_w_in"], w["ffn2_w_out"], i,
               post="replace" if last else "none", g2=w["final_norm"] if last else None)


def _prompt_trunk(x, batch, seq, mem_k, mem_v, w, ffn):
    zero = jnp.zeros((batch, S5_GROUPS, S5_STATE), F32)
    new_re, new_im, new_hg = [], [], []
    for i in range(DEPTH):
        j = i // 2
        if i % 2 == 0:
            x, u = _ffn1(x, w, i, ffn, True)
            x, hr, hi = _s5_mixer(x, u, zero, zero, w["s5"], w["s5_w_glu"], j, batch, seq)
            new_re.append(hr)
            new_im.append(hi)
        else:
            x = _ffn1(x, w, i, ffn, False)
            x, sn = _hgrn_prompt(x, w["mix_norm"], w["hg_w_in"], w["hg_lb_logits"], w["hg_norm"],
                                 w["hg_w_out"], i, j, batch, seq)
            new_hg.append(sn)
        x = _xattn_prompt(x, w["xattn_norm"], w["xattn_w_q"], mem_k, mem_v, w["xattn_w_o"], i, batch, seq)
        x = _ffn2(x, w, i, ffn)
    return x, jnp.stack(new_re), jnp.stack(new_im), jnp.stack(new_hg)


def _sample_trunk(x, batch, s5_re, s5_im, hg_state, w):
    new_re, new_im, new_hg = [], [], []
    for i in range(DEPTH):
        j = i // 2
        if i % 2 == 0:
            x, u = _ffn1(x, w, i, _ffn, True)
            x, hr, hi = _s5_mixer(x, u, s5_re[j], s5_im[j], w["s5"], w["s5_w_glu"], j, batch, 1)
            new_re.append(hr)
            new_im.append(hi)
        else:
            x = _ffn1(x, w, i, _ffn, False)
            proj = _proj(x, w["hg_w_in"], j, g=w["mix_norm"], g_layer=i)
            sn, o = _hgrn_sample(proj, w["hg_lb_logits"], w["hg_norm"], hg_state[j], i, j)
            x = _proj(o, w["hg_w_out"], j, res=x)
            new_hg.append(sn)
        q = _proj(x, w["xattn_w_q"], i, g=w["xattn_norm"], g_layer=i)
        o = yield q, i
        x = _proj(o, w["xattn_w_o"], i, res=x)
        x = _ffn2(x, w, i, _ffn)
    return x, jnp.stack(new_re), jnp.stack(new_im), jnp.stack(new_hg)


def kernel(x_prompt, x_sample, mem_prompt, state_s5_re, state_s5_im, state_hgrn, cache_mem_k, cache_mem_v, ffn1_norm, ffn1_w_in, ffn1_w_out, mix_norm, xattn_norm, mem_norm, xattn_w_q, xattn_w_kv, xattn_w_o, ffn2_norm, ffn2_w_in, ffn2_w_out, s5_a_re, s5_a_im, s5_log_dt, s5_b_re, s5_b_im, s5_c_re, s5_c_im, s5_d, s5_w_glu, hg_w_in, hg_lb_logits, hg_norm, hg_w_out, final_norm):
    bp, seq, _ = x_prompt.shape
    bs = x_sample.shape[0]
    w = dict(ffn1_norm=ffn1_norm, ffn1_w_in=ffn1_w_in, ffn1_w_out=ffn1_w_out, mix_norm=mix_norm,
             xattn_norm=xattn_norm, xattn_w_q=xattn_w_q, xattn_w_o=xattn_w_o, ffn2_norm=ffn2_norm,
             ffn2_w_in=ffn2_w_in, ffn2_w_out=ffn2_w_out,
             s5=(s5_a_re, s5_a_im, s5_log_dt, s5_b_re, s5_b_im, s5_c_re, s5_c_im, s5_d),
             s5_w_glu=s5_w_glu, hg_w_in=hg_w_in, hg_lb_logits=hg_lb_logits, hg_norm=hg_norm,
             hg_w_out=hg_w_out, final_norm=final_norm)

    mem_k, mem_v, mem_k_bf, mem_v_bf = _mem_kv(mem_prompt, mem_norm, xattn_w_kv)

    sample = _sample_trunk(x_sample.reshape(bs, D_MODEL), bs, state_s5_re, state_s5_im, state_hgrn, w)
    pending = [next(sample)]
    sample_out = []

    def resume(o):
        try:
            pending[0] = sample.send(o)
        except StopIteration as done:
            pending[0] = None
            sample_out.append(done.value)

    def ffn_with_rider(x, *args, **kwargs):
        steps = x.shape[0] // min(FFN_ROW_TILE_WITH_RIDER, x.shape[0])
        if pending[0] is None or bs % steps:
            return _ffn(x, *args, **kwargs)
        q, layer = pending[0]
        *outs, o = _ffn(x, *args, rider=(q, cache_mem_k, cache_mem_v, layer), **kwargs)
        resume(o)
        return outs[0] if len(outs) == 1 else tuple(outs)

    y_p, re_p, im_p, hg_p = _prompt_trunk(x_prompt.reshape(bp * seq, D_MODEL), bp, seq, mem_k_bf, mem_v_bf, w,
                                          ffn_with_rider)
    while pending[0] is not None:
        q, layer = pending[0]
        resume(_xattn_sample(q, cache_mem_k, cache_mem_v, layer))
    y_s, re_s, im_s, hg_s = sample_out[0]
    return (y_p.reshape(bp, seq, D_MODEL), y_s.reshape(bs, 1, D_MODEL), re_p, im_p, re_s, im_s, hg_p, hg_s,
            mem_k, mem_v)
```

```python
import functools
import math

import jax
import jax.numpy as jnp
from jax import lax
from jax.experimental import pallas as pl
from jax.experimental.pallas import tpu as pltpu

F32 = jnp.float32
BF16 = jnp.bfloat16

D_MODEL = 1024
DEPTH = 2
S5_GROUP = 16
S5_GROUPS = D_MODEL // S5_GROUP
S5_STATE = 64
S5_CHUNK = 16
S5_BLOCK_GROUPS = 8
S5_TIME_SLICES = 2
HG_DK = 128
HG_HEADS = D_MODEL // HG_DK
HG_DV = D_MODEL // HG_HEADS
HG_WIDTH = HG_HEADS * HG_DK
HG_CHUNK = 128
HG_FACTORED_MAX_DECAY = 60.0
N_MEM = 256
MEM_HEADS = 4
MEM_HD = D_MODEL // MEM_HEADS
FFN_DIM = 2816
EPS = 1e-6

V7X_VMEM_LIMIT_BYTES = 56 * 1024 * 1024

ROW_TILE = 512
FFN_CHUNK = 1408
FFN_ROW_TILE_WITH_RIDER = 256
HG_ROW_TILE = 512
XA_ROW_TILE = 512
S5_GROUPS_PER_STEP = 16
SAMPLE_TOKENS_PER_STEP = 8
XA_SAMPLE_TOKENS_PER_STEP = 4


def _params(*semantics):
    return pltpu.CompilerParams(dimension_semantics=semantics,
                                vmem_limit_bytes=V7X_VMEM_LIMIT_BYTES)


def _resident(shape, index_map):
    return pl.BlockSpec(shape, index_map, pipeline_mode=pl.Buffered(1))


def _rms(x, g):
    ms = jnp.mean(x * x, axis=-1, keepdims=True)
    return x * lax.rsqrt(ms + EPS) * g


def _sigmoid(x):
    return 1.0 / (1.0 + jnp.exp(-x))


def _silu(x):
    return x * _sigmoid(x)


def _gelu_tanh(x):
    c = math.sqrt(2.0 / math.pi)
    return 0.5 * x * (1.0 + jnp.tanh(c * (x + 0.044715 * (x * x * x))))


def _dot(a, w):
    return jnp.dot(a, w.astype(BF16), preferred_element_type=F32)


def _split3(x):
    hi = x.astype(BF16)
    r1 = x - hi.astype(F32)
    mid = r1.astype(BF16)
    lo = (r1 - mid.astype(F32)).astype(BF16)
    return hi, mid, lo


def _ffn_kernel(x_ref, g_ref, win_ref, wout_ref, *rest, post, rider_tokens):
    rest = list(rest)
    g2_ref = rest.pop(0) if post != "none" else None
    if rider_tokens:
        q_ref, k_ref, v_ref = rest[:3]
        rest = rest[3:]
        _xattn_sample_kernel(q_ref, k_ref, v_ref, rest.pop(), tb=rider_tokens)
    x = x_ref[...]
    h = _rms(x, g_ref[...]).astype(BF16)
    acc = jnp.zeros_like(x)
    for c in range(FFN_DIM // FFN_CHUNK):
        lo = c * FFN_CHUNK
        gate = _dot(h, win_ref[:, lo:lo + FFN_CHUNK])
        up = _dot(h, win_ref[:, FFN_DIM + lo:FFN_DIM + lo + FFN_CHUNK])
        act = (_silu(gate) * up).astype(BF16)
        acc = acc + _dot(act, wout_ref[lo:lo + FFN_CHUNK, :])
    y = x + 0.5 * acc
    y_ref = rest[0]
    if post == "replace":
        y_ref[...] = _rms(y, g2_ref[...])
    else:
        y_ref[...] = y
        if post == "emit":
            rest[1][...] = _rms(y, g2_ref[...]).astype(rest[1].dtype)


def _ffn(x, g, w_in, w_out, layer, post="none", g2=None, u_dtype=BF16, rider=None):
    rows = x.shape[0]
    tm = min(ROW_TILE if rider is None else FFN_ROW_TILE_WITH_RIDER, rows)
    steps = rows // tm
    row_spec = pl.BlockSpec((tm, D_MODEL), lambda i: (i, 0))
    vec_spec = pl.BlockSpec((1, D_MODEL), lambda i: (0, 0))
    in_specs = [row_spec,
                pl.BlockSpec((None, 1, D_MODEL), lambda i: (layer, 0, 0)),
                _resident((None, D_MODEL, 2 * FFN_DIM), lambda i: (layer, 0, 0)),
                _resident((None, FFN_DIM, D_MODEL), lambda i: (layer, 0, 0))]
    args = [x, g.reshape(DEPTH, 1, D_MODEL), w_in, w_out]
    out_shape = [jax.ShapeDtypeStruct((rows, D_MODEL), F32)]
    out_specs = [row_spec]
    if post != "none":
        in_specs.append(vec_spec)
        args.append(g2.reshape(1, D_MODEL))
    if post == "emit":
        out_shape.append(jax.ShapeDtypeStruct((rows, D_MODEL), u_dtype))
        out_specs.append(row_spec)
    tb = 0
    if rider is not None:
        q, cache_k, cache_v, r_layer = rider
        nb = q.shape[0]
        assert nb % steps == 0
        tb = nb // steps
        kv_spec = pl.BlockSpec((None, tb, N_MEM, MEM_HEADS, MEM_HD), lambda i: (r_layer, i, 0, 0, 0))
        q_spec = pl.BlockSpec((tb, MEM_HEADS, MEM_HD), lambda i: (i, 0, 0))
        in_specs += [q_spec, kv_spec, kv_spec]
        args += [q.reshape(nb, MEM_HEADS, MEM_HD), cache_k, cache_v]
        out_shape.append(jax.ShapeDtypeStruct((nb, MEM_HEADS, MEM_HD), F32))
        out_specs.append(q_spec)
    outs = list(pl.pallas_call(
        functools.partial(_ffn_kernel, post=post, rider_tokens=tb),
        grid=(steps,),
        in_specs=in_specs, out_specs=tuple(out_specs), out_shape=tuple(out_shape),
        compiler_params=_params("parallel"),
        name="ffn",
    )(*args))
    if rider is not None:
        outs[-1] = outs[-1].reshape(-1, D_MODEL)
    return outs[0] if len(outs) == 1 else tuple(outs)


def _proj_kernel(*refs, norm, glu, residual):
    refs = list(refs)
    x_ref = refs.pop(0)
    g_ref = refs.pop(0) if norm else None
    w_ref = refs.pop(0)
    res_ref = refs.pop(0) if residual else None
    (o_ref,) = refs
    x = x_ref[...]
    if norm:
        x = _rms(x, g_ref[...])
    y = _dot(x.astype(BF16), w_ref[...])
    if glu:
        half = y.shape[-1] // 2
        y = y[:, :half] * _sigmoid(y[:, half:])
    if residual:
        y = y + res_ref[...]
    o_ref[...] = y


def _proj(x, w, layer, g=None, g_layer=0, glu=False, res=None):
    rows, kdim = x.shape
    ndim = w.shape[-1]
    nout = ndim // 2 if glu else ndim
    tm = min(ROW_TILE, rows)
    in_specs = [pl.BlockSpec((tm, kdim), lambda i: (i, 0))]
    args = [x]
    if g is not None:
        in_specs.append(pl.BlockSpec((None, 1, kdim), lambda i: (g_layer, 0, 0)))
        args.append(g.reshape(g.shape[0], 1, kdim))
    in_specs.append(_resident((None, kdim, ndim), lambda i: (layer, 0, 0)))
    args.append(w)
    if res is not None:
        in_specs.append(pl.BlockSpec((tm, nout), lambda i: (i, 0)))
        args.append(res)
    return pl.pallas_call(
        functools.partial(_proj_kernel, norm=g is not None, glu=glu, residual=res is not None),
        grid=(rows // tm,),
        in_specs=in_specs,
        out_specs=pl.BlockSpec((tm, nout), lambda i: (i, 0)),
        out_shape=jax.ShapeDtypeStruct((rows, nout), F32),
        compiler_params=_params("parallel"),
        name="proj",
    )(*args)


def _memkv_kernel(x_ref, g_ref, w_ref, k_ref, v_ref, kb_ref, vb_ref):
    h = _rms(x_ref[...], g_ref[...]).astype(BF16)
    y = _dot(h, w_ref[...])
    k, v = y[:, :D_MODEL], y[:, D_MODEL:]
    k_ref[...] = k.reshape(k_ref.shape)
    v_ref[...] = v.reshape(v_ref.shape)
    kb_ref[...] = k.astype(BF16)
    vb_ref[...] = v.astype(BF16)


def _mem_kv(mem, g, w_kv):
    batch = mem.shape[0]
    rows = batch * N_MEM
    nb = max(1, min(ROW_TILE, rows) // N_MEM)
    tm = nb * N_MEM
    out5 = jax.ShapeDtypeStruct((DEPTH, batch, N_MEM, MEM_HEADS, MEM_HD), F32)
    out2 = jax.ShapeDtypeStruct((DEPTH, rows, D_MODEL), BF16)
    spec5 = pl.BlockSpec((None, nb, N_MEM, MEM_HEADS, MEM_HD), lambda l, i: (l, i, 0, 0, 0))
    spec2 = pl.BlockSpec((None, tm, D_MODEL), lambda l, i: (l, i, 0))
    return pl.pallas_call(
        _memkv_kernel,
        grid=(DEPTH, rows // tm),
        in_specs=[pl.BlockSpec((tm, D_MODEL), lambda l, i: (i, 0)),
                  pl.BlockSpec((None, 1, D_MODEL), lambda l, i: (l, 0, 0)),
                  pl.BlockSpec((None, D_MODEL, 2 * D_MODEL), lambda l, i: (l, 0, 0))],
        out_specs=(spec5, spec5, spec2, spec2), out_shape=(out5, out5, out2, out2),
        compiler_params=_params("parallel", "parallel"),
        name="mem_kv",
    )(mem.reshape(rows, D_MODEL), g.reshape(DEPTH, 1, D_MODEL), w_kv)


def _s5_pieces(a_re, a_im, log_dt, b_re, b_im, c_re, c_im, d, steps):
    G, P, GC = S5_GROUPS, S5_STATE, S5_GROUP
    L = steps
    dt = jnp.exp(log_dt)[:, None]
    xr, xi = a_re * dt, a_im * dt
    j = jnp.arange(L + 1, dtype=F32)[:, None, None]
    mag = jnp.exp(xr[None] * j)
    pw_re, pw_im = mag * jnp.cos(xi[None] * j), mag * jnp.sin(xi[None] * j)
    nr, ni = pw_re[1] - 1.0, pw_im[1]
    den = a_re * a_re + a_im * a_im
    fr, fi = ((nr * a_re + ni * a_im) / den)[:, None, :], ((ni * a_re - nr * a_im) / den)[:, None, :]
    bt_re, bt_im = b_re.transpose(0, 2, 1), b_im.transpose(0, 2, 1)
    bb_re = fr * bt_re - fi * bt_im
    bb_im = fr * bt_im + fi * bt_re

    def times_c(p_re, p_im):
        p_re, p_im = p_re[:, :, None, :], p_im[:, :, None, :]
        return p_re * c_re[None] - p_im * c_im[None], p_re * c_im[None] + p_im * c_re[None]

    w_re, w_im = times_c(pw_re[:L], pw_im[:L])

    def over_state(b, w_):
        w_ = w_.transpose(1, 0, 2, 3).reshape(G, L * GC, P)
        return jnp.einsum("gip,gnp->gin", b, w_, precision=lax.Precision.HIGHEST)

    kern = (over_state(bb_re, w_re) - over_state(bb_im, w_im)).reshape(G, GC, L, GC).transpose(2, 0, 1, 3)
    kern = kern.at[0].add(d[:, :, None] * jnp.eye(GC, dtype=F32)[None])
    rp_re, rp_im = pw_re[:L][::-1][:, :, None, :], pw_im[:L][::-1][:, :, None, :]
    m_re = rp_re * bb_re[None] - rp_im * bb_im[None]
    m_im = rp_re * bb_im[None] + rp_im * bb_re[None]
    q_re, q_im = times_c(pw_re[1:], pw_im[1:])
    return kern, m_re, m_im, q_re, -q_im, pw_re[L], pw_im[L]


def _s5_group_mats(pieces):
    kern, m_re, m_im, p_re, p_im, l_re, l_im = pieces
    return (kern[0], m_re[0], m_im[0], p_re[0].transpose(0, 2, 1), p_im[0].transpose(0, 2, 1),
            l_re[:, None, :], l_im[:, None, :])


def _s5_block_mats(pieces):
    kern, m_re, m_im, p_re, p_im, l_re, l_im = pieces
    L = kern.shape[0]
    GB = S5_BLOCK_GROUPS
    NB = S5_GROUPS // GB
    GC, P = S5_GROUP, S5_STATE
    taps = kern.astype(BF16).reshape(L, NB, GB * GC, GC).transpose(1, 0, 2, 3)

    def rows(m):
        return m.astype(BF16).reshape(L, NB, GB * GC, P).transpose(1, 0, 2, 3).reshape(NB, L * GB * GC, P)

    mm = jnp.stack([rows(m_re), rows(m_im)], axis=1)
    pp = jnp.stack([rows(p_re), rows(p_im)], axis=1)
    lam = jnp.concatenate([l_re.reshape(NB, 1, GB * P), l_im.reshape(NB, 1, GB * P)], axis=2)
    return taps, mm, pp, lam


def _s5_kernel(u_ref, t_ref, mre_ref, mim_ref, pre_ref, pim_ref, lre_ref, lim_ref, h0re_ref, h0im_ref,
               y_ref, hre_ref, him_ref, inj_re, inj_im, hs_re, hs_im, *, n_chunks, rb, gps, precision):
    def mm(a, b):
        return jnp.dot(a, b, preferred_element_type=F32, precision=precision)

    for g in range(gps):
        u = u_ref[g]
        inj_re[g] = mm(u, mre_ref[g])
        inj_im[g] = mm(u, mim_ref[g])

    lam_re = [jnp.broadcast_to(lre_ref[g], (rb, S5_STATE)) for g in range(gps)]
    lam_im = [jnp.broadcast_to(lim_ref[g], (rb, S5_STATE)) for g in range(gps)]

    def step(k, carry):
        rows = pl.ds(pl.multiple_of(k * rb, rb), rb)
        nxt = []
        for g in range(gps):
            hr, hi = carry[2 * g], carry[2 * g + 1]
            hs_re[g, rows, :] = hr
            hs_im[g, rows, :] = hi
            nxt.append(lam_re[g] * hr - lam_im[g] * hi + inj_re[g, rows, :])
            nxt.append(lam_re[g] * hi + lam_im[g] * hr + inj_im[g, rows, :])
        return tuple(nxt)

    init = []
    for g in range(gps):
        init += [h0re_ref[g], h0im_ref[g]]
    fin = lax.fori_loop(0, n_chunks, step, tuple(init))

    for g in range(gps):
        hre_ref[g] = fin[2 * g]
        him_ref[g] = fin[2 * g + 1]
        dt = u_ref.dtype
        y = (mm(u_ref[g], t_ref[g]) + mm(hs_re[g].astype(dt), pre_ref[g])
             + mm(hs_im[g].astype(dt), pim_ref[g]))
        y_ref[g] = _gelu_tanh(y).astype(y_ref.dtype)


def _s5_core(u, mats, h0_re, h0_im, n_chunks, rb, precision, y_dtype):
    G, R, W = u.shape
    P = S5_STATE
    gps = S5_GROUPS_PER_STEP
    tmat, m_re, m_im, p_re, p_im, l_re, l_im = mats

    def spec(a, b):
        return pl.BlockSpec((gps, a, b), lambda i: (i, 0, 0))

    st = jax.ShapeDtypeStruct((G, rb, P), F32)
    scr = lambda: pltpu.VMEM((gps, R, P), F32)
    return pl.pallas_call(
        functools.partial(_s5_kernel, n_chunks=n_chunks, rb=rb, gps=gps, precision=precision),
        grid=(G // gps,),
        in_specs=[spec(R, W), spec(W, W), spec(W, P), spec(W, P), spec(P, W), spec(P, W),
                  spec(1, P), spec(1, P), spec(rb, P), spec(rb, P)],
        out_specs=(spec(R, W), spec(rb, P), spec(rb, P)),
        out_shape=(jax.ShapeDtypeStruct((G, R, W), y_dtype), st, st),
        scratch_shapes=[scr(), scr(), scr(), scr()],
        compiler_params=_params("parallel"),
        name="s5_core",
    )(u, tmat, m_re, m_im, p_re, p_im, l_re, l_im, h0_re, h0_im)


def _iota2(shape):
    return lax.broadcasted_iota(jnp.int32, shape, 0), lax.broadcasted_iota(jnp.int32, shape, 1)


def _s5_expand(taps_ref, mc_ref, pc_ref, w2_ref, m_ref, p_ref):
    L, GC, P = S5_CHUNK, S5_GROUP, S5_STATE
    GB = S5_BLOCK_GROUPS
    W, HS = GB * GC, GB * P
    gc_bits, p_bits = GC.bit_length() - 1, P.bit_length() - 1

    r, c = _iota2((P, HS))
    rep_m = ((c & (P - 1)) == r).astype(BF16)
    r, c = _iota2((L * W, HS))
    mask_m = ((r >> gc_bits) & (GB - 1)) == (c >> p_bits)
    for half in range(2):
        for src, dst in ((mc_ref, m_ref), (pc_ref, p_ref)):
            dst[:, half * HS:(half + 1) * HS] = jnp.where(
                mask_m, jnp.dot(src[half], rep_m, preferred_element_type=F32), 0.0).astype(BF16)

    r, c = _iota2((GC, W))
    rep_k = ((c & (GC - 1)) == r).astype(BF16)
    r, c = _iota2((W, W))
    mask_k = (r >> gc_bits) == (c >> gc_bits)
    zero = jnp.zeros((W, W), BF16)
    w2_ref[L * W:, :W] = zero
    w2_ref[:W, W:] = zero
    for i in range(L):
        kb = jnp.where(mask_k, jnp.dot(taps_ref[L - 1 - i], rep_k, preferred_element_type=F32), 0.0).astype(BF16)
        w2_ref[i * W:(i + 1) * W, :W] = kb
        w2_ref[(i + 1) * W:(i + 2) * W, W:] = kb


def _s5_seq_kernel(u_ref, taps_ref, mc_ref, pc_ref, lam_ref, h0_ref, y_ref, hfin_ref,
                   w2_ref, m_ref, p_ref, lhs_scr, inj_scr, hs_scr, h_scr, *, batch, nck):
    L, W = S5_CHUNK, S5_BLOCK_GROUPS * S5_GROUP
    NT = S5_BLOCK_GROUPS * S5_STATE // W

    @pl.when(pl.program_id(1) == 0)
    def _():
        h_scr[...] = h0_ref[...]
        _s5_expand(taps_ref, mc_ref, pc_ref, w2_ref, m_ref, p_ref)

    for b in range(batch):
        for l in range(L):
            lhs_scr[b * nck:(b + 1) * nck, l * W:(l + 1) * W] = u_ref[b, pl.ds(l, nck, stride=L), :].astype(BF16)

    def swap_major(x, a, b):
        return jnp.swapaxes(x.reshape(a, b, W), 0, 1).reshape(a * b, W)

    inj = jnp.dot(lhs_scr[...], m_ref[...], preferred_element_type=F32)
    for t in range(2 * NT):
        inj_scr[t] = swap_major(inj[:, t * W:(t + 1) * W], batch, nck)

    lam = [jnp.broadcast_to(lam_ref[:, t * W:(t + 1) * W], (batch, W)) for t in range(2 * NT)]

    def step(k, h):
        rows = pl.ds(pl.multiple_of(k * batch, batch), batch)
        nxt_re, nxt_im = [], []
        for t in range(NT):
            hr, hi = h[t], h[NT + t]
            hs_scr[t, rows, :] = hr
            hs_scr[NT + t, rows, :] = hi
            nxt_re.append(lam[t] * hr - lam[NT + t] * hi + inj_scr[t, rows, :])
            nxt_im.append(lam[t] * hi + lam[NT + t] * hr + inj_scr[NT + t, rows, :])
        return tuple(nxt_re + nxt_im)

    h = lax.fori_loop(0, nck, step, tuple(h_scr[:, t * W:(t + 1) * W] for t in range(2 * NT)))
    for t in range(2 * NT):
        h_scr[:, t * W:(t + 1) * W] = h[t]
    hfin_ref[...] = h_scr[...]

    hs = jnp.concatenate([swap_major(hs_scr[t], nck, batch) for t in range(2 * NT)], axis=-1).astype(BF16)
    for pr in range(L // 2):
        kk = (2 * pr + 2) * W
        y = (jnp.dot(lhs_scr[:, :kk], w2_ref[(L - 1 - 2 * pr) * W:, :], preferred_element_type=F32)
             + lax.dot_general(hs, p_ref[2 * pr * W:(2 * pr + 2) * W, :], (((1,), (1,)), ((), ())),
                               preferred_element_type=F32))
        y = _gelu_tanh(y)
        for s in range(2):
            for b in range(batch):
                y_ref[b, pl.ds(2 * pr + s, nck, stride=L), :] = y[b * nck:(b + 1) * nck, s * W:(s + 1) * W]


def _s5_seq(u, mats, h0, batch, seq):
    taps, mm, pp, lam = mats
    L, W = S5_CHUNK, S5_BLOCK_GROUPS * S5_GROUP
    NB = S5_GROUPS // S5_BLOCK_GROUPS
    HS = S5_BLOCK_GROUPS * S5_STATE
    SW = 2 * HS
    ts = seq // S5_TIME_SLICES
    nck = ts // L
    rows = batch * nck
    u_spec = pl.BlockSpec((batch, ts, W), lambda i, t: (0, t, i))
    h_spec = pl.BlockSpec((None, batch, SW), lambda i, t: (i, 0, 0))

    def w_spec(*dims):
        return pl.BlockSpec((None,) + dims, lambda i, t: (i,) + (0,) * len(dims))

    return pl.pallas_call(
        functools.partial(_s5_seq_kernel, batch=batch, nck=nck),
        grid=(NB, S5_TIME_SLICES),
        in_specs=[u_spec, w_spec(L, W, S5_GROUP), w_spec(2, L * W, S5_STATE), w_spec(2, L * W, S5_STATE),
                  w_spec(1, SW), h_spec],
        out_specs=(u_spec, h_spec),
        out_shape=(jax.ShapeDtypeStruct((batch, seq, D_MODEL), F32), jax.ShapeDtypeStruct(h0.shape, F32)),
        scratch_shapes=[pltpu.VMEM(((L + 1) * W, 2 * W), BF16), pltpu.VMEM((L * W, SW), BF16),
                        pltpu.VMEM((L * W, SW), BF16),
                        pltpu.VMEM((rows, L * W), BF16), pltpu.VMEM((SW // W, rows, W), F32),
                        pltpu.VMEM((SW // W, rows, W), F32), pltpu.VMEM((batch, SW), F32)],
        compiler_params=_params("parallel", "arbitrary"),
        name="s5_seq",
    )(u, taps, mm, pp, lam, h0)


def _s5_mixer(x, u, h0_re, h0_im, s5p, w_glu, j, batch, seq):
    G, GC, P = S5_GROUPS, S5_GROUP, S5_STATE
    if seq > 1:
        NB, HS = G // S5_BLOCK_GROUPS, S5_BLOCK_GROUPS * P
        mats = _s5_block_mats(_s5_pieces(*[p[j] for p in s5p], S5_CHUNK))
        to_blocks = lambda h: h.reshape(batch, NB, HS).transpose(1, 0, 2)
        h0 = jnp.concatenate([to_blocks(h0_re), to_blocks(h0_im)], axis=-1)
        y, hfin = _s5_seq(u.reshape(batch, seq, D_MODEL), mats, h0, batch, seq)
        y = y.reshape(batch * seq, D_MODEL)
        from_blocks = lambda h: h.transpose(1, 0, 2).reshape(batch, G, P)
        hre, him = from_blocks(hfin[..., :HS]), from_blocks(hfin[..., HS:])
    else:
        mats = _s5_group_mats(_s5_pieces(*[p[j] for p in s5p], 1))
        mats = tuple(m.astype(BF16) for m in mats[:5]) + mats[5:]
        ug = u.reshape(batch, G, GC).transpose(1, 0, 2).astype(BF16)
        yg, hre, him = _s5_core(ug, mats, h0_re.transpose(1, 0, 2), h0_im.transpose(1, 0, 2),
                                1, batch, None, F32)
        y = yg.transpose(1, 0, 2).reshape(batch, D_MODEL)
        hre, him = hre.transpose(1, 0, 2), him.transpose(1, 0, 2)
    x = _proj(y, w_glu, j, glu=True, res=x)
    return x, hre, him


def _hg_lower_bound(logits, layer):
    m = jnp.max(logits, axis=0, keepdims=True)
    e = jnp.exp(logits - m)
    sm = e / jnp.sum(e, axis=0, keepdims=True)
    return jnp.sum(sm[:layer + 1], axis=0, keepdims=True) - sm[0:1]


def _hg_gates(z, lb):
    e = jnp.exp(-jnp.abs(z))
    r = 1.0 / (1.0 + e)
    er = e * r
    pos = z >= 0.0
    f = lb + (1.0 - lb) * jnp.where(pos, r, er)
    logf = jnp.where(f > 0.0, jnp.log(f), z)
    k = (1.0 - lb) * jnp.where(pos, er, r)
    return logf, k


def _hgrn_prompt_kernel(x_ref, g_ref, win_ref, lbl_ref, ng_ref, wout_ref, y_ref, sfin_ref,
                        s_scr, q_scr, k_scr, v_scr, gt_scr, o_scr, beta_scr, safe_scr, *, layer, tt):
    t = pl.program_id(1)
    C = HG_CHUNK

    @pl.when(t == 0)
    def _():
        s_scr[...] = jnp.zeros_like(s_scr)

    x = x_ref[...]
    h = _rms(x, g_ref[...]).astype(BF16)
    def proj(i):
        return _dot(h, win_ref[:, i * HG_WIDTH:(i + 1) * HG_WIDTH])

    lb = _hg_lower_bound(lbl_ref[...], layer)
    logf, kk = _hg_gates(proj(1), lb)
    k_scr[...] = kk
    q_scr[...] = _silu(proj(0))
    gt_scr[...] = _silu(proj(3))
    v_scr[...] = proj(2)

    row, col = _iota2((C, C))
    causal = row >= col
    tri = causal.astype(BF16)
    ng = ng_ref[...]
    nt_dims = (((1,), (1,)), ((), ()))
    tn_dims = (((0,), (0,)), ((), ()))

    def finish_head(hh, rows, q_dec, k_dec, o_intra, btot_h):
        sv = slice(hh * HG_DV, (hh + 1) * HG_DV)
        st = s_scr[hh]
        o = o_intra + lax.dot_general(q_dec.astype(BF16), st.astype(BF16), nt_dims, preferred_element_type=F32)
        kv_t = lax.dot_general(v_scr[rows, sv].astype(BF16), k_dec.astype(BF16), tn_dims,
                               preferred_element_type=F32)
        return (_rms(o, ng) * gt_scr[rows, sv]).astype(BF16), st * jnp.exp(btot_h) + kv_t

    def store_heads(rows, results):
        o_scr[rows, :] = jnp.concatenate([o for o, _ in results], axis=-1)
        for hh, (_, s_new) in enumerate(results):
            s_scr[hh] = s_new

    for ci in range(tt // C):
        parts = _split3(logf[ci * C:(ci + 1) * C])
        beta = sum(jnp.dot(tri, p, preferred_element_type=F32) for p in parts)
        beta_scr[ci * C:(ci + 1) * C, :] = beta
        mid = beta[C // 2 - 1:C // 2, :]
        spread = jnp.maximum(jnp.max(-mid), jnp.max(mid - beta[C - 1:C, :]))
        safe_scr[ci] = (spread <= HG_FACTORED_MAX_DECAY).astype(jnp.int32)

    def chunk_step(c, carry):
        rows = pl.ds(pl.multiple_of(c * C, C), C)
        base = pl.multiple_of(c * C, C)
        btot = beta_scr[pl.ds(base + (C - 8), 8), :][7:8]
        mid = beta_scr[pl.ds(base + (C // 2 - 8), 8), :][7:8]
        safe = safe_scr[c] == 1

        @pl.when(safe)
        def _():
            e_mid = jnp.exp(mid)
            e_tot = jnp.exp(btot - mid)
            results = []
            for hh in range(HG_HEADS):
                sk = slice(hh * HG_DK, (hh + 1) * HG_DK)
                d = beta_scr[rows, sk] - mid[:, sk]
                q_mid = q_scr[rows, sk] * jnp.exp(d)
                k_mid = k_scr[rows, sk] * jnp.exp(-d)
                att = lax.dot_general(q_mid.astype(BF16), k_mid.astype(BF16), nt_dims, preferred_element_type=F32)
                att = jnp.where(causal, att, 0.0).astype(BF16)
                o_intra = jnp.dot(att, v_scr[rows, hh * HG_DV:(hh + 1) * HG_DV].astype(BF16),
                                  preferred_element_type=F32)
                results.append(finish_head(hh, rows, q_mid * e_mid[:, sk], k_mid * e_tot[:, sk], o_intra,
                                           btot[:, sk]))
            store_heads(rows, results)

        @pl.when(jnp.logical_not(safe))
        def _():
            t_idx = lax.broadcasted_iota(jnp.int32, (C, 1), 0)
            results = []
            for hh in range(HG_HEADS):
                sk = slice(hh * HG_DK, (hh + 1) * HG_DK)
                sv = slice(hh * HG_DV, (hh + 1) * HG_DV)
                b = beta_scr[rows, sk]
                q = q_scr[rows, sk]

                def key_step(s8, acc):
                    off = pl.multiple_of(s8 * 8, 8)
                    b_keys = beta_scr[pl.ds(base + off, 8), sk]
                    k_keys = k_scr[pl.ds(base + off, 8), sk]
                    v_keys = v_scr[pl.ds(base + off, 8), sv]
                    for i in range(8):
                        w = jnp.exp(jnp.minimum(b - b_keys[i:i + 1], 0.0))
                        a = jnp.sum(q * k_keys[i:i + 1] * w, axis=-1, keepdims=True)
                        a = jnp.where(t_idx >= off + i, a, 0.0)
                        acc = acc + a * v_keys[i:i + 1]
                    return acc

                o_intra = lax.fori_loop(0, C // 8, key_step, jnp.zeros((C, HG_DV), F32))
                results.append(finish_head(hh, rows, q * jnp.exp(b), k_scr[rows, sk] * jnp.exp(btot[:, sk] - b),
                                           o_intra, btot[:, sk]))
            store_heads(rows, results)

        return carry

    lax.fori_loop(0, tt // C, chunk_step, 0)
    y_ref[...] = x + _dot(o_scr[...], wout_ref[...])

    @pl.when(t == pl.num_programs(1) - 1)
    def _():
        for hh in range(HG_HEADS):
            sfin_ref[hh] = s_scr[hh].T


def _hgrn_prompt(x, g, w_in, lb_logits, norm_g, w_out, layer, j, batch, seq):
    tt = HG_ROW_TILE
    nt = seq // tt
    x3 = x.reshape(batch, seq, D_MODEL)
    row_spec = pl.BlockSpec((None, tt, D_MODEL), lambda b, t: (b, t, 0))
    y, s_fin = pl.pallas_call(
        functools.partial(_hgrn_prompt_kernel, layer=layer, tt=tt),
        grid=(batch, nt),
        in_specs=[row_spec,
                  pl.BlockSpec((None, 1, D_MODEL), lambda b, t: (layer, 0, 0)),
                  _resident((None, D_MODEL, 4 * HG_WIDTH), lambda b, t: (j, 0, 0)),
                  pl.BlockSpec((DEPTH, HG_WIDTH), lambda b, t: (0, 0)),
                  pl.BlockSpec((None, 1, HG_DV), lambda b, t: (j, 0, 0)),
                  _resident((None, HG_WIDTH, D_MODEL), lambda b, t: (j, 0, 0))],
        out_specs=(row_spec,
                   pl.BlockSpec((None, HG_HEADS, HG_DK, HG_DV), lambda b, t: (b, 0, 0, 0))),
        out_shape=(jax.ShapeDtypeStruct((batch, seq, D_MODEL), F32),
                   jax.ShapeDtypeStruct((batch, HG_HEADS, HG_DK, HG_DV), F32)),
        scratch_shapes=[pltpu.VMEM((HG_HEADS, HG_DV, HG_DK), F32),
                        pltpu.VMEM((tt, HG_WIDTH), F32), pltpu.VMEM((tt, HG_WIDTH), F32),
                        pltpu.VMEM((tt, HG_WIDTH), F32), pltpu.VMEM((tt, HG_WIDTH), F32),
                        pltpu.VMEM((tt, HG_WIDTH), BF16), pltpu.VMEM((tt, HG_WIDTH), F32),
                        pltpu.SMEM((tt // HG_CHUNK,), jnp.int32)],
        compiler_params=_params("parallel", "arbitrary"),
        name="hgrn_prompt",
    )(x3, g.reshape(DEPTH, 1, D_MODEL), w_in, lb_logits, norm_g.reshape(-1, 1, HG_DV), w_out)
    return y.reshape(batch * seq, D_MODEL), s_fin


def _hgrn_sample_kernel(proj_ref, lbl_ref, ng_ref, s_ref, snew_ref, o_ref, *, layer, tb):
    proj = proj_ref[...]
    lb = _hg_lower_bound(lbl_ref[...], layer)
    z = proj[:, HG_WIDTH:2 * HG_WIDTH]
    e = jnp.exp(-jnp.abs(z))
    r = 1.0 / (1.0 + e)
    sig = jnp.where(z >= 0.0, r, e * r)
    f = lb + (1.0 - lb) * sig
    k = (1.0 - lb) * jnp.where(z >= 0.0, e * r, r)
    q = _silu(proj[:, :HG_WIDTH])
    v = proj[:, 2 * HG_WIDTH:3 * HG_WIDTH]
    gt = _silu(proj[:, 3 * HG_WIDTH:])
    ng = ng_ref[...]
    tok, lane = _iota2((tb, tb * HG_DV))
    spread = ((lane // HG_DV) == tok).astype(BF16)

    spread3 = jnp.concatenate([spread] * 3, axis=0)

    def columns(x, exact):
        if exact:
            return jnp.dot(jnp.concatenate(_split3(x.T), axis=1), spread3, preferred_element_type=F32)
        return jnp.dot(x.T.astype(BF16), spread, preferred_element_type=F32)

    for hh in range(HG_HEADS):
        sk = slice(hh * HG_DK, (hh + 1) * HG_DK)
        sv = slice(hh * HG_DV, (hh + 1) * HG_DV)
        f_c, k_c, q_c = columns(f[:, sk], True), columns(k[:, sk], False), columns(q[:, sk], False)
        for b in range(tb):
            blk = slice(b * HG_DV, (b + 1) * HG_DV)
            s_new = f_c[:, blk] * s_ref[b, hh] + k_c[:, blk] * v[b:b + 1, sv]
            snew_ref[b, hh] = s_new
            o = jnp.sum(q_c[:, blk] * s_new, axis=0, keepdims=True)
            o_ref[b:b + 1, sv] = _rms(o, ng) * gt[b:b + 1, sv]


def _hgrn_sample(proj, lb_logits, norm_g, state, layer, j):
    nb = proj.shape[0]
    tb = SAMPLE_TOKENS_PER_STEP
    st_spec = pl.BlockSpec((tb, HG_HEADS, HG_DK, HG_DV), lambda i: (i, 0, 0, 0))
    return pl.pallas_call(
        functools.partial(_hgrn_sample_kernel, layer=layer, tb=tb),
        grid=(nb // tb,),
        in_specs=[pl.BlockSpec((tb, 4 * HG_WIDTH), lambda i: (i, 0)),
                  pl.BlockSpec((DEPTH, HG_WIDTH), lambda i: (0, 0)),
                  pl.BlockSpec((None, 1, HG_DV), lambda i: (j, 0, 0)),
                  st_spec],
        out_specs=(st_spec, pl.BlockSpec((tb, HG_WIDTH), lambda i: (i, 0))),
        out_shape=(jax.ShapeDtypeStruct(state.shape, F32), jax.ShapeDtypeStruct((nb, HG_WIDTH), F32)),
        compiler_params=_params("parallel"),
        name="hgrn_sample",
    )(proj, lb_logits, norm_g.reshape(-1, 1, HG_DV), state)


def _xattn_prompt_kernel(x_ref, g_ref, wq_ref, k_ref, v_ref, wo_ref, y_ref):
    x = x_ref[...]
    h = _rms(x, g_ref[...]).astype(BF16)
    q = _dot(h, wq_ref[...]) * (1.0 / math.sqrt(MEM_HD))
    q = q.astype(BF16)
    outs = []
    for hh in range(MEM_HEADS):
        sl = slice(hh * MEM_HD, (hh + 1) * MEM_HD)
        s = lax.dot_general(q[:, sl], k_ref[:, sl], (((1,), (1,)), ((), ())), preferred_element_type=F32)
        p = jnp.exp(s - jnp.max(s, axis=-1, keepdims=True))
        den = jnp.sum(p, axis=-1, keepdims=True)
        o = jnp.dot(p.astype(BF16), v_ref[:, sl], preferred_element_type=F32)
        outs.append((o / den).astype(BF16))
    o = jnp.concatenate(outs, axis=-1)
    y_ref[...] = x + _dot(o, wo_ref[...])


def _xattn_prompt(x, g, w_q, mem_k, mem_v, w_o, layer, batch, seq):
    tt = XA_ROW_TILE
    row_spec = pl.BlockSpec((None, tt, D_MODEL), lambda b, t: (b, t, 0))
    kv_spec = pl.BlockSpec((None, N_MEM, D_MODEL), lambda b, t: (layer, b, 0))
    w_spec = _resident((None, D_MODEL, D_MODEL), lambda b, t: (layer, 0, 0))
    y = pl.pallas_call(
        _xattn_prompt_kernel,
        grid=(batch, seq // tt),
        in_specs=[row_spec, pl.BlockSpec((None, 1, D_MODEL), lambda b, t: (layer, 0, 0)),
                  w_spec, kv_spec, kv_spec, w_spec],
        out_specs=row_spec,
        out_shape=jax.ShapeDtypeStruct((batch, seq, D_MODEL), F32),
        compiler_params=_params("parallel", "parallel"),
        name="xattn_prompt",
    )(x.reshape(batch, seq, D_MODEL), g.reshape(DEPTH, 1, D_MODEL), w_q, mem_k, mem_v, w_o)
    return y.reshape(batch * seq, D_MODEL)


def _xattn_sample_kernel(q_ref, k_ref, v_ref, o_ref, *, tb):
    scale = 1.0 / math.sqrt(MEM_HD)
    rows = N_MEM * MEM_HEADS
    head, lane = _iota2((MEM_HEADS, rows))
    own = (lane & (MEM_HEADS - 1)) == head
    for b in range(tb):
        q = (q_ref[b] * scale).astype(BF16)
        k = k_ref[b].reshape(rows, MEM_HD).astype(BF16)
        v = v_ref[b].reshape(rows, MEM_HD).astype(BF16)
        s = lax.dot_general(q, k, (((1,), (1,)), ((), ())), preferred_element_type=F32)
        s = jnp.where(own, s, -jnp.inf)
        p = jnp.exp(s - jnp.max(s, axis=-1, keepdims=True))
        den = jnp.sum(p, axis=-1, keepdims=True)
        o_ref[b] = jnp.dot(p.astype(BF16), v, preferred_element_type=F32) / den


def _xattn_sample(q, cache_k, cache_v, layer):
    nb = q.shape[0]
    tb = XA_SAMPLE_TOKENS_PER_STEP
    kv_spec = pl.BlockSpec((None, tb, N_MEM, MEM_HEADS, MEM_HD), lambda i: (layer, i, 0, 0, 0))
    q_spec = pl.BlockSpec((tb, MEM_HEADS, MEM_HD), lambda i: (i, 0, 0))
    return pl.pallas_call(
        functools.partial(_xattn_sample_kernel, tb=tb),
        grid=(nb // tb,),
        in_specs=[q_spec, kv_spec, kv_spec],
        out_specs=q_spec,
        out_shape=jax.ShapeDtypeStruct((nb, MEM_HEADS, MEM_HD), F32),
        compiler_params=_params("parallel"),
        name="xattn_sample",
    )(q.reshape(nb, MEM_HEADS, MEM_HD), cache_k, cache_v).reshape(nb, D_MODEL)


def _ffn1(x, w, i, ffn, emit):
    if emit:
        return ffn(x, w["ffn1_norm"], w["ffn1_w_in"], w["ffn1_w_out"], i, post="emit", g2=w["mix_norm"][i],
                   u_dtype=F32)
    return ffn(x, w["ffn1_norm"], w["ffn1_w_in"], w["ffn1_w_out"], i)


def _ffn2(x, w, i, ffn):
    last = i == DEPTH - 1
    return ffn(x, w["ffn2_norm"], w["ffn2_w_in"], w["ffn2_w_out"], i,
               post="replace" if last else "none", g2=w["final_norm"] if last else None)


def _prompt_trunk(x, batch, seq, mem_k, mem_v, w, ffn):
    zero = jnp.zeros((batch, S5_GROUPS, S5_STATE), F32)
    new_re, new_im, new_hg = [], [], []
    for i in range(DEPTH):
        j = i // 2
        if i % 2 == 0:
            x, u = _ffn1(x, w, i, ffn, True)
            x, hr, hi = _s5_mixer(x, u, zero, zero, w["s5"], w["s5_w_glu"], j, batch, seq)
            new_re.append(hr)
            new_im.append(hi)
        else:
            x = _ffn1(x, w, i, ffn, False)
            x, sn = _hgrn_prompt(x, w["mix_norm"], w["hg_w_in"], w["hg_lb_logits"], w["hg_norm"],
                                 w["hg_w_out"], i, j, batch, seq)
            new_hg.append(sn)
        x = _xattn_prompt(x, w["xattn_norm"], w["xattn_w_q"], mem_k, mem_v, w["xattn_w_o"], i, batch, seq)
        x = _ffn2(x, w, i, ffn)
    return x, jnp.stack(new_re), jnp.stack(new_im), jnp.stack(new_hg)


def _sample_trunk(x, batch, s5_re, s5_im, hg_state, w):
    new_re, new_im, new_hg = [], [], []
    for i in range(DEPTH):
        j = i // 2
        if i % 2 == 0:
            x, u = _ffn1(x, w, i, _ffn, True)
            x, hr, hi = _s5_mixer(x, u, s5_re[j], s5_im[j], w["s5"], w["s5_w_glu"], j, batch, 1)
            new_re.append(hr)
            new_im.append(hi)
        else:
            x = _ffn1(x, w, i, _ffn, False)
            proj = _proj(x, w["hg_w_in"], j, g=w["mix_norm"], g_layer=i)
            sn, o = _hgrn_sample(proj, w["hg_lb_logits"], w["hg_norm"], hg_state[j], i, j)
            x = _proj(o, w["hg_w_out"], j, res=x)
            new_hg.append(sn)
        q = _proj(x, w["xattn_w_q"], i, g=w["xattn_norm"], g_layer=i)
        o = yield q, i
        x = _proj(o, w["xattn_w_o"], i, res=x)
        x = _ffn2(x, w, i, _ffn)
    return x, jnp.stack(new_re), jnp.stack(new_im), jnp.stack(new_hg)


def kernel(x_prompt, x_sample, mem_prompt, state_s5_re, state_s5_im, state_hgrn, cache_mem_k, cache_mem_v, ffn1_norm, ffn1_w_in, ffn1_w_out, mix_norm, xattn_norm, mem_norm, xattn_w_q, xattn_w_kv, xattn_w_o, ffn2_norm, ffn2_w_in, ffn2_w_out, s5_a_re, s5_a_im, s5_log_dt, s5_b_re, s5_b_im, s5_c_re, s5_c_im, s5_d, s5_w_glu, hg_w_in, hg_lb_logits, hg_norm, hg_w_out, final_norm):
    bp, seq, _ = x_prompt.shape
    bs = x_sample.shape[0]
    w = dict(ffn1_norm=ffn1_norm, ffn1_w_in=ffn1_w_in, ffn1_w_out=ffn1_w_out, mix_norm=mix_norm,
             xattn_norm=xattn_norm, xattn_w_q=xattn_w_q, xattn_w_o=xattn_w_o, ffn2_norm=ffn2_norm,
             ffn2_w_in=ffn2_w_in, ffn2_w_out=ffn2_w_out,
             s5=(s5_a_re, s5_a_im, s5_log_dt, s5_b_re, s5_b_im, s5_c_re, s5_c_im, s5_d),
             s5_w_glu=s5_w_glu, hg_w_in=hg_w_in, hg_lb_logits=hg_lb_logits, hg_norm=hg_norm,
             hg_w_out=hg_w_out, final_norm=final_norm)

    mem_k, mem_v, mem_k_bf, mem_v_bf = _mem_kv(mem_prompt, mem_norm, xattn_w_kv)

    sample = _sample_trunk(x_sample.reshape(bs, D_MODEL), bs, state_s5_re, state_s5_im, state_hgrn, w)
    pending = [next(sample)]
    sample_out = []

    def resume(o):
        try:
            pending[0] = sample.send(o)
        except StopIteration as done:
            pending[0] = None
            sample_out.append(done.value)

    def ffn_with_rider(x, *args, **kwargs):
        steps = x.shape[0] // min(FFN_ROW_TILE_WITH_RIDER, x.shape[0])
        if pending[0] is None or bs % steps:
            return _ffn(x, *args, **kwargs)
        q, layer = pending[0]
        *outs, o = _ffn(x, *args, rider=(q, cache_mem_k, cache_mem_v, layer), **kwargs)
        resume(o)
        return outs[0] if len(outs) == 1 else tuple(outs)

    y_p, re_p, im_p, hg_p = _prompt_trunk(x_prompt.reshape(bp * seq, D_MODEL), bp, seq, mem_k_bf, mem_v_bf, w,
                                          ffn_with_rider)
    while pending[0] is not None:
        q, layer = pending[0]
        resume(_xattn_sample(q, cache_mem_k, cache_mem_v, layer))
    y_s, re_s, im_s, hg_s = sample_out[0]
    return (y_p.reshape(bp, seq, D_MODEL), y_s.reshape(bs, 1, D_MODEL), re_p, im_p, re_s, im_s, hg_p, hg_s,
            mem_k, mem_v)
```

```python
import functools
import math

import jax
import jax.numpy as jnp
from jax import lax
from jax.experimental import pallas as pl
from jax.experimental.pallas import tpu as pltpu

F32 = jnp.float32
BF16 = jnp.bfloat16

D_MODEL = 1024
DEPTH = 2
S5_GROUP = 16
S5_GROUPS = D_MODEL // S5_GROUP
S5_STATE = 64
S5_CHUNK = 16
S5_BLOCK_GROUPS = 8
S5_TIME_SLICES = 2
HG_DK = 128
HG_HEADS = D_MODEL // HG_DK
HG_DV = D_MODEL // HG_HEADS
HG_WIDTH = HG_HEADS * HG_DK
HG_CHUNK = 128
HG_FACTORED_MAX_DECAY = 60.0
N_MEM = 256
MEM_HEADS = 4
MEM_HD = D_MODEL // MEM_HEADS
FFN_DIM = 2816
EPS = 1e-6

V7X_VMEM_LIMIT_BYTES = 56 * 1024 * 1024

ROW_TILE = 512
FFN_CHUNK = 1408
FFN_ROW_TILE_WITH_RIDER = 256
HG_ROW_TILE = 512
XA_ROW_TILE = 1024
PROJ_ROW_TILE = 1024
S5_GROUPS_PER_STEP = 16
SAMPLE_TOKENS_PER_STEP = 8
XA_SAMPLE_TOKENS_PER_STEP = 4


def _params(*semantics):
    return pltpu.CompilerParams(dimension_semantics=semantics,
                                vmem_limit_bytes=V7X_VMEM_LIMIT_BYTES)


def _resident(shape, index_map):
    return pl.BlockSpec(shape, index_map, pipeline_mode=pl.Buffered(1))


def _rms(x, g):
    ms = jnp.mean(x * x, axis=-1, keepdims=True)
    return x * lax.rsqrt(ms + EPS) * g


def _sigmoid(x):
    return 1.0 / (1.0 + jnp.exp(-x))


def _silu(x):
    return x * _sigmoid(x)


def _gelu_tanh(x):
    c = math.sqrt(2.0 / math.pi)
    return 0.5 * x * (1.0 + jnp.tanh(c * (x + 0.044715 * (x * x * x))))


def _dot(a, w):
    return jnp.dot(a, w.astype(BF16), preferred_element_type=F32)


def _split3(x):
    hi = x.astype(BF16)
    r1 = x - hi.astype(F32)
    mid = r1.astype(BF16)
    lo = (r1 - mid.astype(F32)).astype(BF16)
    return hi, mid, lo


def _ffn_kernel(x_ref, g_ref, win_ref, wout_ref, *rest, post, rider_tokens):
    rest = list(rest)
    g2_ref = rest.pop(0) if post != "none" else None
    if rider_tokens:
        q_ref, k_ref, v_ref = rest[:3]
        rest = rest[3:]
        _xattn_sample_kernel(q_ref, k_ref, v_ref, rest.pop(), tb=rider_tokens)
    x = x_ref[...]
    h = _rms(x, g_ref[...]).astype(BF16)
    acc = jnp.zeros_like(x)
    for c in range(FFN_DIM // FFN_CHUNK):
        lo = c * FFN_CHUNK
        gate = _dot(h, win_ref[:, lo:lo + FFN_CHUNK])
        up = _dot(h, win_ref[:, FFN_DIM + lo:FFN_DIM + lo + FFN_CHUNK])
        act = (_silu(gate) * up).astype(BF16)
        acc = acc + _dot(act, wout_ref[lo:lo + FFN_CHUNK, :])
    y = x + 0.5 * acc
    y_ref = rest[0]
    if post == "replace":
        y_ref[...] = _rms(y, g2_ref[...])
    else:
        y_ref[...] = y
        if post == "emit":
            rest[1][...] = _rms(y, g2_ref[...]).astype(rest[1].dtype)


def _ffn(x, g, w_in, w_out, layer, post="none", g2=None, u_dtype=BF16, rider=None):
    rows = x.shape[0]
    tm = min(ROW_TILE if rider is None else FFN_ROW_TILE_WITH_RIDER, rows)
    steps = rows // tm
    row_spec = pl.BlockSpec((tm, D_MODEL), lambda i: (i, 0))
    vec_spec = pl.BlockSpec((1, D_MODEL), lambda i: (0, 0))
    in_specs = [row_spec,
                pl.BlockSpec((None, 1, D_MODEL), lambda i: (layer, 0, 0)),
                _resident((None, D_MODEL, 2 * FFN_DIM), lambda i: (layer, 0, 0)),
                _resident((None, FFN_DIM, D_MODEL), lambda i: (layer, 0, 0))]
    args = [x, g.reshape(DEPTH, 1, D_MODEL), w_in, w_out]
    out_shape = [jax.ShapeDtypeStruct((rows, D_MODEL), F32)]
    out_specs = [row_spec]
    if post != "none":
        in_specs.append(vec_spec)
        args.append(g2.reshape(1, D_MODEL))
    if post == "emit":
        out_shape.append(jax.ShapeDtypeStruct((rows, D_MODEL), u_dtype))
        out_specs.append(row_spec)
    tb = 0
    if rider is not None:
        q, cache_k, cache_v, r_layer = rider
        nb = q.shape[0]
        assert nb % steps == 0
        tb = nb // steps
        kv_spec = pl.BlockSpec((None, tb, N_MEM, MEM_HEADS, MEM_HD), lambda i: (r_layer, i, 0, 0, 0))
        q_spec = pl.BlockSpec((tb, MEM_HEADS, MEM_HD), lambda i: (i, 0, 0))
        in_specs += [q_spec, kv_spec, kv_spec]
        args += [q.reshape(nb, MEM_HEADS, MEM_HD), cache_k, cache_v]
        out_shape.append(jax.ShapeDtypeStruct((nb, MEM_HEADS, MEM_HD), F32))
        out_specs.append(q_spec)
    outs = list(pl.pallas_call(
        functools.partial(_ffn_kernel, post=post, rider_tokens=tb),
        grid=(steps,),
        in_specs=in_specs, out_specs=tuple(out_specs), out_shape=tuple(out_shape),
        compiler_params=_params("parallel"),
        name="ffn",
    )(*args))
    if rider is not None:
        outs[-1] = outs[-1].reshape(-1, D_MODEL)
    return outs[0] if len(outs) == 1 else tuple(outs)


def _proj_kernel(*refs, norm, glu, residual):
    refs = list(refs)
    x_ref = refs.pop(0)
    g_ref = refs.pop(0) if norm else None
    w_ref = refs.pop(0)
    res_ref = refs.pop(0) if residual else None
    (o_ref,) = refs
    x = x_ref[...]
    if norm:
        x = _rms(x, g_ref[...])
    y = _dot(x.astype(BF16), w_ref[...])
    if glu:
        half = y.shape[-1] // 2
        y = y[:, :half] * _sigmoid(y[:, half:])
    if residual:
        y = y + res_ref[...]
    o_ref[...] = y


def _proj(x, w, layer, g=None, g_layer=0, glu=False, res=None):
    rows, kdim = x.shape
    ndim = w.shape[-1]
    nout = ndim // 2 if glu else ndim
    tm = min(PROJ_ROW_TILE, rows)
    in_specs = [pl.BlockSpec((tm, kdim), lambda i: (i, 0))]
    args = [x]
    if g is not None:
        in_specs.append(pl.BlockSpec((None, 1, kdim), lambda i: (g_layer, 0, 0)))
        args.append(g.reshape(g.shape[0], 1, kdim))
    in_specs.append(_resident((None, kdim, ndim), lambda i: (layer, 0, 0)))
    args.append(w)
    if res is not None:
        in_specs.append(pl.BlockSpec((tm, nout), lambda i: (i, 0)))
        args.append(res)
    return pl.pallas_call(
        functools.partial(_proj_kernel, norm=g is not None, glu=glu, residual=res is not None),
        grid=(rows // tm,),
        in_specs=in_specs,
        out_specs=pl.BlockSpec((tm, nout), lambda i: (i, 0)),
        out_shape=jax.ShapeDtypeStruct((rows, nout), F32),
        compiler_params=_params("parallel"),
        name="proj",
    )(*args)


def _memkv_kernel(x_ref, g_ref, w_ref, k_ref, v_ref, kb_ref, vb_ref):
    h = _rms(x_ref[...], g_ref[...]).astype(BF16)
    y = _dot(h, w_ref[...])
    k, v = y[:, :D_MODEL], y[:, D_MODEL:]
    k_ref[...] = k.reshape(k_ref.shape)
    v_ref[...] = v.reshape(v_ref.shape)
    kb_ref[...] = k.astype(BF16)
    vb_ref[...] = v.astype(BF16)


def _mem_kv(mem, g, w_kv):
    batch = mem.shape[0]
    rows = batch * N_MEM
    nb = max(1, min(ROW_TILE, rows) // N_MEM)
    tm = nb * N_MEM
    out5 = jax.ShapeDtypeStruct((DEPTH, batch, N_MEM, MEM_HEADS, MEM_HD), F32)
    out2 = jax.ShapeDtypeStruct((DEPTH, rows, D_MODEL), BF16)
    spec5 = pl.BlockSpec((None, nb, N_MEM, MEM_HEADS, MEM_HD), lambda l, i: (l, i, 0, 0, 0))
    spec2 = pl.BlockSpec((None, tm, D_MODEL), lambda l, i: (l, i, 0))
    return pl.pallas_call(
        _memkv_kernel,
        grid=(DEPTH, rows // tm),
        in_specs=[pl.BlockSpec((tm, D_MODEL), lambda l, i: (i, 0)),
                  pl.BlockSpec((None, 1, D_MODEL), lambda l, i: (l, 0, 0)),
                  pl.BlockSpec((None, D_MODEL, 2 * D_MODEL), lambda l, i: (l, 0, 0))],
        out_specs=(spec5, spec5, spec2, spec2), out_shape=(out5, out5, out2, out2),
        compiler_params=_params("parallel", "parallel"),
        name="mem_kv",
    )(mem.reshape(rows, D_MODEL), g.reshape(DEPTH, 1, D_MODEL), w_kv)


def _s5_pieces(a_re, a_im, log_dt, b_re, b_im, c_re, c_im, d, steps):
    G, P, GC = S5_GROUPS, S5_STATE, S5_GROUP
    L = steps
    dt = jnp.exp(log_dt)[:, None]
    xr, xi = a_re * dt, a_im * dt
    j = jnp.arange(L + 1, dtype=F32)[:, None, None]
    mag = jnp.exp(xr[None] * j)
    pw_re, pw_im = mag * jnp.cos(xi[None] * j), mag * jnp.sin(xi[None] * j)
    nr, ni = pw_re[1] - 1.0, pw_im[1]
    den = a_re * a_re + a_im * a_im
    fr, fi = ((nr * a_re + ni * a_im) / den)[:, None, :], ((ni * a_re - nr * a_im) / den)[:, None, :]
    bt_re, bt_im = b_re.transpose(0, 2, 1), b_im.transpose(0, 2, 1)
    bb_re = fr * bt_re - fi * bt_im
    bb_im = fr * bt_im + fi * bt_re

    def times_c(p_re, p_im):
        p_re, p_im = p_re[:, :, None, :], p_im[:, :, None, :]
        return p_re * c_re[None] - p_im * c_im[None], p_re * c_im[None] + p_im * c_re[None]

    w_re, w_im = times_c(pw_re[:L], pw_im[:L])

    def over_state(b, w_):
        w_ = w_.transpose(1, 0, 2, 3).reshape(G, L * GC, P)
        return jnp.einsum("gip,gnp->gin", b, w_, precision=lax.Precision.HIGHEST)

    kern = (over_state(bb_re, w_re) - over_state(bb_im, w_im)).reshape(G, GC, L, GC).transpose(2, 0, 1, 3)
    kern = kern.at[0].add(d[:, :, None] * jnp.eye(GC, dtype=F32)[None])
    rp_re, rp_im = pw_re[:L][::-1][:, :, None, :], pw_im[:L][::-1][:, :, None, :]
    m_re = rp_re * bb_re[None] - rp_im * bb_im[None]
    m_im = rp_re * bb_im[None] + rp_im * bb_re[None]
    q_re, q_im = times_c(pw_re[1:], pw_im[1:])
    return kern, m_re, m_im, q_re, -q_im, pw_re[L], pw_im[L]


def _s5_group_mats(pieces):
    kern, m_re, m_im, p_re, p_im, l_re, l_im = pieces
    return (kern[0], m_re[0], m_im[0], p_re[0].transpose(0, 2, 1), p_im[0].transpose(0, 2, 1),
            l_re[:, None, :], l_im[:, None, :])


def _s5_block_mats(pieces):
    kern, m_re, m_im, p_re, p_im, l_re, l_im = pieces
    L = kern.shape[0]
    GB = S5_BLOCK_GROUPS
    NB = S5_GROUPS // GB
    GC, P = S5_GROUP, S5_STATE
    taps = kern.astype(BF16).reshape(L, NB, GB * GC, GC).transpose(1, 0, 2, 3)

    def rows(m):
        return m.astype(BF16).reshape(L, NB, GB * GC, P).transpose(1, 0, 2, 3).reshape(NB, L * GB * GC, P)

    mm = jnp.stack([rows(m_re), rows(m_im)], axis=1)
    pp = jnp.stack([rows(p_re), rows(p_im)], axis=1)
    lam = jnp.concatenate([l_re.reshape(NB, 1, GB * P), l_im.reshape(NB, 1, GB * P)], axis=2)
    return taps, mm, pp, lam


def _s5_kernel(u_ref, t_ref, mre_ref, mim_ref, pre_ref, pim_ref, lre_ref, lim_ref, h0re_ref, h0im_ref,
               y_ref, hre_ref, him_ref, inj_re, inj_im, hs_re, hs_im, *, n_chunks, rb, gps, precision):
    def mm(a, b):
        return jnp.dot(a, b, preferred_element_type=F32, precision=precision)

    for g in range(gps):
        u = u_ref[g]
        inj_re[g] = mm(u, mre_ref[g])
        inj_im[g] = mm(u, mim_ref[g])

    lam_re = [jnp.broadcast_to(lre_ref[g], (rb, S5_STATE)) for g in range(gps)]
    lam_im = [jnp.broadcast_to(lim_ref[g], (rb, S5_STATE)) for g in range(gps)]

    def step(k, carry):
        rows = pl.ds(pl.multiple_of(k * rb, rb), rb)
        nxt = []
        for g in range(gps):
            hr, hi = carry[2 * g], carry[2 * g + 1]
            hs_re[g, rows, :] = hr
            hs_im[g, rows, :] = hi
            nxt.append(lam_re[g] * hr - lam_im[g] * hi + inj_re[g, rows, :])
            nxt.append(lam_re[g] * hi + lam_im[g] * hr + inj_im[g, rows, :])
        return tuple(nxt)

    init = []
    for g in range(gps):
        init += [h0re_ref[g], h0im_ref[g]]
    fin = lax.fori_loop(0, n_chunks, step, tuple(init))

    for g in range(gps):
        hre_ref[g] = fin[2 * g]
        him_ref[g] = fin[2 * g + 1]
        dt = u_ref.dtype
        y = (mm(u_ref[g], t_ref[g]) + mm(hs_re[g].astype(dt), pre_ref[g])
             + mm(hs_im[g].astype(dt), pim_ref[g]))
        y_ref[g] = _gelu_tanh(y).astype(y_ref.dtype)


def _s5_core(u, mats, h0_re, h0_im, n_chunks, rb, precision, y_dtype):
    G, R, W = u.shape
    P = S5_STATE
    gps = S5_GROUPS_PER_STEP
    tmat, m_re, m_im, p_re, p_im, l_re, l_im = mats

    def spec(a, b):
        return pl.BlockSpec((gps, a, b), lambda i: (i, 0, 0))

    st = jax.ShapeDtypeStruct((G, rb, P), F32)
    scr = lambda: pltpu.VMEM((gps, R, P), F32)
    return pl.pallas_call(
        functools.partial(_s5_kernel, n_chunks=n_chunks, rb=rb, gps=gps, precision=precision),
        grid=(G // gps,),
        in_specs=[spec(R, W), spec(W, W), spec(W, P), spec(W, P), spec(P, W), spec(P, W),
                  spec(1, P), spec(1, P), spec(rb, P), spec(rb, P)],
        out_specs=(spec(R, W), spec(rb, P), spec(rb, P)),
        out_shape=(jax.ShapeDtypeStruct((G, R, W), y_dtype), st, st),
        scratch_shapes=[scr(), scr(), scr(), scr()],
        compiler_params=_params("parallel"),
        name="s5_core",
    )(u, tmat, m_re, m_im, p_re, p_im, l_re, l_im, h0_re, h0_im)


def _iota2(shape):
    return lax.broadcasted_iota(jnp.int32, shape, 0), lax.broadcasted_iota(jnp.int32, shape, 1)


def _s5_expand(taps_ref, mc_ref, pc_ref, w2_ref, m_ref, p_ref):
    L, GC, P = S5_CHUNK, S5_GROUP, S5_STATE
    GB = S5_BLOCK_GROUPS
    W, HS = GB * GC, GB * P
    gc_bits, p_bits = GC.bit_length() - 1, P.bit_length() - 1

    r, c = _iota2((P, HS))
    rep_m = ((c & (P - 1)) == r).astype(BF16)
    r, c = _iota2((L * W, HS))
    mask_m = ((r >> gc_bits) & (GB - 1)) == (c >> p_bits)
    for half in range(2):
        for src, dst in ((mc_ref, m_ref), (pc_ref, p_ref)):
            dst[:, half * HS:(half + 1) * HS] = jnp.where(
                mask_m, jnp.dot(src[half], rep_m, preferred_element_type=F32), 0.0).astype(BF16)

    r, c = _iota2((GC, W))
    rep_k = ((c & (GC - 1)) == r).astype(BF16)
    r, c = _iota2((W, W))
    mask_k = (r >> gc_bits) == (c >> gc_bits)
    zero = jnp.zeros((W, W), BF16)
    w2_ref[L * W:, :W] = zero
    w2_ref[:W, W:] = zero
    for i in range(L):
        kb = jnp.where(mask_k, jnp.dot(taps_ref[L - 1 - i], rep_k, preferred_element_type=F32), 0.0).astype(BF16)
        w2_ref[i * W:(i + 1) * W, :W] = kb
        w2_ref[(i + 1) * W:(i + 2) * W, W:] = kb


def _s5_seq_kernel(u_ref, taps_ref, mc_ref, pc_ref, lam_ref, h0_ref, y_ref, hfin_ref,
                   w2_ref, m_ref, p_ref, lhs_scr, inj_scr, hs_scr, h_scr, *, batch, nck):
    L, W = S5_CHUNK, S5_BLOCK_GROUPS * S5_GROUP
    NT = S5_BLOCK_GROUPS * S5_STATE // W

    @pl.when(pl.program_id(1) == 0)
    def _():
        h_scr[...] = h0_ref[...]
        _s5_expand(taps_ref, mc_ref, pc_ref, w2_ref, m_ref, p_ref)

    for b in range(batch):
        for l in range(L):
            lhs_scr[b * nck:(b + 1) * nck, l * W:(l + 1) * W] = u_ref[b, pl.ds(l, nck, stride=L), :].astype(BF16)

    def swap_major(x, a, b):
        return jnp.swapaxes(x.reshape(a, b, W), 0, 1).reshape(a * b, W)

    inj = jnp.dot(lhs_scr[...], m_ref[...], preferred_element_type=F32)
    for t in range(2 * NT):
        inj_scr[t] = swap_major(inj[:, t * W:(t + 1) * W], batch, nck)

    lam = [jnp.broadcast_to(lam_ref[:, t * W:(t + 1) * W], (batch, W)) for t in range(2 * NT)]

    def step(k, h):
        rows = pl.ds(pl.multiple_of(k * batch, batch), batch)
        nxt_re, nxt_im = [], []
        for t in range(NT):
            hr, hi = h[t], h[NT + t]
            hs_scr[t, rows, :] = hr
            hs_scr[NT + t, rows, :] = hi
            nxt_re.append(lam[t] * hr - lam[NT + t] * hi + inj_scr[t, rows, :])
            nxt_im.append(lam[t] * hi + lam[NT + t] * hr + inj_scr[NT + t, rows, :])
        return tuple(nxt_re + nxt_im)

    h = lax.fori_loop(0, nck, step, tuple(h_scr[:, t * W:(t + 1) * W] for t in range(2 * NT)))
    for t in range(2 * NT):
        h_scr[:, t * W:(t + 1) * W] = h[t]
    hfin_ref[...] = h_scr[...]

    hs = jnp.concatenate([swap_major(hs_scr[t], nck, batch) for t in range(2 * NT)], axis=-1).astype(BF16)
    for pr in range(L // 2):
        kk = (2 * pr + 2) * W
        y = (jnp.dot(lhs_scr[:, :kk], w2_ref[(L - 1 - 2 * pr) * W:, :], preferred_element_type=F32)
             + lax.dot_general(hs, p_ref[2 * pr * W:(2 * pr + 2) * W, :], (((1,), (1,)), ((), ())),
                               preferred_element_type=F32))
        y = _gelu_tanh(y)
        for s in range(2):
            for b in range(batch):
                y_ref[b, pl.ds(2 * pr + s, nck, stride=L), :] = y[b * nck:(b + 1) * nck, s * W:(s + 1) * W]


def _s5_seq(u, mats, h0, batch, seq):
    taps, mm, pp, lam = mats
    L, W = S5_CHUNK, S5_BLOCK_GROUPS * S5_GROUP
    NB = S5_GROUPS // S5_BLOCK_GROUPS
    HS = S5_BLOCK_GROUPS * S5_STATE
    SW = 2 * HS
    ts = seq // S5_TIME_SLICES
    nck = ts // L
    rows = batch * nck
    u_spec = pl.BlockSpec((batch, ts, W), lambda i, t: (0, t, i))
    h_spec = pl.BlockSpec((None, batch, SW), lambda i, t: (i, 0, 0))

    def w_spec(*dims):
        return pl.BlockSpec((None,) + dims, lambda i, t: (i,) + (0,) * len(dims))

    return pl.pallas_call(
        functools.partial(_s5_seq_kernel, batch=batch, nck=nck),
        grid=(NB, S5_TIME_SLICES),
        in_specs=[u_spec, w_spec(L, W, S5_GROUP), w_spec(2, L * W, S5_STATE), w_spec(2, L * W, S5_STATE),
                  w_spec(1, SW), h_spec],
        out_specs=(u_spec, h_spec),
        out_shape=(jax.ShapeDtypeStruct((batch, seq, D_MODEL), F32), jax.ShapeDtypeStruct(h0.shape, F32)),
        scratch_shapes=[pltpu.VMEM(((L + 1) * W, 2 * W), BF16), pltpu.VMEM((L * W, SW), BF16),
                        pltpu.VMEM((L * W, SW), BF16),
                        pltpu.VMEM((rows, L * W), BF16), pltpu.VMEM((SW // W, rows, W), F32),
                        pltpu.VMEM((SW // W, rows, W), F32), pltpu.VMEM((batch, SW), F32)],
        compiler_params=_params("parallel", "arbitrary"),
        name="s5_seq",
    )(u, taps, mm, pp, lam, h0)


def _s5_mixer(x, u, h0_re, h0_im, s5p, w_glu, j, batch, seq):
    G, GC, P = S5_GROUPS, S5_GROUP, S5_STATE
    if seq > 1:
        NB, HS = G // S5_BLOCK_GROUPS, S5_BLOCK_GROUPS * P
        mats = _s5_block_mats(_s5_pieces(*[p[j] for p in s5p], S5_CHUNK))
        to_blocks = lambda h: h.reshape(batch, NB, HS).transpose(1, 0, 2)
        h0 = jnp.concatenate([to_blocks(h0_re), to_blocks(h0_im)], axis=-1)
        y, hfin = _s5_seq(u.reshape(batch, seq, D_MODEL), mats, h0, batch, seq)
        y = y.reshape(batch * seq, D_MODEL)
        from_blocks = lambda h: h.transpose(1, 0, 2).reshape(batch, G, P)
        hre, him = from_blocks(hfin[..., :HS]), from_blocks(hfin[..., HS:])
    else:
        mats = _s5_group_mats(_s5_pieces(*[p[j] for p in s5p], 1))
        mats = tuple(m.astype(BF16) for m in mats[:5]) + mats[5:]
        ug = u.reshape(batch, G, GC).transpose(1, 0, 2).astype(BF16)
        yg, hre, him = _s5_core(ug, mats, h0_re.transpose(1, 0, 2), h0_im.transpose(1, 0, 2),
                                1, batch, None, F32)
        y = yg.transpose(1, 0, 2).reshape(batch, D_MODEL)
        hre, him = hre.transpose(1, 0, 2), him.transpose(1, 0, 2)
    x = _proj(y, w_glu, j, glu=True, res=x)
    return x, hre, him


def _hg_lower_bound(logits, layer):
    m = jnp.max(logits, axis=0, keepdims=True)
    e = jnp.exp(logits - m)
    sm = e / jnp.sum(e, axis=0, keepdims=True)
    return jnp.sum(sm[:layer + 1], axis=0, keepdims=True) - sm[0:1]


def _hg_gates(z, lb):
    e = jnp.exp(-jnp.abs(z))
    r = 1.0 / (1.0 + e)
    er = e * r
    pos = z >= 0.0
    f = lb + (1.0 - lb) * jnp.where(pos, r, er)
    logf = jnp.where(f > 0.0, jnp.log(f), z)
    k = (1.0 - lb) * jnp.where(pos, er, r)
    return logf, k


def _hgrn_prompt_kernel(x_ref, g_ref, win_ref, lbl_ref, ng_ref, wout_ref, y_ref, sfin_ref,
                        s_scr, q_scr, k_scr, v_scr, gt_scr, o_scr, beta_scr, safe_scr, *, layer, tt):
    t = pl.program_id(1)
    C = HG_CHUNK

    @pl.when(t == 0)
    def _():
        s_scr[...] = jnp.zeros_like(s_scr)

    x = x_ref[...]
    h = _rms(x, g_ref[...]).astype(BF16)
    def proj(i):
        return _dot(h, win_ref[:, i * HG_WIDTH:(i + 1) * HG_WIDTH])

    lb = _hg_lower_bound(lbl_ref[...], layer)
    logf, kk = _hg_gates(proj(1), lb)
    k_scr[...] = kk
    q_scr[...] = _silu(proj(0))
    gt_scr[...] = _silu(proj(3))
    v_scr[...] = proj(2)

    row, col = _iota2((C, C))
    causal = row >= col
    tri = causal.astype(BF16)
    ng = ng_ref[...]
    nt_dims = (((1,), (1,)), ((), ()))
    tn_dims = (((0,), (0,)), ((), ()))

    def finish_head(hh, rows, q_dec, k_dec, o_intra, btot_h):
        sv = slice(hh * HG_DV, (hh + 1) * HG_DV)
        st = s_scr[hh]
        o = o_intra + lax.dot_general(q_dec.astype(BF16), st.astype(BF16), nt_dims, preferred_element_type=F32)
        kv_t = lax.dot_general(v_scr[rows, sv].astype(BF16), k_dec.astype(BF16), tn_dims,
                               preferred_element_type=F32)
        return (_rms(o, ng) * gt_scr[rows, sv]).astype(BF16), st * jnp.exp(btot_h) + kv_t

    def store_heads(rows, results):
        o_scr[rows, :] = jnp.concatenate([o for o, _ in results], axis=-1)
        for hh, (_, s_new) in enumerate(results):
            s_scr[hh] = s_new

    for ci in range(tt // C):
        parts = _split3(logf[ci * C:(ci + 1) * C])
        beta = sum(jnp.dot(tri, p, preferred_element_type=F32) for p in parts)
        beta_scr[ci * C:(ci + 1) * C, :] = beta
        mid = beta[C // 2 - 1:C // 2, :]
        spread = jnp.maximum(jnp.max(-mid), jnp.max(mid - beta[C - 1:C, :]))
        safe_scr[ci] = (spread <= HG_FACTORED_MAX_DECAY).astype(jnp.int32)

    def chunk_step(c, carry):
        rows = pl.ds(pl.multiple_of(c * C, C), C)
        base = pl.multiple_of(c * C, C)
        btot = beta_scr[pl.ds(base + (C - 8), 8), :][7:8]
        mid = beta_scr[pl.ds(base + (C // 2 - 8), 8), :][7:8]
        safe = safe_scr[c] == 1

        @pl.when(safe)
        def _():
            e_mid = jnp.exp(mid)
            e_tot = jnp.exp(btot - mid)
            results = []
            for hh in range(HG_HEADS):
                sk = slice(hh * HG_DK, (hh + 1) * HG_DK)
                d = beta_scr[rows, sk] - mid[:, sk]
                q_mid = q_scr[rows, sk] * jnp.exp(d)
                k_mid = k_scr[rows, sk] * jnp.exp(-d)
                att = lax.dot_general(q_mid.astype(BF16), k_mid.astype(BF16), nt_dims, preferred_element_type=F32)
                att = jnp.where(causal, att, 0.0).astype(BF16)
                o_intra = jnp.dot(att, v_scr[rows, hh * HG_DV:(hh + 1) * HG_DV].astype(BF16),
                                  preferred_element_type=F32)
                results.append(finish_head(hh, rows, q_mid * e_mid[:, sk], k_mid * e_tot[:, sk], o_intra,
                                           btot[:, sk]))
            store_heads(rows, results)

        @pl.when(jnp.logical_not(safe))
        def _():
            t_idx = lax.broadcasted_iota(jnp.int32, (C, 1), 0)
            results = []
            for hh in range(HG_HEADS):
                sk = slice(hh * HG_DK, (hh + 1) * HG_DK)
                sv = slice(hh * HG_DV, (hh + 1) * HG_DV)
                b = beta_scr[rows, sk]
                q = q_scr[rows, sk]

                def key_step(s8, acc):
                    off = pl.multiple_of(s8 * 8, 8)
                    b_keys = beta_scr[pl.ds(base + off, 8), sk]
                    k_keys = k_scr[pl.ds(base + off, 8), sk]
                    v_keys = v_scr[pl.ds(base + off, 8), sv]
                    for i in range(8):
                        w = jnp.exp(jnp.minimum(b - b_keys[i:i + 1], 0.0))
                        a = jnp.sum(q * k_keys[i:i + 1] * w, axis=-1, keepdims=True)
                        a = jnp.where(t_idx >= off + i, a, 0.0)
                        acc = acc + a * v_keys[i:i + 1]
                    return acc

                o_intra = lax.fori_loop(0, C // 8, key_step, jnp.zeros((C, HG_DV), F32))
                results.append(finish_head(hh, rows, q * jnp.exp(b), k_scr[rows, sk] * jnp.exp(btot[:, sk] - b),
                                           o_intra, btot[:, sk]))
            store_heads(rows, results)

        return carry

    lax.fori_loop(0, tt // C, chunk_step, 0)
    y_ref[...] = x + _dot(o_scr[...], wout_ref[...])

    @pl.when(t == pl.num_programs(1) - 1)
    def _():
        for hh in range(HG_HEADS):
            sfin_ref[hh] = s_scr[hh].T


def _hgrn_prompt(x, g, w_in, lb_logits, norm_g, w_out, layer, j, batch, seq):
    tt = HG_ROW_TILE
    nt = seq // tt
    x3 = x.reshape(batch, seq, D_MODEL)
    row_spec = pl.BlockSpec((None, tt, D_MODEL), lambda b, t: (b, t, 0))
    y, s_fin = pl.pallas_call(
        functools.partial(_hgrn_prompt_kernel, layer=layer, tt=tt),
        grid=(batch, nt),
        in_specs=[row_spec,
                  pl.BlockSpec((None, 1, D_MODEL), lambda b, t: (layer, 0, 0)),
                  _resident((None, D_MODEL, 4 * HG_WIDTH), lambda b, t: (j, 0, 0)),
                  pl.BlockSpec((DEPTH, HG_WIDTH), lambda b, t: (0, 0)),
                  pl.BlockSpec((None, 1, HG_DV), lambda b, t: (j, 0, 0)),
                  _resident((None, HG_WIDTH, D_MODEL), lambda b, t: (j, 0, 0))],
        out_specs=(row_spec,
                   pl.BlockSpec((None, HG_HEADS, HG_DK, HG_DV), lambda b, t: (b, 0, 0, 0))),
        out_shape=(jax.ShapeDtypeStruct((batch, seq, D_MODEL), F32),
                   jax.ShapeDtypeStruct((batch, HG_HEADS, HG_DK, HG_DV), F32)),
        scratch_shapes=[pltpu.VMEM((HG_HEADS, HG_DV, HG_DK), F32),
                        pltpu.VMEM((tt, HG_WIDTH), F32), pltpu.VMEM((tt, HG_WIDTH), F32),
                        pltpu.VMEM((tt, HG_WIDTH), F32), pltpu.VMEM((tt, HG_WIDTH), F32),
                        pltpu.VMEM((tt, HG_WIDTH), BF16), pltpu.VMEM((tt, HG_WIDTH), F32),
                        pltpu.SMEM((tt // HG_CHUNK,), jnp.int32)],
        compiler_params=_params("parallel", "arbitrary"),
        name="hgrn_prompt",
    )(x3, g.reshape(DEPTH, 1, D_MODEL), w_in, lb_logits, norm_g.reshape(-1, 1, HG_DV), w_out)
    return y.reshape(batch * seq, D_MODEL), s_fin


def _hgrn_sample_kernel(proj_ref, lbl_ref, ng_ref, s_ref, snew_ref, o_ref, *, layer, tb):
    proj = proj_ref[...]
    lb = _hg_lower_bound(lbl_ref[...], layer)
    z = proj[:, HG_WIDTH:2 * HG_WIDTH]
    e = jnp.exp(-jnp.abs(z))
    r = 1.0 / (1.0 + e)
    sig = jnp.where(z >= 0.0, r, e * r)
    f = lb + (1.0 - lb) * sig
    k = (1.0 - lb) * jnp.where(z >= 0.0, e * r, r)
    q = _silu(proj[:, :HG_WIDTH])
    v = proj[:, 2 * HG_WIDTH:3 * HG_WIDTH]
    gt = _silu(proj[:, 3 * HG_WIDTH:])
    ng = ng_ref[...]
    tok, lane = _iota2((tb, tb * HG_DV))
    spread = ((lane // HG_DV) == tok).astype(BF16)

    spread3 = jnp.concatenate([spread] * 3, axis=0)

    def columns(x, exact):
        if exact:
            return jnp.dot(jnp.concatenate(_split3(x.T), axis=1), spread3, preferred_element_type=F32)
        return jnp.dot(x.T.astype(BF16), spread, preferred_element_type=F32)

    for hh in range(HG_HEADS):
        sk = slice(hh * HG_DK, (hh + 1) * HG_DK)
        sv = slice(hh * HG_DV, (hh + 1) * HG_DV)
        f_c, k_c, q_c = columns(f[:, sk], True), columns(k[:, sk], False), columns(q[:, sk], False)
        for b in range(tb):
            blk = slice(b * HG_DV, (b + 1) * HG_DV)
            s_new = f_c[:, blk] * s_ref[b, hh] + k_c[:, blk] * v[b:b + 1, sv]
            snew_ref[b, hh] = s_new
            o = jnp.sum(q_c[:, blk] * s_new, axis=0, keepdims=True)
            o_ref[b:b + 1, sv] = _rms(o, ng) * gt[b:b + 1, sv]


def _hgrn_sample(proj, lb_logits, norm_g, state, layer, j):
    nb = proj.shape[0]
    tb = SAMPLE_TOKENS_PER_STEP
    st_spec = pl.BlockSpec((tb, HG_HEADS, HG_DK, HG_DV), lambda i: (i, 0, 0, 0))
    return pl.pallas_call(
        functools.partial(_hgrn_sample_kernel, layer=layer, tb=tb),
        grid=(nb // tb,),
        in_specs=[pl.BlockSpec((tb, 4 * HG_WIDTH), lambda i: (i, 0)),
                  pl.BlockSpec((DEPTH, HG_WIDTH), lambda i: (0, 0)),
                  pl.BlockSpec((None, 1, HG_DV), lambda i: (j, 0, 0)),
                  st_spec],
        out_specs=(st_spec, pl.BlockSpec((tb, HG_WIDTH), lambda i: (i, 0))),
        out_shape=(jax.ShapeDtypeStruct(state.shape, F32), jax.ShapeDtypeStruct((nb, HG_WIDTH), F32)),
        compiler_params=_params("parallel"),
        name="hgrn_sample",
    )(proj, lb_logits, norm_g.reshape(-1, 1, HG_DV), state)


def _xattn_prompt_kernel(x_ref, g_ref, wq_ref, k_ref, v_ref, wo_ref, y_ref):
    x = x_ref[...]
    h = _rms(x, g_ref[...]).astype(BF16)
    q = _dot(h, wq_ref[...]) * (1.0 / math.sqrt(MEM_HD))
    q = q.astype(BF16)
    outs = []
    for hh in range(MEM_HEADS):
        sl = slice(hh * MEM_HD, (hh + 1) * MEM_HD)
        s = lax.dot_general(q[:, sl], k_ref[:, sl], (((1,), (1,)), ((), ())), preferred_element_type=F32)
        p = jnp.exp(s - jnp.max(s, axis=-1, keepdims=True))
        den = jnp.sum(p, axis=-1, keepdims=True)
        o = jnp.dot(p.astype(BF16), v_ref[:, sl], preferred_element_type=F32)
        outs.append((o / den).astype(BF16))
    o = jnp.concatenate(outs, axis=-1)
    y_ref[...] = x + _dot(o, wo_ref[...])


def _xattn_prompt(x, g, w_q, mem_k, mem_v, w_o, layer, batch, seq):
    tt = XA_ROW_TILE
    row_spec = pl.BlockSpec((None, tt, D_MODEL), lambda b, t: (b, t, 0))
    kv_spec = pl.BlockSpec((None, N_MEM, D_MODEL), lambda b, t: (layer, b, 0))
    w_spec = _resident((None, D_MODEL, D_MODEL), lambda b, t: (layer, 0, 0))
    y = pl.pallas_call(
        _xattn_prompt_kernel,
        grid=(batch, seq // tt),
        in_specs=[row_spec, pl.BlockSpec((None, 1, D_MODEL), lambda b, t: (layer, 0, 0)),
                  w_spec, kv_spec, kv_spec, w_spec],
        out_specs=row_spec,
        out_shape=jax.ShapeDtypeStruct((batch, seq, D_MODEL), F32),
        compiler_params=_params("parallel", "parallel"),
        name="xattn_prompt",
    )(x.reshape(batch, seq, D_MODEL), g.reshape(DEPTH, 1, D_MODEL), w_q, mem_k, mem_v, w_o)
    return y.reshape(batch * seq, D_MODEL)


def _xattn_sample_kernel(q_ref, k_ref, v_ref, o_ref, *, tb):
    scale = 1.0 / math.sqrt(MEM_HD)
    rows = N_MEM * MEM_HEADS
    head, lane = _iota2((MEM_HEADS, rows))
    own = (lane & (MEM_HEADS - 1)) == head
    for b in range(tb):
        q = (q_ref[b] * scale).astype(BF16)
        k = k_ref[b].reshape(rows, MEM_HD).astype(BF16)
        v = v_ref[b].reshape(rows, MEM_HD).astype(BF16)
        s = lax.dot_general(q, k, (((1,), (1,)), ((), ())), preferred_element_type=F32)
        s = jnp.where(own, s, -jnp.inf)
        p = jnp.exp(s - jnp.max(s, axis=-1, keepdims=True))
        den = jnp.sum(p, axis=-1, keepdims=True)
        o_ref[b] = jnp.dot(p.astype(BF16), v, preferred_element_type=F32) / den


def _xattn_sample(q, cache_k, cache_v, layer):
    nb = q.shape[0]
    tb = XA_SAMPLE_TOKENS_PER_STEP
    kv_spec = pl.BlockSpec((None, tb, N_MEM, MEM_HEADS, MEM_HD), lambda i: (layer, i, 0, 0, 0))
    q_spec = pl.BlockSpec((tb, MEM_HEADS, MEM_HD), lambda i: (i, 0, 0))
    return pl.pallas_call(
        functools.partial(_xattn_sample_kernel, tb=tb),
        grid=(nb // tb,),
        in_specs=[q_spec, kv_spec, kv_spec],
        out_specs=q_spec,
        out_shape=jax.ShapeDtypeStruct((nb, MEM_HEADS, MEM_HD), F32),
        compiler_params=_params("parallel"),
        name="xattn_sample",
    )(q.reshape(nb, MEM_HEADS, MEM_HD), cache_k, cache_v).reshape(nb, D_MODEL)


def _ffn1(x, w, i, ffn, emit):
    if emit:
        return ffn(x, w["ffn1_norm"], w["ffn1_w_in"], w["ffn1_w_out"], i, post="emit", g2=w["mix_norm"][i],
                   u_dtype=F32)
    return ffn(x, w["ffn1_norm"], w["ffn1_w_in"], w["ffn1_w_out"], i)


def _ffn2(x, w, i, ffn):
    last = i == DEPTH - 1
    return ffn(x, w["ffn2_norm"], w["ffn2_w_in"], w["ffn2_w_out"], i,
               post="replace" if last else "none", g2=w["final_norm"] if last else None)


def _prompt_trunk(x, batch, seq, mem_k, mem_v, w, ffn):
    zero = jnp.zeros((batch, S5_GROUPS, S5_STATE), F32)
    new_re, new_im, new_hg = [], [], []
    for i in range(DEPTH):
        j = i // 2
        if i % 2 == 0:
            x, u = _ffn1(x, w, i, ffn, True)
            x, hr, hi = _s5_mixer(x, u, zero, zero, w["s5"], w["s5_w_glu"], j, batch, seq)
            new_re.append(hr)
            new_im.append(hi)
        else:
            x = _ffn1(x, w, i, ffn, False)
            x, sn = _hgrn_prompt(x, w["mix_norm"], w["hg_w_in"], w["hg_lb_logits"], w["hg_norm"],
                                 w["hg_w_out"], i, j, batch, seq)
            new_hg.append(sn)
        x = _xattn_prompt(x, w["xattn_norm"], w["xattn_w_q"], mem_k, mem_v, w["xattn_w_o"], i, batch, seq)
        x = _ffn2(x, w, i, ffn)
    return x, jnp.stack(new_re), jnp.stack(new_im), jnp.stack(new_hg)


def _sample_trunk(x, batch, s5_re, s5_im, hg_state, w):
    new_re, new_im, new_hg = [], [], []
    for i in range(DEPTH):
        j = i // 2
        if i % 2 == 0:
            x, u = _ffn1(x, w, i, _ffn, True)
            x, hr, hi = _s5_mixer(x, u, s5_re[j], s5_im[j], w["s5"], w["s5_w_glu"], j, batch, 1)
            new_re.append(hr)
            new_im.append(hi)
        else:
            x = _ffn1(x, w, i, _ffn, False)
            proj = _proj(x, w["hg_w_in"], j, g=w["mix_norm"], g_layer=i)
            sn, o = _hgrn_sample(proj, w["hg_lb_logits"], w["hg_norm"], hg_state[j], i, j)
            x = _proj(o, w["hg_w_out"], j, res=x)
            new_hg.append(sn)
        q = _proj(x, w["xattn_w_q"], i, g=w["xattn_norm"], g_layer=i)
        o = yield q, i
        x = _proj(o, w["xattn_w_o"], i, res=x)
        x = _ffn2(x, w, i, _ffn)
    return x, jnp.stack(new_re), jnp.stack(new_im), jnp.stack(new_hg)


def kernel(x_prompt, x_sample, mem_prompt, state_s5_re, state_s5_im, state_hgrn, cache_mem_k, cache_mem_v, ffn1_norm, ffn1_w_in, ffn1_w_out, mix_norm, xattn_norm, mem_norm, xattn_w_q, xattn_w_kv, xattn_w_o, ffn2_norm, ffn2_w_in, ffn2_w_out, s5_a_re, s5_a_im, s5_log_dt, s5_b_re, s5_b_im, s5_c_re, s5_c_im, s5_d, s5_w_glu, hg_w_in, hg_lb_logits, hg_norm, hg_w_out, final_norm):
    bp, seq, _ = x_prompt.shape
    bs = x_sample.shape[0]
    w = dict(ffn1_norm=ffn1_norm, ffn1_w_in=ffn1_w_in, ffn1_w_out=ffn1_w_out, mix_norm=mix_norm,
             xattn_norm=xattn_norm, xattn_w_q=xattn_w_q, xattn_w_o=xattn_w_o, ffn2_norm=ffn2_norm,
             ffn2_w_in=ffn2_w_in, ffn2_w_out=ffn2_w_out,
             s5=(s5_a_re, s5_a_im, s5_log_dt, s5_b_re, s5_b_im, s5_c_re, s5_c_im, s5_d),
             s5_w_glu=s5_w_glu, hg_w_in=hg_w_in, hg_lb_logits=hg_lb_logits, hg_norm=hg_norm,
             hg_w_out=hg_w_out, final_norm=final_norm)

    mem_k, mem_v, mem_k_bf, mem_v_bf = _mem_kv(mem_prompt, mem_norm, xattn_w_kv)

    sample = _sample_trunk(x_sample.reshape(bs, D_MODEL), bs, state_s5_re, state_s5_im, state_hgrn, w)
    pending = [next(sample)]
    sample_out = []

    def resume(o):
        try:
            pending[0] = sample.send(o)
        except StopIteration as done:
            pending[0] = None
            sample_out.append(done.value)

    def ffn_with_rider(x, *args, **kwargs):
        steps = x.shape[0] // min(FFN_ROW_TILE_WITH_RIDER, x.shape[0])
        if pending[0] is None or bs % steps:
            return _ffn(x, *args, **kwargs)
        q, layer = pending[0]
        *outs, o = _ffn(x, *args, rider=(q, cache_mem_k, cache_mem_v, layer), **kwargs)
        resume(o)
        return outs[0] if len(outs) == 1 else tuple(outs)

    y_p, re_p, im_p, hg_p = _prompt_trunk(x_prompt.reshape(bp * seq, D_MODEL), bp, seq, mem_k_bf, mem_v_bf, w,
                                          ffn_with_rider)
    while pending[0] is not None:
        q, layer = pending[0]
        resume(_xattn_sample(q, cache_mem_k, cache_mem_v, layer))
    y_s, re_s, im_s, hg_s = sample_out[0]
    return (y_p.reshape(bp, seq, D_MODEL), y_s.reshape(bs, 1, D_MODEL), re_p, im_p, re_s, im_s, hg_p, hg_s,
            mem_k, mem_v)
```

```python
import functools
import math

import jax
import jax.numpy as jnp
from jax import lax
from jax.experimental import pallas as pl
from jax.experimental.pallas import tpu as pltpu

F32 = jnp.float32
BF16 = jnp.bfloat16

D_MODEL = 1024
DEPTH = 2
S5_GROUP = 16
S5_GROUPS = D_MODEL // S5_GROUP
S5_STATE = 64
S5_CHUNK = 16
S5_BLOCK_GROUPS = 8
S5_TIME_SLICES = 2
HG_DK = 128
HG_HEADS = D_MODEL // HG_DK
HG_DV = D_MODEL // HG_HEADS
HG_WIDTH = HG_HEADS * HG_DK
HG_CHUNK = 128
HG_FACTORED_MAX_DECAY = 60.0
N_MEM = 256
MEM_HEADS = 4
MEM_HD = D_MODEL // MEM_HEADS
FFN_DIM = 2816
EPS = 1e-6

V7X_VMEM_LIMIT_BYTES = 56 * 1024 * 1024

ROW_TILE = 512
FFN_CHUNK = 1408
FFN_ROW_TILE_WITH_RIDER = 256
HG_ROW_TILE = 512
XA_ROW_TILE = 1024
PROJ_ROW_TILE = 1024
S5_GROUPS_PER_STEP = 16
SAMPLE_TOKENS_PER_STEP = 8
XA_SAMPLE_TOKENS_PER_STEP = 4


def _params(*semantics):
    return pltpu.CompilerParams(dimension_semantics=semantics,
                                vmem_limit_bytes=V7X_VMEM_LIMIT_BYTES)


def _resident(shape, index_map):
    return pl.BlockSpec(shape, index_map, pipeline_mode=pl.Buffered(1))


def _rms(x, g):
    ms = jnp.mean(x * x, axis=-1, keepdims=True)
    return x * lax.rsqrt(ms + EPS) * g


def _sigmoid(x):
    return 1.0 / (1.0 + jnp.exp(-x))


def _silu(x):
    return x * _sigmoid(x)


def _gelu_tanh(x):
    c = math.sqrt(2.0 / math.pi)
    return 0.5 * x * (1.0 + jnp.tanh(c * (x + 0.044715 * (x * x * x))))


def _dot(a, w):
    return jnp.dot(a, w.astype(BF16), preferred_element_type=F32)


def _split3(x):
    hi = x.astype(BF16)
    r1 = x - hi.astype(F32)
    mid = r1.astype(BF16)
    lo = (r1 - mid.astype(F32)).astype(BF16)
    return hi, mid, lo


def _ffn_kernel(x_ref, g_ref, win_ref, wout_ref, *rest, post, rider_tokens):
    rest = list(rest)
    g2_ref = rest.pop(0) if post != "none" else None
    if rider_tokens:
        q_ref, k_ref, v_ref = rest[:3]
        rest = rest[3:]
        _xattn_sample_kernel(q_ref, k_ref, v_ref, rest.pop(), tb=rider_tokens)
    x = x_ref[...]
    h = _rms(x, g_ref[...]).astype(BF16)
    acc = jnp.zeros_like(x)
    for c in range(FFN_DIM // FFN_CHUNK):
        lo = c * FFN_CHUNK
        gate = _dot(h, win_ref[:, lo:lo + FFN_CHUNK])
        up = _dot(h, win_ref[:, FFN_DIM + lo:FFN_DIM + lo + FFN_CHUNK])
        act = (_silu(gate) * up).astype(BF16)
        acc = acc + _dot(act, wout_ref[lo:lo + FFN_CHUNK, :])
    y = x + 0.5 * acc
    y_ref = rest[0]
    if post == "replace":
        y_ref[...] = _rms(y, g2_ref[...])
    else:
        y_ref[...] = y
        if post == "emit":
            rest[1][...] = _rms(y, g2_ref[...]).astype(rest[1].dtype)


def _ffn(x, g, w_in, w_out, layer, post="none", g2=None, u_dtype=BF16, rider=None):
    rows = x.shape[0]
    tm = min(ROW_TILE if rider is None else FFN_ROW_TILE_WITH_RIDER, rows)
    steps = rows // tm
    row_spec = pl.BlockSpec((tm, D_MODEL), lambda i: (i, 0))
    vec_spec = pl.BlockSpec((1, D_MODEL), lambda i: (0, 0))
    in_specs = [row_spec,
                pl.BlockSpec((None, 1, D_MODEL), lambda i: (layer, 0, 0)),
                _resident((None, D_MODEL, 2 * FFN_DIM), lambda i: (layer, 0, 0)),
                _resident((None, FFN_DIM, D_MODEL), lambda i: (layer, 0, 0))]
    args = [x, g.reshape(DEPTH, 1, D_MODEL), w_in, w_out]
    out_shape = [jax.ShapeDtypeStruct((rows, D_MODEL), F32)]
    out_specs = [row_spec]
    if post != "none":
        in_specs.append(vec_spec)
        args.append(g2.reshape(1, D_MODEL))
    if post == "emit":
        out_shape.append(jax.ShapeDtypeStruct((rows, D_MODEL), u_dtype))
        out_specs.append(row_spec)
    tb = 0
    if rider is not None:
        q, cache_k, cache_v, r_layer = rider
        nb = q.shape[0]
        assert nb % steps == 0
        tb = nb // steps
        kv_spec = pl.BlockSpec((None, tb, N_MEM, MEM_HEADS, MEM_HD), lambda i: (r_layer, i, 0, 0, 0))
        q_spec = pl.BlockSpec((tb, MEM_HEADS, MEM_HD), lambda i: (i, 0, 0))
        in_specs += [q_spec, kv_spec, kv_spec]
        args += [q.reshape(nb, MEM_HEADS, MEM_HD), cache_k, cache_v]
        out_shape.append(jax.ShapeDtypeStruct((nb, MEM_HEADS, MEM_HD), F32))
        out_specs.append(q_spec)
    outs = list(pl.pallas_call(
        functools.partial(_ffn_kernel, post=post, rider_tokens=tb),
        grid=(steps,),
        in_specs=in_specs, out_specs=tuple(out_specs), out_shape=tuple(out_shape),
        compiler_params=_params("parallel"),
        name="ffn",
    )(*args))
    if rider is not None:
        outs[-1] = outs[-1].reshape(-1, D_MODEL)
    return outs[0] if len(outs) == 1 else tuple(outs)


def _proj_kernel(*refs, norm, glu, residual):
    refs = list(refs)
    x_ref = refs.pop(0)
    g_ref = refs.pop(0) if norm else None
    w_ref = refs.pop(0)
    res_ref = refs.pop(0) if residual else None
    (o_ref,) = refs
    x = x_ref[...]
    if norm:
        x = _rms(x, g_ref[...])
    y = _dot(x.astype(BF16), w_ref[...])
    if glu:
        half = y.shape[-1] // 2
        y = y[:, :half] * _sigmoid(y[:, half:])
    if residual:
        y = y + res_ref[...]
    o_ref[...] = y


def _proj(x, w, layer, g=None, g_layer=0, glu=False, res=None):
    rows, kdim = x.shape
    ndim = w.shape[-1]
    nout = ndim // 2 if glu else ndim
    tm = min(PROJ_ROW_TILE, rows)
    in_specs = [pl.BlockSpec((tm, kdim), lambda i: (i, 0))]
    args = [x]
    if g is not None:
        in_specs.append(pl.BlockSpec((None, 1, kdim), lambda i: (g_layer, 0, 0)))
        args.append(g.reshape(g.shape[0], 1, kdim))
    in_specs.append(_resident((None, kdim, ndim), lambda i: (layer, 0, 0)))
    args.append(w)
    if res is not None:
        in_specs.append(pl.BlockSpec((tm, nout), lambda i: (i, 0)))
        args.append(res)
    return pl.pallas_call(
        functools.partial(_proj_kernel, norm=g is not None, glu=glu, residual=res is not None),
        grid=(rows // tm,),
        in_specs=in_specs,
        out_specs=pl.BlockSpec((tm, nout), lambda i: (i, 0)),
        out_shape=jax.ShapeDtypeStruct((rows, nout), F32),
        compiler_params=_params("parallel"),
        name="proj",
    )(*args)


def _memkv_kernel(x_ref, g_ref, w_ref, k_ref, v_ref, kb_ref, vb_ref):
    h = _rms(x_ref[...], g_ref[...]).astype(BF16)
    y = _dot(h, w_ref[...])
    k, v = y[:, :D_MODEL], y[:, D_MODEL:]
    k_ref[...] = k.reshape(k_ref.shape)
    v_ref[...] = v.reshape(v_ref.shape)
    kb_ref[...] = k.astype(BF16)
    vb_ref[...] = v.astype(BF16)


def _mem_kv(mem, g, w_kv):
    batch = mem.shape[0]
    rows = batch * N_MEM
    nb = max(1, min(ROW_TILE, rows) // N_MEM)
    tm = nb * N_MEM
    out5 = jax.ShapeDtypeStruct((DEPTH, batch, N_MEM, MEM_HEADS, MEM_HD), F32)
    out2 = jax.ShapeDtypeStruct((DEPTH, rows, D_MODEL), BF16)
    spec5 = pl.BlockSpec((None, nb, N_MEM, MEM_HEADS, MEM_HD), lambda l, i: (l, i, 0, 0, 0))
    spec2 = pl.BlockSpec((None, tm, D_MODEL), lambda l, i: (l, i, 0))
    return pl.pallas_call(
        _memkv_kernel,
        grid=(DEPTH, rows // tm),
        in_specs=[pl.BlockSpec((tm, D_MODEL), lambda l, i: (i, 0)),
                  pl.BlockSpec((None, 1, D_MODEL), lambda l, i: (l, 0, 0)),
                  pl.BlockSpec((None, D_MODEL, 2 * D_MODEL), lambda l, i: (l, 0, 0))],
        out_specs=(spec5, spec5, spec2, spec2), out_shape=(out5, out5, out2, out2),
        compiler_params=_params("parallel", "parallel"),
        name="mem_kv",
    )(mem.reshape(rows, D_MODEL), g.reshape(DEPTH, 1, D_MODEL), w_kv)


def _s5_pieces(a_re, a_im, log_dt, b_re, b_im, c_re, c_im, d, steps):
    G, P, GC = S5_GROUPS, S5_STATE, S5_GROUP
    L = steps
    dt = jnp.exp(log_dt)[:, None]
    xr, xi = a_re * dt, a_im * dt
    j = jnp.arange(L + 1, dtype=F32)[:, None, None]
    mag = jnp.exp(xr[None] * j)
    pw_re, pw_im = mag * jnp.cos(xi[None] * j), mag * jnp.sin(xi[None] * j)
    nr, ni = pw_re[1] - 1.0, pw_im[1]
    den = a_re * a_re + a_im * a_im
    fr, fi = ((nr * a_re + ni * a_im) / den)[:, None, :], ((ni * a_re - nr * a_im) / den)[:, None, :]
    bt_re, bt_im = b_re.transpose(0, 2, 1), b_im.transpose(0, 2, 1)
    bb_re = fr * bt_re - fi * bt_im
    bb_im = fr * bt_im + fi * bt_re

    def times_c(p_re, p_im):
        p_re, p_im = p_re[:, :, None, :], p_im[:, :, None, :]
        return p_re * c_re[None] - p_im * c_im[None], p_re * c_im[None] + p_im * c_re[None]

    w_re, w_im = times_c(pw_re[:L], pw_im[:L])

    def over_state(b, w_):
        w_ = w_.transpose(1, 0, 2, 3).reshape(G, L * GC, P)
        return jnp.einsum("gip,gnp->gin", b, w_, precision=lax.Precision.HIGHEST)

    kern = (over_state(bb_re, w_re) - over_state(bb_im, w_im)).reshape(G, GC, L, GC).transpose(2, 0, 1, 3)
    kern = kern.at[0].add(d[:, :, None] * jnp.eye(GC, dtype=F32)[None])
    rp_re, rp_im = pw_re[:L][::-1][:, :, None, :], pw_im[:L][::-1][:, :, None, :]
    m_re = rp_re * bb_re[None] - rp_im * bb_im[None]
    m_im = rp_re * bb_im[None] + rp_im * bb_re[None]
    q_re, q_im = times_c(pw_re[1:], pw_im[1:])
    return kern, m_re, m_im, q_re, -q_im, pw_re[L], pw_im[L], pw_re[1], pw_im[1]


def _s5_group_mats(pieces):
    kern, m_re, m_im, p_re, p_im, _, _, l_re, l_im = pieces
    return (kern[0], m_re[-1], m_im[-1], p_re[0].transpose(0, 2, 1), p_im[0].transpose(0, 2, 1),
            l_re[:, None, :], l_im[:, None, :])


def _s5_block_mats(pieces):
    kern, m_re, m_im, p_re, p_im, l_re, l_im = pieces[:7]
    L = kern.shape[0]
    GB = S5_BLOCK_GROUPS
    NB = S5_GROUPS // GB
    GC, P = S5_GROUP, S5_STATE
    taps = kern.astype(BF16).reshape(L, NB, GB * GC, GC).transpose(1, 0, 2, 3)

    def rows(m):
        return m.astype(BF16).reshape(L, NB, GB * GC, P).transpose(1, 0, 2, 3).reshape(NB, L * GB * GC, P)

    lam = jnp.concatenate([l_re.reshape(NB, 1, GB * P), l_im.reshape(NB, 1, GB * P)], axis=2)
    return taps, rows(m_re), rows(m_im), rows(p_re), rows(p_im), lam


def _s5_kernel(u_ref, t_ref, mre_ref, mim_ref, pre_ref, pim_ref, lre_ref, lim_ref, h0re_ref, h0im_ref,
               y_ref, hre_ref, him_ref, inj_re, inj_im, hs_re, hs_im, *, n_chunks, rb, gps, precision):
    def mm(a, b):
        return jnp.dot(a, b, preferred_element_type=F32, precision=precision)

    for g in range(gps):
        u = u_ref[g]
        inj_re[g] = mm(u, mre_ref[g])
        inj_im[g] = mm(u, mim_ref[g])

    lam_re = [jnp.broadcast_to(lre_ref[g], (rb, S5_STATE)) for g in range(gps)]
    lam_im = [jnp.broadcast_to(lim_ref[g], (rb, S5_STATE)) for g in range(gps)]

    def step(k, carry):
        rows = pl.ds(pl.multiple_of(k * rb, rb), rb)
        nxt = []
        for g in range(gps):
            hr, hi = carry[2 * g], carry[2 * g + 1]
            hs_re[g, rows, :] = hr
            hs_im[g, rows, :] = hi
            nxt.append(lam_re[g] * hr - lam_im[g] * hi + inj_re[g, rows, :])
            nxt.append(lam_re[g] * hi + lam_im[g] * hr + inj_im[g, rows, :])
        return tuple(nxt)

    init = []
    for g in range(gps):
        init += [h0re_ref[g], h0im_ref[g]]
    fin = lax.fori_loop(0, n_chunks, step, tuple(init))

    for g in range(gps):
        hre_ref[g] = fin[2 * g]
        him_ref[g] = fin[2 * g + 1]
        dt = u_ref.dtype
        y = (mm(u_ref[g], t_ref[g]) + mm(hs_re[g].astype(dt), pre_ref[g])
             + mm(hs_im[g].astype(dt), pim_ref[g]))
        y_ref[g] = _gelu_tanh(y).astype(y_ref.dtype)


def _s5_core(u, mats, h0_re, h0_im, n_chunks, rb, precision, y_dtype):
    G, R, W = u.shape
    P = S5_STATE
    gps = S5_GROUPS_PER_STEP
    tmat, m_re, m_im, p_re, p_im, l_re, l_im = mats

    def spec(a, b):
        return pl.BlockSpec((gps, a, b), lambda i: (i, 0, 0))

    st = jax.ShapeDtypeStruct((G, rb, P), F32)
    scr = lambda: pltpu.VMEM((gps, R, P), F32)
    return pl.pallas_call(
        functools.partial(_s5_kernel, n_chunks=n_chunks, rb=rb, gps=gps, precision=precision),
        grid=(G // gps,),
        in_specs=[spec(R, W), spec(W, W), spec(W, P), spec(W, P), spec(P, W), spec(P, W),
                  spec(1, P), spec(1, P), spec(rb, P), spec(rb, P)],
        out_specs=(spec(R, W), spec(rb, P), spec(rb, P)),
        out_shape=(jax.ShapeDtypeStruct((G, R, W), y_dtype), st, st),
        scratch_shapes=[scr(), scr(), scr(), scr()],
        compiler_params=_params("parallel"),
        name="s5_core",
    )(u, tmat, m_re, m_im, p_re, p_im, l_re, l_im, h0_re, h0_im)


def _iota2(shape):
    return lax.broadcasted_iota(jnp.int32, shape, 0), lax.broadcasted_iota(jnp.int32, shape, 1)


def _s5_expand(taps_ref, compact_refs, w2_ref, m_ref, p_ref):
    L, GC, P = S5_CHUNK, S5_GROUP, S5_STATE
    GB = S5_BLOCK_GROUPS
    W, HS = GB * GC, GB * P
    gc_bits, p_bits = GC.bit_length() - 1, P.bit_length() - 1

    r, c = _iota2((P, HS))
    rep_m = ((c & (P - 1)) == r).astype(BF16)
    r, c = _iota2((L * W, HS))
    mask_m = ((r >> gc_bits) & (GB - 1)) == (c >> p_bits)
    mre_ref, mim_ref, pre_ref, pim_ref = compact_refs
    for src, dst, half in ((mre_ref, m_ref, 0), (mim_ref, m_ref, 1), (pre_ref, p_ref, 0), (pim_ref, p_ref, 1)):
        dst[:, half * HS:(half + 1) * HS] = jnp.where(
            mask_m, jnp.dot(src[...], rep_m, preferred_element_type=F32), 0.0).astype(BF16)

    r, c = _iota2((GC, W))
    rep_k = ((c & (GC - 1)) == r).astype(BF16)
    r, c = _iota2((W, W))
    mask_k = (r >> gc_bits) == (c >> gc_bits)
    zero = jnp.zeros((W, W), BF16)
    w2_ref[L * W:, :W] = zero
    w2_ref[:W, W:] = zero
    for i in range(L):
        kb = jnp.where(mask_k, jnp.dot(taps_ref[L - 1 - i], rep_k, preferred_element_type=F32), 0.0).astype(BF16)
        w2_ref[i * W:(i + 1) * W, :W] = kb
        w2_ref[(i + 1) * W:(i + 2) * W, W:] = kb


def _s5_seq_kernel(u_ref, taps_ref, mre_ref, mim_ref, pre_ref, pim_ref, lam_ref, h0_ref, y_ref, hfin_ref,
                   w2_ref, m_ref, p_ref, lhs_scr, inj_scr, hs_scr, h_scr, *, batch, nck):
    L, W = S5_CHUNK, S5_BLOCK_GROUPS * S5_GROUP
    NT = S5_BLOCK_GROUPS * S5_STATE // W

    @pl.when(pl.program_id(1) == 0)
    def _():
        h_scr[...] = h0_ref[...]
        _s5_expand(taps_ref, (mre_ref, mim_ref, pre_ref, pim_ref), w2_ref, m_ref, p_ref)

    for b in range(batch):
        for l in range(L):
            lhs_scr[b * nck:(b + 1) * nck, l * W:(l + 1) * W] = u_ref[b, pl.ds(l, nck, stride=L), :].astype(BF16)

    def swap_major(x, a, b):
        return jnp.swapaxes(x.reshape(a, b, W), 0, 1).reshape(a * b, W)

    inj = jnp.dot(lhs_scr[...], m_ref[...], preferred_element_type=F32)
    for t in range(2 * NT):
        inj_scr[t] = swap_major(inj[:, t * W:(t + 1) * W], batch, nck)

    lam = [jnp.broadcast_to(lam_ref[:, t * W:(t + 1) * W], (batch, W)) for t in range(2 * NT)]

    def step(k, h):
        rows = pl.ds(pl.multiple_of(k * batch, batch), batch)
        nxt_re, nxt_im = [], []
        for t in range(NT):
            hr, hi = h[t], h[NT + t]
            hs_scr[t, rows, :] = hr
            hs_scr[NT + t, rows, :] = hi
            nxt_re.append(lam[t] * hr - lam[NT + t] * hi + inj_scr[t, rows, :])
            nxt_im.append(lam[t] * hi + lam[NT + t] * hr + inj_scr[NT + t, rows, :])
        return tuple(nxt_re + nxt_im)

    h = lax.fori_loop(0, nck, step, tuple(h_scr[:, t * W:(t + 1) * W] for t in range(2 * NT)))
    for t in range(2 * NT):
        h_scr[:, t * W:(t + 1) * W] = h[t]
    hfin_ref[...] = h_scr[...]

    hs = jnp.concatenate([swap_major(hs_scr[t], nck, batch) for t in range(2 * NT)], axis=-1).astype(BF16)
    for pr in range(L // 2):
        kk = (2 * pr + 2) * W
        y = (jnp.dot(lhs_scr[:, :kk], w2_ref[(L - 1 - 2 * pr) * W:, :], preferred_element_type=F32)
             + lax.dot_general(hs, p_ref[2 * pr * W:(2 * pr + 2) * W, :], (((1,), (1,)), ((), ())),
                               preferred_element_type=F32))
        y = _gelu_tanh(y)
        for s in range(2):
            for b in range(batch):
                y_ref[b, pl.ds(2 * pr + s, nck, stride=L), :] = y[b * nck:(b + 1) * nck, s * W:(s + 1) * W]


def _s5_seq(u, mats, h0, batch, seq):
    taps, m_re, m_im, p_re, p_im, lam = mats
    L, W = S5_CHUNK, S5_BLOCK_GROUPS * S5_GROUP
    NB = S5_GROUPS // S5_BLOCK_GROUPS
    HS = S5_BLOCK_GROUPS * S5_STATE
    SW = 2 * HS
    ts = seq // S5_TIME_SLICES
    nck = ts // L
    rows = batch * nck
    u_spec = pl.BlockSpec((batch, ts, W), lambda i, t: (0, t, i))
    h_spec = pl.BlockSpec((None, batch, SW), lambda i, t: (i, 0, 0))

    def w_spec(*dims):
        return pl.BlockSpec((None,) + dims, lambda i, t: (i,) + (0,) * len(dims))

    return pl.pallas_call(
        functools.partial(_s5_seq_kernel, batch=batch, nck=nck),
        grid=(NB, S5_TIME_SLICES),
        in_specs=[u_spec, w_spec(L, W, S5_GROUP)] + [w_spec(L * W, S5_STATE)] * 4 + [w_spec(1, SW), h_spec],
        out_specs=(u_spec, h_spec),
        out_shape=(jax.ShapeDtypeStruct((batch, seq, D_MODEL), F32), jax.ShapeDtypeStruct(h0.shape, F32)),
        scratch_shapes=[pltpu.VMEM(((L + 1) * W, 2 * W), BF16), pltpu.VMEM((L * W, SW), BF16),
                        pltpu.VMEM((L * W, SW), BF16),
                        pltpu.VMEM((rows, L * W), BF16), pltpu.VMEM((SW // W, rows, W), F32),
                        pltpu.VMEM((SW // W, rows, W), F32), pltpu.VMEM((batch, SW), F32)],
        compiler_params=_params("parallel", "arbitrary"),
        name="s5_seq",
    )(u, taps, m_re, m_im, p_re, p_im, lam, h0)


def _s5_mixer(x, u, h0_re, h0_im, pieces, w_glu, j, batch, seq):
    G, GC, P = S5_GROUPS, S5_GROUP, S5_STATE
    if seq > 1:
        NB, HS = G // S5_BLOCK_GROUPS, S5_BLOCK_GROUPS * P
        mats = _s5_block_mats(pieces)
        to_blocks = lambda h: h.reshape(batch, NB, HS).transpose(1, 0, 2)
        h0 = jnp.concatenate([to_blocks(h0_re), to_blocks(h0_im)], axis=-1)
        y, hfin = _s5_seq(u.reshape(batch, seq, D_MODEL), mats, h0, batch, seq)
        y = y.reshape(batch * seq, D_MODEL)
        from_blocks = lambda h: h.transpose(1, 0, 2).reshape(batch, G, P)
        hre, him = from_blocks(hfin[..., :HS]), from_blocks(hfin[..., HS:])
    else:
        mats = _s5_group_mats(pieces)
        mats = tuple(m.astype(BF16) for m in mats[:5]) + mats[5:]
        ug = u.reshape(batch, G, GC).transpose(1, 0, 2).astype(BF16)
        yg, hre, him = _s5_core(ug, mats, h0_re.transpose(1, 0, 2), h0_im.transpose(1, 0, 2),
                                1, batch, None, F32)
        y = yg.transpose(1, 0, 2).reshape(batch, D_MODEL)
        hre, him = hre.transpose(1, 0, 2), him.transpose(1, 0, 2)
    x = _proj(y, w_glu, j, glu=True, res=x)
    return x, hre, him


def _hg_lower_bound(logits, layer):
    m = jnp.max(logits, axis=0, keepdims=True)
    e = jnp.exp(logits - m)
    sm = e / jnp.sum(e, axis=0, keepdims=True)
    return jnp.sum(sm[:layer + 1], axis=0, keepdims=True) - sm[0:1]


def _hg_gates(z, lb):
    e = jnp.exp(-jnp.abs(z))
    r = 1.0 / (1.0 + e)
    er = e * r
    pos = z >= 0.0
    f = lb + (1.0 - lb) * jnp.where(pos, r, er)
    logf = jnp.where(f > 0.0, jnp.log(f), z)
    k = (1.0 - lb) * jnp.where(pos, er, r)
    return logf, k


def _hgrn_prompt_kernel(x_ref, g_ref, win_ref, lbl_ref, ng_ref, wout_ref, y_ref, sfin_ref,
                        s_scr, q_scr, k_scr, v_scr, gt_scr, o_scr, beta_scr, safe_scr, *, layer, tt):
    t = pl.program_id(1)
    C = HG_CHUNK

    @pl.when(t == 0)
    def _():
        s_scr[...] = jnp.zeros_like(s_scr)

    x = x_ref[...]
    h = _rms(x, g_ref[...]).astype(BF16)
    def proj(i):
        return _dot(h, win_ref[:, i * HG_WIDTH:(i + 1) * HG_WIDTH])

    lb = _hg_lower_bound(lbl_ref[...], layer)
    logf, kk = _hg_gates(proj(1), lb)
    k_scr[...] = kk
    q_scr[...] = _silu(proj(0))
    gt_scr[...] = _silu(proj(3))
    v_scr[...] = proj(2)

    row, col = _iota2((C, C))
    causal = row >= col
    tri = causal.astype(BF16)
    ng = ng_ref[...]
    nt_dims = (((1,), (1,)), ((), ()))
    tn_dims = (((0,), (0,)), ((), ()))

    def finish_head(hh, rows, q_dec, k_dec, o_intra, btot_h):
        sv = slice(hh * HG_DV, (hh + 1) * HG_DV)
        st = s_scr[hh]
        o = o_intra + lax.dot_general(q_dec.astype(BF16), st.astype(BF16), nt_dims, preferred_element_type=F32)
        kv_t = lax.dot_general(v_scr[rows, sv].astype(BF16), k_dec.astype(BF16), tn_dims,
                               preferred_element_type=F32)
        return (_rms(o, ng) * gt_scr[rows, sv]).astype(BF16), st * jnp.exp(btot_h) + kv_t

    def store_heads(rows, results):
        o_scr[rows, :] = jnp.concatenate([o for o, _ in results], axis=-1)
        for hh, (_, s_new) in enumerate(results):
            s_scr[hh] = s_new

    for ci in range(tt // C):
        parts = _split3(logf[ci * C:(ci + 1) * C])
        beta = sum(jnp.dot(tri, p, preferred_element_type=F32) for p in parts)
        beta_scr[ci * C:(ci + 1) * C, :] = beta
        mid = beta[C // 2 - 1:C // 2, :]
        spread = jnp.maximum(jnp.max(-mid), jnp.max(mid - beta[C - 1:C, :]))
        safe_scr[ci] = (spread <= HG_FACTORED_MAX_DECAY).astype(jnp.int32)

    def chunk_step(c, carry):
        rows = pl.ds(pl.multiple_of(c * C, C), C)
        base = pl.multiple_of(c * C, C)
        btot = beta_scr[pl.ds(base + (C - 8), 8), :][7:8]
        mid = beta_scr[pl.ds(base + (C // 2 - 8), 8), :][7:8]
        safe = safe_scr[c] == 1

        @pl.when(safe)
        def _():
            e_mid = jnp.exp(mid)
            e_tot = jnp.exp(btot - mid)
            results = []
            for hh in range(HG_HEADS):
                sk = slice(hh * HG_DK, (hh + 1) * HG_DK)
                d = beta_scr[rows, sk] - mid[:, sk]
                q_mid = q_scr[rows, sk] * jnp.exp(d)
                k_mid = k_scr[rows, sk] * jnp.exp(-d)
                att = lax.dot_general(q_mid.astype(BF16), k_mid.astype(BF16), nt_dims, preferred_element_type=F32)
                att = jnp.where(causal, att, 0.0).astype(BF16)
                o_intra = jnp.dot(att, v_scr[rows, hh * HG_DV:(hh + 1) * HG_DV].astype(BF16),
                                  preferred_element_type=F32)
                results.append(finish_head(hh, rows, q_mid * e_mid[:, sk], k_mid * e_tot[:, sk], o_intra,
                                           btot[:, sk]))
            store_heads(rows, results)

        @pl.when(jnp.logical_not(safe))
        def _():
            t_idx = lax.broadcasted_iota(jnp.int32, (C, 1), 0)
            results = []
            for hh in range(HG_HEADS):
                sk = slice(hh * HG_DK, (hh + 1) * HG_DK)
                sv = slice(hh * HG_DV, (hh + 1) * HG_DV)
                b = beta_scr[rows, sk]
                q = q_scr[rows, sk]

                def key_step(s8, acc):
                    off = pl.multiple_of(s8 * 8, 8)
                    b_keys = beta_scr[pl.ds(base + off, 8), sk]
                    k_keys = k_scr[pl.ds(base + off, 8), sk]
                    v_keys = v_scr[pl.ds(base + off, 8), sv]
                    for i in range(8):
                        w = jnp.exp(jnp.minimum(b - b_keys[i:i + 1], 0.0))
                        a = jnp.sum(q * k_keys[i:i + 1] * w, axis=-1, keepdims=True)
                        a = jnp.where(t_idx >= off + i, a, 0.0)
                        acc = acc + a * v_keys[i:i + 1]
                    return acc

                o_intra = lax.fori_loop(0, C // 8, key_step, jnp.zeros((C, HG_DV), F32))
                results.append(finish_head(hh, rows, q * jnp.exp(b), k_scr[rows, sk] * jnp.exp(btot[:, sk] - b),
                                           o_intra, btot[:, sk]))
            store_heads(rows, results)

        return carry

    lax.fori_loop(0, tt // C, chunk_step, 0)
    y_ref[...] = x + _dot(o_scr[...], wout_ref[...])

    @pl.when(t == pl.num_programs(1) - 1)
    def _():
        for hh in range(HG_HEADS):
            sfin_ref[hh] = s_scr[hh].T


def _hgrn_prompt(x, g, w_in, lb_logits, norm_g, w_out, layer, j, batch, seq):
    tt = HG_ROW_TILE
    nt = seq // tt
    x3 = x.reshape(batch, seq, D_MODEL)
    row_spec = pl.BlockSpec((None, tt, D_MODEL), lambda b, t: (b, t, 0))
    y, s_fin = pl.pallas_call(
        functools.partial(_hgrn_prompt_kernel, layer=layer, tt=tt),
        grid=(batch, nt),
        in_specs=[row_spec,
                  pl.BlockSpec((None, 1, D_MODEL), lambda b, t: (layer, 0, 0)),
                  _resident((None, D_MODEL, 4 * HG_WIDTH), lambda b, t: (j, 0, 0)),
                  pl.BlockSpec((DEPTH, HG_WIDTH), lambda b, t: (0, 0)),
                  pl.BlockSpec((None, 1, HG_DV), lambda b, t: (j, 0, 0)),
                  _resident((None, HG_WIDTH, D_MODEL), lambda b, t: (j, 0, 0))],
        out_specs=(row_spec,
                   pl.BlockSpec((None, HG_HEADS, HG_DK, HG_DV), lambda b, t: (b, 0, 0, 0))),
        out_shape=(jax.ShapeDtypeStruct((batch, seq, D_MODEL), F32),
                   jax.ShapeDtypeStruct((batch, HG_HEADS, HG_DK, HG_DV), F32)),
        scratch_shapes=[pltpu.VMEM((HG_HEADS, HG_DV, HG_DK), F32),
                        pltpu.VMEM((tt, HG_WIDTH), F32), pltpu.VMEM((tt, HG_WIDTH), F32),
                        pltpu.VMEM((tt, HG_WIDTH), F32), pltpu.VMEM((tt, HG_WIDTH), F32),
                        pltpu.VMEM((tt, HG_WIDTH), BF16), pltpu.VMEM((tt, HG_WIDTH), F32),
                        pltpu.SMEM((tt // HG_CHUNK,), jnp.int32)],
        compiler_params=_params("parallel", "arbitrary"),
        name="hgrn_prompt",
    )(x3, g.reshape(DEPTH, 1, D_MODEL), w_in, lb_logits, norm_g.reshape(-1, 1, HG_DV), w_out)
    return y.reshape(batch * seq, D_MODEL), s_fin


def _hgrn_sample_kernel(proj_ref, lbl_ref, ng_ref, s_ref, snew_ref, o_ref, *, layer, tb):
    proj = proj_ref[...]
    lb = _hg_lower_bound(lbl_ref[...], layer)
    z = proj[:, HG_WIDTH:2 * HG_WIDTH]
    e = jnp.exp(-jnp.abs(z))
    r = 1.0 / (1.0 + e)
    sig = jnp.where(z >= 0.0, r, e * r)
    f = lb + (1.0 - lb) * sig
    k = (1.0 - lb) * jnp.where(z >= 0.0, e * r, r)
    q = _silu(proj[:, :HG_WIDTH])
    v = proj[:, 2 * HG_WIDTH:3 * HG_WIDTH]
    gt = _silu(proj[:, 3 * HG_WIDTH:])
    ng = ng_ref[...]
    tok, lane = _iota2((tb, tb * HG_DV))
    spread = ((lane // HG_DV) == tok).astype(BF16)

    spread3 = jnp.concatenate([spread] * 3, axis=0)

    def columns(x, exact):
        if exact:
            return jnp.dot(jnp.concatenate(_split3(x.T), axis=1), spread3, preferred_element_type=F32)
        return jnp.dot(x.T.astype(BF16), spread, preferred_element_type=F32)

    for hh in range(HG_HEADS):
        sk = slice(hh * HG_DK, (hh + 1) * HG_DK)
        sv = slice(hh * HG_DV, (hh + 1) * HG_DV)
        f_c, k_c, q_c = columns(f[:, sk], True), columns(k[:, sk], False), columns(q[:, sk], False)
        for b in range(tb):
            blk = slice(b * HG_DV, (b + 1) * HG_DV)
            s_new = f_c[:, blk] * s_ref[b, hh] + k_c[:, blk] * v[b:b + 1, sv]
            snew_ref[b, hh] = s_new
            o = jnp.sum(q_c[:, blk] * s_new, axis=0, keepdims=True)
            o_ref[b:b + 1, sv] = _rms(o, ng) * gt[b:b + 1, sv]


def _hgrn_sample(proj, lb_logits, norm_g, state, layer, j):
    nb = proj.shape[0]
    tb = SAMPLE_TOKENS_PER_STEP
    st_spec = pl.BlockSpec((tb, HG_HEADS, HG_DK, HG_DV), lambda i: (i, 0, 0, 0))
    return pl.pallas_call(
        functools.partial(_hgrn_sample_kernel, layer=layer, tb=tb),
        grid=(nb // tb,),
        in_specs=[pl.BlockSpec((tb, 4 * HG_WIDTH), lambda i: (i, 0)),
                  pl.BlockSpec((DEPTH, HG_WIDTH), lambda i: (0, 0)),
                  pl.BlockSpec((None, 1, HG_DV), lambda i: (j, 0, 0)),
                  st_spec],
        out_specs=(st_spec, pl.BlockSpec((tb, HG_WIDTH), lambda i: (i, 0))),
        out_shape=(jax.ShapeDtypeStruct(state.shape, F32), jax.ShapeDtypeStruct((nb, HG_WIDTH), F32)),
        compiler_params=_params("parallel"),
        name="hgrn_sample",
    )(proj, lb_logits, norm_g.reshape(-1, 1, HG_DV), state)


def _xattn_prompt_kernel(x_ref, g_ref, wq_ref, k_ref, v_ref, wo_ref, y_ref):
    x = x_ref[...]
    h = _rms(x, g_ref[...]).astype(BF16)
    q = _dot(h, wq_ref[...]) * (1.0 / math.sqrt(MEM_HD))
    q = q.astype(BF16)
    outs = []
    for hh in range(MEM_HEADS):
        sl = slice(hh * MEM_HD, (hh + 1) * MEM_HD)
        s = lax.dot_general(q[:, sl], k_ref[:, sl], (((1,), (1,)), ((), ())), preferred_element_type=F32)
        p = jnp.exp(s - jnp.max(s, axis=-1, keepdims=True))
        den = jnp.sum(p, axis=-1, keepdims=True)
        o = jnp.dot(p.astype(BF16), v_ref[:, sl], preferred_element_type=F32)
        outs.append((o / den).astype(BF16))
    o = jnp.concatenate(outs, axis=-1)
    y_ref[...] = x + _dot(o, wo_ref[...])


def _xattn_prompt(x, g, w_q, mem_k, mem_v, w_o, layer, batch, seq):
    tt = XA_ROW_TILE
    row_spec = pl.BlockSpec((None, tt, D_MODEL), lambda b, t: (b, t, 0))
    kv_spec = pl.BlockSpec((None, N_MEM, D_MODEL), lambda b, t: (layer, b, 0))
    w_spec = _resident((None, D_MODEL, D_MODEL), lambda b, t: (layer, 0, 0))
    y = pl.pallas_call(
        _xattn_prompt_kernel,
        grid=(batch, seq // tt),
        in_specs=[row_spec, pl.BlockSpec((None, 1, D_MODEL), lambda b, t: (layer, 0, 0)),
                  w_spec, kv_spec, kv_spec, w_spec],
        out_specs=row_spec,
        out_shape=jax.ShapeDtypeStruct((batch, seq, D_MODEL), F32),
        compiler_params=_params("parallel", "parallel"),
        name="xattn_prompt",
    )(x.reshape(batch, seq, D_MODEL), g.reshape(DEPTH, 1, D_MODEL), w_q, mem_k, mem_v, w_o)
    return y.reshape(batch * seq, D_MODEL)


def _xattn_sample_kernel(q_ref, k_ref, v_ref, o_ref, *, tb):
    scale = 1.0 / math.sqrt(MEM_HD)
    rows = N_MEM * MEM_HEADS
    head, lane = _iota2((MEM_HEADS, rows))
    own = (lane & (MEM_HEADS - 1)) == head
    for b in range(tb):
        q = (q_ref[b] * scale).astype(BF16)
        k = k_ref[b].reshape(rows, MEM_HD).astype(BF16)
        v = v_ref[b].reshape(rows, MEM_HD).astype(BF16)
        s = lax.dot_general(q, k, (((1,), (1,)), ((), ())), preferred_element_type=F32)
        s = jnp.where(own, s, -jnp.inf)
        p = jnp.exp(s - jnp.max(s, axis=-1, keepdims=True))
        den = jnp.sum(p, axis=-1, keepdims=True)
        o_ref[b] = jnp.dot(p.astype(BF16), v, preferred_element_type=F32) / den


def _xattn_sample(q, cache_k, cache_v, layer):
    nb = q.shape[0]
    tb = XA_SAMPLE_TOKENS_PER_STEP
    kv_spec = pl.BlockSpec((None, tb, N_MEM, MEM_HEADS, MEM_HD), lambda i: (layer, i, 0, 0, 0))
    q_spec = pl.BlockSpec((tb, MEM_HEADS, MEM_HD), lambda i: (i, 0, 0))
    return pl.pallas_call(
        functools.partial(_xattn_sample_kernel, tb=tb),
        grid=(nb // tb,),
        in_specs=[q_spec, kv_spec, kv_spec],
        out_specs=q_spec,
        out_shape=jax.ShapeDtypeStruct((nb, MEM_HEADS, MEM_HD), F32),
        compiler_params=_params("parallel"),
        name="xattn_sample",
    )(q.reshape(nb, MEM_HEADS, MEM_HD), cache_k, cache_v).reshape(nb, D_MODEL)


def _ffn1(x, w, i, ffn, emit):
    if emit:
        return ffn(x, w["ffn1_norm"], w["ffn1_w_in"], w["ffn1_w_out"], i, post="emit", g2=w["mix_norm"][i],
                   u_dtype=F32)
    return ffn(x, w["ffn1_norm"], w["ffn1_w_in"], w["ffn1_w_out"], i)


def _ffn2(x, w, i, ffn):
    last = i == DEPTH - 1
    return ffn(x, w["ffn2_norm"], w["ffn2_w_in"], w["ffn2_w_out"], i,
               post="replace" if last else "none", g2=w["final_norm"] if last else None)


def _prompt_trunk(x, batch, seq, mem_k, mem_v, w, ffn):
    zero = jnp.zeros((batch, S5_GROUPS, S5_STATE), F32)
    new_re, new_im, new_hg = [], [], []
    for i in range(DEPTH):
        j = i // 2
        if i % 2 == 0:
            x, u = _ffn1(x, w, i, ffn, True)
            x, hr, hi = _s5_mixer(x, u, zero, zero, w["s5_pieces"][j], w["s5_w_glu"], j, batch, seq)
            new_re.append(hr)
            new_im.append(hi)
        else:
            x = _ffn1(x, w, i, ffn, False)
            x, sn = _hgrn_prompt(x, w["mix_norm"], w["hg_w_in"], w["hg_lb_logits"], w["hg_norm"],
                                 w["hg_w_out"], i, j, batch, seq)
            new_hg.append(sn)
        x = _xattn_prompt(x, w["xattn_norm"], w["xattn_w_q"], mem_k, mem_v, w["xattn_w_o"], i, batch, seq)
        x = _ffn2(x, w, i, ffn)
    return x, jnp.stack(new_re), jnp.stack(new_im), jnp.stack(new_hg)


def _sample_trunk(x, batch, s5_re, s5_im, hg_state, w):
    new_re, new_im, new_hg = [], [], []
    for i in range(DEPTH):
        j = i // 2
        if i % 2 == 0:
            x, u = _ffn1(x, w, i, _ffn, True)
            x, hr, hi = _s5_mixer(x, u, s5_re[j], s5_im[j], w["s5_pieces"][j], w["s5_w_glu"], j, batch, 1)
            new_re.append(hr)
            new_im.append(hi)
        else:
            x = _ffn1(x, w, i, _ffn, False)
            proj = _proj(x, w["hg_w_in"], j, g=w["mix_norm"], g_layer=i)
            sn, o = _hgrn_sample(proj, w["hg_lb_logits"], w["hg_norm"], hg_state[j], i, j)
            x = _proj(o, w["hg_w_out"], j, res=x)
            new_hg.append(sn)
        q = _proj(x, w["xattn_w_q"], i, g=w["xattn_norm"], g_layer=i)
        o = yield q, i
        x = _proj(o, w["xattn_w_o"], i, res=x)
        x = _ffn2(x, w, i, _ffn)
    return x, jnp.stack(new_re), jnp.stack(new_im), jnp.stack(new_hg)


def kernel(x_prompt, x_sample, mem_prompt, state_s5_re, state_s5_im, state_hgrn, cache_mem_k, cache_mem_v, ffn1_norm, ffn1_w_in, ffn1_w_out, mix_norm, xattn_norm, mem_norm, xattn_w_q, xattn_w_kv, xattn_w_o, ffn2_norm, ffn2_w_in, ffn2_w_out, s5_a_re, s5_a_im, s5_log_dt, s5_b_re, s5_b_im, s5_c_re, s5_c_im, s5_d, s5_w_glu, hg_w_in, hg_lb_logits, hg_norm, hg_w_out, final_norm):
    bp, seq, _ = x_prompt.shape
    bs = x_sample.shape[0]
    w = dict(ffn1_norm=ffn1_norm, ffn1_w_in=ffn1_w_in, ffn1_w_out=ffn1_w_out, mix_norm=mix_norm,
             xattn_norm=xattn_norm, xattn_w_q=xattn_w_q, xattn_w_o=xattn_w_o, ffn2_norm=ffn2_norm,
             ffn2_w_in=ffn2_w_in, ffn2_w_out=ffn2_w_out,
             s5_pieces=[_s5_pieces(s5_a_re[j], s5_a_im[j], s5_log_dt[j], s5_b_re[j], s5_b_im[j], s5_c_re[j],
                                   s5_c_im[j], s5_d[j], S5_CHUNK) for j in range(s5_a_re.shape[0])],
             s5_w_glu=s5_w_glu, hg_w_in=hg_w_in, hg_lb_logits=hg_lb_logits, hg_norm=hg_norm,
             hg_w_out=hg_w_out, final_norm=final_norm)

    mem_k, mem_v, mem_k_bf, mem_v_bf = _mem_kv(mem_prompt, mem_norm, xattn_w_kv)

    sample = _sample_trunk(x_sample.reshape(bs, D_MODEL), bs, state_s5_re, state_s5_im, state_hgrn, w)
    pending = [next(sample)]
    sample_out = []

    def resume(o):
        try:
            pending[0] = sample.send(o)
        except StopIteration as done:
            pending[0] = None
            sample_out.append(done.value)

    def ffn_with_rider(x, *args, **kwargs):
        steps = x.shape[0] // min(FFN_ROW_TILE_WITH_RIDER, x.shape[0])
        if pending[0] is None or bs % steps:
            return _ffn(x, *args, **kwargs)
        q, layer = pending[0]
        *outs, o = _ffn(x, *args, rider=(q, cache_mem_k, cache_mem_v, layer), **kwargs)
        resume(o)
        return outs[0] if len(outs) == 1 else tuple(outs)

    y_p, re_p, im_p, hg_p = _prompt_trunk(x_prompt.reshape(bp * seq, D_MODEL), bp, seq, mem_k_bf, mem_v_bf, w,
                                          ffn_with_rider)
    while pending[0] is not None:
        q, layer = pending[0]
        resume(_xattn_sample(q, cache_mem_k, cache_mem_v, layer))
    y_s, re_s, im_s, hg_s = sample_out[0]
    return (y_p.reshape(bp, seq, D_MODEL), y_s.reshape(bs, 1, D_MODEL), re_p, im_p, re_s, im_s, hg_p, hg_s,
            mem_k, mem_v)
```

```python
import functools
import math

import jax
import jax.numpy as jnp
from jax import lax
from jax.experimental import pallas as pl
from jax.experimental.pallas import tpu as pltpu

F32 = jnp.float32
BF16 = jnp.bfloat16

D_MODEL = 1024
DEPTH = 2
S5_GROUP = 16
S5_GROUPS = D_MODEL // S5_GROUP
S5_STATE = 64
S5_CHUNK = 16
S5_BLOCK_GROUPS = 8
S5_TIME_SLICES = 2
HG_DK = 128
HG_HEADS = D_MODEL // HG_DK
HG_DV = D_MODEL // HG_HEADS
HG_WIDTH = HG_HEADS * HG_DK
HG_CHUNK = 128
HG_FACTORED_MAX_DECAY = 60.0
N_MEM = 256
MEM_HEADS = 4
MEM_HD = D_MODEL // MEM_HEADS
FFN_DIM = 2816
EPS = 1e-6

V7X_VMEM_LIMIT_BYTES = 56 * 1024 * 1024

ROW_TILE = 512
FFN_CHUNK = 2816
FFN_ROW_TILE_WITH_RIDER = 256
HG_ROW_TILE = 512
XA_ROW_TILE = 1024
PROJ_ROW_TILE = 1024
S5_GROUPS_PER_STEP = 16
SAMPLE_TOKENS_PER_STEP = 8
XA_SAMPLE_TOKENS_PER_STEP = 4


def _params(*semantics):
    return pltpu.CompilerParams(dimension_semantics=semantics,
                                vmem_limit_bytes=V7X_VMEM_LIMIT_BYTES)


def _resident(shape, index_map):
    return pl.BlockSpec(shape, index_map, pipeline_mode=pl.Buffered(1))


def _rms(x, g):
    ms = jnp.mean(x * x, axis=-1, keepdims=True)
    return x * lax.rsqrt(ms + EPS) * g


def _sigmoid(x):
    return 1.0 / (1.0 + jnp.exp(-x))


def _silu(x):
    return x * _sigmoid(x)


def _gelu_tanh(x):
    c = math.sqrt(2.0 / math.pi)
    return 0.5 * x * (1.0 + jnp.tanh(c * (x + 0.044715 * (x * x * x))))


def _dot(a, w):
    return jnp.dot(a, w.astype(BF16), preferred_element_type=F32)


def _split3(x):
    hi = x.astype(BF16)
    r1 = x - hi.astype(F32)
    mid = r1.astype(BF16)
    lo = (r1 - mid.astype(F32)).astype(BF16)
    return hi, mid, lo


def _ffn_kernel(x_ref, g_ref, win_ref, wout_ref, *rest, post, rider_tokens):
    rest = list(rest)
    g2_ref = rest.pop(0) if post != "none" else None
    if rider_tokens:
        q_ref, k_ref, v_ref = rest[:3]
        rest = rest[3:]
        _xattn_sample_vpu(q_ref, k_ref, v_ref, rest.pop(), tb=rider_tokens)
    x = x_ref[...]
    h = _rms(x, g_ref[...]).astype(BF16)
    acc = jnp.zeros_like(x)
    for c in range(FFN_DIM // FFN_CHUNK):
        lo = c * FFN_CHUNK
        gate = _dot(h, win_ref[:, lo:lo + FFN_CHUNK])
        up = _dot(h, win_ref[:, FFN_DIM + lo:FFN_DIM + lo + FFN_CHUNK])
        act = (_silu(gate) * up).astype(BF16)
        acc = acc + _dot(act, wout_ref[lo:lo + FFN_CHUNK, :])
    y = x + 0.5 * acc
    y_ref = rest[0]
    if post == "replace":
        y_ref[...] = _rms(y, g2_ref[...])
    else:
        y_ref[...] = y
        if post == "emit":
            rest[1][...] = _rms(y, g2_ref[...]).astype(rest[1].dtype)


def _ffn(x, g, w_in, w_out, layer, post="none", g2=None, u_dtype=BF16, rider=None):
    rows = x.shape[0]
    tm = min(ROW_TILE if rider is None else FFN_ROW_TILE_WITH_RIDER, rows)
    steps = rows // tm
    row_spec = pl.BlockSpec((tm, D_MODEL), lambda i: (i, 0))
    vec_spec = pl.BlockSpec((1, D_MODEL), lambda i: (0, 0))
    in_specs = [row_spec,
                pl.BlockSpec((None, 1, D_MODEL), lambda i: (layer, 0, 0)),
                _resident((None, D_MODEL, 2 * FFN_DIM), lambda i: (layer, 0, 0)),
                _resident((None, FFN_DIM, D_MODEL), lambda i: (layer, 0, 0))]
    args = [x, g.reshape(DEPTH, 1, D_MODEL), w_in, w_out]
    out_shape = [jax.ShapeDtypeStruct((rows, D_MODEL), F32)]
    out_specs = [row_spec]
    if post != "none":
        in_specs.append(vec_spec)
        args.append(g2.reshape(1, D_MODEL))
    if post == "emit":
        out_shape.append(jax.ShapeDtypeStruct((rows, D_MODEL), u_dtype))
        out_specs.append(row_spec)
    tb = 0
    if rider is not None:
        q, cache_k, cache_v, r_layer = rider
        nb = q.shape[0]
        assert nb % steps == 0
        tb = nb // steps
        kv_spec = pl.BlockSpec((None, tb, N_MEM, MEM_HEADS, MEM_HD), lambda i: (r_layer, i, 0, 0, 0))
        q_spec = pl.BlockSpec((tb, MEM_HEADS, MEM_HD), lambda i: (i, 0, 0))
        in_specs += [q_spec, kv_spec, kv_spec]
        args += [q.reshape(nb, MEM_HEADS, MEM_HD), cache_k, cache_v]
        out_shape.append(jax.ShapeDtypeStruct((nb, MEM_HEADS, MEM_HD), F32))
        out_specs.append(q_spec)
    outs = list(pl.pallas_call(
        functools.partial(_ffn_kernel, post=post, rider_tokens=tb),
        grid=(steps,),
        in_specs=in_specs, out_specs=tuple(out_specs), out_shape=tuple(out_shape),
        compiler_params=_params("parallel"),
        name="ffn",
    )(*args))
    if rider is not None:
        outs[-1] = outs[-1].reshape(-1, D_MODEL)
    return outs[0] if len(outs) == 1 else tuple(outs)


def _proj_kernel(*refs, norm, glu, residual):
    refs = list(refs)
    x_ref = refs.pop(0)
    g_ref = refs.pop(0) if norm else None
    w_ref = refs.pop(0)
    res_ref = refs.pop(0) if residual else None
    (o_ref,) = refs
    x = x_ref[...]
    if norm:
        x = _rms(x, g_ref[...])
    y = _dot(x.astype(BF16), w_ref[...])
    if glu:
        half = y.shape[-1] // 2
        y = y[:, :half] * _sigmoid(y[:, half:])
    if residual:
        y = y + res_ref[...]
    o_ref[...] = y


def _proj(x, w, layer, g=None, g_layer=0, glu=False, res=None):
    rows, kdim = x.shape
    ndim = w.shape[-1]
    nout = ndim // 2 if glu else ndim
    tm = min(PROJ_ROW_TILE, rows)
    in_specs = [pl.BlockSpec((tm, kdim), lambda i: (i, 0))]
    args = [x]
    if g is not None:
        in_specs.append(pl.BlockSpec((None, 1, kdim), lambda i: (g_layer, 0, 0)))
        args.append(g.reshape(g.shape[0], 1, kdim))
    in_specs.append(_resident((None, kdim, ndim), lambda i: (layer, 0, 0)))
    args.append(w)
    if res is not None:
        in_specs.append(pl.BlockSpec((tm, nout), lambda i: (i, 0)))
        args.append(res)
    return pl.pallas_call(
        functools.partial(_proj_kernel, norm=g is not None, glu=glu, residual=res is not None),
        grid=(rows // tm,),
        in_specs=in_specs,
        out_specs=pl.BlockSpec((tm, nout), lambda i: (i, 0)),
        out_shape=jax.ShapeDtypeStruct((rows, nout), F32),
        compiler_params=_params("parallel"),
        name="proj",
    )(*args)


def _memkv_kernel(x_ref, g_ref, w_ref, k_ref, v_ref, kb_ref, vb_ref):
    h = _rms(x_ref[...], g_ref[...]).astype(BF16)
    y = _dot(h, w_ref[...])
    k, v = y[:, :D_MODEL], y[:, D_MODEL:]
    k_ref[...] = k.reshape(k_ref.shape)
    v_ref[...] = v.reshape(v_ref.shape)
    kb_ref[...] = k.astype(BF16)
    vb_ref[...] = v.astype(BF16)


def _mem_kv(mem, g, w_kv):
    batch = mem.shape[0]
    rows = batch * N_MEM
    nb = max(1, min(ROW_TILE, rows) // N_MEM)
    tm = nb * N_MEM
    out5 = jax.ShapeDtypeStruct((DEPTH, batch, N_MEM, MEM_HEADS, MEM_HD), F32)
    out2 = jax.ShapeDtypeStruct((DEPTH, rows, D_MODEL), BF16)
    spec5 = pl.BlockSpec((None, nb, N_MEM, MEM_HEADS, MEM_HD), lambda l, i: (l, i, 0, 0, 0))
    spec2 = pl.BlockSpec((None, tm, D_MODEL), lambda l, i: (l, i, 0))
    return pl.pallas_call(
        _memkv_kernel,
        grid=(DEPTH, rows // tm),
        in_specs=[pl.BlockSpec((tm, D_MODEL), lambda l, i: (i, 0)),
                  pl.BlockSpec((None, 1, D_MODEL), lambda l, i: (l, 0, 0)),
                  pl.BlockSpec((None, D_MODEL, 2 * D_MODEL), lambda l, i: (l, 0, 0))],
        out_specs=(spec5, spec5, spec2, spec2), out_shape=(out5, out5, out2, out2),
        compiler_params=_params("parallel", "parallel"),
        name="mem_kv",
    )(mem.reshape(rows, D_MODEL), g.reshape(DEPTH, 1, D_MODEL), w_kv)


def _s5_pieces(a_re, a_im, log_dt, b_re, b_im, c_re, c_im, d, steps):
    G, P, GC = S5_GROUPS, S5_STATE, S5_GROUP
    L = steps
    dt = jnp.exp(log_dt)[:, None]
    xr, xi = a_re * dt, a_im * dt
    j = jnp.arange(L + 1, dtype=F32)[:, None, None]
    mag = jnp.exp(xr[None] * j)
    pw_re, pw_im = mag * jnp.cos(xi[None] * j), mag * jnp.sin(xi[None] * j)
    nr, ni = pw_re[1] - 1.0, pw_im[1]
    den = a_re * a_re + a_im * a_im
    fr, fi = ((nr * a_re + ni * a_im) / den)[:, None, :], ((ni * a_re - nr * a_im) / den)[:, None, :]
    bt_re, bt_im = b_re.transpose(0, 2, 1), b_im.transpose(0, 2, 1)
    bb_re = fr * bt_re - fi * bt_im
    bb_im = fr * bt_im + fi * bt_re

    def times_c(p_re, p_im):
        p_re, p_im = p_re[:, :, None, :], p_im[:, :, None, :]
        return p_re * c_re[None] - p_im * c_im[None], p_re * c_im[None] + p_im * c_re[None]

    w_re, w_im = times_c(pw_re[:L], pw_im[:L])

    def over_state(b, w_):
        w_ = w_.transpose(1, 0, 2, 3).reshape(G, L * GC, P)
        return jnp.einsum("gip,gnp->gin", b, w_, precision=lax.Precision.HIGHEST)

    kern = (over_state(bb_re, w_re) - over_state(bb_im, w_im)).reshape(G, GC, L, GC).transpose(2, 0, 1, 3)
    kern = kern.at[0].add(d[:, :, None] * jnp.eye(GC, dtype=F32)[None])
    return dict(kern=kern, pw=(pw_re, pw_im), bbar=(bb_re, bb_im), c=(c_re, c_im))


def _s5_group_mats(pieces):
    (pw_re, pw_im), (c_re, c_im) = pieces["pw"], pieces["c"]
    l_re, l_im = pw_re[1][:, None, :], pw_im[1][:, None, :]
    p_re = l_re * c_re - l_im * c_im
    p_im = -(l_re * c_im + l_im * c_re)
    return (pieces["kern"][0], *pieces["bbar"], p_re.transpose(0, 2, 1), p_im.transpose(0, 2, 1), l_re, l_im)


def _s5_block_mats(pieces):
    kern = pieces["kern"]
    pw_re, pw_im = pieces["pw"]
    L = kern.shape[0]
    GB = S5_BLOCK_GROUPS
    NB = S5_GROUPS // GB
    GC, P = S5_GROUP, S5_STATE
    taps = kern.astype(BF16).reshape(L, NB, GB * GC, GC).transpose(1, 0, 2, 3)
    powers = jnp.stack([pw_re, pw_im]).reshape(2, L + 1, NB, GB, P).transpose(2, 0, 1, 3, 4)
    powers = powers.reshape(NB, 2, (L + 1) * GB, P)
    per_block = lambda pair: jnp.stack(pair).reshape(2, NB, GB * GC, P).transpose(1, 0, 2, 3)
    lam = jnp.concatenate([pw_re[L].reshape(NB, 1, GB * P), pw_im[L].reshape(NB, 1, GB * P)], axis=2)
    return taps, powers, per_block(pieces["bbar"]), per_block(pieces["c"]), lam


def _s5_kernel(u_ref, t_ref, mre_ref, mim_ref, pre_ref, pim_ref, lre_ref, lim_ref, h0re_ref, h0im_ref,
               y_ref, hre_ref, him_ref, inj_re, inj_im, hs_re, hs_im, *, n_chunks, rb, gps, precision):
    def mm(a, b):
        return jnp.dot(a, b, preferred_element_type=F32, precision=precision)

    for g in range(gps):
        u = u_ref[g]
        inj_re[g] = mm(u, mre_ref[g])
        inj_im[g] = mm(u, mim_ref[g])

    lam_re = [jnp.broadcast_to(lre_ref[g], (rb, S5_STATE)) for g in range(gps)]
    lam_im = [jnp.broadcast_to(lim_ref[g], (rb, S5_STATE)) for g in range(gps)]

    def step(k, carry):
        rows = pl.ds(pl.multiple_of(k * rb, rb), rb)
        nxt = []
        for g in range(gps):
            hr, hi = carry[2 * g], carry[2 * g + 1]
            hs_re[g, rows, :] = hr
            hs_im[g, rows, :] = hi
            nxt.append(lam_re[g] * hr - lam_im[g] * hi + inj_re[g, rows, :])
            nxt.append(lam_re[g] * hi + lam_im[g] * hr + inj_im[g, rows, :])
        return tuple(nxt)

    init = []
    for g in range(gps):
        init += [h0re_ref[g], h0im_ref[g]]
    fin = lax.fori_loop(0, n_chunks, step, tuple(init))

    for g in range(gps):
        hre_ref[g] = fin[2 * g]
        him_ref[g] = fin[2 * g + 1]
        dt = u_ref.dtype
        y = (mm(u_ref[g], t_ref[g]) + mm(hs_re[g].astype(dt), pre_ref[g])
             + mm(hs_im[g].astype(dt), pim_ref[g]))
        y_ref[g] = _gelu_tanh(y).astype(y_ref.dtype)


def _s5_core(u, mats, h0_re, h0_im, n_chunks, rb, precision, y_dtype):
    G, R, W = u.shape
    P = S5_STATE
    gps = S5_GROUPS_PER_STEP
    tmat, m_re, m_im, p_re, p_im, l_re, l_im = mats

    def spec(a, b):
        return pl.BlockSpec((gps, a, b), lambda i: (i, 0, 0))

    st = jax.ShapeDtypeStruct((G, rb, P), F32)
    scr = lambda: pltpu.VMEM((gps, R, P), F32)
    return pl.pallas_call(
        functools.partial(_s5_kernel, n_chunks=n_chunks, rb=rb, gps=gps, precision=precision),
        grid=(G // gps,),
        in_specs=[spec(R, W), spec(W, W), spec(W, P), spec(W, P), spec(P, W), spec(P, W),
                  spec(1, P), spec(1, P), spec(rb, P), spec(rb, P)],
        out_specs=(spec(R, W), spec(rb, P), spec(rb, P)),
        out_shape=(jax.ShapeDtypeStruct((G, R, W), y_dtype), st, st),
        scratch_shapes=[scr(), scr(), scr(), scr()],
        compiler_params=_params("parallel"),
        name="s5_core",
    )(u, tmat, m_re, m_im, p_re, p_im, l_re, l_im, h0_re, h0_im)


def _iota2(shape):
    return lax.broadcasted_iota(jnp.int32, shape, 0), lax.broadcasted_iota(jnp.int32, shape, 1)


def _s5_expand(taps_ref, pw_ref, bb_ref, c_ref, w2_ref, m_ref, p_ref):
    L, GC, P = S5_CHUNK, S5_GROUP, S5_STATE
    GB = S5_BLOCK_GROUPS
    W, HS = GB * GC, GB * P
    gc_bits = GC.bit_length() - 1

    m_ref[...] = jnp.zeros_like(m_ref)
    p_ref[...] = jnp.zeros_like(p_ref)
    for g in range(GB):
        chans = slice(g * GC, (g + 1) * GC)
        b_re, b_im, c_re, c_im = bb_ref[0, chans, :], bb_ref[1, chans, :], c_ref[0, chans, :], c_ref[1, chans, :]
        for l in range(L):
            rows = slice((l * GB + g) * GC, (l * GB + g + 1) * GC)
            jm, jp = (L - 1 - l) * GB + g, (l + 1) * GB + g
            a_re, a_im = pw_ref[0, jm:jm + 1, :], pw_ref[1, jm:jm + 1, :]
            q_re, q_im = pw_ref[0, jp:jp + 1, :], pw_ref[1, jp:jp + 1, :]
            m_ref[rows, g * P:(g + 1) * P] = (a_re * b_re - a_im * b_im).astype(BF16)
            m_ref[rows, HS + g * P:HS + (g + 1) * P] = (a_re * b_im + a_im * b_re).astype(BF16)
            p_ref[rows, g * P:(g + 1) * P] = (q_re * c_re - q_im * c_im).astype(BF16)
            p_ref[rows, HS + g * P:HS + (g + 1) * P] = (-(q_re * c_im + q_im * c_re)).astype(BF16)

    r, c = _iota2((GC, W))
    rep_k = ((c & (GC - 1)) == r).astype(BF16)
    r, c = _iota2((W, W))
    mask_k = (r >> gc_bits) == (c >> gc_bits)
    zero = jnp.zeros((W, W), BF16)
    w2_ref[L * W:, :W] = zero
    w2_ref[:W, W:] = zero
    for i in range(L):
        kb = jnp.where(mask_k, jnp.dot(taps_ref[L - 1 - i], rep_k, preferred_element_type=F32), 0.0).astype(BF16)
        w2_ref[i * W:(i + 1) * W, :W] = kb
        w2_ref[(i + 1) * W:(i + 2) * W, W:] = kb


def _s5_seq_kernel(u_ref, taps_ref, pw_ref, bb_ref, c_ref, lam_ref, h0_ref, y_ref, hfin_ref,
                   w2_ref, m_ref, p_ref, lhs_scr, inj_scr, hs_scr, h_scr, *, batch, nck):
    L, W = S5_CHUNK, S5_BLOCK_GROUPS * S5_GROUP
    NT = S5_BLOCK_GROUPS * S5_STATE // W

    @pl.when(pl.program_id(1) == 0)
    def _():
        h_scr[...] = h0_ref[...]
        _s5_expand(taps_ref, pw_ref, bb_ref, c_ref, w2_ref, m_ref, p_ref)

    for b in range(batch):
        for l in range(L):
            lhs_scr[b * nck:(b + 1) * nck, l * W:(l + 1) * W] = u_ref[b, pl.ds(l, nck, stride=L), :].astype(BF16)

    def swap_major(x, a, b):
        return jnp.swapaxes(x.reshape(a, b, W), 0, 1).reshape(a * b, W)

    inj = jnp.dot(lhs_scr[...], m_ref[...], preferred_element_type=F32)
    for t in range(2 * NT):
        inj_scr[t] = swap_major(inj[:, t * W:(t + 1) * W], batch, nck)

    lam = [jnp.broadcast_to(lam_ref[:, t * W:(t + 1) * W], (batch, W)) for t in range(2 * NT)]

    def step(k, h):
        rows = pl.ds(pl.multiple_of(k * batch, batch), batch)
        nxt_re, nxt_im = [], []
        for t in range(NT):
            hr, hi = h[t], h[NT + t]
            hs_scr[t, rows, :] = hr
            hs_scr[NT + t, rows, :] = hi
            nxt_re.append(lam[t] * hr - lam[NT + t] * hi + inj_scr[t, rows, :])
            nxt_im.append(lam[t] * hi + lam[NT + t] * hr + inj_scr[NT + t, rows, :])
        return tuple(nxt_re + nxt_im)

    h = lax.fori_loop(0, nck, step, tuple(h_scr[:, t * W:(t + 1) * W] for t in range(2 * NT)))
    for t in range(2 * NT):
        h_scr[:, t * W:(t + 1) * W] = h[t]
    hfin_ref[...] = h_scr[...]

    hs = jnp.concatenate([swap_major(hs_scr[t], nck, batch) for t in range(2 * NT)], axis=-1).astype(BF16)
    for pr in range(L // 2):
        kk = (2 * pr + 2) * W
        y = (jnp.dot(lhs_scr[:, :kk], w2_ref[(L - 1 - 2 * pr) * W:, :], preferred_element_type=F32)
             + lax.dot_general(hs, p_ref[2 * pr * W:(2 * pr + 2) * W, :], (((1,), (1,)), ((), ())),
                               preferred_element_type=F32))
        y = _gelu_tanh(y)
        for s in range(2):
            for b in range(batch):
                y_ref[b, pl.ds(2 * pr + s, nck, stride=L), :] = y[b * nck:(b + 1) * nck, s * W:(s + 1) * W]


def _s5_seq(u, mats, h0, batch, seq):
    taps, powers, bbar, cmat, lam = mats
    L, W = S5_CHUNK, S5_BLOCK_GROUPS * S5_GROUP
    NB = S5_GROUPS // S5_BLOCK_GROUPS
    HS = S5_BLOCK_GROUPS * S5_STATE
    SW = 2 * HS
    ts = seq // S5_TIME_SLICES
    nck = ts // L
    rows = batch * nck
    u_spec = pl.BlockSpec((batch, ts, W), lambda i, t: (0, t, i))
    h_spec = pl.BlockSpec((None, batch, SW), lambda i, t: (i, 0, 0))

    def w_spec(*dims):
        return pl.BlockSpec((None,) + dims, lambda i, t: (i,) + (0,) * len(dims))

    return pl.pallas_call(
        functools.partial(_s5_seq_kernel, batch=batch, nck=nck),
        grid=(NB, S5_TIME_SLICES),
        in_specs=[u_spec, w_spec(L, W, S5_GROUP), w_spec(2, (L + 1) * S5_BLOCK_GROUPS, S5_STATE),
                  w_spec(2, W, S5_STATE), w_spec(2, W, S5_STATE), w_spec(1, SW), h_spec],
        out_specs=(u_spec, h_spec),
        out_shape=(jax.ShapeDtypeStruct((batch, seq, D_MODEL), F32), jax.ShapeDtypeStruct(h0.shape, F32)),
        scratch_shapes=[pltpu.VMEM(((L + 1) * W, 2 * W), BF16), pltpu.VMEM((L * W, SW), BF16),
                        pltpu.VMEM((L * W, SW), BF16),
                        pltpu.VMEM((rows, L * W), BF16), pltpu.VMEM((SW // W, rows, W), F32),
                        pltpu.VMEM((SW // W, rows, W), F32), pltpu.VMEM((batch, SW), F32)],
        compiler_params=_params("parallel", "arbitrary"),
        name="s5_seq",
    )(u, taps, powers, bbar, cmat, lam, h0)


def _s5_mixer(x, u, h0_re, h0_im, pieces, w_glu, j, batch, seq):
    G, GC, P = S5_GROUPS, S5_GROUP, S5_STATE
    if seq > 1:
        NB, HS = G // S5_BLOCK_GROUPS, S5_BLOCK_GROUPS * P
        mats = _s5_block_mats(pieces)
        to_blocks = lambda h: h.reshape(batch, NB, HS).transpose(1, 0, 2)
        h0 = jnp.concatenate([to_blocks(h0_re), to_blocks(h0_im)], axis=-1)
        y, hfin = _s5_seq(u.reshape(batch, seq, D_MODEL), mats, h0, batch, seq)
        y = y.reshape(batch * seq, D_MODEL)
        from_blocks = lambda h: h.transpose(1, 0, 2).reshape(batch, G, P)
        hre, him = from_blocks(hfin[..., :HS]), from_blocks(hfin[..., HS:])
    else:
        mats = _s5_group_mats(pieces)
        mats = tuple(m.astype(BF16) for m in mats[:5]) + mats[5:]
        ug = u.reshape(batch, G, GC).transpose(1, 0, 2).astype(BF16)
        yg, hre, him = _s5_core(ug, mats, h0_re.transpose(1, 0, 2), h0_im.transpose(1, 0, 2),
                                1, batch, None, F32)
        y = yg.transpose(1, 0, 2).reshape(batch, D_MODEL)
        hre, him = hre.transpose(1, 0, 2), him.transpose(1, 0, 2)
    x = _proj(y, w_glu, j, glu=True, res=x)
    return x, hre, him


def _hg_lower_bound(logits, layer):
    m = jnp.max(logits, axis=0, keepdims=True)
    e = jnp.exp(logits - m)
    sm = e / jnp.sum(e, axis=0, keepdims=True)
    return jnp.sum(sm[:layer + 1], axis=0, keepdims=True) - sm[0:1]


def _hg_gates(z, lb):
    e = jnp.exp(-jnp.abs(z))
    r = 1.0 / (1.0 + e)
    er = e * r
    pos = z >= 0.0
    f = lb + (1.0 - lb) * jnp.where(pos, r, er)
    logf = jnp.where(f > 0.0, jnp.log(f), z)
    k = (1.0 - lb) * jnp.where(pos, er, r)
    return logf, k


def _hgrn_prompt_kernel(x_ref, g_ref, win_ref, lbl_ref, ng_ref, wout_ref, y_ref, sfin_ref,
                        s_scr, q_scr, k_scr, v_scr, gt_scr, o_scr, beta_scr, safe_scr, *, layer, tt):
    t = pl.program_id(1)
    C = HG_CHUNK

    @pl.when(t == 0)
    def _():
        s_scr[...] = jnp.zeros_like(s_scr)

    x = x_ref[...]
    h = _rms(x, g_ref[...]).astype(BF16)
    def proj(i):
        return _dot(h, win_ref[:, i * HG_WIDTH:(i + 1) * HG_WIDTH])

    lb = _hg_lower_bound(lbl_ref[...], layer)
    logf, kk = _hg_gates(proj(1), lb)
    k_scr[...] = kk
    q_scr[...] = _silu(proj(0))
    gt_scr[...] = _silu(proj(3))
    v_scr[...] = proj(2)

    row, col = _iota2((C, C))
    causal = row >= col
    tri = causal.astype(BF16)
    ng = ng_ref[...]
    nt_dims = (((1,), (1,)), ((), ()))
    tn_dims = (((0,), (0,)), ((), ()))

    def finish_head(hh, rows, q_dec, k_dec, o_intra, btot_h):
        sv = slice(hh * HG_DV, (hh + 1) * HG_DV)
        st = s_scr[hh]
        o = o_intra + lax.dot_general(q_dec.astype(BF16), st.astype(BF16), nt_dims, preferred_element_type=F32)
        kv_t = lax.dot_general(v_scr[rows, sv].astype(BF16), k_dec.astype(BF16), tn_dims,
                               preferred_element_type=F32)
        return (_rms(o, ng) * gt_scr[rows, sv]).astype(BF16), st * jnp.exp(btot_h) + kv_t

    def store_heads(rows, results):
        o_scr[rows, :] = jnp.concatenate([o for o, _ in results], axis=-1)
        for hh, (_, s_new) in enumerate(results):
            s_scr[hh] = s_new

    for ci in range(tt // C):
        parts = _split3(logf[ci * C:(ci + 1) * C])
        beta = sum(jnp.dot(tri, p, preferred_element_type=F32) for p in parts)
        beta_scr[ci * C:(ci + 1) * C, :] = beta
        mid = beta[C // 2 - 1:C // 2, :]
        spread = jnp.maximum(jnp.max(-mid), jnp.max(mid - beta[C - 1:C, :]))
        safe_scr[ci] = (spread <= HG_FACTORED_MAX_DECAY).astype(jnp.int32)

    def chunk_step(c, carry):
        rows = pl.ds(pl.multiple_of(c * C, C), C)
        base = pl.multiple_of(c * C, C)
        btot = beta_scr[pl.ds(base + (C - 8), 8), :][7:8]
        mid = beta_scr[pl.ds(base + (C // 2 - 8), 8), :][7:8]
        safe = safe_scr[c] == 1

        @pl.when(safe)
        def _():
            e_mid = jnp.exp(mid)
            e_tot = jnp.exp(btot - mid)
            results = []
            for hh in range(HG_HEADS):
                sk = slice(hh * HG_DK, (hh + 1) * HG_DK)
                d = beta_scr[rows, sk] - mid[:, sk]
                q_mid = q_scr[rows, sk] * jnp.exp(d)
                k_mid = k_scr[rows, sk] * jnp.exp(-d)
                att = lax.dot_general(q_mid.astype(BF16), k_mid.astype(BF16), nt_dims, preferred_element_type=F32)
                att = jnp.where(causal, att, 0.0).astype(BF16)
                o_intra = jnp.dot(att, v_scr[rows, hh * HG_DV:(hh + 1) * HG_DV].astype(BF16),
                                  preferred_element_type=F32)
                results.append(finish_head(hh, rows, q_mid * e_mid[:, sk], k_mid * e_tot[:, sk], o_intra,
                                           btot[:, sk]))
            store_heads(rows, results)

        @pl.when(jnp.logical_not(safe))
        def _():
            t_idx = lax.broadcasted_iota(jnp.int32, (C, 1), 0)
            results = []
            for hh in range(HG_HEADS):
                sk = slice(hh * HG_DK, (hh + 1) * HG_DK)
                sv = slice(hh * HG_DV, (hh + 1) * HG_DV)
                b = beta_scr[rows, sk]
                q = q_scr[rows, sk]

                def key_step(s8, acc):
                    off = pl.multiple_of(s8 * 8, 8)
                    b_keys = beta_scr[pl.ds(base + off, 8), sk]
                    k_keys = k_scr[pl.ds(base + off, 8), sk]
                    v_keys = v_scr[pl.ds(base + off, 8), sv]
                    for i in range(8):
                        w = jnp.exp(jnp.minimum(b - b_keys[i:i + 1], 0.0))
                        a = jnp.sum(q * k_keys[i:i + 1] * w, axis=-1, keepdims=True)
                        a = jnp.where(t_idx >= off + i, a, 0.0)
                        acc = acc + a * v_keys[i:i + 1]
                    return acc

                o_intra = lax.fori_loop(0, C // 8, key_step, jnp.zeros((C, HG_DV), F32))
                results.append(finish_head(hh, rows, q * jnp.exp(b), k_scr[rows, sk] * jnp.exp(btot[:, sk] - b),
                                           o_intra, btot[:, sk]))
            store_heads(rows, results)

        return carry

    lax.fori_loop(0, tt // C, chunk_step, 0)
    y_ref[...] = x + _dot(o_scr[...], wout_ref[...])

    @pl.when(t == pl.num_programs(1) - 1)
    def _():
        for hh in range(HG_HEADS):
            sfin_ref[hh] = s_scr[hh].T


def _hgrn_prompt(x, g, w_in, lb_logits, norm_g, w_out, layer, j, batch, seq):
    tt = HG_ROW_TILE
    nt = seq // tt
    x3 = x.reshape(batch, seq, D_MODEL)
    row_spec = pl.BlockSpec((None, tt, D_MODEL), lambda b, t: (b, t, 0))
    y, s_fin = pl.pallas_call(
        functools.partial(_hgrn_prompt_kernel, layer=layer, tt=tt),
        grid=(batch, nt),
        in_specs=[row_spec,
                  pl.BlockSpec((None, 1, D_MODEL), lambda b, t: (layer, 0, 0)),
                  _resident((None, D_MODEL, 4 * HG_WIDTH), lambda b, t: (j, 0, 0)),
                  pl.BlockSpec((DEPTH, HG_WIDTH), lambda b, t: (0, 0)),
                  pl.BlockSpec((None, 1, HG_DV), lambda b, t: (j, 0, 0)),
                  _resident((None, HG_WIDTH, D_MODEL), lambda b, t: (j, 0, 0))],
        out_specs=(row_spec,
                   pl.BlockSpec((None, HG_HEADS, HG_DK, HG_DV), lambda b, t: (b, 0, 0, 0))),
        out_shape=(jax.ShapeDtypeStruct((batch, seq, D_MODEL), F32),
                   jax.ShapeDtypeStruct((batch, HG_HEADS, HG_DK, HG_DV), F32)),
        scratch_shapes=[pltpu.VMEM((HG_HEADS, HG_DV, HG_DK), F32),
                        pltpu.VMEM((tt, HG_WIDTH), F32), pltpu.VMEM((tt, HG_WIDTH), F32),
                        pltpu.VMEM((tt, HG_WIDTH), F32), pltpu.VMEM((tt, HG_WIDTH), F32),
                        pltpu.VMEM((tt, HG_WIDTH), BF16), pltpu.VMEM((tt, HG_WIDTH), F32),
                        pltpu.SMEM((tt // HG_CHUNK,), jnp.int32)],
        compiler_params=_params("parallel", "arbitrary"),
        name="hgrn_prompt",
    )(x3, g.reshape(DEPTH, 1, D_MODEL), w_in, lb_logits, norm_g.reshape(-1, 1, HG_DV), w_out)
    return y.reshape(batch * seq, D_MODEL), s_fin


def _hgrn_sample_kernel(proj_ref, lbl_ref, ng_ref, s_ref, snew_ref, o_ref, *, layer, tb):
    proj = proj_ref[...]
    lb = _hg_lower_bound(lbl_ref[...], layer)
    z = proj[:, HG_WIDTH:2 * HG_WIDTH]
    e = jnp.exp(-jnp.abs(z))
    r = 1.0 / (1.0 + e)
    sig = jnp.where(z >= 0.0, r, e * r)
    f = lb + (1.0 - lb) * sig
    k = (1.0 - lb) * jnp.where(z >= 0.0, e * r, r)
    q = _silu(proj[:, :HG_WIDTH])
    v = proj[:, 2 * HG_WIDTH:3 * HG_WIDTH]
    gt = _silu(proj[:, 3 * HG_WIDTH:])
    ng = ng_ref[...]
    tok, lane = _iota2((tb, tb * HG_DV))
    spread = ((lane // HG_DV) == tok).astype(BF16)

    spread3 = jnp.concatenate([spread] * 3, axis=0)

    def columns(x, exact):
        if exact:
            return jnp.dot(jnp.concatenate(_split3(x.T), axis=1), spread3, preferred_element_type=F32)
        return jnp.dot(x.T.astype(BF16), spread, preferred_element_type=F32)

    for hh in range(HG_HEADS):
        sk = slice(hh * HG_DK, (hh + 1) * HG_DK)
        sv = slice(hh * HG_DV, (hh + 1) * HG_DV)
        f_c, k_c, q_c = columns(f[:, sk], True), columns(k[:, sk], False), columns(q[:, sk], False)
        for b in range(tb):
            blk = slice(b * HG_DV, (b + 1) * HG_DV)
            s_new = f_c[:, blk] * s_ref[b, hh] + k_c[:, blk] * v[b:b + 1, sv]
            snew_ref[b, hh] = s_new
            o = jnp.sum(q_c[:, blk] * s_new, axis=0, keepdims=True)
            o_ref[b:b + 1, sv] = _rms(o, ng) * gt[b:b + 1, sv]


def _hgrn_sample(proj, lb_logits, norm_g, state, layer, j):
    nb = proj.shape[0]
    tb = SAMPLE_TOKENS_PER_STEP
    st_spec = pl.BlockSpec((tb, HG_HEADS, HG_DK, HG_DV), lambda i: (i, 0, 0, 0))
    return pl.pallas_call(
        functools.partial(_hgrn_sample_kernel, layer=layer, tb=tb),
        grid=(nb // tb,),
        in_specs=[pl.BlockSpec((tb, 4 * HG_WIDTH), lambda i: (i, 0)),
                  pl.BlockSpec((DEPTH, HG_WIDTH), lambda i: (0, 0)),
                  pl.BlockSpec((None, 1, HG_DV), lambda i: (j, 0, 0)),
                  st_spec],
        out_specs=(st_spec, pl.BlockSpec((tb, HG_WIDTH), lambda i: (i, 0))),
        out_shape=(jax.ShapeDtypeStruct(state.shape, F32), jax.ShapeDtypeStruct((nb, HG_WIDTH), F32)),
        compiler_params=_params("parallel"),
        name="hgrn_sample",
    )(proj, lb_logits, norm_g.reshape(-1, 1, HG_DV), state)


def _xattn_prompt_kernel(x_ref, g_ref, wq_ref, k_ref, v_ref, wo_ref, y_ref):
    x = x_ref[...]
    h = _rms(x, g_ref[...]).astype(BF16)
    q = _dot(h, wq_ref[...]) * (1.0 / math.sqrt(MEM_HD))
    q = q.astype(BF16)
    outs = []
    for hh in range(MEM_HEADS):
        sl = slice(hh * MEM_HD, (hh + 1) * MEM_HD)
        s = lax.dot_general(q[:, sl], k_ref[:, sl], (((1,), (1,)), ((), ())), preferred_element_type=F32)
        p = jnp.exp(s - jnp.max(s, axis=-1, keepdims=True))
        den = jnp.sum(p, axis=-1, keepdims=True)
        o = jnp.dot(p.astype(BF16), v_ref[:, sl], preferred_element_type=F32)
        outs.append((o / den).astype(BF16))
    o = jnp.concatenate(outs, axis=-1)
    y_ref[...] = x + _dot(o, wo_ref[...])


def _xattn_prompt(x, g, w_q, mem_k, mem_v, w_o, layer, batch, seq):
    tt = XA_ROW_TILE
    row_spec = pl.BlockSpec((None, tt, D_MODEL), lambda b, t: (b, t, 0))
    kv_spec = pl.BlockSpec((None, N_MEM, D_MODEL), lambda b, t: (layer, b, 0))
    w_spec = _resident((None, D_MODEL, D_MODEL), lambda b, t: (layer, 0, 0))
    y = pl.pallas_call(
        _xattn_prompt_kernel,
        grid=(batch, seq // tt),
        in_specs=[row_spec, pl.BlockSpec((None, 1, D_MODEL), lambda b, t: (layer, 0, 0)),
                  w_spec, kv_spec, kv_spec, w_spec],
        out_specs=row_spec,
        out_shape=jax.ShapeDtypeStruct((batch, seq, D_MODEL), F32),
        compiler_params=_params("parallel", "parallel"),
        name="xattn_prompt",
    )(x.reshape(batch, seq, D_MODEL), g.reshape(DEPTH, 1, D_MODEL), w_q, mem_k, mem_v, w_o)
    return y.reshape(batch * seq, D_MODEL)


def _xattn_sample_kernel(q_ref, k_ref, v_ref, o_ref, *, tb):
    scale = 1.0 / math.sqrt(MEM_HD)
    rows = N_MEM * MEM_HEADS
    head, lane = _iota2((MEM_HEADS, rows))
    own = (lane & (MEM_HEADS - 1)) == head
    for b in range(tb):
        q = (q_ref[b] * scale).astype(BF16)
        k = k_ref[b].reshape(rows, MEM_HD).astype(BF16)
        v = v_ref[b].reshape(rows, MEM_HD).astype(BF16)
        s = lax.dot_general(q, k, (((1,), (1,)), ((), ())), preferred_element_type=F32)
        s = jnp.where(own, s, -jnp.inf)
        p = jnp.exp(s - jnp.max(s, axis=-1, keepdims=True))
        den = jnp.sum(p, axis=-1, keepdims=True)
        o_ref[b] = jnp.dot(p.astype(BF16), v, preferred_element_type=F32) / den


def _xattn_sample_vpu(q_ref, k_ref, v_ref, o_ref, *, tb):
    scale = 1.0 / math.sqrt(MEM_HD)
    for b in range(tb):
        q = q_ref[b] * scale
        s = jnp.sum(k_ref[b] * q[None], axis=-1, keepdims=True)
        p = jnp.exp(s - jnp.max(s, axis=0, keepdims=True))
        p = p / jnp.sum(p, axis=0, keepdims=True)
        o_ref[b] = jnp.sum(p * v_ref[b], axis=0)


def _xattn_sample(q, cache_k, cache_v, layer):
    nb = q.shape[0]
    tb = XA_SAMPLE_TOKENS_PER_STEP
    kv_spec = pl.BlockSpec((None, tb, N_MEM, MEM_HEADS, MEM_HD), lambda i: (layer, i, 0, 0, 0))
    q_spec = pl.BlockSpec((tb, MEM_HEADS, MEM_HD), lambda i: (i, 0, 0))
    return pl.pallas_call(
        functools.partial(_xattn_sample_kernel, tb=tb),
        grid=(nb // tb,),
        in_specs=[q_spec, kv_spec, kv_spec],
        out_specs=q_spec,
        out_shape=jax.ShapeDtypeStruct((nb, MEM_HEADS, MEM_HD), F32),
        compiler_params=_params("parallel"),
        name="xattn_sample",
    )(q.reshape(nb, MEM_HEADS, MEM_HD), cache_k, cache_v).reshape(nb, D_MODEL)


def _ffn1(x, w, i, ffn, emit):
    if emit:
        return ffn(x, w["ffn1_norm"], w["ffn1_w_in"], w["ffn1_w_out"], i, post="emit", g2=w["mix_norm"][i],
                   u_dtype=F32)
    return ffn(x, w["ffn1_norm"], w["ffn1_w_in"], w["ffn1_w_out"], i)


def _ffn2(x, w, i, ffn):
    last = i == DEPTH - 1
    return ffn(x, w["ffn2_norm"], w["ffn2_w_in"], w["ffn2_w_out"], i,
               post="replace" if last else "none", g2=w["final_norm"] if last else None)


def _prompt_trunk(x, batch, seq, mem_k, mem_v, w, ffn):
    zero = jnp.zeros((batch, S5_GROUPS, S5_STATE), F32)
    new_re, new_im, new_hg = [], [], []
    for i in range(DEPTH):
        j = i // 2
        if i % 2 == 0:
            x, u = _ffn1(x, w, i, ffn, True)
            x, hr, hi = _s5_mixer(x, u, zero, zero, w["s5_pieces"][j], w["s5_w_glu"], j, batch, seq)
            new_re.append(hr)
            new_im.append(hi)
        else:
            x = _ffn1(x, w, i, ffn, False)
            x, sn = _hgrn_prompt(x, w["mix_norm"], w["hg_w_in"], w["hg_lb_logits"], w["hg_norm"],
                                 w["hg_w_out"], i, j, batch, seq)
            new_hg.append(sn)
        x = _xattn_prompt(x, w["xattn_norm"], w["xattn_w_q"], mem_k, mem_v, w["xattn_w_o"], i, batch, seq)
        x = _ffn2(x, w, i, ffn)
    return x, jnp.stack(new_re), jnp.stack(new_im), jnp.stack(new_hg)


def _sample_trunk(x, batch, s5_re, s5_im, hg_state, w):
    new_re, new_im, new_hg = [], [], []
    for i in range(DEPTH):
        j = i // 2
        if i % 2 == 0:
            x, u = _ffn1(x, w, i, _ffn, True)
            x, hr, hi = _s5_mixer(x, u, s5_re[j], s5_im[j], w["s5_pieces"][j], w["s5_w_glu"], j, batch, 1)
            new_re.append(hr)
            new_im.append(hi)
        else:
            x = _ffn1(x, w, i, _ffn, False)
            proj = _proj(x, w["hg_w_in"], j, g=w["mix_norm"], g_layer=i)
            sn, o = _hgrn_sample(proj, w["hg_lb_logits"], w["hg_norm"], hg_state[j], i, j)
            x = _proj(o, w["hg_w_out"], j, res=x)
            new_hg.append(sn)
        q = _proj(x, w["xattn_w_q"], i, g=w["xattn_norm"], g_layer=i)
        o = yield q, i
        x = _proj(o, w["xattn_w_o"], i, res=x)
        x = _ffn2(x, w, i, _ffn)
    return x, jnp.stack(new_re), jnp.stack(new_im), jnp.stack(new_hg)


def kernel(x_prompt, x_sample, mem_prompt, state_s5_re, state_s5_im, state_hgrn, cache_mem_k, cache_mem_v, ffn1_norm, ffn1_w_in, ffn1_w_out, mix_norm, xattn_norm, mem_norm, xattn_w_q, xattn_w_kv, xattn_w_o, ffn2_norm, ffn2_w_in, ffn2_w_out, s5_a_re, s5_a_im, s5_log_dt, s5_b_re, s5_b_im, s5_c_re, s5_c_im, s5_d, s5_w_glu, hg_w_in, hg_lb_logits, hg_norm, hg_w_out, final_norm):
    bp, seq, _ = x_prompt.shape
    bs = x_sample.shape[0]
    w = dict(ffn1_norm=ffn1_norm, ffn1_w_in=ffn1_w_in, ffn1_w_out=ffn1_w_out, mix_norm=mix_norm,
             xattn_norm=xattn_norm, xattn_w_q=xattn_w_q, xattn_w_o=xattn_w_o, ffn2_norm=ffn2_norm,
             ffn2_w_in=ffn2_w_in, ffn2_w_out=ffn2_w_out,
             s5_pieces=[_s5_pieces(s5_a_re[j], s5_a_im[j], s5_log_dt[j], s5_b_re[j], s5_b_im[j], s5_c_re[j],
                                   s5_c_im[j], s5_d[j], S5_CHUNK) for j in range(s5_a_re.shape[0])],
             s5_w_glu=s5_w_glu, hg_w_in=hg_w_in, hg_lb_logits=hg_lb_logits, hg_norm=hg_norm,
             hg_w_out=hg_w_out, final_norm=final_norm)

    mem_k, mem_v, mem_k_bf, mem_v_bf = _mem_kv(mem_prompt, mem_norm, xattn_w_kv)

    sample = _sample_trunk(x_sample.reshape(bs, D_MODEL), bs, state_s5_re, state_s5_im, state_hgrn, w)
    pending = [next(sample)]
    sample_out = []

    def resume(o):
        try:
            pending[0] = sample.send(o)
        except StopIteration as done:
            pending[0] = None
            sample_out.append(done.value)

    def ffn_with_rider(x, *args, **kwargs):
        steps = x.shape[0] // min(FFN_ROW_TILE_WITH_RIDER, x.shape[0])
        if pending[0] is None or bs % steps or kwargs.get("post") == "emit":
            return _ffn(x, *args, **kwargs)
        q, layer = pending[0]
        *outs, o = _ffn(x, *args, rider=(q, cache_mem_k, cache_mem_v, layer), **kwargs)
        resume(o)
        return outs[0] if len(outs) == 1 else tuple(outs)

    y_p, re_p, im_p, hg_p = _prompt_trunk(x_prompt.reshape(bp * seq, D_MODEL), bp, seq, mem_k_bf, mem_v_bf, w,
                                          ffn_with_rider)
    while pending[0] is not None:
        q, layer = pending[0]
        resume(_xattn_sample(q, cache_mem_k, cache_mem_v, layer))
    y_s, re_s, im_s, hg_s = sample_out[0]
    return (y_p.reshape(bp, seq, D_MODEL), y_s.reshape(bs, 1, D_MODEL), re_p, im_p, re_s, im_s, hg_p, hg_s,
            mem_k, mem_v)
```

```python
import functools
import math

import jax
import jax.numpy as jnp
from jax import lax
from jax.experimental import pallas as pl
from jax.experimental.pallas import tpu as pltpu

F32 = jnp.float32
BF16 = jnp.bfloat16

D_MODEL = 1024
DEPTH = 2
S5_GROUP = 16
S5_GROUPS = D_MODEL // S5_GROUP
S5_STATE = 64
S5_CHUNK = 16
S5_BLOCK_GROUPS = 8
S5_TIME_SLICES = 2
HG_DK = 128
HG_HEADS = D_MODEL // HG_DK
HG_DV = D_MODEL // HG_HEADS
HG_WIDTH = HG_HEADS * HG_DK
HG_CHUNK = 128
HG_FACTORED_MAX_DECAY = 60.0
N_MEM = 256
MEM_HEADS = 4
MEM_HD = D_MODEL // MEM_HEADS
FFN_DIM = 2816
EPS = 1e-6

V7X_VMEM_LIMIT_BYTES = 56 * 1024 * 1024

ROW_TILE = 512
FFN_CHUNK = 2816
FFN_ROW_TILE_WITH_RIDER = 256
HG_ROW_TILE = 512
XA_ROW_TILE = 1024
PROJ_ROW_TILE = 1024
S5_GROUPS_PER_STEP = 16
SAMPLE_TOKENS_PER_STEP = 8
XA_SAMPLE_TOKENS_PER_STEP = 4


def _params(*semantics):
    return pltpu.CompilerParams(dimension_semantics=semantics,
                                vmem_limit_bytes=V7X_VMEM_LIMIT_BYTES)


def _resident(shape, index_map):
    return pl.BlockSpec(shape, index_map, pipeline_mode=pl.Buffered(1))


def _rms(x, g):
    ms = jnp.mean(x * x, axis=-1, keepdims=True)
    return x * lax.rsqrt(ms + EPS) * g


def _sigmoid(x):
    return 1.0 / (1.0 + jnp.exp(-x))


def _silu(x):
    return x * _sigmoid(x)


def _gelu_tanh(x):
    c = math.sqrt(2.0 / math.pi)
    return 0.5 * x * (1.0 + jnp.tanh(c * (x + 0.044715 * (x * x * x))))


def _dot(a, w):
    return jnp.dot(a, w.astype(BF16), preferred_element_type=F32)


def _split3(x):
    hi = x.astype(BF16)
    r1 = x - hi.astype(F32)
    mid = r1.astype(BF16)
    lo = (r1 - mid.astype(F32)).astype(BF16)
    return hi, mid, lo


def _ffn_kernel(x_ref, g_ref, win_ref, wout_ref, *rest, post, rider_tokens):
    rest = list(rest)
    g2_ref = rest.pop(0) if post != "none" else None
    if rider_tokens:
        q_ref, k_ref, v_ref = rest[:3]
        rest = rest[3:]
        _xattn_sample_vpu(q_ref, k_ref, v_ref, rest.pop(), tb=rider_tokens)
    x = x_ref[...]
    h = _rms(x, g_ref[...]).astype(BF16)
    acc = jnp.zeros_like(x)
    for c in range(FFN_DIM // FFN_CHUNK):
        lo = c * FFN_CHUNK
        gate = _dot(h, win_ref[:, lo:lo + FFN_CHUNK])
        up = _dot(h, win_ref[:, FFN_DIM + lo:FFN_DIM + lo + FFN_CHUNK])
        act = (_silu(gate) * up).astype(BF16)
        acc = acc + _dot(act, wout_ref[lo:lo + FFN_CHUNK, :])
    y = x + 0.5 * acc
    y_ref = rest[0]
    if post == "replace":
        y_ref[...] = _rms(y, g2_ref[...])
    else:
        y_ref[...] = y
        if post == "emit":
            rest[1][...] = _rms(y, g2_ref[...]).astype(rest[1].dtype)


def _ffn(x, g, w_in, w_out, layer, post="none", g2=None, u_dtype=BF16, rider=None):
    rows = x.shape[0]
    tm = min(ROW_TILE if rider is None else FFN_ROW_TILE_WITH_RIDER, rows)
    steps = rows // tm
    row_spec = pl.BlockSpec((tm, D_MODEL), lambda i: (i, 0))
    vec_spec = pl.BlockSpec((1, D_MODEL), lambda i: (0, 0))
    in_specs = [row_spec,
                pl.BlockSpec((None, 1, D_MODEL), lambda i: (layer, 0, 0)),
                _resident((None, D_MODEL, 2 * FFN_DIM), lambda i: (layer, 0, 0)),
                _resident((None, FFN_DIM, D_MODEL), lambda i: (layer, 0, 0))]
    args = [x, g.reshape(DEPTH, 1, D_MODEL), w_in, w_out]
    out_shape = [jax.ShapeDtypeStruct((rows, D_MODEL), F32)]
    out_specs = [row_spec]
    if post != "none":
        in_specs.append(vec_spec)
        args.append(g2.reshape(1, D_MODEL))
    if post == "emit":
        out_shape.append(jax.ShapeDtypeStruct((rows, D_MODEL), u_dtype))
        out_specs.append(row_spec)
    tb = 0
    if rider is not None:
        q, cache_k, cache_v, r_layer = rider
        nb = q.shape[0]
        assert nb % steps == 0
        tb = nb // steps
        kv_spec = pl.BlockSpec((None, tb, N_MEM, MEM_HEADS, MEM_HD), lambda i: (r_layer, i, 0, 0, 0))
        q_spec = pl.BlockSpec((tb, MEM_HEADS, MEM_HD), lambda i: (i, 0, 0))
        in_specs += [q_spec, kv_spec, kv_spec]
        args += [q.reshape(nb, MEM_HEADS, MEM_HD), cache_k, cache_v]
        out_shape.append(jax.ShapeDtypeStruct((nb, MEM_HEADS, MEM_HD), F32))
        out_specs.append(q_spec)
    outs = list(pl.pallas_call(
        functools.partial(_ffn_kernel, post=post, rider_tokens=tb),
        grid=(steps,),
        in_specs=in_specs, out_specs=tuple(out_specs), out_shape=tuple(out_shape),
        compiler_params=_params("parallel"),
        name="ffn",
    )(*args))
    if rider is not None:
        outs[-1] = outs[-1].reshape(-1, D_MODEL)
    return outs[0] if len(outs) == 1 else tuple(outs)


def _proj_kernel(*refs, norm, glu, residual):
    refs = list(refs)
    x_ref = refs.pop(0)
    g_ref = refs.pop(0) if norm else None
    w_ref = refs.pop(0)
    res_ref = refs.pop(0) if residual else None
    (o_ref,) = refs
    x = x_ref[...]
    if norm:
        x = _rms(x, g_ref[...])
    y = _dot(x.astype(BF16), w_ref[...])
    if glu:
        half = y.shape[-1] // 2
        y = y[:, :half] * _sigmoid(y[:, half:])
    if residual:
        y = y + res_ref[...]
    o_ref[...] = y


def _proj(x, w, layer, g=None, g_layer=0, glu=False, res=None):
    rows, kdim = x.shape
    ndim = w.shape[-1]
    nout = ndim // 2 if glu else ndim
    tm = min(PROJ_ROW_TILE, rows)
    in_specs = [pl.BlockSpec((tm, kdim), lambda i: (i, 0))]
    args = [x]
    if g is not None:
        in_specs.append(pl.BlockSpec((None, 1, kdim), lambda i: (g_layer, 0, 0)))
        args.append(g.reshape(g.shape[0], 1, kdim))
    in_specs.append(_resident((None, kdim, ndim), lambda i: (layer, 0, 0)))
    args.append(w)
    if res is not None:
        in_specs.append(pl.BlockSpec((tm, nout), lambda i: (i, 0)))
        args.append(res)
    return pl.pallas_call(
        functools.partial(_proj_kernel, norm=g is not None, glu=glu, residual=res is not None),
        grid=(rows // tm,),
        in_specs=in_specs,
        out_specs=pl.BlockSpec((tm, nout), lambda i: (i, 0)),
        out_shape=jax.ShapeDtypeStruct((rows, nout), F32),
        compiler_params=_params("parallel"),
        name="proj",
    )(*args)


def _memkv_kernel(x_ref, g_ref, w_ref, k_ref, v_ref, kb_ref, vb_ref):
    h = _rms(x_ref[...], g_ref[...]).astype(BF16)
    y = _dot(h, w_ref[...])
    k, v = y[:, :D_MODEL], y[:, D_MODEL:]
    k_ref[...] = k.reshape(k_ref.shape)
    v_ref[...] = v.reshape(v_ref.shape)
    kb_ref[...] = k.astype(BF16)
    vb_ref[...] = v.astype(BF16)


def _mem_kv(mem, g, w_kv):
    batch = mem.shape[0]
    rows = batch * N_MEM
    nb = max(1, min(ROW_TILE, rows) // N_MEM)
    tm = nb * N_MEM
    out5 = jax.ShapeDtypeStruct((DEPTH, batch, N_MEM, MEM_HEADS, MEM_HD), F32)
    out2 = jax.ShapeDtypeStruct((DEPTH, rows, D_MODEL), BF16)
    spec5 = pl.BlockSpec((None, nb, N_MEM, MEM_HEADS, MEM_HD), lambda l, i: (l, i, 0, 0, 0))
    spec2 = pl.BlockSpec((None, tm, D_MODEL), lambda l, i: (l, i, 0))
    return pl.pallas_call(
        _memkv_kernel,
        grid=(DEPTH, rows // tm),
        in_specs=[pl.BlockSpec((tm, D_MODEL), lambda l, i: (i, 0)),
                  pl.BlockSpec((None, 1, D_MODEL), lambda l, i: (l, 0, 0)),
                  pl.BlockSpec((None, D_MODEL, 2 * D_MODEL), lambda l, i: (l, 0, 0))],
        out_specs=(spec5, spec5, spec2, spec2), out_shape=(out5, out5, out2, out2),
        compiler_params=_params("parallel", "parallel"),
        name="mem_kv",
    )(mem.reshape(rows, D_MODEL), g.reshape(DEPTH, 1, D_MODEL), w_kv)


def _s5_pieces(a_re, a_im, log_dt, b_re, b_im, c_re, c_im, d, steps):
    G, P, GC = S5_GROUPS, S5_STATE, S5_GROUP
    L = steps
    dt = jnp.exp(log_dt)[:, None]
    xr, xi = a_re * dt, a_im * dt
    j = jnp.arange(L + 1, dtype=F32)[:, None, None]
    mag = jnp.exp(xr[None] * j)
    pw_re, pw_im = mag * jnp.cos(xi[None] * j), mag * jnp.sin(xi[None] * j)
    nr, ni = pw_re[1] - 1.0, pw_im[1]
    den = a_re * a_re + a_im * a_im
    fr, fi = ((nr * a_re + ni * a_im) / den)[:, None, :], ((ni * a_re - nr * a_im) / den)[:, None, :]
    bt_re, bt_im = b_re.transpose(0, 2, 1), b_im.transpose(0, 2, 1)
    bb_re = fr * bt_re - fi * bt_im
    bb_im = fr * bt_im + fi * bt_re

    def times_c(p_re, p_im):
        p_re, p_im = p_re[:, :, None, :], p_im[:, :, None, :]
        return p_re * c_re[None] - p_im * c_im[None], p_re * c_im[None] + p_im * c_re[None]

    w_re, w_im = times_c(pw_re[:L], pw_im[:L])

    def over_state(b, w_):
        w_ = w_.transpose(1, 0, 2, 3).reshape(G, L * GC, P)
        return jnp.einsum("gip,gnp->gin", b, w_, precision=lax.Precision.HIGHEST)

    kern = (over_state(bb_re, w_re) - over_state(bb_im, w_im)).reshape(G, GC, L, GC).transpose(2, 0, 1, 3)
    kern = kern.at[0].add(d[:, :, None] * jnp.eye(GC, dtype=F32)[None])
    return dict(kern=kern, pw=(pw_re, pw_im), bbar=(bb_re, bb_im), c=(c_re, c_im))


def _s5_group_mats(pieces):
    (pw_re, pw_im), (c_re, c_im) = pieces["pw"], pieces["c"]
    l_re, l_im = pw_re[1][:, None, :], pw_im[1][:, None, :]
    p_re = l_re * c_re - l_im * c_im
    p_im = -(l_re * c_im + l_im * c_re)
    return (pieces["kern"][0], *pieces["bbar"], p_re.transpose(0, 2, 1), p_im.transpose(0, 2, 1), l_re, l_im)


def _s5_block_mats(pieces):
    kern = pieces["kern"]
    pw_re, pw_im = pieces["pw"]
    L = kern.shape[0]
    GB = S5_BLOCK_GROUPS
    NB = S5_GROUPS // GB
    GC, P = S5_GROUP, S5_STATE
    taps = kern.astype(BF16).reshape(L, NB, GB * GC, GC).transpose(1, 0, 2, 3)
    powers = jnp.stack([pw_re, pw_im]).reshape(2, L + 1, NB, GB, P).transpose(2, 0, 1, 3, 4)
    powers = powers.reshape(NB, 2, (L + 1) * GB, P)
    per_block = lambda pair: jnp.stack(pair).reshape(2, NB, GB * GC, P).transpose(1, 0, 2, 3)
    lam = jnp.concatenate([pw_re[L].reshape(NB, 1, GB * P), pw_im[L].reshape(NB, 1, GB * P)], axis=2)
    return taps, powers, per_block(pieces["bbar"]), per_block(pieces["c"]), lam


def _s5_kernel(u_ref, t_ref, mre_ref, mim_ref, pre_ref, pim_ref, lre_ref, lim_ref, h0re_ref, h0im_ref,
               y_ref, hre_ref, him_ref, inj_re, inj_im, hs_re, hs_im, *, n_chunks, rb, gps, precision):
    def mm(a, b):
        return jnp.dot(a, b, preferred_element_type=F32, precision=precision)

    for g in range(gps):
        u = u_ref[g]
        inj_re[g] = mm(u, mre_ref[g])
        inj_im[g] = mm(u, mim_ref[g])

    lam_re = [jnp.broadcast_to(lre_ref[g], (rb, S5_STATE)) for g in range(gps)]
    lam_im = [jnp.broadcast_to(lim_ref[g], (rb, S5_STATE)) for g in range(gps)]

    def step(k, carry):
        rows = pl.ds(pl.multiple_of(k * rb, rb), rb)
        nxt = []
        for g in range(gps):
            hr, hi = carry[2 * g], carry[2 * g + 1]
            hs_re[g, rows, :] = hr
            hs_im[g, rows, :] = hi
            nxt.append(lam_re[g] * hr - lam_im[g] * hi + inj_re[g, rows, :])
            nxt.append(lam_re[g] * hi + lam_im[g] * hr + inj_im[g, rows, :])
        return tuple(nxt)

    init = []
    for g in range(gps):
        init += [h0re_ref[g], h0im_ref[g]]
    fin = lax.fori_loop(0, n_chunks, step, tuple(init))

    for g in range(gps):
        hre_ref[g] = fin[2 * g]
        him_ref[g] = fin[2 * g + 1]
        dt = u_ref.dtype
        y = (mm(u_ref[g], t_ref[g]) + mm(hs_re[g].astype(dt), pre_ref[g])
             + mm(hs_im[g].astype(dt), pim_ref[g]))
        y_ref[g] = _gelu_tanh(y).astype(y_ref.dtype)


def _s5_core(u, mats, h0_re, h0_im, n_chunks, rb, precision, y_dtype):
    G, R, W = u.shape
    P = S5_STATE
    gps = S5_GROUPS_PER_STEP
    tmat, m_re, m_im, p_re, p_im, l_re, l_im = mats

    def spec(a, b):
        return pl.BlockSpec((gps, a, b), lambda i: (i, 0, 0))

    st = jax.ShapeDtypeStruct((G, rb, P), F32)
    scr = lambda: pltpu.VMEM((gps, R, P), F32)
    return pl.pallas_call(
        functools.partial(_s5_kernel, n_chunks=n_chunks, rb=rb, gps=gps, precision=precision),
        grid=(G // gps,),
        in_specs=[spec(R, W), spec(W, W), spec(W, P), spec(W, P), spec(P, W), spec(P, W),
                  spec(1, P), spec(1, P), spec(rb, P), spec(rb, P)],
        out_specs=(spec(R, W), spec(rb, P), spec(rb, P)),
        out_shape=(jax.ShapeDtypeStruct((G, R, W), y_dtype), st, st),
        scratch_shapes=[scr(), scr(), scr(), scr()],
        compiler_params=_params("parallel"),
        name="s5_core",
    )(u, tmat, m_re, m_im, p_re, p_im, l_re, l_im, h0_re, h0_im)


def _iota2(shape):
    return lax.broadcasted_iota(jnp.int32, shape, 0), lax.broadcasted_iota(jnp.int32, shape, 1)


def _s5_expand(taps_ref, pw_ref, bb_ref, c_ref, w2_ref, m_ref, p_ref):
    L, GC, P = S5_CHUNK, S5_GROUP, S5_STATE
    GB = S5_BLOCK_GROUPS
    W, HS = GB * GC, GB * P
    gc_bits = GC.bit_length() - 1

    m_ref[...] = jnp.zeros_like(m_ref)
    p_ref[...] = jnp.zeros_like(p_ref)
    for g in range(GB):
        chans = slice(g * GC, (g + 1) * GC)
        b_re, b_im, c_re, c_im = bb_ref[0, chans, :], bb_ref[1, chans, :], c_ref[0, chans, :], c_ref[1, chans, :]
        for l in range(L):
            rows = slice((l * GB + g) * GC, (l * GB + g + 1) * GC)
            jm, jp = (L - 1 - l) * GB + g, (l + 1) * GB + g
            a_re, a_im = pw_ref[0, jm:jm + 1, :], pw_ref[1, jm:jm + 1, :]
            q_re, q_im = pw_ref[0, jp:jp + 1, :], pw_ref[1, jp:jp + 1, :]
            m_ref[rows, g * P:(g + 1) * P] = (a_re * b_re - a_im * b_im).astype(BF16)
            m_ref[rows, HS + g * P:HS + (g + 1) * P] = (a_re * b_im + a_im * b_re).astype(BF16)
            p_ref[rows, g * P:(g + 1) * P] = (q_re * c_re - q_im * c_im).astype(BF16)
            p_ref[rows, HS + g * P:HS + (g + 1) * P] = (-(q_re * c_im + q_im * c_re)).astype(BF16)

    r, c = _iota2((GC, W))
    rep_k = ((c & (GC - 1)) == r).astype(BF16)
    r, c = _iota2((W, W))
    mask_k = (r >> gc_bits) == (c >> gc_bits)
    zero = jnp.zeros((W, W), BF16)
    w2_ref[L * W:, :W] = zero
    w2_ref[:W, W:] = zero
    for i in range(L):
        kb = jnp.where(mask_k, jnp.dot(taps_ref[L - 1 - i], rep_k, preferred_element_type=F32), 0.0).astype(BF16)
        w2_ref[i * W:(i + 1) * W, :W] = kb
        w2_ref[(i + 1) * W:(i + 2) * W, W:] = kb


def _s5_seq_kernel(u_ref, taps_ref, pw_ref, bb_ref, c_ref, lam_ref, h0_ref, y_ref, hfin_ref,
                   w2_ref, m_ref, p_ref, lhs_scr, inj_scr, hs_scr, h_scr, *, batch, nck):
    L, W = S5_CHUNK, S5_BLOCK_GROUPS * S5_GROUP
    NT = S5_BLOCK_GROUPS * S5_STATE // W

    @pl.when(pl.program_id(1) == 0)
    def _():
        h_scr[...] = h0_ref[...]
        _s5_expand(taps_ref, pw_ref, bb_ref, c_ref, w2_ref, m_ref, p_ref)

    for b in range(batch):
        for l in range(L):
            lhs_scr[b * nck:(b + 1) * nck, l * W:(l + 1) * W] = u_ref[b, pl.ds(l, nck, stride=L), :].astype(BF16)

    def swap_major(x, a, b):
        return jnp.swapaxes(x.reshape(a, b, W), 0, 1).reshape(a * b, W)

    inj = jnp.dot(lhs_scr[...], m_ref[...], preferred_element_type=F32)
    for t in range(2 * NT):
        inj_scr[t] = swap_major(inj[:, t * W:(t + 1) * W], batch, nck)

    lam = [jnp.broadcast_to(lam_ref[:, t * W:(t + 1) * W], (batch, W)) for t in range(2 * NT)]

    def step(k, h):
        rows = pl.ds(pl.multiple_of(k * batch, batch), batch)
        nxt_re, nxt_im = [], []
        for t in range(NT):
            hr, hi = h[t], h[NT + t]
            hs_scr[t, rows, :] = hr
            hs_scr[NT + t, rows, :] = hi
            nxt_re.append(lam[t] * hr - lam[NT + t] * hi + inj_scr[t, rows, :])
            nxt_im.append(lam[t] * hi + lam[NT + t] * hr + inj_scr[NT + t, rows, :])
        return tuple(nxt_re + nxt_im)

    h = lax.fori_loop(0, nck, step, tuple(h_scr[:, t * W:(t + 1) * W] for t in range(2 * NT)))
    for t in range(2 * NT):
        h_scr[:, t * W:(t + 1) * W] = h[t]
    hfin_ref[...] = h_scr[...]

    hs = jnp.concatenate([swap_major(hs_scr[t], nck, batch) for t in range(2 * NT)], axis=-1).astype(BF16)
    for pr in range(L // 2):
        kk = (2 * pr + 2) * W
        y = (jnp.dot(lhs_scr[:, :kk], w2_ref[(L - 1 - 2 * pr) * W:, :], preferred_element_type=F32)
             + lax.dot_general(hs, p_ref[2 * pr * W:(2 * pr + 2) * W, :], (((1,), (1,)), ((), ())),
                               preferred_element_type=F32))
        y = _gelu_tanh(y)
        for s in range(2):
            for b in range(batch):
                y_ref[b, pl.ds(2 * pr + s, nck, stride=L), :] = y[b * nck:(b + 1) * nck, s * W:(s + 1) * W]


def _s5_seq(u, mats, h0, batch, seq):
    taps, powers, bbar, cmat, lam = mats
    L, W = S5_CHUNK, S5_BLOCK_GROUPS * S5_GROUP
    NB = S5_GROUPS // S5_BLOCK_GROUPS
    HS = S5_BLOCK_GROUPS * S5_STATE
    SW = 2 * HS
    ts = seq // S5_TIME_SLICES
    nck = ts // L
    rows = batch * nck
    u_spec = pl.BlockSpec((batch, ts, W), lambda i, t: (0, t, i))
    h_spec = pl.BlockSpec((None, batch, SW), lambda i, t: (i, 0, 0))

    def w_spec(*dims):
        return pl.BlockSpec((None,) + dims, lambda i, t: (i,) + (0,) * len(dims))

    return pl.pallas_call(
        functools.partial(_s5_seq_kernel, batch=batch, nck=nck),
        grid=(NB, S5_TIME_SLICES),
        in_specs=[u_spec, w_spec(L, W, S5_GROUP), w_spec(2, (L + 1) * S5_BLOCK_GROUPS, S5_STATE),
                  w_spec(2, W, S5_STATE), w_spec(2, W, S5_STATE), w_spec(1, SW), h_spec],
        out_specs=(u_spec, h_spec),
        out_shape=(jax.ShapeDtypeStruct((batch, seq, D_MODEL), F32), jax.ShapeDtypeStruct(h0.shape, F32)),
        scratch_shapes=[pltpu.VMEM(((L + 1) * W, 2 * W), BF16), pltpu.VMEM((L * W, SW), BF16),
                        pltpu.VMEM((L * W, SW), BF16),
                        pltpu.VMEM((rows, L * W), BF16), pltpu.VMEM((SW // W, rows, W), F32),
                        pltpu.VMEM((SW // W, rows, W), F32), pltpu.VMEM((batch, SW), F32)],
        compiler_params=_params("parallel", "arbitrary"),
        name="s5_seq",
    )(u, taps, powers, bbar, cmat, lam, h0)


def _s5_mixer(x, u, h0_re, h0_im, pieces, w_glu, j, batch, seq):
    G, GC, P = S5_GROUPS, S5_GROUP, S5_STATE
    if seq > 1:
        NB, HS = G // S5_BLOCK_GROUPS, S5_BLOCK_GROUPS * P
        mats = _s5_block_mats(pieces)
        to_blocks = lambda h: h.reshape(batch, NB, HS).transpose(1, 0, 2)
        h0 = jnp.concatenate([to_blocks(h0_re), to_blocks(h0_im)], axis=-1)
        y, hfin = _s5_seq(u.reshape(batch, seq, D_MODEL), mats, h0, batch, seq)
        y = y.reshape(batch * seq, D_MODEL)
        from_blocks = lambda h: h.transpose(1, 0, 2).reshape(batch, G, P)
        hre, him = from_blocks(hfin[..., :HS]), from_blocks(hfin[..., HS:])
    else:
        mats = _s5_group_mats(pieces)
        mats = tuple(m.astype(BF16) for m in mats[:5]) + mats[5:]
        ug = u.reshape(batch, G, GC).transpose(1, 0, 2).astype(BF16)
        yg, hre, him = _s5_core(ug, mats, h0_re.transpose(1, 0, 2), h0_im.transpose(1, 0, 2),
                                1, batch, None, F32)
        y = yg.transpose(1, 0, 2).reshape(batch, D_MODEL)
        hre, him = hre.transpose(1, 0, 2), him.transpose(1, 0, 2)
    x = _proj(y, w_glu, j, glu=True, res=x)
    return x, hre, him


def _hg_lower_bound(logits, layer):
    m = jnp.max(logits, axis=0, keepdims=True)
    e = jnp.exp(logits - m)
    sm = e / jnp.sum(e, axis=0, keepdims=True)
    return jnp.sum(sm[:layer + 1], axis=0, keepdims=True) - sm[0:1]


def _hg_gates(z, lb):
    e = jnp.exp(-jnp.abs(z))
    r = 1.0 / (1.0 + e)
    er = e * r
    pos = z >= 0.0
    f = lb + (1.0 - lb) * jnp.where(pos, r, er)
    logf = jnp.where(f > 0.0, jnp.log(f), z)
    k = (1.0 - lb) * jnp.where(pos, er, r)
    return logf, k


def _hgrn_prompt_kernel(x_ref, g_ref, win_ref, lbl_ref, ng_ref, wout_ref, y_ref, sfin_ref,
                        s_scr, q_scr, k_scr, v_scr, gt_scr, o_scr, beta_scr, safe_scr, *, layer, tt):
    t = pl.program_id(1)
    C = HG_CHUNK

    @pl.when(t == 0)
    def _():
        s_scr[...] = jnp.zeros_like(s_scr)

    x = x_ref[...]
    h = _rms(x, g_ref[...]).astype(BF16)
    def proj(i):
        return _dot(h, win_ref[:, i * HG_WIDTH:(i + 1) * HG_WIDTH])

    lb = _hg_lower_bound(lbl_ref[...], layer)
    logf, kk = _hg_gates(proj(1), lb)
    k_scr[...] = kk
    q_scr[...] = _silu(proj(0))
    gt_scr[...] = _silu(proj(3))
    v_scr[...] = proj(2)

    row, col = _iota2((C, C))
    causal = row >= col
    tri = causal.astype(BF16)
    ng = ng_ref[...]
    nt_dims = (((1,), (1,)), ((), ()))
    tn_dims = (((0,), (0,)), ((), ()))

    def finish_head(hh, rows, q_dec, k_dec, o_intra, btot_h):
        sv = slice(hh * HG_DV, (hh + 1) * HG_DV)
        st = s_scr[hh]
        o = o_intra + lax.dot_general(q_dec.astype(BF16), st.astype(BF16), nt_dims, preferred_element_type=F32)
        kv_t = lax.dot_general(v_scr[rows, sv].astype(BF16), k_dec.astype(BF16), tn_dims,
                               preferred_element_type=F32)
        return (_rms(o, ng) * gt_scr[rows, sv]).astype(BF16), st * jnp.exp(btot_h) + kv_t

    def store_heads(rows, results):
        o_scr[rows, :] = jnp.concatenate([o for o, _ in results], axis=-1)
        for hh, (_, s_new) in enumerate(results):
            s_scr[hh] = s_new

    for ci in range(tt // C):
        parts = _split3(logf[ci * C:(ci + 1) * C])
        beta = sum(jnp.dot(tri, p, preferred_element_type=F32) for p in parts)
        beta_scr[ci * C:(ci + 1) * C, :] = beta
        mid = beta[C // 2 - 1:C // 2, :]
        spread = jnp.maximum(jnp.max(-mid), jnp.max(mid - beta[C - 1:C, :]))
        safe_scr[ci] = (spread <= HG_FACTORED_MAX_DECAY).astype(jnp.int32)

    def fast_chunk(rows, btot, mid):
        e_mid = jnp.exp(mid)
        e_tot = jnp.exp(btot - mid)
        results = []
        for hh in range(HG_HEADS):
            sk = slice(hh * HG_DK, (hh + 1) * HG_DK)
            d = beta_scr[rows, sk] - mid[:, sk]
            q_mid = q_scr[rows, sk] * jnp.exp(d)
            k_mid = k_scr[rows, sk] * jnp.exp(-d)
            att = lax.dot_general(q_mid.astype(BF16), k_mid.astype(BF16), nt_dims, preferred_element_type=F32)
            att = jnp.where(causal, att, 0.0).astype(BF16)
            o_intra = jnp.dot(att, v_scr[rows, hh * HG_DV:(hh + 1) * HG_DV].astype(BF16),
                              preferred_element_type=F32)
            results.append(finish_head(hh, rows, q_mid * e_mid[:, sk], k_mid * e_tot[:, sk], o_intra,
                                       btot[:, sk]))
        store_heads(rows, results)

    def chunk_step(c, carry):
        rows = pl.ds(pl.multiple_of(c * C, C), C)
        base = pl.multiple_of(c * C, C)
        btot = beta_scr[pl.ds(base + (C - 8), 8), :][7:8]
        mid = beta_scr[pl.ds(base + (C // 2 - 8), 8), :][7:8]
        safe = safe_scr[c] == 1

        @pl.when(safe)
        def _():
            fast_chunk(rows, btot, mid)

        @pl.when(jnp.logical_not(safe))
        def _():
            t_idx = lax.broadcasted_iota(jnp.int32, (C, 1), 0)
            results = []
            for hh in range(HG_HEADS):
                sk = slice(hh * HG_DK, (hh + 1) * HG_DK)
                sv = slice(hh * HG_DV, (hh + 1) * HG_DV)
                b = beta_scr[rows, sk]
                q = q_scr[rows, sk]

                def key_step(s8, acc):
                    off = pl.multiple_of(s8 * 8, 8)
                    b_keys = beta_scr[pl.ds(base + off, 8), sk]
                    k_keys = k_scr[pl.ds(base + off, 8), sk]
                    v_keys = v_scr[pl.ds(base + off, 8), sv]
                    for i in range(8):
                        w = jnp.exp(jnp.minimum(b - b_keys[i:i + 1], 0.0))
                        a = jnp.sum(q * k_keys[i:i + 1] * w, axis=-1, keepdims=True)
                        a = jnp.where(t_idx >= off + i, a, 0.0)
                        acc = acc + a * v_keys[i:i + 1]
                    return acc

                o_intra = lax.fori_loop(0, C // 8, key_step, jnp.zeros((C, HG_DV), F32))
                results.append(finish_head(hh, rows, q * jnp.exp(b), k_scr[rows, sk] * jnp.exp(btot[:, sk] - b),
                                           o_intra, btot[:, sk]))
            store_heads(rows, results)

        return carry

    n_chunks = tt // C
    all_safe = sum(safe_scr[ci] for ci in range(n_chunks)) == n_chunks

    @pl.when(all_safe)
    def _():
        for ci in range(n_chunks):
            fast_chunk(pl.ds(ci * C, C), beta_scr[ci * C + C - 1:ci * C + C, :],
                       beta_scr[ci * C + C // 2 - 1:ci * C + C // 2, :])

    @pl.when(jnp.logical_not(all_safe))
    def _():
        lax.fori_loop(0, n_chunks, chunk_step, 0)
    y_ref[...] = x + _dot(o_scr[...], wout_ref[...])

    @pl.when(t == pl.num_programs(1) - 1)
    def _():
        for hh in range(HG_HEADS):
            sfin_ref[hh] = s_scr[hh].T


def _hgrn_prompt(x, g, w_in, lb_logits, norm_g, w_out, layer, j, batch, seq):
    tt = HG_ROW_TILE
    nt = seq // tt
    x3 = x.reshape(batch, seq, D_MODEL)
    row_spec = pl.BlockSpec((None, tt, D_MODEL), lambda b, t: (b, t, 0))
    y, s_fin = pl.pallas_call(
        functools.partial(_hgrn_prompt_kernel, layer=layer, tt=tt),
        grid=(batch, nt),
        in_specs=[row_spec,
                  pl.BlockSpec((None, 1, D_MODEL), lambda b, t: (layer, 0, 0)),
                  _resident((None, D_MODEL, 4 * HG_WIDTH), lambda b, t: (j, 0, 0)),
                  pl.BlockSpec((DEPTH, HG_WIDTH), lambda b, t: (0, 0)),
                  pl.BlockSpec((None, 1, HG_DV), lambda b, t: (j, 0, 0)),
                  _resident((None, HG_WIDTH, D_MODEL), lambda b, t: (j, 0, 0))],
        out_specs=(row_spec,
                   pl.BlockSpec((None, HG_HEADS, HG_DK, HG_DV), lambda b, t: (b, 0, 0, 0))),
        out_shape=(jax.ShapeDtypeStruct((batch, seq, D_MODEL), F32),
                   jax.ShapeDtypeStruct((batch, HG_HEADS, HG_DK, HG_DV), F32)),
        scratch_shapes=[pltpu.VMEM((HG_HEADS, HG_DV, HG_DK), F32),
                        pltpu.VMEM((tt, HG_WIDTH), F32), pltpu.VMEM((tt, HG_WIDTH), F32),
                        pltpu.VMEM((tt, HG_WIDTH), F32), pltpu.VMEM((tt, HG_WIDTH), F32),
                        pltpu.VMEM((tt, HG_WIDTH), BF16), pltpu.VMEM((tt, HG_WIDTH), F32),
                        pltpu.SMEM((tt // HG_CHUNK,), jnp.int32)],
        compiler_params=_params("parallel", "arbitrary"),
        name="hgrn_prompt",
    )(x3, g.reshape(DEPTH, 1, D_MODEL), w_in, lb_logits, norm_g.reshape(-1, 1, HG_DV), w_out)
    return y.reshape(batch * seq, D_MODEL), s_fin


def _hgrn_sample_kernel(proj_ref, lbl_ref, ng_ref, s_ref, snew_ref, o_ref, *, layer, tb):
    proj = proj_ref[...]
    lb = _hg_lower_bound(lbl_ref[...], layer)
    z = proj[:, HG_WIDTH:2 * HG_WIDTH]
    e = jnp.exp(-jnp.abs(z))
    r = 1.0 / (1.0 + e)
    sig = jnp.where(z >= 0.0, r, e * r)
    f = lb + (1.0 - lb) * sig
    k = (1.0 - lb) * jnp.where(z >= 0.0, e * r, r)
    q = _silu(proj[:, :HG_WIDTH])
    v = proj[:, 2 * HG_WIDTH:3 * HG_WIDTH]
    gt = _silu(proj[:, 3 * HG_WIDTH:])
    ng = ng_ref[...]
    tok, lane = _iota2((tb, tb * HG_DV))
    spread = ((lane // HG_DV) == tok).astype(BF16)

    spread3 = jnp.concatenate([spread] * 3, axis=0)

    def columns(x, exact):
        if exact:
            return jnp.dot(jnp.concatenate(_split3(x.T), axis=1), spread3, preferred_element_type=F32)
        return jnp.dot(x.T.astype(BF16), spread, preferred_element_type=F32)

    for hh in range(HG_HEADS):
        sk = slice(hh * HG_DK, (hh + 1) * HG_DK)
        sv = slice(hh * HG_DV, (hh + 1) * HG_DV)
        f_c, k_c, q_c = columns(f[:, sk], True), columns(k[:, sk], False), columns(q[:, sk], False)
        for b in range(tb):
            blk = slice(b * HG_DV, (b + 1) * HG_DV)
            s_new = f_c[:, blk] * s_ref[b, hh] + k_c[:, blk] * v[b:b + 1, sv]
            snew_ref[b, hh] = s_new
            o = jnp.sum(q_c[:, blk] * s_new, axis=0, keepdims=True)
            o_ref[b:b + 1, sv] = _rms(o, ng) * gt[b:b + 1, sv]


def _hgrn_sample(proj, lb_logits, norm_g, state, layer, j):
    nb = proj.shape[0]
    tb = SAMPLE_TOKENS_PER_STEP
    st_spec = pl.BlockSpec((tb, HG_HEADS, HG_DK, HG_DV), lambda i: (i, 0, 0, 0))
    return pl.pallas_call(
        functools.partial(_hgrn_sample_kernel, layer=layer, tb=tb),
        grid=(nb // tb,),
        in_specs=[pl.BlockSpec((tb, 4 * HG_WIDTH), lambda i: (i, 0)),
                  pl.BlockSpec((DEPTH, HG_WIDTH), lambda i: (0, 0)),
                  pl.BlockSpec((None, 1, HG_DV), lambda i: (j, 0, 0)),
                  st_spec],
        out_specs=(st_spec, pl.BlockSpec((tb, HG_WIDTH), lambda i: (i, 0))),
        out_shape=(jax.ShapeDtypeStruct(state.shape, F32), jax.ShapeDtypeStruct((nb, HG_WIDTH), F32)),
        compiler_params=_params("parallel"),
        name="hgrn_sample",
    )(proj, lb_logits, norm_g.reshape(-1, 1, HG_DV), state)


def _xattn_prompt_kernel(x_ref, g_ref, wq_ref, k_ref, v_ref, wo_ref, y_ref):
    x = x_ref[...]
    h = _rms(x, g_ref[...]).astype(BF16)
    q = _dot(h, wq_ref[...]) * (1.0 / math.sqrt(MEM_HD))
    q = q.astype(BF16)
    outs = []
    for hh in range(MEM_HEADS):
        sl = slice(hh * MEM_HD, (hh + 1) * MEM_HD)
        s = lax.dot_general(q[:, sl], k_ref[:, sl], (((1,), (1,)), ((), ())), preferred_element_type=F32)
        p = jnp.exp(s - jnp.max(s, axis=-1, keepdims=True))
        den = jnp.sum(p, axis=-1, keepdims=True)
        o = jnp.dot(p.astype(BF16), v_ref[:, sl], preferred_element_type=F32)
        outs.append((o / den).astype(BF16))
    o = jnp.concatenate(outs, axis=-1)
    y_ref[...] = x + _dot(o, wo_ref[...])


def _xattn_prompt(x, g, w_q, mem_k, mem_v, w_o, layer, batch, seq):
    tt = XA_ROW_TILE
    row_spec = pl.BlockSpec((None, tt, D_MODEL), lambda b, t: (b, t, 0))
    kv_spec = pl.BlockSpec((None, N_MEM, D_MODEL), lambda b, t: (layer, b, 0))
    w_spec = _resident((None, D_MODEL, D_MODEL), lambda b, t: (layer, 0, 0))
    y = pl.pallas_call(
        _xattn_prompt_kernel,
        grid=(batch, seq // tt),
        in_specs=[row_spec, pl.BlockSpec((None, 1, D_MODEL), lambda b, t: (layer, 0, 0)),
                  w_spec, kv_spec, kv_spec, w_spec],
        out_specs=row_spec,
        out_shape=jax.ShapeDtypeStruct((batch, seq, D_MODEL), F32),
        compiler_params=_params("parallel", "parallel"),
        name="xattn_prompt",
    )(x.reshape(batch, seq, D_MODEL), g.reshape(DEPTH, 1, D_MODEL), w_q, mem_k, mem_v, w_o)
    return y.reshape(batch * seq, D_MODEL)


def _xattn_sample_kernel(q_ref, k_ref, v_ref, o_ref, *, tb):
    scale = 1.0 / math.sqrt(MEM_HD)
    rows = N_MEM * MEM_HEADS
    head, lane = _iota2((MEM_HEADS, rows))
    own = (lane & (MEM_HEADS - 1)) == head
    for b in range(tb):
        q = (q_ref[b] * scale).astype(BF16)
        k = k_ref[b].reshape(rows, MEM_HD).astype(BF16)
        v = v_ref[b].reshape(rows, MEM_HD).astype(BF16)
        s = lax.dot_general(q, k, (((1,), (1,)), ((), ())), preferred_element_type=F32)
        s = jnp.where(own, s, -jnp.inf)
        p = jnp.exp(s - jnp.max(s, axis=-1, keepdims=True))
        den = jnp.sum(p, axis=-1, keepdims=True)
        o_ref[b] = jnp.dot(p.astype(BF16), v, preferred_element_type=F32) / den


def _xattn_sample_vpu(q_ref, k_ref, v_ref, o_ref, *, tb):
    scale = 1.0 / math.sqrt(MEM_HD)
    for b in range(tb):
        q = q_ref[b] * scale
        s = jnp.sum(k_ref[b] * q[None], axis=-1, keepdims=True)
        p = jnp.exp(s - jnp.max(s, axis=0, keepdims=True))
        p = p / jnp.sum(p, axis=0, keepdims=True)
        o_ref[b] = jnp.sum(p * v_ref[b], axis=0)


def _xattn_sample(q, cache_k, cache_v, layer):
    nb = q.shape[0]
    tb = XA_SAMPLE_TOKENS_PER_STEP
    kv_spec = pl.BlockSpec((None, tb, N_MEM, MEM_HEADS, MEM_HD), lambda i: (layer, i, 0, 0, 0))
    q_spec = pl.BlockSpec((tb, MEM_HEADS, MEM_HD), lambda i: (i, 0, 0))
    return pl.pallas_call(
        functools.partial(_xattn_sample_kernel, tb=tb),
        grid=(nb // tb,),
        in_specs=[q_spec, kv_spec, kv_spec],
        out_specs=q_spec,
        out_shape=jax.ShapeDtypeStruct((nb, MEM_HEADS, MEM_HD), F32),
        compiler_params=_params("parallel"),
        name="xattn_sample",
    )(q.reshape(nb, MEM_HEADS, MEM_HD), cache_k, cache_v).reshape(nb, D_MODEL)


def _ffn1(x, w, i, ffn, emit):
    if emit:
        return ffn(x, w["ffn1_norm"], w["ffn1_w_in"], w["ffn1_w_out"], i, post="emit", g2=w["mix_norm"][i],
                   u_dtype=F32)
    return ffn(x, w["ffn1_norm"], w["ffn1_w_in"], w["ffn1_w_out"], i)


def _ffn2(x, w, i, ffn):
    last = i == DEPTH - 1
    return ffn(x, w["ffn2_norm"], w["ffn2_w_in"], w["ffn2_w_out"], i,
               post="replace" if last else "none", g2=w["final_norm"] if last else None)


def _prompt_trunk(x, batch, seq, mem_k, mem_v, w, ffn):
    zero = jnp.zeros((batch, S5_GROUPS, S5_STATE), F32)
    new_re, new_im, new_hg = [], [], []
    for i in range(DEPTH):
        j = i // 2
        if i % 2 == 0:
            x, u = _ffn1(x, w, i, ffn, True)
            x, hr, hi = _s5_mixer(x, u, zero, zero, w["s5_pieces"][j], w["s5_w_glu"], j, batch, seq)
            new_re.append(hr)
            new_im.append(hi)
        else:
            x = _ffn1(x, w, i, ffn, False)
            x, sn = _hgrn_prompt(x, w["mix_norm"], w["hg_w_in"], w["hg_lb_logits"], w["hg_norm"],
                                 w["hg_w_out"], i, j, batch, seq)
            new_hg.append(sn)
        x = _xattn_prompt(x, w["xattn_norm"], w["xattn_w_q"], mem_k, mem_v, w["xattn_w_o"], i, batch, seq)
        x = _ffn2(x, w, i, ffn)
    return x, jnp.stack(new_re), jnp.stack(new_im), jnp.stack(new_hg)


def _sample_trunk(x, batch, s5_re, s5_im, hg_state, w):
    new_re, new_im, new_hg = [], [], []
    for i in range(DEPTH):
        j = i // 2
        if i % 2 == 0:
            x, u = _ffn1(x, w, i, _ffn, True)
            x, hr, hi = _s5_mixer(x, u, s5_re[j], s5_im[j], w["s5_pieces"][j], w["s5_w_glu"], j, batch, 1)
            new_re.append(hr)
            new_im.append(hi)
        else:
            x = _ffn1(x, w, i, _ffn, False)
            proj = _proj(x, w["hg_w_in"], j, g=w["mix_norm"], g_layer=i)
            sn, o = _hgrn_sample(proj, w["hg_lb_logits"], w["hg_norm"], hg_state[j], i, j)
            x = _proj(o, w["hg_w_out"], j, res=x)
            new_hg.append(sn)
        q = _proj(x, w["xattn_w_q"], i, g=w["xattn_norm"], g_layer=i)
        o = yield q, i
        x = _proj(o, w["xattn_w_o"], i, res=x)
        x = _ffn2(x, w, i, _ffn)
    return x, jnp.stack(new_re), jnp.stack(new_im), jnp.stack(new_hg)


def kernel(x_prompt, x_sample, mem_prompt, state_s5_re, state_s5_im, state_hgrn, cache_mem_k, cache_mem_v, ffn1_norm, ffn1_w_in, ffn1_w_out, mix_norm, xattn_norm, mem_norm, xattn_w_q, xattn_w_kv, xattn_w_o, ffn2_norm, ffn2_w_in, ffn2_w_out, s5_a_re, s5_a_im, s5_log_dt, s5_b_re, s5_b_im, s5_c_re, s5_c_im, s5_d, s5_w_glu, hg_w_in, hg_lb_logits, hg_norm, hg_w_out, final_norm):
    bp, seq, _ = x_prompt.shape
    bs = x_sample.shape[0]
    w = dict(ffn1_norm=ffn1_norm, ffn1_w_in=ffn1_w_in, ffn1_w_out=ffn1_w_out, mix_norm=mix_norm,
             xattn_norm=xattn_norm, xattn_w_q=xattn_w_q, xattn_w_o=xattn_w_o, ffn2_norm=ffn2_norm,
             ffn2_w_in=ffn2_w_in, ffn2_w_out=ffn2_w_out,
             s5_pieces=[_s5_pieces(s5_a_re[j], s5_a_im[j], s5_log_dt[j], s5_b_re[j], s5_b_im[j], s5_c_re[j],
                                   s5_c_im[j], s5_d[j], S5_CHUNK) for j in range(s5_a_re.shape[0])],
             s5_w_glu=s5_w_glu, hg_w_in=hg_w_in, hg_lb_logits=hg_lb_logits, hg_norm=hg_norm,
             hg_w_out=hg_w_out, final_norm=final_norm)

    mem_k, mem_v, mem_k_bf, mem_v_bf = _mem_kv(mem_prompt, mem_norm, xattn_w_kv)

    sample = _sample_trunk(x_sample.reshape(bs, D_MODEL), bs, state_s5_re, state_s5_im, state_hgrn, w)
    pending = [next(sample)]
    sample_out = []

    def resume(o):
        try:
            pending[0] = sample.send(o)
        except StopIteration as done:
            pending[0] = None
            sample_out.append(done.value)

    def ffn_with_rider(x, *args, **kwargs):
        steps = x.shape[0] // min(FFN_ROW_TILE_WITH_RIDER, x.shape[0])
        if pending[0] is None or bs % steps or kwargs.get("post") == "emit":
            return _ffn(x, *args, **kwargs)
        q, layer = pending[0]
        *outs, o = _ffn(x, *args, rider=(q, cache_mem_k, cache_mem_v, layer), **kwargs)
        resume(o)
        return outs[0] if len(outs) == 1 else tuple(outs)

    y_p, re_p, im_p, hg_p = _prompt_trunk(x_prompt.reshape(bp * seq, D_MODEL), bp, seq, mem_k_bf, mem_v_bf, w,
                                          ffn_with_rider)
    while pending[0] is not None:
        q, layer = pending[0]
        resume(_xattn_sample(q, cache_mem_k, cache_mem_v, layer))
    y_s, re_s, im_s, hg_s = sample_out[0]
    return (y_p.reshape(bp, seq, D_MODEL), y_s.reshape(bs, 1, D_MODEL), re_p, im_p, re_s, im_s, hg_p, hg_s,
            mem_k, mem_v)
```
